```python
import math
import jax, jax.numpy as jnp
from jax import lax
import numpy as np

D_MODEL = 1024
BATCH = 8
SEQ = 4096
DEPTH = 4

D_FF = 2816
GDN_HEADS = 8
GDN_DK = 128
GDN_DV = 128
CONV_K = 4
CHUNK = 64
MLA_HEADS = 8
Q_LORA = 384
KV_LORA = 256
QK_NOPE = 128
QK_ROPE = 64
V_HEAD = 128
ROPE_THETA = 10000.0
Q_BLOCK = 128
EPS = 1e-6
N_MOD = 9

GDN_QK = GDN_HEADS * GDN_DK
GDN_V = GDN_HEADS * GDN_DV
GDN_CONV_C = 2 * GDN_QK + GDN_V
MLA_QK = QK_NOPE + QK_ROPE
IN_SIZES = (GDN_CONV_C, GDN_V, GDN_HEADS, GDN_HEADS, Q_LORA, KV_LORA, QK_ROPE, D_MODEL, D_MODEL)
N_IN = GDN_CONV_C + GDN_V + 2 * GDN_HEADS + Q_LORA + KV_LORA + QK_ROPE + 2 * D_MODEL

kernel_name = 'hybrid_gdn_mla_macaron_adaln'


def rms_norm(x, gain):
    xf = x.astype(jnp.float32)
    y = xf * lax.rsqrt(jnp.mean(xf * xf, axis=-1, keepdims=True) + EPS)
    return (y * gain.astype(jnp.float32)).astype(x.dtype)


def l2_norm(x):
    xf = x.astype(jnp.float32)
    return xf * lax.rsqrt(jnp.sum(xf * xf, axis=-1, keepdims=True) + EPS)


def modulate(x, gain, shift, scale):
    return rms_norm(x, gain) * (1.0 + scale[:, None]) + shift[:, None]


def swiglu(h, w1, w3, w2):
    return (jax.nn.silu(h @ w1) * (h @ w3)) @ w2


def causal_dwconv(x, w):
    S = x.shape[1]
    xp = jnp.pad(x, ((0, 0), (CONV_K - 1, 0), (0, 0)))
    return sum(xp[:, j:j + S] * w[j] for j in range(CONV_K))


def rope_tables(positions):
    half = QK_ROPE // 2
    inv_freq = ROPE_THETA ** (-jnp.arange(half, dtype=jnp.float32) / half)
    ang = positions.astype(jnp.float32)[..., None] * inv_freq
    return jnp.cos(ang), jnp.sin(ang)


def apply_rope(x, cos, sin):
    x1, x2 = jnp.split(x, 2, axis=-1)
    c = cos[:, :, None].astype(x.dtype)
    s = sin[:, :, None].astype(x.dtype)
    return jnp.concatenate([x1 * c - x2 * s, x2 * c + x1 * s], axis=-1)


def gated_delta_rule(q, k, v, g, beta):
    B, S, H, DK = q.shape
    DV = v.shape[-1]
    N = S // CHUNK
    f32 = jnp.float32

    def chunks(t):
        t = t.astype(f32).reshape((B, N, CHUNK, H) + t.shape[3:])
        return jnp.moveaxis(t, (1, 3), (0, 2))

    qc = chunks(q) * (DK ** -0.5)
    kc, vc = chunks(k), chunks(v)
    gc, bc = chunks(g), chunks(beta)
    gcum = jnp.cumsum(gc, axis=-1)
    causal = jnp.tril(jnp.ones((CHUNK, CHUNK), bool))
    strict = jnp.tril(jnp.ones((CHUNK, CHUNK), bool), -1)
    decay = jnp.exp(jnp.where(causal, gcum[..., :, None] - gcum[..., None, :], -jnp.inf))
    kb = kc * bc[..., None]
    m = jnp.where(strict, jnp.einsum('nbhid,nbhjd->nbhij', kb, kc) * decay, 0.0)
    eye = jnp.eye(CHUNK, dtype=f32)
    t_inv = lax.linalg.triangular_solve(eye + m, jnp.broadcast_to(eye, m.shape), left_side=True, lower=True)
    u = t_inv @ (vc * bc[..., None])
    w = t_inv @ (kb * jnp.exp(gcum)[..., None])
    qk = jnp.einsum('nbhid,nbhjd->nbhij', qc, kc) * decay
    q_dec = qc * jnp.exp(gcum)[..., None]
    k_dec = kc * jnp.exp(gcum[..., -1:] - gcum)[..., None]
    g_last = jnp.exp(gcum[..., -1])

    def step(state, inp):
        u_i, w_i, qk_i, qd_i, kd_i, gl_i = inp
        v_new = u_i - w_i @ state
        o_i = qd_i @ state + qk_i @ v_new
        state = state * gl_i[..., None, None] + jnp.swapaxes(kd_i, -1, -2) @ v_new
        return state, o_i

    s0 = jnp.zeros((B, H, DK, DV), f32)
    _, o = lax.scan(step, s0, (u, w, qk, q_dec, k_dec, g_last))
    return jnp.moveaxis(o, (0, 2), (1, 3)).reshape(B, S, H, DV)


def gated_deltanet(qkv, z, b_raw, a_raw, conv_w, a_log, dt_bias, out_gain):
    B, S, _ = qkv.shape
    qkv_c = jax.nn.silu(causal_dwconv(qkv, conv_w))
    q, k, v = jnp.split(qkv_c, [GDN_QK, 2 * GDN_QK], axis=-1)
    q = l2_norm(q.reshape(B, S, GDN_HEADS, GDN_DK))
    k = l2_norm(k.reshape(B, S, GDN_HEADS, GDN_DK))
    v = v.reshape(B, S, GDN_HEADS, GDN_DV)
    beta = jax.nn.sigmoid(b_raw.astype(jnp.float32))
    g = -jnp.exp(a_log.astype(jnp.float32)) * jax.nn.softplus(a_raw.astype(jnp.float32) + dt_bias.astype(jnp.float32))
    o = gated_delta_rule(q, k, v, g, beta)
    o = rms_norm(o, out_gain) * jax.nn.silu(z.astype(jnp.float32).reshape(B, S, GDN_HEADS, GDN_DV))
    return o.reshape(B, S, GDN_V).astype(qkv.dtype)


def causal_block_attention(q, k, v):
    B, S, H, Dqk = q.shape
    nb = S // Q_BLOCK
    scale = Dqk ** -0.5
    qb = jnp.moveaxis(q.reshape(B, nb, Q_BLOCK, H, Dqk), 1, 0)
    kpos = jnp.arange(S)

    def one_block(args):
        i, q_i = args
        s = jnp.einsum('bqhd,bkhd->bhqk', q_i, k, preferred_element_type=jnp.float32) * scale
        qpos = i * Q_BLOCK + jnp.arange(Q_BLOCK)
        s = jnp.where(kpos[None, :] <= qpos[:, None], s, -jnp.inf)
        p = jax.nn.softmax(s, axis=-1)
        return jnp.einsum('bhqk,bkhv->bqhv', p.astype(v.dtype), v)

    o = lax.map(one_block, (jnp.arange(nb), qb))
    return jnp.moveaxis(o, 0, 1).reshape(B, S, H, v.shape[-1])


def mla(q_lat, kv_lat, k_pe, cos, sin, q_lat_gain, kv_lat_gain, w_q_up, w_kv_up, q_norm, k_norm):
    B, S, _ = q_lat.shape
    q = (rms_norm(q_lat, q_lat_gain) @ w_q_up).reshape(B, S, MLA_HEADS, MLA_QK)
    kv = (rms_norm(kv_lat, kv_lat_gain) @ w_kv_up).reshape(B, S, MLA_HEADS, QK_NOPE + V_HEAD)
    k_nope, v = jnp.split(kv, [QK_NOPE], axis=-1)
    k = jnp.concatenate([k_nope, jnp.broadcast_to(k_pe[:, :, None, :], (B, S, MLA_HEADS, QK_ROPE))], axis=-1)
    q = rms_norm(q, q_norm)
    k = rms_norm(k, k_norm)
    q = jnp.concatenate([q[..., :QK_NOPE], apply_rope(q[..., QK_NOPE:], cos, sin)], axis=-1)
    k = jnp.concatenate([k[..., :QK_NOPE], apply_rope(k[..., QK_NOPE:], cos, sin)], axis=-1)
    o = causal_block_attention(q, k, v)
    return o.reshape(B, S, MLA_HEADS * V_HEAD)


def token_mix(h, cos, sin, w_in, gdn_conv, gdn_a_log, gdn_dt_bias, gdn_out_gain, q_lat_gain, kv_lat_gain,
              w_q_up, w_kv_up, q_norm, k_norm, w_branch_a, w_branch_b, w_out):
    offsets = np.cumsum(IN_SIZES)[:-1].tolist()
    qkv, z, b_raw, a_raw, q_lat, kv_lat, k_pe, gate_a, gate_b = jnp.split(h @ w_in, offsets, axis=-1)
    y_a = gated_deltanet(qkv, z, b_raw, a_raw, gdn_conv, gdn_a_log, gdn_dt_bias, gdn_out_gain) @ w_branch_a
    y_b = mla(q_lat, kv_lat, k_pe, cos, sin, q_lat_gain, kv_lat_gain, w_q_up, w_kv_up, q_norm, k_norm) @ w_branch_b
    y = jax.nn.sigmoid(gate_a) * y_a + jax.nn.sigmoid(gate_b) * y_b
    return y @ w_out


def setup_inputs(seed: int = 0) -> dict:
    key = jax.random.key(seed)
    ks = iter(jax.random.split(key, 40))
    L, D = DEPTH, D_MODEL

    def normal(shape, fan_in, gain=1.0):
        return jax.random.normal(next(ks), shape, jnp.float32) * (gain * fan_in ** -0.5)

    def ones_noise(shape):
        return 1.0 + 0.05 * jax.random.normal(next(ks), shape, jnp.float32)

    x = jax.random.normal(next(ks), (BATCH, SEQ, D), jnp.float32)
    c = jax.random.normal(next(ks), (BATCH, D), jnp.float32)
    positions = (jax.random.randint(next(ks), (BATCH, 1), 0, 1024, jnp.int32)
                 + jnp.arange(SEQ, dtype=jnp.int32)[None, :])
    ada_w = normal((L, D, N_MOD * D), D, 0.25)
    ada_b = 0.01 * jax.random.normal(next(ks), (L, N_MOD * D), jnp.float32)
    norm_ffn1 = ones_noise((L, D))
    ffn1_w1 = normal((L, D, D_FF), D)
    ffn1_w3 = normal((L, D, D_FF), D)
    ffn1_w2 = normal((L, D_FF, D), D_FF)
    norm_mix = ones_noise((L, D))
    w_in = normal((L, D, N_IN), D)
    gdn_conv = normal((L, CONV_K, GDN_CONV_C), CONV_K)
    gdn_a_log = jnp.log(jax.random.uniform(next(ks), (L, GDN_HEADS), jnp.float32, 1.0, 16.0))
    dt = jnp.exp(jax.random.uniform(next(ks), (L, GDN_HEADS), jnp.float32, math.log(1e-3), math.log(1e-1)))
    gdn_dt_bias = dt + jnp.log(-jnp.expm1(-dt))
    gdn_out_gain = ones_noise((L, GDN_DV))
    mla_q_lat_gain = ones_noise((L, Q_LORA))
    mla_kv_lat_gain = ones_noise((L, KV_LORA))
    mla_w_q_up = normal((L, Q_LORA, MLA_HEADS * MLA_QK), Q_LORA)
    mla_w_kv_up = normal((L, KV_LORA, MLA_HEADS * (QK_NOPE + V_HEAD)), KV_LORA)
    mla_q_norm = ones_noise((L, MLA_QK))
    mla_k_norm = ones_noise((L, MLA_QK))
    w_branch_a = normal((L, GDN_V, D), GDN_V)
    w_branch_b = normal((L, MLA_HEADS * V_HEAD, D), MLA_HEADS * V_HEAD)
    w_out = normal((L, D, D), D)
    norm_ffn2 = ones_noise((L, D))
    ffn2_w1 = normal((L, D, D_FF), D)
    ffn2_w3 = normal((L, D, D_FF), D)
    ffn2_w2 = normal((L, D_FF, D), D_FF)
    return {'x': x, 'c': c, 'positions': positions, 'ada_w': ada_w, 'ada_b': ada_b,
            'norm_ffn1': norm_ffn1, 'ffn1_w1': ffn1_w1, 'ffn1_w3': ffn1_w3, 'ffn1_w2': ffn1_w2,
            'norm_mix': norm_mix, 'w_in': w_in, 'gdn_conv': gdn_conv, 'gdn_a_log': gdn_a_log,
            'gdn_dt_bias': gdn_dt_bias, 'gdn_out_gain': gdn_out_gain, 'mla_q_lat_gain': mla_q_lat_gain,
            'mla_kv_lat_gain': mla_kv_lat_gain, 'mla_w_q_up': mla_w_q_up, 'mla_w_kv_up': mla_w_kv_up,
            'mla_q_norm': mla_q_norm, 'mla_k_norm': mla_k_norm, 'w_branch_a': w_branch_a,
            'w_branch_b': w_branch_b, 'w_out': w_out, 'norm_ffn2': norm_ffn2, 'ffn2_w1': ffn2_w1,
            'ffn2_w3': ffn2_w3, 'ffn2_w2': ffn2_w2}


def reference(x, c, positions, ada_w, ada_b, norm_ffn1, ffn1_w1, ffn1_w3, ffn1_w2, norm_mix, w_in, gdn_conv,
              gdn_a_log, gdn_dt_bias, gdn_out_gain, mla_q_lat_gain, mla_kv_lat_gain, mla_w_q_up, mla_w_kv_up,
              mla_q_norm, mla_k_norm, w_branch_a, w_branch_b, w_out, norm_ffn2, ffn2_w1, ffn2_w3, ffn2_w2):
    cos, sin = rope_tables(positions)
    cond = jax.nn.silu(c)
    for l in range(DEPTH):
        mod = cond @ ada_w[l] + ada_b[l]
        sh1, sc1, g1, sh2, sc2, g2, sh3, sc3, g3 = jnp.split(mod, N_MOD, axis=-1)
        h = modulate(x, norm_ffn1[l], sh1, sc1)
        x = x + 0.5 * (1.0 + g1)[:, None] * swiglu(h, ffn1_w1[l], ffn1_w3[l], ffn1_w2[l])
        h = modulate(x, norm_mix[l], sh2, sc2)
        y = token_mix(h, cos, sin, w_in[l], gdn_conv[l], gdn_a_log[l], gdn_dt_bias[l], gdn_out_gain[l],
                      mla_q_lat_gain[l], mla_kv_lat_gain[l], mla_w_q_up[l], mla_w_kv_up[l], mla_q_norm[l],
                      mla_k_norm[l], w_branch_a[l], w_branch_b[l], w_out[l])
        x = x + (1.0 + g2)[:, None] * y
        h = modulate(x, norm_ffn2[l], sh3, sc3)
        x = x + 0.5 * (1.0 + g3)[:, None] * swiglu(h, ffn2_w1[l], ffn2_w3[l], ffn2_w2[l])
    return x
```

```python
import functools
import math

import numpy as np
import jax
import jax.numpy as jnp
from jax import lax
from jax.experimental import pallas as pl
from jax.experimental.pallas import tpu as pltpu

F32 = jnp.float32
BF16 = jnp.bfloat16

EPS = 1e-6
N_MOD = 9
GDN_HEADS = 8
GDN_DK = 128
GDN_DV = 128
CONV_K = 4
CHUNK = 64
LOG2_CHUNK = 6
MLA_HEADS = 8
Q_LORA = 384
KV_LORA = 256
QK_NOPE = 128
QK_ROPE = 64
V_HEAD = 128
ROPE_THETA = 10000.0
MLA_QK = QK_NOPE + QK_ROPE

V7X_VMEM_BYTES = 64 * 1024 * 1024
V7X_LANES = 128
V7X_SUBLANES = 8
VMEM_CAP_BYTES = V7X_VMEM_BYTES - 6 * 1024 * 1024

FFN_TOKENS = 512
MIX_TOKENS = 256
GATE_TOKENS = 512
GDN_TOKENS = 256
MLA_TOKENS = 512
ATTN_Q = 512
ATTN_K = 512
MERGE_TOKENS = 1024
HALO = V7X_SUBLANES


def _params(vmem_bytes, n_axes):
    limit = int(min(VMEM_CAP_BYTES, max(vmem_bytes, 16 * 1024 * 1024)))
    return pltpu.CompilerParams(dimension_semantics=("arbitrary",) * n_axes, vmem_limit_bytes=limit)


def _resident(block_shape, index_map):
    return pl.BlockSpec(block_shape, index_map, pipeline_mode=pl.Buffered(1))


def _dot(a, b):
    return jnp.dot(a, b, preferred_element_type=F32)


def _dot_nt(a, b):
    return lax.dot_general(a, b, (((1,), (1,)), ((), ())), preferred_element_type=F32)


def _sigmoid(x):
    return 1.0 / (1.0 + jnp.exp(-x))


def _silu(x):
    return x * _sigmoid(x)


def _modulate(x, gain, shift, scale):
    y = x * lax.rsqrt(jnp.mean(x * x, axis=-1, keepdims=True) + EPS)
    return (y * gain) * (1.0 + scale) + shift


def _split3(x):
    x1 = x.astype(BF16)
    r1 = x - x1.astype(F32)
    x2 = r1.astype(BF16)
    x3 = (r1 - x2.astype(F32)).astype(BF16)
    return x1, x2, x3


def _mod_kernel(c_ref, w_ref, b_ref, o_ref):
    cond = _silu(c_ref[...]).astype(BF16)
    o_ref[...] = _dot(cond, w_ref[...].astype(BF16)) + b_ref[...]


def _ada_mod(c, ada_w, ada_b):
    L, D, _ = ada_w.shape
    B = c.shape[0]
    ada_b3 = ada_b.reshape(L, N_MOD, 1, D)
    out = pl.pallas_call(
        _mod_kernel,
        grid=(L, N_MOD),
        in_specs=[
            pl.BlockSpec((B, D), lambda l, j: (0, 0)),
            pl.BlockSpec((None, D, D), lambda l, j: (l, 0, j)),
            pl.BlockSpec((None, None, 1, D), lambda l, j: (l, j, 0, 0)),
        ],
        out_specs=pl.BlockSpec((None, None, B, D), lambda l, j: (l, j, 0, 0)),
        out_shape=jax.ShapeDtypeStruct((L, N_MOD, B, D), F32),
        compiler_params=_params(4 * D * D * 4, 2),
        name="ada_mod",
    )(c, ada_w, ada_b3)
    return jnp.transpose(out, (0, 2, 1, 3))


def _rope_kernel(pos_ref, freq_ref, sign_ref, cos_ref, sin_ref):
    ang = pos_ref[...].astype(F32) * freq_ref[...]
    cos_ref[...] = jnp.cos(ang)
    sin_ref[...] = jnp.sin(ang) * sign_ref[...]


def _rope_tables(positions):
    B, S = positions.shape
    half = QK_ROPE // 2
    inv_freq = ROPE_THETA ** (-jnp.arange(half, dtype=F32) / half)
    freq2 = jnp.concatenate([inv_freq, inv_freq]).reshape(1, QK_ROPE)
    sign2 = jnp.concatenate([-jnp.ones((half,), F32), jnp.ones((half,), F32)]).reshape(1, QK_ROPE)
    ts = min(S, 1024)
    return pl.pallas_call(
        _rope_kernel,
        grid=(B, S // ts),
        in_specs=[
            pl.BlockSpec((None, ts, 1), lambda b, s: (b, s, 0)),
            pl.BlockSpec((1, QK_ROPE), lambda b, s: (0, 0)),
            pl.BlockSpec((1, QK_ROPE), lambda b, s: (0, 0)),
        ],
        out_specs=[pl.BlockSpec((None, ts, QK_ROPE), lambda b, s: (b, s, 0))] * 2,
        out_shape=[jax.ShapeDtypeStruct((B, S, QK_ROPE), F32)] * 2,
        compiler_params=_params(8 * ts * V7X_LANES * 4, 2),
        name="rope_tables",
    )(positions.reshape(B, S, 1), freq2, sign2)


def _ffn_kernel(x_ref, mod_ref, gain_ref, w1_ref, w3_ref, w2_ref, o_ref, *, row):
    x = x_ref[...]
    mod = mod_ref[...]
    h = _modulate(x, gain_ref[...], mod[row:row + 1], mod[row + 1:row + 2]).astype(BF16)
    a = _dot(h, w1_ref[...])
    b = _dot(h, w3_ref[...])
    g = (_silu(a) * b).astype(BF16)
    y = _dot(g, w2_ref[...])
    o_ref[...] = x + (0.5 * (1.0 + mod[row + 2:row + 3])) * y


def _ffn(x, mod, gain, w1, w3, w2, l, row):
    B, S, D = x.shape
    F = w1.shape[-1]
    tm = min(S, FFN_TOKENS)
    vmem = 3 * D * F * 2 + 4 * tm * D * 4 + 3 * tm * F * 4 + 2 * tm * D * 4
    return pl.pallas_call(
        functools.partial(_ffn_kernel, row=row),
        grid=(B, S // tm),
        in_specs=[
            pl.BlockSpec((None, tm, D), lambda b, s: (b, s, 0)),
            pl.BlockSpec((None, None, N_MOD, D), lambda b, s: (l, b, 0, 0)),
            pl.BlockSpec((None, 1, D), lambda b, s: (l, 0, 0)),
            _resident((None, D, F), lambda b, s: (l, 0, 0)),
            _resident((None, D, F), lambda b, s: (l, 0, 0)),
            _resident((None, F, D), lambda b, s: (l, 0, 0)),
        ],
        out_specs=pl.BlockSpec((None, tm, D), lambda b, s: (b, s, 0)),
        out_shape=jax.ShapeDtypeStruct((B, S, D), F32),
        compiler_params=_params(vmem, 2),
        name="ffn",
    )(x, mod, gain, w1, w3, w2)


def _mixin_kernel(x_ref, mod_ref, gain_ref, wqkv_ref, wz_ref, wg_ref, wlat_ref, wba_ref, conv_ref,
                  qkv_ref, zs_ref, gates_ref, lat_ref, ba_ref, pbuf_ref, *, tm):
    s = pl.program_id(1)
    x = x_ref[...]
    mod = mod_ref[...]
    h = _modulate(x, gain_ref[...], mod[3:4], mod[4:5]).astype(BF16)

    @pl.when(s == 0)
    def _():
        pbuf_ref[0:HALO, :] = jnp.zeros((HALO, pbuf_ref.shape[1]), F32)

    pbuf_ref[HALO:HALO + tm, :] = _dot(h, wqkv_ref[...])
    conv = conv_ref[...]
    n_qk = 2 * GDN_HEADS
    for cb in range(pbuf_ref.shape[1] // V7X_LANES):
        c0 = cb * V7X_LANES
        acc = None
        for j in range(CONV_K):
            r0 = HALO - (CONV_K - 1) + j
            term = conv[j:j + 1, c0:c0 + V7X_LANES] * pbuf_ref[r0:r0 + tm, c0:c0 + V7X_LANES]
            acc = term if acc is None else acc + term
        y = _silu(acc)
        if cb < n_qk:
            y = y * lax.rsqrt(jnp.sum(y * y, axis=-1, keepdims=True) + EPS)
        qkv_ref[cb] = y.astype(BF16)
    pbuf_ref[0:HALO, :] = pbuf_ref[tm:tm + HALO, :]

    z = _silu(_dot(h, wz_ref[...]))
    for hh in range(GDN_HEADS):
        zs_ref[hh] = z[:, hh * GDN_DV:(hh + 1) * GDN_DV].astype(BF16)
    gates_ref[...] = _sigmoid(_dot(h, wg_ref[...])).astype(BF16)
    lat_ref[...] = _dot(h, wlat_ref[...])
    ba_ref[...] = _dot_nt(wba_ref[...], h)


def _mixin(x, mod, gain, wqkv, wz, wg, wlat, wba, conv, l):
    B, S, D = x.shape
    C = wqkv.shape[-1]
    nz = wz.shape[-1]
    ng = wg.shape[-1]
    nl = wlat.shape[-1]
    nb = wba.shape[1]
    tm = min(S, MIX_TOKENS)
    ncb = C // V7X_LANES
    vmem = ((C + nz + ng + nl) * D * 2 + 2 * tm * D * 4 + 2 * tm * (C + nz + ng) * 2 + 2 * tm * nl * 4
            + (tm + HALO) * C * 4 + tm * C * 4 + tm * (nz + ng) * 4 + 4 * 1024 * 1024)
    return pl.pallas_call(
        functools.partial(_mixin_kernel, tm=tm),
        grid=(B, S // tm),
        in_specs=[
            pl.BlockSpec((None, tm, D), lambda b, s: (b, s, 0)),
            pl.BlockSpec((None, None, N_MOD, D), lambda b, s: (l, b, 0, 0)),
            pl.BlockSpec((None, 1, D), lambda b, s: (l, 0, 0)),
            _resident((None, D, C), lambda b, s: (l, 0, 0)),
            _resident((None, D, nz), lambda b, s: (l, 0, 0)),
            _resident((None, D, ng), lambda b, s: (l, 0, 0)),
            _resident((None, D, nl), lambda b, s: (l, 0, 0)),
            _resident((None, nb, D), lambda b, s: (l, 0, 0)),
            pl.BlockSpec((None, CONV_K, C), lambda b, s: (l, 0, 0)),
        ],
        out_specs=[
            pl.BlockSpec((None, ncb, tm, V7X_LANES), lambda b, s: (b, 0, s, 0)),
            pl.BlockSpec((None, GDN_HEADS, tm, GDN_DV), lambda b, s: (b, 0, s, 0)),
            pl.BlockSpec((None, tm, ng), lambda b, s: (b, s, 0)),
            pl.BlockSpec((None, tm, nl), lambda b, s: (b, s, 0)),
            pl.BlockSpec((None, nb, tm), lambda b, s: (b, 0, s)),
        ],
        out_shape=[
            jax.ShapeDtypeStruct((B, ncb, S, V7X_LANES), BF16),
            jax.ShapeDtypeStruct((B, GDN_HEADS, S, GDN_DV), BF16),
            jax.ShapeDtypeStruct((B, S, ng), BF16),
            jax.ShapeDtypeStruct((B, S, nl), F32),
            jax.ShapeDtypeStruct((B, nb, S), F32),
        ],
        scratch_shapes=[pltpu.VMEM((tm + HALO, C), F32)],
        compiler_params=_params(vmem, 2),
        name="mixin",
    )(x, mod, gain, wqkv, wz, wg, wlat, wba, conv)


def _gate_kernel(ba_ref, alog_ref, dtb_ref, grow_ref, gcol_ref, *, tm):
    H = GDN_HEADS
    ba = ba_ref[...]
    beta = _sigmoid(ba[0:H])
    xa = ba[H:2 * H] + dtb_ref[...]
    softplus = jnp.maximum(xa, 0.0) + jnp.log(1.0 + jnp.exp(-jnp.abs(xa)))
    g = -jnp.exp(alog_ref[...]) * softplus

    ii = lax.broadcasted_iota(jnp.int32, (tm, tm), 0)
    jj = lax.broadcasted_iota(jnp.int32, (tm, tm), 1)
    same = (ii >> LOG2_CHUNK) == (jj >> LOG2_CHUNK)
    upper =jnp.where(same & (ii <= jj), 1.0, 0.0).astype(BF16)
    block = jnp.where(same, 1.0, 0.0).astype(BF16)
    eye = jnp.where(ii == jj, 1.0, 0.0).astype(BF16)
    gam = None
    glast = None
    for part in _split3(g):
        t1 = _dot(part, upper)
        t2 = _dot(part, block)
        gam = t1 if gam is None else gam + t1
        glast = t2 if glast is None else glast + t2
    egam = jnp.exp(gam)
    ekd = jnp.exp(glast - gam)
    egl = jnp.exp(glast)
    zero = jnp.zeros_like(gam[0:1])
    rows = []
    for hh in range(H):
        r = jnp.concatenate([beta[hh:hh + 1], gam[hh:hh + 1], egam[hh:hh + 1], ekd[hh:hh + 1],
                             egl[hh:hh + 1], zero, zero, zero], axis=0)
        grow_ref[hh] = r
        rows.append(r)
    allrows = jnp.concatenate(rows, axis=0)
    cols = None
    for part in _split3(allrows):
        t = _dot_nt(eye, part)
        cols = t if cols is None else cols + t
    for hh in range(H):
        gcol_ref[hh] = cols[:, 8 * hh:8 * hh + 8]


def _gdn_gates(ba, a_log, dt_bias, l):
    B, nb, S = ba.shape
    H = GDN_HEADS
    tm = min(S, GATE_TOKENS)
    vmem = 8 * tm * tm * 4 + 4 * H * tm * V7X_LANES * 4
    return pl.pallas_call(
        functools.partial(_gate_kernel, tm=tm),
        grid=(B, S // tm),
        in_specs=[
            pl.BlockSpec((None, nb, tm), lambda b, s: (b, 0, s)),
            pl.BlockSpec((None, H, 1), lambda b, s: (l, 0, 0)),
            pl.BlockSpec((None, H, 1), lambda b, s: (l, 0, 0)),
        ],
        out_specs=[
            pl.BlockSpec((None, H, 8, tm), lambda b, s: (b, 0, 0, s)),
            pl.BlockSpec((None, H, tm, 8), lambda b, s: (b, 0, s, 0)),
        ],
        out_shape=[
            jax.ShapeDtypeStruct((B, H, 8, S), F32),
            jax.ShapeDtypeStruct((B, H, S, 8), F32),
        ],
        compiler_params=_params(vmem, 2),
        name="gdn_gates",
    )(ba, a_log, dt_bias)


def _gdn_kernel(q_ref, k_ref, v_ref, zs_ref, grow_ref, gcol_ref, og_ref, o_ref, state_ref, vn_ref, *, tc):
    s = pl.program_id(2)

    @pl.when(s == 0)
    def _():
        state_ref[...] = jnp.zeros(state_ref.shape, F32)
        vn_ref[...] = jnp.zeros(vn_ref.shape, BF16)

    q = q_ref[...]
    k = k_ref[...]
    kf = k.astype(F32)
    vf = v_ref[...].astype(F32)
    gc = gcol_ref[...]
    gr = grow_ref[...]
    beta = gc[:, 0:1]
    egam = gc[:, 2:3]
    ekd = gc[:, 3:4]
    scale = GDN_DK ** -0.5

    ii = lax.broadcasted_iota(jnp.int32, (tc, tc), 0)
    jj = lax.broadcasted_iota(jnp.int32, (tc, tc), 1)
    same = (ii >> LOG2_CHUNK) == (jj >> LOG2_CHUNK)
    causal =same & (ii >= jj)
    strict = same & (ii > jj)
    decay = jnp.where(causal, jnp.exp(jnp.minimum(gc[:, 1:2] - gr[1:2, :], 0.0)), 0.0)

    kb = kf * beta
    m = jnp.where(strict, _dot_nt(kb.astype(BF16), k) * decay, 0.0)
    t_inv = jnp.where(ii == jj, 1.0, 0.0) - m
    p = m
    for _ in range(LOG2_CHUNK - 1):
        pb = p.astype(BF16)
        p = _dot(pb, pb)
        t_inv = t_inv + _dot(t_inv.astype(BF16), p.astype(BF16))
    rhs = jnp.concatenate([vf * beta, kb * egam], axis=1).astype(BF16)
    uw = _dot(t_inv.astype(BF16), rhs)
    u = uw[:, :GDN_DV]
    w = uw[:, GDN_DV:].astype(BF16)
    qk = (_dot_nt(q, k) * (decay * scale)).astype(BF16)
    qd = (q.astype(F32) * (egam * scale)).astype(BF16)
    kdt = (kf * ekd).T
    lane = lax.broadcasted_iota(jnp.int32, kdt.shape, 1) >> LOG2_CHUNK

    state = state_ref[...]
    outs = []
    for c in range(tc // CHUNK):
        r0 = c * CHUNK
        ws = _dot(jnp.concatenate([w[r0:r0 + CHUNK], qd[r0:r0 + CHUNK]], axis=0), state.astype(BF16))
        v_new = u[r0:r0 + CHUNK] - ws[:CHUNK]
        vn_ref[r0:r0 + CHUNK, :] = v_new.astype(BF16)
        lhs = jnp.concatenate([qk[r0:r0 + CHUNK], jnp.where(lane == c, kdt, 0.0).astype(BF16)], axis=0)
        r2 = _dot(lhs, vn_ref[...])
        outs.append(ws[CHUNK:] + r2[:CHUNK])
        state = state * gr[4:5, r0:r0 + 1] + r2[CHUNK:]
    state_ref[...] = state
    o = jnp.concatenate(outs, axis=0)
    o = o * lax.rsqrt(jnp.mean(o * o, axis=-1, keepdims=True) + EPS) * og_ref[...]
    o_ref[...] = (o * zs_ref[...].astype(F32)).astype(BF16)


def _gdn(qkv, zs, grow, gcol, out_gain, l):
    B, _, S, _ = qkv.shape
    H = GDN_HEADS
    tc = min(S, GDN_TOKENS)
    vmem = 24 * tc * tc * 4 + 16 * tc * V7X_LANES * 4
    return pl.pallas_call(
        functools.partial(_gdn_kernel, tc=tc),
        grid=(B, H, S // tc),
        in_specs=[
            pl.BlockSpec((None, None, tc, GDN_DK), lambda b, h, s: (b, h, s, 0)),
            pl.BlockSpec((None, None, tc, GDN_DK), lambda b, h, s: (b, H + h, s, 0)),
            pl.BlockSpec((None, None, tc, GDN_DV), lambda b, h, s: (b, 2 * H + h, s, 0)),
            pl.BlockSpec((None, None, tc, GDN_DV), lambda b, h, s: (b, h, s, 0)),
            pl.BlockSpec((None, None, 8, tc), lambda b, h, s: (b, h, 0, s)),
            pl.BlockSpec((None, None, tc, 8), lambda b, h, s: (b, h, s, 0)),
            pl.BlockSpec((None, 1, GDN_DV), lambda b, h, s: (l, 0, 0)),
        ],
        out_specs=pl.BlockSpec((None, tc, GDN_DV), lambda b, h, s: (b, s, h)),
        out_shape=jax.ShapeDtypeStruct((B, S, H * GDN_DV), BF16),
        scratch_shapes=[pltpu.VMEM((GDN_DK, GDN_DV), F32), pltpu.VMEM((tc, GDN_DV), BF16)],
        compiler_params=_params(vmem, 3),
        name="gdn",
    )(qkv, qkv, qkv, zs, grow, gcol, out_gain)


def _swap_halves(y):
    half = y.shape[-1] // 2
    return jnp.concatenate([y[:, half:], y[:, :half]], axis=1)


def _mla_prep_kernel(lat_ref, cos_ref, sin_ref, qg_ref, kvg_ref, wq_ref, wkv_ref, qn_ref, kn_ref,
                     q_ref, k_ref, v_ref):
    H = MLA_HEADS
    lat = lat_ref[...]
    cos2 = cos_ref[...]
    sin2 = sin_ref[...]
    ql = lat[:, :Q_LORA]
    kvl = lat[:, Q_LORA:Q_LORA + KV_LORA]
    kpe = lat[:, Q_LORA + KV_LORA:Q_LORA + KV_LORA + QK_ROPE]
    qln = ql * lax.rsqrt(jnp.mean(ql * ql, axis=-1, keepdims=True) + EPS) * qg_ref[...]
    kvn = kvl * lax.rsqrt(jnp.mean(kvl * kvl, axis=-1, keepdims=True) + EPS) * kvg_ref[...]
    qr = _dot(qln.astype(BF16), wq_ref[...])
    kvr = _dot(kvn.astype(BF16), wkv_ref[...])
    qn = qn_ref[...]
    kn = kn_ref[...]
    scale = MLA_QK ** -0.5
    kpe_ss = jnp.sum(kpe * kpe, axis=-1, keepdims=True)
    for hh in range(H):
        q_nope = qr[:, hh * QK_NOPE:(hh + 1) * QK_NOPE]
        q_rope = qr[:, H * QK_NOPE + hh * QK_ROPE:H * QK_NOPE + (hh + 1) * QK_ROPE]
        ss = jnp.sum(q_nope * q_nope, axis=-1, keepdims=True) + jnp.sum(q_rope * q_rope, axis=-1, keepdims=True)
        r = lax.rsqrt(ss * (1.0 / MLA_QK) + EPS) * scale
        y = q_rope * r * qn[:, QK_NOPE:]
        q_ref[hh, :, 0:QK_NOPE] = (q_nope * r * qn[:, :QK_NOPE]).astype(BF16)
        q_ref[hh, :, QK_NOPE:MLA_QK] = (y * cos2 + _swap_halves(y) * sin2).astype(BF16)

        c0 = hh * (QK_NOPE + V_HEAD)
        k_nope = kvr[:, c0:c0 + QK_NOPE]
        ss = jnp.sum(k_nope * k_nope, axis=-1, keepdims=True) + kpe_ss
        r = lax.rsqrt(ss * (1.0 / MLA_QK) + EPS)
        y = kpe * r * kn[:, QK_NOPE:]
        k_ref[hh, :, 0:QK_NOPE] = (k_nope * r * kn[:, :QK_NOPE]).astype(BF16)
        k_ref[hh, :, QK_NOPE:MLA_QK] = (y * cos2 + _swap_halves(y) * sin2).astype(BF16)
        v_ref[hh] = kvr[:, c0 + QK_NOPE:c0 + QK_NOPE + V_HEAD].astype(BF16)


def _mla_prep(lat, cos2, sin2, q_lat_gain, kv_lat_gain, wq, wkv, q_norm, k_norm, l):
    B, S, nl = lat.shape
    H = MLA_HEADS
    tm = min(S, MLA_TOKENS)
    nq = wq.shape[-1]
    nkv = wkv.shape[-1]
    vmem = (Q_LORA * nq + KV_LORA * nkv) * 2 + 2 * tm * nl * 4 + 2 * tm * (nq + nkv) * 4 + 6 * H * tm * 256 * 2
    return pl.pallas_call(
        _mla_prep_kernel,
        grid=(B, S // tm),
        in_specs=[
            pl.BlockSpec((None, tm, nl), lambda b, s: (b, s, 0)),
            pl.BlockSpec((None, tm, QK_ROPE), lambda b, s: (b, s, 0)),
            pl.BlockSpec((None, tm, QK_ROPE), lambda b, s: (b, s, 0)),
            pl.BlockSpec((None, 1, Q_LORA), lambda b, s: (l, 0, 0)),
            pl.BlockSpec((None, 1, KV_LORA), lambda b, s: (l, 0, 0)),
            _resident((None, Q_LORA, nq), lambda b, s: (l, 0, 0)),
            _resident((None, KV_LORA, nkv), lambda b, s: (l, 0, 0)),
            pl.BlockSpec((None, 1, MLA_QK), lambda b, s: (l, 0, 0)),
            pl.BlockSpec((None, 1, MLA_QK), lambda b, s: (l, 0, 0)),
        ],
        out_specs=[
            pl.BlockSpec((None, H, tm, MLA_QK), lambda b, s: (b, 0, s, 0)),
            pl.BlockSpec((None, H, tm, MLA_QK), lambda b, s: (b, 0, s, 0)),
            pl.BlockSpec((None, H, tm, V_HEAD), lambda b, s: (b, 0, s, 0)),
        ],
        out_shape=[
            jax.ShapeDtypeStruct((B, H, S, MLA_QK), BF16),
            jax.ShapeDtypeStruct((B, H, S, MLA_QK), BF16),
            jax.ShapeDtypeStruct((B, H, S, V_HEAD), BF16),
        ],
        compiler_params=_params(vmem, 2),
        name="mla_prep",
    )(lat, cos2, sin2, q_lat_gain, kv_lat_gain, wq, wkv, q_norm, k_norm)


def _attn_kernel(q_ref, k_ref, v_ref, o_ref, m_ref, l_ref, acc_ref, *, tq, tk):
    i = pl.program_id(2)
    j = pl.program_id(3)

    @pl.when(j == 0)
    def _():
        m_ref[...] = jnp.full(m_ref.shape, -jnp.inf, F32)
        l_ref[...] = jnp.zeros(l_ref.shape, F32)
        acc_ref[...] = jnp.zeros(acc_ref.shape, F32)

    def update(masked):
        s = _dot_nt(q_ref[...], k_ref[...])
        if masked:
            qpos = i * tq + lax.broadcasted_iota(jnp.int32, (tq, tk), 0)
            kpos = j * tk + lax.broadcasted_iota(jnp.int32, (tq, tk), 1)
            s = jnp.where(kpos <= qpos, s, -jnp.inf)
        m_old = m_ref[...]
        m_new = jnp.maximum(m_old, jnp.max(s, axis=-1, keepdims=True))
        alpha = jnp.exp(m_old - m_new)
        p = jnp.exp(s - m_new)
        l_ref[...] = alpha * l_ref[...] + jnp.sum(p, axis=-1, keepdims=True)
        acc_ref[...] = alpha * acc_ref[...] + _dot(p.astype(BF16), v_ref[...])
        m_ref[...] = m_new

    last_full = (i * tq) // tk

    @pl.when(j < last_full)
    def _():
        update(False)

    @pl.when((j >= last_full) & (j * tk <= i * tq + tq - 1))
    def _():
        update(True)

    @pl.when(j == pl.num_programs(3) - 1)
    def _():
        o_ref[...] = (acc_ref[...] / l_ref[...]).astype(BF16)


def _attention(q, k, v):
    B, H, S, _ = q.shape
    tq = min(S, ATTN_Q)
    tk = min(S, ATTN_K)

    def kv_map(b, h, i, j):
        return (b, h, jnp.minimum(j, (i * tq + tq - 1) // tk), 0)

    vmem = 4 * tq * tk * 4 + 4 * (tq + tk) * 256 * 2 + 4 * tq * V7X_LANES * 4
    return pl.pallas_call(
        functools.partial(_attn_kernel, tq=tq, tk=tk),
        grid=(B, H, S // tq, S // tk),
        in_specs=[
            pl.BlockSpec((None, None, tq, MLA_QK), lambda b, h, i, j: (b, h, i, 0)),
            pl.BlockSpec((None, None, tk, MLA_QK), kv_map),
            pl.BlockSpec((None, None, tk, V_HEAD), kv_map),
        ],
        out_specs=pl.BlockSpec((None, tq, V_HEAD), lambda b, h, i, j: (b, i, h)),
        out_shape=jax.ShapeDtypeStruct((B, S, H * V_HEAD), BF16),
        scratch_shapes=[pltpu.VMEM((tq, 1), F32), pltpu.VMEM((tq, 1), F32), pltpu.VMEM((tq, V_HEAD), F32)],
        compiler_params=_params(vmem, 4),
        name="mla_attention",
    )(q, k, v)


def _merge_kernel(x_ref, mod_ref, oa_ref, ob_ref, gates_ref, wa_ref, wb_ref, wo_ref, o_ref):
    D = x_ref.shape[-1]
    mod = mod_ref[...]
    gates = gates_ref[...].astype(F32)
    ya = _dot(oa_ref[...], wa_ref[...])
    yb = _dot(ob_ref[...], wb_ref[...])
    y = (gates[:, :D] * ya + gates[:, D:] * yb).astype(BF16)
    o_ref[...] = x_ref[...] + (1.0 + mod[5:6]) * _dot(y, wo_ref[...])


def _merge(x, mod, oa, ob, gates, wa, wb, wo, l):
    B, S, D = x.shape
    tm = min(S, MERGE_TOKENS)
    na = oa.shape[-1]
    nbv = ob.shape[-1]
    vmem = (na + nbv + D) * D * 2 + 4 * tm * D * 4 + 2 * tm * (na + nbv + 2 * D) * 2 + 4 * tm * D * 4
    return pl.pallas_call(
        _merge_kernel,
        grid=(B, S // tm),
        in_specs=[
            pl.BlockSpec((None, tm, D), lambda b, s: (b, s, 0)),
            pl.BlockSpec((None, None, N_MOD, D), lambda b, s: (l, b, 0, 0)),
            pl.BlockSpec((None, tm, na), lambda b, s: (b, s, 0)),
            pl.BlockSpec((None, tm, nbv), lambda b, s: (b, s, 0)),
            pl.BlockSpec((None, tm, 2 * D), lambda b, s: (b, s, 0)),
            _resident((None, na, D), lambda b, s: (l, 0, 0)),
            _resident((None, nbv, D), lambda b, s: (l, 0, 0)),
            _resident((None, D, D), lambda b, s: (l, 0, 0)),
        ],
        out_specs=pl.BlockSpec((None, tm, D), lambda b, s: (b, s, 0)),
        out_shape=jax.ShapeDtypeStruct((B, S, D), F32),
        compiler_params=_params(vmem, 2),
        name="merge",
    )(x, mod, oa, ob, gates, wa, wb, wo)


def _pack_w_in(w_in):
    conv_c = 2 * GDN_HEADS * GDN_DK + GDN_HEADS * GDN_DV
    gdn_v = GDN_HEADS * GDN_DV
    D = w_in.shape[1]
    sizes = (conv_c, gdn_v, GDN_HEADS, GDN_HEADS, Q_LORA, KV_LORA, QK_ROPE, D, D)
    offs = np.cumsum((0,) + sizes)
    qkv, z, b_raw, a_raw, q_lat, kv_lat, k_pe, gate_a, gate_b = (
        w_in[:, :, int(offs[i]):int(offs[i + 1])] for i in range(len(sizes)))
    n_lat = Q_LORA + KV_LORA + QK_ROPE
    pad = (-n_lat) % V7X_LANES
    lat = jnp.concatenate([q_lat, kv_lat, k_pe, jnp.zeros(w_in.shape[:2] + (pad,), w_in.dtype)], axis=-1)
    ba = jnp.swapaxes(jnp.concatenate([b_raw, a_raw], axis=-1), 1, 2)
    return (qkv.astype(BF16), z.astype(BF16), jnp.concatenate([gate_a, gate_b], axis=-1).astype(BF16),
            lat.astype(BF16), ba.astype(BF16))


def _pack_q_up(w_q_up):
    L = w_q_up.shape[0]
    w = w_q_up.reshape(L, Q_LORA, MLA_HEADS, MLA_QK)
    nope = w[..., :QK_NOPE].reshape(L, Q_LORA, MLA_HEADS * QK_NOPE)
    rope = w[..., QK_NOPE:].reshape(L, Q_LORA, MLA_HEADS * QK_ROPE)
    return jnp.concatenate([nope, rope], axis=-1).astype(BF16)


def kernel(x, c, positions, ada_w, ada_b, norm_ffn1, ffn1_w1, ffn1_w3, ffn1_w2, norm_mix, w_in, gdn_conv, gdn_a_log, gdn_dt_bias, gdn_out_gain, mla_q_lat_gain, mla_kv_lat_gain, mla_w_q_up, mla_w_kv_up, mla_q_norm, mla_k_norm, w_branch_a, w_branch_b, w_out, norm_ffn2, ffn2_w1, ffn2_w3, ffn2_w2):
    L = ada_w.shape[0]
    S = x.shape[1]
    assert S % CHUNK == 0 and S % min(S, GDN_TOKENS) == 0 and S % min(S, FFN_TOKENS) == 0
    assert S % min(S, MERGE_TOKENS) == 0 and S % min(S, ATTN_Q) == 0

    mod = _ada_mod(c, ada_w, ada_b)
    cos2, sin2 = _rope_tables(positions)

    row = lambda a: a.reshape(L, 1, a.shape[-1])
    col = lambda a: a.reshape(L, a.shape[-1], 1)
    bf = lambda a: a.astype(BF16)
    wqkv, wz, wg, wlat, wba = _pack_w_in(w_in)
    wq = _pack_q_up(mla_w_q_up)
    f1 = (bf(ffn1_w1), bf(ffn1_w3), bf(ffn1_w2))
    f2 = (bf(ffn2_w1), bf(ffn2_w3), bf(ffn2_w2))
    wkv, wa, wb, wo = bf(mla_w_kv_up), bf(w_branch_a), bf(w_branch_b), bf(w_out)
    g_ffn1, g_mix, g_ffn2 = row(norm_ffn1), row(norm_mix), row(norm_ffn2)
    og, qlg, kvlg = row(gdn_out_gain), row(mla_q_lat_gain), row(mla_kv_lat_gain)
    qn, kn = row(mla_q_norm), row(mla_k_norm)
    a_log, dt_bias = col(gdn_a_log), col(gdn_dt_bias)

    for l in range(L):
        x = _ffn(x, mod, g_ffn1, *f1, l, 0)
        qkv, zs, gates, lat, ba = _mixin(x, mod, g_mix, wqkv, wz, wg, wlat, wba, gdn_conv, l)
        grow, gcol = _gdn_gates(ba, a_log, dt_bias, l)
        o_gdn = _gdn(qkv, zs, grow, gcol, og, l)
        q, k, v = _mla_prep(lat, cos2, sin2, qlg, kvlg, wq, wkv, qn, kn, l)
        o_mla = _attention(q, k, v)
        x = _merge(x, mod, o_gdn, o_mla, gates, wa, wb, wo, l)
        x = _ffn(x, mod, g_ffn2, *f2, l, 6)
    return x
```

```python
import functools
import math

import numpy as np
import jax
import jax.numpy as jnp
from jax import lax
from jax.experimental import pallas as pl
from jax.experimental.pallas import tpu as pltpu

F32 = jnp.float32
BF16 = jnp.bfloat16

EPS = 1e-6
N_MOD = 9
GDN_HEADS = 8
GDN_DK = 128
GDN_DV = 128
CONV_K = 4
CHUNK = 64
LOG2_CHUNK = 6
MLA_HEADS = 8
Q_LORA = 384
KV_LORA = 256
QK_NOPE = 128
QK_ROPE = 64
V_HEAD = 128
ROPE_THETA = 10000.0
MLA_QK = QK_NOPE + QK_ROPE

V7X_VMEM_BYTES = 64 * 1024 * 1024
V7X_LANES = 128
V7X_SUBLANES = 8
VMEM_CAP_BYTES = V7X_VMEM_BYTES - 6 * 1024 * 1024

FFN_TOKENS = 512
MIX_TOKENS = 256
GATE_TOKENS = 512
GDN_TOKENS = 256
GDN_GROUP = 4
MLA_TOKENS = 512
ATTN_Q = 512
MERGE_TOKENS = 1024
HALO = V7X_SUBLANES


def _params(vmem_bytes, n_axes):
    limit = int(min(VMEM_CAP_BYTES, max(vmem_bytes, 16 * 1024 * 1024)))
    return pltpu.CompilerParams(dimension_semantics=("arbitrary",) * n_axes, vmem_limit_bytes=limit)


def _resident(block_shape, index_map):
    return pl.BlockSpec(block_shape, index_map, pipeline_mode=pl.Buffered(1))


def _dot(a, b):
    return jnp.dot(a, b, preferred_element_type=F32)


def _dot_nt(a, b):
    return lax.dot_general(a, b, (((1,), (1,)), ((), ())), preferred_element_type=F32)


def _sigmoid(x):
    return 1.0 / (1.0 + jnp.exp(-x))


def _silu(x):
    return x * _sigmoid(x)


def _modulate(x, gain, shift, scale):
    y = x * lax.rsqrt(jnp.mean(x * x, axis=-1, keepdims=True) + EPS)
    return (y * gain) * (1.0 + scale) + shift


def _split3(x):
    x1 = x.astype(BF16)
    r1 = x - x1.astype(F32)
    x2 = r1.astype(BF16)
    x3 = (r1 - x2.astype(F32)).astype(BF16)
    return x1, x2, x3


def _mod_kernel(c_ref, w_ref, b_ref, o_ref):
    cond = _silu(c_ref[...]).astype(BF16)
    o_ref[...] = _dot(cond, w_ref[...].astype(BF16)) + b_ref[...]


def _ada_mod(c, ada_w, ada_b):
    L, D, _ = ada_w.shape
    B = c.shape[0]
    ada_b3 = ada_b.reshape(L, N_MOD, 1, D)
    out = pl.pallas_call(
        _mod_kernel,
        grid=(L, N_MOD),
        in_specs=[
            pl.BlockSpec((B, D), lambda l, j: (0, 0)),
            pl.BlockSpec((None, D, D), lambda l, j: (l, 0, j)),
            pl.BlockSpec((None, None, 1, D), lambda l, j: (l, j, 0, 0)),
        ],
        out_specs=pl.BlockSpec((None, None, B, D), lambda l, j: (l, j, 0, 0)),
        out_shape=jax.ShapeDtypeStruct((L, N_MOD, B, D), F32),
        compiler_params=_params(4 * D * D * 4, 2),
        name="ada_mod",
    )(c, ada_w, ada_b3)
    return jnp.transpose(out, (0, 2, 1, 3))


def _rope_kernel(pos_ref, freq_ref, sign_ref, cos_ref, sin_ref):
    ang = pos_ref[...].astype(F32) * freq_ref[...]
    cos_ref[...] = jnp.cos(ang)
    sin_ref[...] = jnp.sin(ang) * sign_ref[...]


def _rope_tables(positions):
    B, S = positions.shape
    half = QK_ROPE // 2
    inv_freq = ROPE_THETA ** (-jnp.arange(half, dtype=F32) / half)
    freq2 = jnp.concatenate([inv_freq, inv_freq]).reshape(1, QK_ROPE)
    sign2 = jnp.concatenate([-jnp.ones((half,), F32), jnp.ones((half,), F32)]).reshape(1, QK_ROPE)
    ts = min(S, 1024)
    return pl.pallas_call(
        _rope_kernel,
        grid=(B, S // ts),
        in_specs=[
            pl.BlockSpec((None, ts, 1), lambda b, s: (b, s, 0)),
            pl.BlockSpec((1, QK_ROPE), lambda b, s: (0, 0)),
            pl.BlockSpec((1, QK_ROPE), lambda b, s: (0, 0)),
        ],
        out_specs=[pl.BlockSpec((None, ts, QK_ROPE), lambda b, s: (b, s, 0))] * 2,
        out_shape=[jax.ShapeDtypeStruct((B, S, QK_ROPE), F32)] * 2,
        compiler_params=_params(8 * ts * V7X_LANES * 4, 2),
        name="rope_tables",
    )(positions.reshape(B, S, 1), freq2, sign2)


def _ffn_kernel(x_ref, mod_ref, gain_ref, w1_ref, w3_ref, w2_ref, o_ref, *, row):
    x = x_ref[...]
    mod = mod_ref[...]
    h = _modulate(x, gain_ref[...], mod[row:row + 1], mod[row + 1:row + 2]).astype(BF16)
    a = _dot(h, w1_ref[...])
    b = _dot(h, w3_ref[...])
    g = (_silu(a) * b).astype(BF16)
    y = _dot(g, w2_ref[...])
    o_ref[...] = x + (0.5 * (1.0 + mod[row + 2:row + 3])) * y


def _ffn(x, mod, gain, w1, w3, w2, l, row):
    B, S, D = x.shape
    F = w1.shape[-1]
    tm = min(S, FFN_TOKENS)
    vmem = 3 * D * F * 2 + 4 * tm * D * 4 + 3 * tm * F * 4 + 2 * tm * D * 4
    return pl.pallas_call(
        functools.partial(_ffn_kernel, row=row),
        grid=(B, S // tm),
        in_specs=[
            pl.BlockSpec((None, tm, D), lambda b, s: (b, s, 0)),
            pl.BlockSpec((None, None, N_MOD, D), lambda b, s: (l, b, 0, 0)),
            pl.BlockSpec((None, 1, D), lambda b, s: (l, 0, 0)),
            _resident((None, D, F), lambda b, s: (l, 0, 0)),
            _resident((None, D, F), lambda b, s: (l, 0, 0)),
            _resident((None, F, D), lambda b, s: (l, 0, 0)),
        ],
        out_specs=pl.BlockSpec((None, tm, D), lambda b, s: (b, s, 0)),
        out_shape=jax.ShapeDtypeStruct((B, S, D), F32),
        compiler_params=_params(vmem, 2),
        name="ffn",
    )(x, mod, gain, w1, w3, w2)


def _mixin_kernel(x_ref, mod_ref, gain_ref, wqkv_ref, wz_ref, wg_ref, wlat_ref, wba_ref, conv_ref,
                  qkv_ref, zs_ref, gates_ref, lat_ref, ba_ref, pbuf_ref, *, tm):
    s = pl.program_id(1)
    x = x_ref[...]
    mod = mod_ref[...]
    h = _modulate(x, gain_ref[...], mod[3:4], mod[4:5]).astype(BF16)

    @pl.when(s == 0)
    def _():
        pbuf_ref[0:HALO, :] = jnp.zeros((HALO, pbuf_ref.shape[1]), F32)

    pbuf_ref[HALO:HALO + tm, :] = _dot(h, wqkv_ref[...])
    conv = conv_ref[...]
    n_qk = 2 * GDN_HEADS
    for cb in range(pbuf_ref.shape[1] // V7X_LANES):
        c0 = cb * V7X_LANES
        acc = None
        for j in range(CONV_K):
            r0 = HALO - (CONV_K - 1) + j
            term = conv[j:j + 1, c0:c0 + V7X_LANES] * pbuf_ref[r0:r0 + tm, c0:c0 + V7X_LANES]
            acc = term if acc is None else acc + term
        y = _silu(acc)
        if cb < n_qk:
            y = y * lax.rsqrt(jnp.sum(y * y, axis=-1, keepdims=True) + EPS)
        qkv_ref[cb] = y.astype(BF16)
    pbuf_ref[0:HALO, :] = pbuf_ref[tm:tm + HALO, :]

    z = _silu(_dot(h, wz_ref[...]))
    for hh in range(GDN_HEADS):
        zs_ref[hh] = z[:, hh * GDN_DV:(hh + 1) * GDN_DV].astype(BF16)
    gates_ref[...] = _sigmoid(_dot(h, wg_ref[...])).astype(BF16)
    lat_ref[...] = _dot(h, wlat_ref[...])
    ba_ref[...] = _dot_nt(wba_ref[...], h)


def _mixin(x, mod, gain, wqkv, wz, wg, wlat, wba, conv, l):
    B, S, D = x.shape
    C = wqkv.shape[-1]
    nz = wz.shape[-1]
    ng = wg.shape[-1]
    nl = wlat.shape[-1]
    nb = wba.shape[1]
    tm = min(S, MIX_TOKENS)
    ncb = C // V7X_LANES
    vmem = ((C + nz + ng + nl) * D * 2 + 2 * tm * D * 4 + 2 * tm * (C + nz + ng) * 2 + 2 * tm * nl * 4
            + (tm + HALO) * C * 4 + tm * C * 4 + tm * (nz + ng) * 4 + 4 * 1024 * 1024)
    return pl.pallas_call(
        functools.partial(_mixin_kernel, tm=tm),
        grid=(B, S // tm),
        in_specs=[
            pl.BlockSpec((None, tm, D), lambda b, s: (b, s, 0)),
            pl.BlockSpec((None, None, N_MOD, D), lambda b, s: (l, b, 0, 0)),
            pl.BlockSpec((None, 1, D), lambda b, s: (l, 0, 0)),
            _resident((None, D, C), lambda b, s: (l, 0, 0)),
            _resident((None, D, nz), lambda b, s: (l, 0, 0)),
            _resident((None, D, ng), lambda b, s: (l, 0, 0)),
            _resident((None, D, nl), lambda b, s: (l, 0, 0)),
            _resident((None, nb, D), lambda b, s: (l, 0, 0)),
            pl.BlockSpec((None, CONV_K, C), lambda b, s: (l, 0, 0)),
        ],
        out_specs=[
            pl.BlockSpec((None, ncb, tm, V7X_LANES), lambda b, s: (b, 0, s, 0)),
            pl.BlockSpec((None, GDN_HEADS, tm, GDN_DV), lambda b, s: (b, 0, s, 0)),
            pl.BlockSpec((None, tm, ng), lambda b, s: (b, s, 0)),
            pl.BlockSpec((None, tm, nl), lambda b, s: (b, s, 0)),
            pl.BlockSpec((None, nb, tm), lambda b, s: (b, 0, s)),
        ],
        out_shape=[
            jax.ShapeDtypeStruct((B, ncb, S, V7X_LANES), BF16),
            jax.ShapeDtypeStruct((B, GDN_HEADS, S, GDN_DV), BF16),
            jax.ShapeDtypeStruct((B, S, ng), BF16),
            jax.ShapeDtypeStruct((B, S, nl), F32),
            jax.ShapeDtypeStruct((B, nb, S), F32),
        ],
        scratch_shapes=[pltpu.VMEM((tm + HALO, C), F32)],
        compiler_params=_params(vmem, 2),
        name="mixin",
    )(x, mod, gain, wqkv, wz, wg, wlat, wba, conv)


def _gate_kernel(ba_ref, alog_ref, dtb_ref, grow_ref, gcol_ref, *, tm):
    H = GDN_HEADS
    ba = ba_ref[...]
    beta = _sigmoid(ba[0:H])
    xa = ba[H:2 * H] + dtb_ref[...]
    softplus = jnp.maximum(xa, 0.0) + jnp.log(1.0 + jnp.exp(-jnp.abs(xa)))
    g = -jnp.exp(alog_ref[...]) * softplus

    ii = lax.broadcasted_iota(jnp.int32, (tm, tm), 0)
    jj = lax.broadcasted_iota(jnp.int32, (tm, tm), 1)
    same = (ii >> LOG2_CHUNK) == (jj >> LOG2_CHUNK)
    upper =jnp.where(same & (ii <= jj), 1.0, 0.0).astype(BF16)
    block = jnp.where(same, 1.0, 0.0).astype(BF16)
    eye = jnp.where(ii == jj, 1.0, 0.0).astype(BF16)
    gam = None
    glast = None
    for part in _split3(g):
        t1 = _dot(part, upper)
        t2 = _dot(part, block)
        gam = t1 if gam is None else gam + t1
        glast = t2 if glast is None else glast + t2
    egam = jnp.exp(gam)
    ekd = jnp.exp(glast - gam)
    egl = jnp.exp(glast)
    zero = jnp.zeros_like(gam[0:1])
    rows = []
    for hh in range(H):
        r = jnp.concatenate([beta[hh:hh + 1], gam[hh:hh + 1], egam[hh:hh + 1], ekd[hh:hh + 1],
                             egl[hh:hh + 1], zero, zero, zero], axis=0)
        grow_ref[hh] = r
        rows.append(r)
    allrows = jnp.concatenate(rows, axis=0)
    cols = None
    for part in _split3(allrows):
        t = _dot_nt(eye, part)
        cols = t if cols is None else cols + t
    for hh in range(H):
        gcol_ref[hh] = cols[:, 8 * hh:8 * hh + 8]


def _gdn_gates(ba, a_log, dt_bias, l):
    B, nb, S = ba.shape
    H = GDN_HEADS
    tm = min(S, GATE_TOKENS)
    vmem = 8 * tm * tm * 4 + 4 * H * tm * V7X_LANES * 4
    return pl.pallas_call(
        functools.partial(_gate_kernel, tm=tm),
        grid=(B, S // tm),
        in_specs=[
            pl.BlockSpec((None, nb, tm), lambda b, s: (b, 0, s)),
            pl.BlockSpec((None, H, 1), lambda b, s: (l, 0, 0)),
            pl.BlockSpec((None, H, 1), lambda b, s: (l, 0, 0)),
        ],
        out_specs=[
            pl.BlockSpec((None, H, 8, tm), lambda b, s: (b, 0, 0, s)),
            pl.BlockSpec((None, H, tm, 8), lambda b, s: (b, 0, s, 0)),
        ],
        out_shape=[
            jax.ShapeDtypeStruct((B, H, 8, S), F32),
            jax.ShapeDtypeStruct((B, H, S, 8), F32),
        ],
        compiler_params=_params(vmem, 2),
        name="gdn_gates",
    )(ba, a_log, dt_bias)


def _gdn_kernel(q_ref, k_ref, v_ref, zs_ref, grow_ref, gcol_ref, og_ref, o_ref, state_ref, vn_ref, *, tc, G):
    s = pl.program_id(2)

    @pl.when(s == 0)
    def _():
        state_ref[...] = jnp.zeros(state_ref.shape, F32)
        vn_ref[...] = jnp.zeros(vn_ref.shape, BF16)

    scale = GDN_DK ** -0.5
    ii = lax.broadcasted_iota(jnp.int32, (tc, tc), 0)
    jj = lax.broadcasted_iota(jnp.int32, (tc, tc), 1)
    same = (ii >> LOG2_CHUNK) == (jj >> LOG2_CHUNK)
    causal = same & (ii >= jj)
    strict = same & (ii > jj)
    eye = jnp.where(ii == jj, 1.0, 0.0)
    lane = lax.broadcasted_iota(jnp.int32, (GDN_DK, tc), 1) >> LOG2_CHUNK

    def prepare(g):
        q = q_ref[g]
        k = k_ref[g]
        kf = k.astype(F32)
        vf = v_ref[g].astype(F32)
        gc = gcol_ref[g]
        gr = grow_ref[g]
        beta = gc[:, 0:1]
        egam = gc[:, 2:3]
        decay = jnp.where(causal, jnp.exp(jnp.minimum(gc[:, 1:2] - gr[1:2, :], 0.0)), 0.0)
        kb = kf * beta
        m = jnp.where(strict, _dot_nt(kb.astype(BF16), k) * decay, 0.0)
        t_inv = eye - m
        p = m
        for _ in range(LOG2_CHUNK - 1):
            pb = p.astype(BF16)
            p = _dot(pb, pb)
            t_inv = t_inv + _dot(t_inv.astype(BF16), p.astype(BF16))
        rhs = jnp.concatenate([vf * beta, kb * egam], axis=1).astype(BF16)
        uw = _dot(t_inv.astype(BF16), rhs)
        qk = (_dot_nt(q, k) * (decay * scale)).astype(BF16)
        qd = (q.astype(F32) * (egam * scale)).astype(BF16)
        kdt = (kf * gc[:, 3:4]).T
        return uw[:, :GDN_DV], uw[:, GDN_DV:].astype(BF16), qk, qd, kdt, gr

    prepared = [prepare(g) for g in range(G)]
    states = [state_ref[g] for g in range(G)]
    outs = [[] for _ in range(G)]
    for c in range(tc // CHUNK):
        r0 = c * CHUNK
        for g in range(G):
            u, w, qk, qd, kdt, gr = prepared[g]
            ws = _dot(jnp.concatenate([w[r0:r0 + CHUNK], qd[r0:r0 + CHUNK]], axis=0), states[g].astype(BF16))
            v_new = u[r0:r0 + CHUNK] - ws[:CHUNK]
            vn_ref[g, r0:r0 + CHUNK, :] = v_new.astype(BF16)
            lhs = jnp.concatenate([qk[r0:r0 + CHUNK], jnp.where(lane == c, kdt, 0.0).astype(BF16)], axis=0)
            r2 = _dot(lhs, vn_ref[g])
            outs[g].append(ws[CHUNK:] + r2[:CHUNK])
            states[g] = states[g] * gr[4:5, r0:r0 + 1] + r2[CHUNK:]
    for g in range(G):
        state_ref[g] = states[g]
        o = jnp.concatenate(outs[g], axis=0)
        o = o * lax.rsqrt(jnp.mean(o * o, axis=-1, keepdims=True) + EPS) * og_ref[...]
        o_ref[:, g * GDN_DV:(g + 1) * GDN_DV] = (o * zs_ref[g].astype(F32)).astype(BF16)


def _gdn(qkv, zs, grow, gcol, out_gain, l):
    B, _, S, _ = qkv.shape
    H = GDN_HEADS
    G = GDN_GROUP
    tc = min(S, GDN_TOKENS)
    vmem = G * (24 * tc * tc * 4 + 24 * tc * V7X_LANES * 4)
    return pl.pallas_call(
        functools.partial(_gdn_kernel, tc=tc, G=G),
        grid=(B, H // G, S // tc),
        in_specs=[
            pl.BlockSpec((None, G, tc, GDN_DK), lambda b, h, s: (b, h, s, 0)),
            pl.BlockSpec((None, G, tc, GDN_DK), lambda b, h, s: (b, H // G + h, s, 0)),
            pl.BlockSpec((None, G, tc, GDN_DV), lambda b, h, s: (b, 2 * H // G + h, s, 0)),
            pl.BlockSpec((None, G, tc, GDN_DV), lambda b, h, s: (b, h, s, 0)),
            pl.BlockSpec((None, G, 8, tc), lambda b, h, s: (b, h, 0, s)),
            pl.BlockSpec((None, G, tc, 8), lambda b, h, s: (b, h, s, 0)),
            pl.BlockSpec((None, 1, GDN_DV), lambda b, h, s: (l, 0, 0)),
        ],
        out_specs=pl.BlockSpec((None, tc, G * GDN_DV), lambda b, h, s: (b, s, h)),
        out_shape=jax.ShapeDtypeStruct((B, S, H * GDN_DV), BF16),
        scratch_shapes=[pltpu.VMEM((G, GDN_DK, GDN_DV), F32), pltpu.VMEM((G, tc, GDN_DV), BF16)],
        compiler_params=_params(vmem, 3),
        name="gdn",
    )(qkv, qkv, qkv, zs, grow, gcol, out_gain)


def _swap_halves(y):
    half = y.shape[-1] // 2
    return jnp.concatenate([y[:, half:], y[:, :half]], axis=1)


def _mla_prep_kernel(lat_ref, cos_ref, sin_ref, qg_ref, kvg_ref, wq_ref, wkv_ref, qn_ref, kn_ref,
                     q_ref, k_ref, v_ref):
    H = MLA_HEADS
    lat = lat_ref[...]
    cos2 = cos_ref[...]
    sin2 = sin_ref[...]
    ql = lat[:, :Q_LORA]
    kvl = lat[:, Q_LORA:Q_LORA + KV_LORA]
    kpe = lat[:, Q_LORA + KV_LORA:Q_LORA + KV_LORA + QK_ROPE]
    qln = ql * lax.rsqrt(jnp.mean(ql * ql, axis=-1, keepdims=True) + EPS) * qg_ref[...]
    kvn = kvl * lax.rsqrt(jnp.mean(kvl * kvl, axis=-1, keepdims=True) + EPS) * kvg_ref[...]
    qr = _dot(qln.astype(BF16), wq_ref[...])
    kvr = _dot(kvn.astype(BF16), wkv_ref[...])
    qn = qn_ref[...]
    kn = kn_ref[...]
    scale = MLA_QK ** -0.5 * math.log2(math.e)
    kpe_ss = jnp.sum(kpe * kpe, axis=-1, keepdims=True)
    for hh in range(H):
        q_nope = qr[:, hh * QK_NOPE:(hh + 1) * QK_NOPE]
        q_rope = qr[:, H * QK_NOPE + hh * QK_ROPE:H * QK_NOPE + (hh + 1) * QK_ROPE]
        ss = jnp.sum(q_nope * q_nope, axis=-1, keepdims=True) + jnp.sum(q_rope * q_rope, axis=-1, keepdims=True)
        r = lax.rsqrt(ss * (1.0 / MLA_QK) + EPS) * scale
        y = q_rope * r * qn[:, QK_NOPE:]
        q_ref[hh, :, 0:QK_NOPE] = (q_nope * r * qn[:, :QK_NOPE]).astype(BF16)
        q_ref[hh, :, QK_NOPE:MLA_QK] = (y * cos2 + _swap_halves(y) * sin2).astype(BF16)

        c0 = hh * (QK_NOPE + V_HEAD)
        k_nope = kvr[:, c0:c0 + QK_NOPE]
        ss = jnp.sum(k_nope * k_nope, axis=-1, keepdims=True) + kpe_ss
        r = lax.rsqrt(ss * (1.0 / MLA_QK) + EPS)
        y = kpe * r * kn[:, QK_NOPE:]
        k_ref[hh, :, 0:QK_NOPE] = (k_nope * r * kn[:, :QK_NOPE]).astype(BF16)
        k_ref[hh, :, QK_NOPE:MLA_QK] = (y * cos2 + _swap_halves(y) * sin2).astype(BF16)
        v_ref[hh] = kvr[:, c0 + QK_NOPE:c0 + QK_NOPE + V_HEAD].astype(BF16)


def _mla_prep(lat, cos2, sin2, q_lat_gain, kv_lat_gain, wq, wkv, q_norm, k_norm, l):
    B, S, nl = lat.shape
    H = MLA_HEADS
    tm = min(S, MLA_TOKENS)
    nq = wq.shape[-1]
    nkv = wkv.shape[-1]
    vmem = (Q_LORA * nq + KV_LORA * nkv) * 2 + 2 * tm * nl * 4 + 2 * tm * (nq + nkv) * 4 + 6 * H * tm * 256 * 2
    return pl.pallas_call(
        _mla_prep_kernel,
        grid=(B, S // tm),
        in_specs=[
            pl.BlockSpec((None, tm, nl), lambda b, s: (b, s, 0)),
            pl.BlockSpec((None, tm, QK_ROPE), lambda b, s: (b, s, 0)),
            pl.BlockSpec((None, tm, QK_ROPE), lambda b, s: (b, s, 0)),
            pl.BlockSpec((None, 1, Q_LORA), lambda b, s: (l, 0, 0)),
            pl.BlockSpec((None, 1, KV_LORA), lambda b, s: (l, 0, 0)),
            _resident((None, Q_LORA, nq), lambda b, s: (l, 0, 0)),
            _resident((None, KV_LORA, nkv), lambda b, s: (l, 0, 0)),
            pl.BlockSpec((None, 1, MLA_QK), lambda b, s: (l, 0, 0)),
            pl.BlockSpec((None, 1, MLA_QK), lambda b, s: (l, 0, 0)),
        ],
        out_specs=[
            pl.BlockSpec((None, H, tm, MLA_QK), lambda b, s: (b, 0, s, 0)),
            pl.BlockSpec((None, H, tm, MLA_QK), lambda b, s: (b, 0, s, 0)),
            pl.BlockSpec((None, H, tm, V_HEAD), lambda b, s: (b, 0, s, 0)),
        ],
        out_shape=[
            jax.ShapeDtypeStruct((B, H, S, MLA_QK), BF16),
            jax.ShapeDtypeStruct((B, H, S, MLA_QK), BF16),
            jax.ShapeDtypeStruct((B, H, S, V_HEAD), BF16),
        ],
        compiler_params=_params(vmem, 2),
        name="mla_prep",
    )(lat, cos2, sin2, q_lat_gain, kv_lat_gain, wq, wkv, q_norm, k_norm)


def _attn_kernel(q_ref, k_ref, v_ref, o_ref, *, tq):
    i = pl.program_id(2)
    q = q_ref[...]

    def scores(j):
        return _dot_nt(q, k_ref[pl.ds(pl.multiple_of(j * tq, tq), tq), :])

    def update(s, j, m, l, acc):
        m_new = jnp.maximum(m, jnp.max(s, axis=-1, keepdims=True))
        alpha = jnp.exp2(m - m_new)
        p = jnp.exp2(s - m_new)
        l = alpha * l + jnp.sum(p, axis=-1, keepdims=True)
        pv = _dot(p.astype(BF16), v_ref[pl.ds(pl.multiple_of(j * tq, tq), tq), :])
        return m_new, l, alpha * acc + pv

    def body(j, carry):
        s, m, l, acc = carry
        s_next = scores(j + 1)
        m, l, acc = update(s, j, m, l, acc)
        return s_next, m, l, acc

    init = (scores(0), jnp.full((tq, 1), -jnp.inf, F32), jnp.zeros((tq, 1), F32),
            jnp.zeros((tq, V_HEAD), F32))
    s, m, l, acc = lax.fori_loop(0, i, body, init)
    row = lax.broadcasted_iota(jnp.int32, (tq, tq), 0)
    col = lax.broadcasted_iota(jnp.int32, (tq, tq), 1)
    m, l, acc = update(jnp.where(col <= row, s, -jnp.inf), i, m, l, acc)
    o_ref[...] = (acc / l).astype(BF16)


def _attention(q, k, v):
    B, H, S, _ = q.shape
    tq = min(S, ATTN_Q)
    vmem = 6 * tq * tq * 4 + 4 * S * 256 * 2 + 4 * tq * 256 * 2 + 8 * tq * V7X_LANES * 4
    return pl.pallas_call(
        functools.partial(_attn_kernel, tq=tq),
        grid=(B, H, S // tq),
        in_specs=[
            pl.BlockSpec((None, None, tq, MLA_QK), lambda b, h, i: (b, h, i, 0)),
            pl.BlockSpec((None, None, S, MLA_QK), lambda b, h, i: (b, h, 0, 0)),
            pl.BlockSpec((None, None, S, V_HEAD), lambda b, h, i: (b, h, 0, 0)),
        ],
        out_specs=pl.BlockSpec((None, tq, V_HEAD), lambda b, h, i: (b, i, h)),
        out_shape=jax.ShapeDtypeStruct((B, S, H * V_HEAD), BF16),
        compiler_params=_params(vmem, 3),
        name="mla_attention",
    )(q, k, v)


def _merge_kernel(x_ref, mod_ref, oa_ref, ob_ref, gates_ref, wa_ref, wb_ref, wo_ref, o_ref):
    D = x_ref.shape[-1]
    mod = mod_ref[...]
    gates = gates_ref[...].astype(F32)
    ya = _dot(oa_ref[...], wa_ref[...])
    yb = _dot(ob_ref[...], wb_ref[...])
    y = (gates[:, :D] * ya + gates[:, D:] * yb).astype(BF16)
    o_ref[...] = x_ref[...] + (1.0 + mod[5:6]) * _dot(y, wo_ref[...])


def _merge(x, mod, oa, ob, gates, wa, wb, wo, l):
    B, S, D = x.shape
    tm = min(S, MERGE_TOKENS)
    na = oa.shape[-1]
    nbv = ob.shape[-1]
    vmem = (na + nbv + D) * D * 2 + 4 * tm * D * 4 + 2 * tm * (na + nbv + 2 * D) * 2 + 4 * tm * D * 4
    return pl.pallas_call(
        _merge_kernel,
        grid=(B, S // tm),
        in_specs=[
            pl.BlockSpec((None, tm, D), lambda b, s: (b, s, 0)),
            pl.BlockSpec((None, None, N_MOD, D), lambda b, s: (l, b, 0, 0)),
            pl.BlockSpec((None, tm, na), lambda b, s: (b, s, 0)),
            pl.BlockSpec((None, tm, nbv), lambda b, s: (b, s, 0)),
            pl.BlockSpec((None, tm, 2 * D), lambda b, s: (b, s, 0)),
            _resident((None, na, D), lambda b, s: (l, 0, 0)),
            _resident((None, nbv, D), lambda b, s: (l, 0, 0)),
            _resident((None, D, D), lambda b, s: (l, 0, 0)),
        ],
        out_specs=pl.BlockSpec((None, tm, D), lambda b, s: (b, s, 0)),
        out_shape=jax.ShapeDtypeStruct((B, S, D), F32),
        compiler_params=_params(vmem, 2),
        name="merge",
    )(x, mod, oa, ob, gates, wa, wb, wo)


def _pack_w_in(w_in):
    conv_c = 2 * GDN_HEADS * GDN_DK + GDN_HEADS * GDN_DV
    gdn_v = GDN_HEADS * GDN_DV
    D = w_in.shape[1]
    sizes = (conv_c, gdn_v, GDN_HEADS, GDN_HEADS, Q_LORA, KV_LORA, QK_ROPE, D, D)
    offs = np.cumsum((0,) + sizes)
    qkv, z, b_raw, a_raw, q_lat, kv_lat, k_pe, gate_a, gate_b = (
        w_in[:, :, int(offs[i]):int(offs[i + 1])] for i in range(len(sizes)))
    n_lat = Q_LORA + KV_LORA + QK_ROPE
    pad = (-n_lat) % V7X_LANES
    lat = jnp.concatenate([q_lat, kv_lat, k_pe, jnp.zeros(w_in.shape[:2] + (pad,), w_in.dtype)], axis=-1)
    ba = jnp.swapaxes(jnp.concatenate([b_raw, a_raw], axis=-1), 1, 2)
    return (qkv.astype(BF16), z.astype(BF16), jnp.concatenate([gate_a, gate_b], axis=-1).astype(BF16),
            lat.astype(BF16), ba.astype(BF16))


def _pack_q_up(w_q_up):
    L = w_q_up.shape[0]
    w = w_q_up.reshape(L, Q_LORA, MLA_HEADS, MLA_QK)
    nope = w[..., :QK_NOPE].reshape(L, Q_LORA, MLA_HEADS * QK_NOPE)
    rope = w[..., QK_NOPE:].reshape(L, Q_LORA, MLA_HEADS * QK_ROPE)
    return jnp.concatenate([nope, rope], axis=-1).astype(BF16)


def kernel(x, c, positions, ada_w, ada_b, norm_ffn1, ffn1_w1, ffn1_w3, ffn1_w2, norm_mix, w_in, gdn_conv, gdn_a_log, gdn_dt_bias, gdn_out_gain, mla_q_lat_gain, mla_kv_lat_gain, mla_w_q_up, mla_w_kv_up, mla_q_norm, mla_k_norm, w_branch_a, w_branch_b, w_out, norm_ffn2, ffn2_w1, ffn2_w3, ffn2_w2):
    L = ada_w.shape[0]
    S = x.shape[1]
    assert S % CHUNK == 0 and S % min(S, GDN_TOKENS) == 0 and S % min(S, FFN_TOKENS) == 0
    assert S % min(S, MERGE_TOKENS) == 0 and S % min(S, ATTN_Q) == 0

    mod = _ada_mod(c, ada_w, ada_b)
    cos2, sin2 = _rope_tables(positions)

    row = lambda a: a.reshape(L, 1, a.shape[-1])
    col = lambda a: a.reshape(L, a.shape[-1], 1)
    bf = lambda a: a.astype(BF16)
    wqkv, wz, wg, wlat, wba = _pack_w_in(w_in)
    wq = _pack_q_up(mla_w_q_up)
    f1 = (bf(ffn1_w1), bf(ffn1_w3), bf(ffn1_w2))
    f2 = (bf(ffn2_w1), bf(ffn2_w3), bf(ffn2_w2))
    wkv, wa, wb, wo = bf(mla_w_kv_up), bf(w_branch_a), bf(w_branch_b), bf(w_out)
    g_ffn1, g_mix, g_ffn2 = row(norm_ffn1), row(norm_mix), row(norm_ffn2)
    og, qlg, kvlg = row(gdn_out_gain), row(mla_q_lat_gain), row(mla_kv_lat_gain)
    qn, kn = row(mla_q_norm), row(mla_k_norm)
    a_log, dt_bias = col(gdn_a_log), col(gdn_dt_bias)

    for l in range(L):
        x = _ffn(x, mod, g_ffn1, *f1, l, 0)
        qkv, zs, gates, lat, ba = _mixin(x, mod, g_mix, wqkv, wz, wg, wlat, wba, gdn_conv, l)
        grow, gcol = _gdn_gates(ba, a_log, dt_bias, l)
        o_gdn = _gdn(qkv, zs, grow, gcol, og, l)
        q, k, v = _mla_prep(lat, cos2, sin2, qlg, kvlg, wq, wkv, qn, kn, l)
        o_mla = _attention(q, k, v)
        x = _merge(x, mod, o_gdn, o_mla, gates, wa, wb, wo, l)
        x = _ffn(x, mod, g_ffn2, *f2, l, 6)
    return x
```

```python
import functools
import math

import numpy as np
import jax
import jax.numpy as jnp
from jax import lax
from jax.experimental import pallas as pl
from jax.experimental.pallas import tpu as pltpu

F32 = jnp.float32
BF16 = jnp.bfloat16

EPS = 1e-6
N_MOD = 9
GDN_HEADS = 8
GDN_DK = 128
GDN_DV = 128
CONV_K = 4
CHUNK = 64
LOG2_CHUNK = 6
MLA_HEADS = 8
Q_LORA = 384
KV_LORA = 256
QK_NOPE = 128
QK_ROPE = 64
V_HEAD = 128
ROPE_THETA = 10000.0
MLA_QK = QK_NOPE + QK_ROPE

V7X_VMEM_BYTES = 64 * 1024 * 1024
V7X_LANES = 128
V7X_SUBLANES = 8
VMEM_CAP_BYTES = V7X_VMEM_BYTES - 6 * 1024 * 1024

FFN_TOKENS = 512
MIX_TOKENS = 256
GATE_TOKENS = 512
GDN_TOKENS = 256
GDN_GROUP = 4
ATTN_Q = 512
MERGE_TOKENS = 1024
HALO = V7X_SUBLANES


def _params(vmem_bytes, n_axes):
    limit = int(min(VMEM_CAP_BYTES, max(vmem_bytes, 16 * 1024 * 1024)))
    return pltpu.CompilerParams(dimension_semantics=("arbitrary",) * n_axes, vmem_limit_bytes=limit)


def _resident(block_shape, index_map):
    return pl.BlockSpec(block_shape, index_map, pipeline_mode=pl.Buffered(1))


def _dot(a, b):
    return jnp.dot(a, b, preferred_element_type=F32)


def _dot_nt(a, b):
    return lax.dot_general(a, b, (((1,), (1,)), ((), ())), preferred_element_type=F32)


def _sigmoid(x):
    return 1.0 / (1.0 + jnp.exp(-x))


def _silu(x):
    return x * _sigmoid(x)


def _modulate(x, gain, shift, scale):
    y = x * lax.rsqrt(jnp.mean(x * x, axis=-1, keepdims=True) + EPS)
    return (y * gain) * (1.0 + scale) + shift


def _split3(x):
    x1 = x.astype(BF16)
    r1 = x - x1.astype(F32)
    x2 = r1.astype(BF16)
    x3 = (r1 - x2.astype(F32)).astype(BF16)
    return x1, x2, x3


def _mod_kernel(c_ref, w_ref, b_ref, o_ref):
    cond = _silu(c_ref[...]).astype(BF16)
    o_ref[...] = _dot(cond, w_ref[...].astype(BF16)) + b_ref[...]


def _ada_mod(c, ada_w, ada_b):
    L, D, _ = ada_w.shape
    B = c.shape[0]
    ada_b3 = ada_b.reshape(L, N_MOD, 1, D)
    out = pl.pallas_call(
        _mod_kernel,
        grid=(L, N_MOD),
        in_specs=[
            pl.BlockSpec((B, D), lambda l, j: (0, 0)),
            pl.BlockSpec((None, D, D), lambda l, j: (l, 0, j)),
            pl.BlockSpec((None, None, 1, D), lambda l, j: (l, j, 0, 0)),
        ],
        out_specs=pl.BlockSpec((None, None, B, D), lambda l, j: (l, j, 0, 0)),
        out_shape=jax.ShapeDtypeStruct((L, N_MOD, B, D), F32),
        compiler_params=_params(4 * D * D * 4, 2),
        name="ada_mod",
    )(c, ada_w, ada_b3)
    return jnp.transpose(out, (0, 2, 1, 3))


def _rope_kernel(pos_ref, post_ref, freq_ref, sign_ref, freqc_ref, cos_ref, sin_ref, cost_ref, sint_ref):
    ang = pos_ref[...].astype(F32) * freq_ref[...]
    cos_ref[...] = jnp.cos(ang)
    sin_ref[...] = jnp.sin(ang) * sign_ref[...]
    angt = freqc_ref[...] * post_ref[...].astype(F32)
    cost_ref[...] = jnp.cos(angt)
    sint_ref[...] = jnp.sin(angt)


def _rope_tables(positions):
    B, S = positions.shape
    half = QK_ROPE // 2
    inv_freq = ROPE_THETA ** (-jnp.arange(half, dtype=F32) / half)
    freq2 = jnp.concatenate([inv_freq, inv_freq]).reshape(1, QK_ROPE)
    sign2 = jnp.concatenate([-jnp.ones((half,), F32), jnp.ones((half,), F32)]).reshape(1, QK_ROPE)
    ts = min(S, 1024)
    return pl.pallas_call(
        _rope_kernel,
        grid=(B, S // ts),
        in_specs=[
            pl.BlockSpec((None, ts, 1), lambda b, s: (b, s, 0)),
            pl.BlockSpec((None, 1, ts), lambda b, s: (b, 0, s)),
            pl.BlockSpec((1, QK_ROPE), lambda b, s: (0, 0)),
            pl.BlockSpec((1, QK_ROPE), lambda b, s: (0, 0)),
            pl.BlockSpec((half, 1), lambda b, s: (0, 0)),
        ],
        out_specs=[pl.BlockSpec((None, ts, QK_ROPE), lambda b, s: (b, s, 0))] * 2
        + [pl.BlockSpec((None, half, ts), lambda b, s: (b, 0, s))] * 2,
        out_shape=[jax.ShapeDtypeStruct((B, S, QK_ROPE), F32)] * 2
        + [jax.ShapeDtypeStruct((B, half, S), F32)] * 2,
        compiler_params=_params(8 * ts * V7X_LANES * 4, 2),
        name="rope_tables",
    )(positions.reshape(B, S, 1), positions.reshape(B, 1, S), freq2, sign2, inv_freq.reshape(half, 1))


def _ffn_kernel(x_ref, mod_ref, gain_ref, w1_ref, w3_ref, w2_ref, o_ref, *, row):
    x = x_ref[...]
    mod = mod_ref[...]
    h = _modulate(x, gain_ref[...], mod[row:row + 1], mod[row + 1:row + 2]).astype(BF16)
    a = _dot(h, w1_ref[...])
    b = _dot(h, w3_ref[...])
    g = (_silu(a) * b).astype(BF16)
    y = _dot(g, w2_ref[...])
    o_ref[...] = x + (0.5 * (1.0 + mod[row + 2:row + 3])) * y


def _ffn(x, mod, gain, w1, w3, w2, l, row):
    B, S, D = x.shape
    F = w1.shape[-1]
    tm = min(S, FFN_TOKENS)
    vmem = 3 * D * F * 2 + 4 * tm * D * 4 + 3 * tm * F * 4 + 2 * tm * D * 4
    return pl.pallas_call(
        functools.partial(_ffn_kernel, row=row),
        grid=(B, S // tm),
        in_specs=[
            pl.BlockSpec((None, tm, D), lambda b, s: (b, s, 0)),
            pl.BlockSpec((None, None, N_MOD, D), lambda b, s: (l, b, 0, 0)),
            pl.BlockSpec((None, 1, D), lambda b, s: (l, 0, 0)),
            _resident((None, D, F), lambda b, s: (l, 0, 0)),
            _resident((None, D, F), lambda b, s: (l, 0, 0)),
            _resident((None, F, D), lambda b, s: (l, 0, 0)),
        ],
        out_specs=pl.BlockSpec((None, tm, D), lambda b, s: (b, s, 0)),
        out_shape=jax.ShapeDtypeStruct((B, S, D), F32),
        compiler_params=_params(vmem, 2),
        name="ffn",
    )(x, mod, gain, w1, w3, w2)


def _mixin_kernel(x_ref, mod_ref, gain_ref, wqkv_ref, wz_ref, wg_ref, wlat_ref, wba_ref, conv_ref,
                  qkv_ref, zs_ref, gates_ref, lat_ref, ba_ref, pbuf_ref, *, tm):
    s = pl.program_id(1)
    x = x_ref[...]
    mod = mod_ref[...]
    h = _modulate(x, gain_ref[...], mod[3:4], mod[4:5]).astype(BF16)

    @pl.when(s == 0)
    def _():
        pbuf_ref[0:HALO, :] = jnp.zeros((HALO, pbuf_ref.shape[1]), F32)

    pbuf_ref[HALO:HALO + tm, :] = _dot(h, wqkv_ref[...])
    conv = conv_ref[...]
    n_qk = 2 * GDN_HEADS
    for cb in range(pbuf_ref.shape[1] // V7X_LANES):
        c0 = cb * V7X_LANES
        acc = None
        for j in range(CONV_K):
            r0 = HALO - (CONV_K - 1) + j
            term = conv[j:j + 1, c0:c0 + V7X_LANES] * pbuf_ref[r0:r0 + tm, c0:c0 + V7X_LANES]
            acc = term if acc is None else acc + term
        y = _silu(acc)
        if cb < n_qk:
            y = y * lax.rsqrt(jnp.sum(y * y, axis=-1, keepdims=True) + EPS)
        qkv_ref[cb] = y.astype(BF16)
    pbuf_ref[0:HALO, :] = pbuf_ref[tm:tm + HALO, :]

    z = _silu(_dot(h, wz_ref[...]))
    for hh in range(GDN_HEADS):
        zs_ref[hh] = z[:, hh * GDN_DV:(hh + 1) * GDN_DV].astype(BF16)
    gates_ref[...] = _sigmoid(_dot(h, wg_ref[...])).astype(BF16)
    lat_ref[...] = _dot(h, wlat_ref[...])
    ba_ref[...] = _dot_nt(wba_ref[...], h)


def _mixin(x, mod, gain, wqkv, wz, wg, wlat, wba, conv, l):
    B, S, D = x.shape
    C = wqkv.shape[-1]
    nz = wz.shape[-1]
    ng = wg.shape[-1]
    nl = wlat.shape[-1]
    nb = wba.shape[1]
    tm = min(S, MIX_TOKENS)
    ncb = C // V7X_LANES
    vmem = ((C + nz + ng + nl) * D * 2 + 2 * tm * D * 4 + 2 * tm * (C + nz + ng) * 2 + 2 * tm * nl * 4
            + (tm + HALO) * C * 4 + tm * C * 4 + tm * (nz + ng) * 4 + 4 * 1024 * 1024)
    return pl.pallas_call(
        functools.partial(_mixin_kernel, tm=tm),
        grid=(B, S // tm),
        in_specs=[
            pl.BlockSpec((None, tm, D), lambda b, s: (b, s, 0)),
            pl.BlockSpec((None, None, N_MOD, D), lambda b, s: (l, b, 0, 0)),
            pl.BlockSpec((None, 1, D), lambda b, s: (l, 0, 0)),
            _resident((None, D, C), lambda b, s: (l, 0, 0)),
            _resident((None, D, nz), lambda b, s: (l, 0, 0)),
            _resident((None, D, ng), lambda b, s: (l, 0, 0)),
            _resident((None, D, nl), lambda b, s: (l, 0, 0)),
            _resident((None, nb, D), lambda b, s: (l, 0, 0)),
            pl.BlockSpec((None, CONV_K, C), lambda b, s: (l, 0, 0)),
        ],
        out_specs=[
            pl.BlockSpec((None, ncb, tm, V7X_LANES), lambda b, s: (b, 0, s, 0)),
            pl.BlockSpec((None, GDN_HEADS, tm, GDN_DV), lambda b, s: (b, 0, s, 0)),
            pl.BlockSpec((None, tm, ng), lambda b, s: (b, s, 0)),
            pl.BlockSpec((None, tm, nl), lambda b, s: (b, s, 0)),
            pl.BlockSpec((None, nb, tm), lambda b, s: (b, 0, s)),
        ],
        out_shape=[
            jax.ShapeDtypeStruct((B, ncb, S, V7X_LANES), BF16),
            jax.ShapeDtypeStruct((B, GDN_HEADS, S, GDN_DV), BF16),
            jax.ShapeDtypeStruct((B, S, ng), BF16),
            jax.ShapeDtypeStruct((B, S, nl), F32),
            jax.ShapeDtypeStruct((B, nb, S), F32),
        ],
        scratch_shapes=[pltpu.VMEM((tm + HALO, C), F32)],
        compiler_params=_params(vmem, 2),
        name="mixin",
    )(x, mod, gain, wqkv, wz, wg, wlat, wba, conv)


def _gate_kernel(ba_ref, alog_ref, dtb_ref, grow_ref, gcol_ref, *, tm):
    H = GDN_HEADS
    ba = ba_ref[...]
    beta = _sigmoid(ba[0:H])
    xa = ba[H:2 * H] + dtb_ref[...]
    softplus = jnp.maximum(xa, 0.0) + jnp.log(1.0 + jnp.exp(-jnp.abs(xa)))
    g = -jnp.exp(alog_ref[...]) * softplus

    ii = lax.broadcasted_iota(jnp.int32, (tm, tm), 0)
    jj = lax.broadcasted_iota(jnp.int32, (tm, tm), 1)
    same = (ii >> LOG2_CHUNK) == (jj >> LOG2_CHUNK)
    upper =jnp.where(same & (ii <= jj), 1.0, 0.0).astype(BF16)
    block = jnp.where(same, 1.0, 0.0).astype(BF16)
    eye = jnp.where(ii == jj, 1.0, 0.0).astype(BF16)
    gam = None
    glast = None
    for part in _split3(g):
        t1 = _dot(part, upper)
        t2 = _dot(part, block)
        gam = t1 if gam is None else gam + t1
        glast = t2 if glast is None else glast + t2
    egam = jnp.exp(gam)
    ekd = jnp.exp(glast - gam)
    egl = jnp.exp(glast)
    zero = jnp.zeros_like(gam[0:1])
    rows = []
    for hh in range(H):
        r = jnp.concatenate([beta[hh:hh + 1], gam[hh:hh + 1], egam[hh:hh + 1], ekd[hh:hh + 1],
                             egl[hh:hh + 1], zero, zero, zero], axis=0)
        grow_ref[hh] = r
        rows.append(r)
    allrows = jnp.concatenate(rows, axis=0)
    cols = None
    for part in _split3(allrows):
        t = _dot_nt(eye, part)
        cols = t if cols is None else cols + t
    for hh in range(H):
        gcol_ref[hh] = cols[:, 8 * hh:8 * hh + 8]


def _gdn_gates(ba, a_log, dt_bias, l):
    B, nb, S = ba.shape
    H = GDN_HEADS
    tm = min(S, GATE_TOKENS)
    vmem = 8 * tm * tm * 4 + 4 * H * tm * V7X_LANES * 4
    return pl.pallas_call(
        functools.partial(_gate_kernel, tm=tm),
        grid=(B, S // tm),
        in_specs=[
            pl.BlockSpec((None, nb, tm), lambda b, s: (b, 0, s)),
            pl.BlockSpec((None, H, 1), lambda b, s: (l, 0, 0)),
            pl.BlockSpec((None, H, 1), lambda b, s: (l, 0, 0)),
        ],
        out_specs=[
            pl.BlockSpec((None, H, 8, tm), lambda b, s: (b, 0, 0, s)),
            pl.BlockSpec((None, H, tm, 8), lambda b, s: (b, 0, s, 0)),
        ],
        out_shape=[
            jax.ShapeDtypeStruct((B, H, 8, S), F32),
            jax.ShapeDtypeStruct((B, H, S, 8), F32),
        ],
        compiler_params=_params(vmem, 2),
        name="gdn_gates",
    )(ba, a_log, dt_bias)


def _gdn_kernel(q_ref, k_ref, v_ref, zs_ref, grow_ref, gcol_ref, og_ref, o_ref, state_ref, vn_ref, *, tc, G):
    s = pl.program_id(2)

    @pl.when(s == 0)
    def _():
        state_ref[...] = jnp.zeros(state_ref.shape, F32)
        vn_ref[...] = jnp.zeros(vn_ref.shape, BF16)

    scale = GDN_DK ** -0.5
    ii = lax.broadcasted_iota(jnp.int32, (tc, tc), 0)
    jj = lax.broadcasted_iota(jnp.int32, (tc, tc), 1)
    same = (ii >> LOG2_CHUNK) == (jj >> LOG2_CHUNK)
    causal = same & (ii >= jj)
    strict = same & (ii > jj)
    eye = jnp.where(ii == jj, 1.0, 0.0)
    lane = lax.broadcasted_iota(jnp.int32, (GDN_DK, tc), 1) >> LOG2_CHUNK

    def prepare(g):
        q = q_ref[g]
        k = k_ref[g]
        kf = k.astype(F32)
        vf = v_ref[g].astype(F32)
        gc = gcol_ref[g]
        gr = grow_ref[g]
        beta = gc[:, 0:1]
        egam = gc[:, 2:3]
        decay = jnp.where(causal, jnp.exp(jnp.minimum(gc[:, 1:2] - gr[1:2, :], 0.0)), 0.0)
        kb = kf * beta
        m = jnp.where(strict, _dot_nt(kb.astype(BF16), k) * decay, 0.0)
        t_inv = eye - m
        p = m
        for _ in range(LOG2_CHUNK - 1):
            pb = p.astype(BF16)
            p = _dot(pb, pb)
            t_inv = t_inv + _dot(t_inv.astype(BF16), p.astype(BF16))
        rhs = jnp.concatenate([vf * beta, kb * egam], axis=1).astype(BF16)
        uw = _dot(t_inv.astype(BF16), rhs)
        qk = (_dot_nt(q, k) * (decay * scale)).astype(BF16)
        qd = (q.astype(F32) * (egam * scale)).astype(BF16)
        kdt = (kf * gc[:, 3:4]).T
        return uw[:, :GDN_DV], uw[:, GDN_DV:].astype(BF16), qk, qd, kdt, gr

    prepared = [prepare(g) for g in range(G)]
    states = [state_ref[g] for g in range(G)]
    outs = [[] for _ in range(G)]
    for c in range(tc // CHUNK):
        r0 = c * CHUNK
        for g in range(G):
            u, w, qk, qd, kdt, gr = prepared[g]
            ws = _dot(jnp.concatenate([w[r0:r0 + CHUNK], qd[r0:r0 + CHUNK]], axis=0), states[g].astype(BF16))
            v_new = u[r0:r0 + CHUNK] - ws[:CHUNK]
            vn_ref[g, r0:r0 + CHUNK, :] = v_new.astype(BF16)
            lhs = jnp.concatenate([qk[r0:r0 + CHUNK], jnp.where(lane == c, kdt, 0.0).astype(BF16)], axis=0)
            r2 = _dot(lhs, vn_ref[g])
            outs[g].append(ws[CHUNK:] + r2[:CHUNK])
            states[g] = states[g] * gr[4:5, r0:r0 + 1] + r2[CHUNK:]
    for g in range(G):
        state_ref[g] = states[g]
        o = jnp.concatenate(outs[g], axis=0)
        o = o * lax.rsqrt(jnp.mean(o * o, axis=-1, keepdims=True) + EPS) * og_ref[...]
        o_ref[:, g * GDN_DV:(g + 1) * GDN_DV] = (o * zs_ref[g].astype(F32)).astype(BF16)


def _gdn(qkv, zs, grow, gcol, out_gain, l):
    B, _, S, _ = qkv.shape
    H = GDN_HEADS
    G = GDN_GROUP
    tc = min(S, GDN_TOKENS)
    vmem = G * (24 * tc * tc * 4 + 24 * tc * V7X_LANES * 4)
    return pl.pallas_call(
        functools.partial(_gdn_kernel, tc=tc, G=G),
        grid=(B, H // G, S // tc),
        in_specs=[
            pl.BlockSpec((None, G, tc, GDN_DK), lambda b, h, s: (b, h, s, 0)),
            pl.BlockSpec((None, G, tc, GDN_DK), lambda b, h, s: (b, H // G + h, s, 0)),
            pl.BlockSpec((None, G, tc, GDN_DV), lambda b, h, s: (b, 2 * H // G + h, s, 0)),
            pl.BlockSpec((None, G, tc, GDN_DV), lambda b, h, s: (b, h, s, 0)),
            pl.BlockSpec((None, G, 8, tc), lambda b, h, s: (b, h, 0, s)),
            pl.BlockSpec((None, G, tc, 8), lambda b, h, s: (b, h, s, 0)),
            pl.BlockSpec((None, 1, GDN_DV), lambda b, h, s: (l, 0, 0)),
        ],
        out_specs=pl.BlockSpec((None, tc, G * GDN_DV), lambda b, h, s: (b, s, h)),
        out_shape=jax.ShapeDtypeStruct((B, S, H * GDN_DV), BF16),
        scratch_shapes=[pltpu.VMEM((G, GDN_DK, GDN_DV), F32), pltpu.VMEM((G, tc, GDN_DV), BF16)],
        compiler_params=_params(vmem, 3),
        name="gdn",
    )(qkv, qkv, qkv, zs, grow, gcol, out_gain)


def _swap_halves(y):
    half = y.shape[-1] // 2
    return jnp.concatenate([y[:, half:], y[:, :half]], axis=1)


def _mla_prep_kernel(lat_ref, cos_ref, sin_ref, cost_ref, sint_ref, qg_ref, kvg_ref, wqt_ref, wk_ref, wvt_ref,
                     qnc_ref, kn_ref, qt_ref, k_ref, vt_ref):
    H = MLA_HEADS
    half = QK_ROPE // 2
    lat = lat_ref[...]
    ql = lat[:, :Q_LORA]
    kvl = lat[:, Q_LORA:Q_LORA + KV_LORA]
    kpe = lat[:, Q_LORA + KV_LORA:Q_LORA + KV_LORA + QK_ROPE]
    qln = (ql * lax.rsqrt(jnp.mean(ql * ql, axis=-1, keepdims=True) + EPS) * qg_ref[...]).astype(BF16)
    kvn = (kvl * lax.rsqrt(jnp.mean(kvl * kvl, axis=-1, keepdims=True) + EPS) * kvg_ref[...]).astype(BF16)
    qt = _dot_nt(wqt_ref[...], qln)
    knope = _dot(kvn, wk_ref[...])
    vt = _dot_nt(wvt_ref[...], kvn)

    cost = cost_ref[...]
    sint = sint_ref[...]
    qnc = qnc_ref[...]
    scale = MLA_QK ** -0.5 * math.log2(math.e)
    for hh in range(H):
        blk = qt[hh * MLA_QK:(hh + 1) * MLA_QK]
        r = lax.rsqrt(jnp.sum(blk * blk, axis=0, keepdims=True) * (1.0 / MLA_QK) + EPS) * scale
        y = blk * qnc * r
        y1 = y[QK_NOPE:QK_NOPE + half]
        y2 = y[QK_NOPE + half:]
        qt_ref[hh, 0:QK_NOPE, :] = y[:QK_NOPE].astype(BF16)
        qt_ref[hh, QK_NOPE:QK_NOPE + half, :] = (y1 * cost - y2 * sint).astype(BF16)
        qt_ref[hh, QK_NOPE + half:MLA_QK, :] = (y2 * cost + y1 * sint).astype(BF16)
        vt_ref[hh] = vt[hh * V_HEAD:(hh + 1) * V_HEAD].astype(BF16)

    kn = kn_ref[...]
    yk = kpe * kn[:, QK_NOPE:]
    kr = yk * cos_ref[...] + _swap_halves(yk) * sin_ref[...]
    kpe_ss = jnp.sum(kpe * kpe, axis=-1, keepdims=True)
    for hh in range(H):
        k_nope = knope[:, hh * QK_NOPE:(hh + 1) * QK_NOPE]
        ss = jnp.sum(k_nope * k_nope, axis=-1, keepdims=True) + kpe_ss
        r = lax.rsqrt(ss * (1.0 / MLA_QK) + EPS)
        k_ref[hh, :, 0:QK_NOPE] = (k_nope * r * kn[:, :QK_NOPE]).astype(BF16)
        k_ref[hh, :, QK_NOPE:MLA_QK] = (kr * r).astype(BF16)


def _mla_prep(lat, cos2, sin2, cost, sint, q_lat_gain, kv_lat_gain, wqt, wk, wvt, q_norm_col, k_norm, l):
    B, S, nl = lat.shape
    H = MLA_HEADS
    half = QK_ROPE // 2
    tm = min(S, ATTN_Q)
    nq = wqt.shape[1]
    nk = wk.shape[-1]
    nv = wvt.shape[1]
    vmem = ((nq * Q_LORA + (nk + nv) * KV_LORA) * 2 + 2 * tm * nl * 4 + 3 * tm * (nq + nk + nv) * 4
            + 4 * H * tm * (256 + MLA_QK + V_HEAD) * 2)
    return pl.pallas_call(
        _mla_prep_kernel,
        grid=(B, S // tm),
        in_specs=[
            pl.BlockSpec((None, tm, nl), lambda b, s: (b, s, 0)),
            pl.BlockSpec((None, tm, QK_ROPE), lambda b, s: (b, s, 0)),
            pl.BlockSpec((None, tm, QK_ROPE), lambda b, s: (b, s, 0)),
            pl.BlockSpec((None, half, tm), lambda b, s: (b, 0, s)),
            pl.BlockSpec((None, half, tm), lambda b, s: (b, 0, s)),
            pl.BlockSpec((None, 1, Q_LORA), lambda b, s: (l, 0, 0)),
            pl.BlockSpec((None, 1, KV_LORA), lambda b, s: (l, 0, 0)),
            _resident((None, nq, Q_LORA), lambda b, s: (l, 0, 0)),
            _resident((None, KV_LORA, nk), lambda b, s: (l, 0, 0)),
            _resident((None, nv, KV_LORA), lambda b, s: (l, 0, 0)),
            pl.BlockSpec((None, MLA_QK, 1), lambda b, s: (l, 0, 0)),
            pl.BlockSpec((None, 1, MLA_QK), lambda b, s: (l, 0, 0)),
        ],
        out_specs=[
            pl.BlockSpec((None, H, MLA_QK, tm), lambda b, s: (b, 0, 0, s)),
            pl.BlockSpec((None, H, tm, MLA_QK), lambda b, s: (b, 0, s, 0)),
            pl.BlockSpec((None, H, None, V_HEAD, tm), lambda b, s: (b, 0, s, 0, 0)),
        ],
        out_shape=[
            jax.ShapeDtypeStruct((B, H, MLA_QK, S), BF16),
            jax.ShapeDtypeStruct((B, H, S, MLA_QK), BF16),
            jax.ShapeDtypeStruct((B, H, S // tm, V_HEAD, tm), BF16),
        ],
        compiler_params=_params(vmem, 2),
        name="mla_prep",
    )(lat, cos2, sin2, cost, sint, q_lat_gain, kv_lat_gain, wqt, wk, wvt, q_norm_col, k_norm)


def _attn_kernel(qt_ref, k_ref, vt_ref, o_ref, *, tq):
    i = pl.program_id(2)
    qt = qt_ref[...]

    def scores(j):
        return _dot(k_ref[pl.ds(pl.multiple_of(j * tq, tq), tq), :], qt)

    def update(st, mx, j, m, l, acc):
        m_new = jnp.maximum(m, mx)
        alpha = jnp.exp2(m - m_new)
        p = jnp.exp2(st - m_new)
        l = alpha * l + jnp.sum(p, axis=0, keepdims=True)
        return m_new, l, alpha * acc + _dot(vt_ref[j], p.astype(BF16))

    def body(j, carry):
        st, mx, m, l, acc = carry
        st_next = scores(j + 1)
        mx_next = jnp.max(st_next, axis=0, keepdims=True)
        m, l, acc = update(st, mx, j, m, l, acc)
        return st_next, mx_next, m, l, acc

    st0 = scores(0)
    init = (st0, jnp.max(st0, axis=0, keepdims=True), jnp.full((1, tq), -jnp.inf, F32),
            jnp.zeros((1, tq), F32), jnp.zeros((V_HEAD, tq), F32))
    st, _, m, l, acc = lax.fori_loop(0, i, body, init)
    kpos = lax.broadcasted_iota(jnp.int32, (tq, tq), 0)
    qpos = lax.broadcasted_iota(jnp.int32, (tq, tq), 1)
    st = jnp.where(kpos <= qpos, st, -jnp.inf)
    m, l, acc = update(st, jnp.max(st, axis=0, keepdims=True), i, m, l, acc)
    o_ref[...] = (acc / l).T.astype(BF16)


def _attention(qt, k, vt):
    B, H, S, _ = k.shape
    tq = vt.shape[-1]
    vmem = 6 * tq * tq * 4 + 4 * S * (256 + V_HEAD) * 2 + 4 * tq * 256 * 2 + 8 * tq * V7X_LANES * 4
    return pl.pallas_call(
        functools.partial(_attn_kernel, tq=tq),
        grid=(B, H, S // tq),
        in_specs=[
            pl.BlockSpec((None, None, MLA_QK, tq), lambda b, h, i: (b, h, 0, i)),
            pl.BlockSpec((None, None, S, MLA_QK), lambda b, h, i: (b, h, 0, 0)),
            pl.BlockSpec((None, None, S // tq, V_HEAD, tq), lambda b, h, i: (b, h, 0, 0, 0)),
        ],
        out_specs=pl.BlockSpec((None, tq, V_HEAD), lambda b, h, i: (b, i, h)),
        out_shape=jax.ShapeDtypeStruct((B, S, H * V_HEAD), BF16),
        compiler_params=_params(vmem, 3),
        name="mla_attention",
    )(qt, k, vt)


def _merge_kernel(x_ref, mod_ref, oa_ref, ob_ref, gates_ref, wa_ref, wb_ref, wo_ref, o_ref):
    D = x_ref.shape[-1]
    mod = mod_ref[...]
    gates = gates_ref[...].astype(F32)
    ya = _dot(oa_ref[...], wa_ref[...])
    yb = _dot(ob_ref[...], wb_ref[...])
    y = (gates[:, :D] * ya + gates[:, D:] * yb).astype(BF16)
    o_ref[...] = x_ref[...] + (1.0 + mod[5:6]) * _dot(y, wo_ref[...])


def _merge(x, mod, oa, ob, gates, wa, wb, wo, l):
    B, S, D = x.shape
    tm = min(S, MERGE_TOKENS)
    na = oa.shape[-1]
    nbv = ob.shape[-1]
    vmem = (na + nbv + D) * D * 2 + 4 * tm * D * 4 + 2 * tm * (na + nbv + 2 * D) * 2 + 4 * tm * D * 4
    return pl.pallas_call(
        _merge_kernel,
        grid=(B, S // tm),
        in_specs=[
            pl.BlockSpec((None, tm, D), lambda b, s: (b, s, 0)),
            pl.BlockSpec((None, None, N_MOD, D), lambda b, s: (l, b, 0, 0)),
            pl.BlockSpec((None, tm, na), lambda b, s: (b, s, 0)),
            pl.BlockSpec((None, tm, nbv), lambda b, s: (b, s, 0)),
            pl.BlockSpec((None, tm, 2 * D), lambda b, s: (b, s, 0)),
            _resident((None, na, D), lambda b, s: (l, 0, 0)),
            _resident((None, nbv, D), lambda b, s: (l, 0, 0)),
            _resident((None, D, D), lambda b, s: (l, 0, 0)),
        ],
        out_specs=pl.BlockSpec((None, tm, D), lambda b, s: (b, s, 0)),
        out_shape=jax.ShapeDtypeStruct((B, S, D), F32),
        compiler_params=_params(vmem, 2),
        name="merge",
    )(x, mod, oa, ob, gates, wa, wb, wo)


def _pack_w_in(w_in):
    conv_c = 2 * GDN_HEADS * GDN_DK + GDN_HEADS * GDN_DV
    gdn_v = GDN_HEADS * GDN_DV
    D = w_in.shape[1]
    sizes = (conv_c, gdn_v, GDN_HEADS, GDN_HEADS, Q_LORA, KV_LORA, QK_ROPE, D, D)
    offs = np.cumsum((0,) + sizes)
    qkv, z, b_raw, a_raw, q_lat, kv_lat, k_pe, gate_a, gate_b = (
        w_in[:, :, int(offs[i]):int(offs[i + 1])] for i in range(len(sizes)))
    n_lat = Q_LORA + KV_LORA + QK_ROPE
    pad = (-n_lat) % V7X_LANES
    lat = jnp.concatenate([q_lat, kv_lat, k_pe, jnp.zeros(w_in.shape[:2] + (pad,), w_in.dtype)], axis=-1)
    ba = jnp.swapaxes(jnp.concatenate([b_raw, a_raw], axis=-1), 1, 2)
    return (qkv.astype(BF16), z.astype(BF16), jnp.concatenate([gate_a, gate_b], axis=-1).astype(BF16),
            lat.astype(BF16), ba.astype(BF16))


def _pack_kv_up(w_kv_up):
    L = w_kv_up.shape[0]
    w = w_kv_up.reshape(L, KV_LORA, MLA_HEADS, QK_NOPE + V_HEAD)
    wk = w[..., :QK_NOPE].reshape(L, KV_LORA, MLA_HEADS * QK_NOPE)
    wv = w[..., QK_NOPE:].reshape(L, KV_LORA, MLA_HEADS * V_HEAD)
    return wk.astype(BF16), jnp.swapaxes(wv, 1, 2).astype(BF16)


def kernel(x, c, positions, ada_w, ada_b, norm_ffn1, ffn1_w1, ffn1_w3, ffn1_w2, norm_mix, w_in, gdn_conv, gdn_a_log, gdn_dt_bias, gdn_out_gain, mla_q_lat_gain, mla_kv_lat_gain, mla_w_q_up, mla_w_kv_up, mla_q_norm, mla_k_norm, w_branch_a, w_branch_b, w_out, norm_ffn2, ffn2_w1, ffn2_w3, ffn2_w2):
    L = ada_w.shape[0]
    S = x.shape[1]
    assert S % CHUNK == 0 and S % min(S, GDN_TOKENS) == 0 and S % min(S, FFN_TOKENS) == 0
    assert S % min(S, MERGE_TOKENS) == 0 and S % min(S, ATTN_Q) == 0

    mod = _ada_mod(c, ada_w, ada_b)
    cos2, sin2, cost, sint = _rope_tables(positions)

    row = lambda a: a.reshape(L, 1, a.shape[-1])
    col = lambda a: a.reshape(L, a.shape[-1], 1)
    bf = lambda a: a.astype(BF16)
    wqkv, wz, wg, wlat, wba = _pack_w_in(w_in)
    wqt = bf(jnp.swapaxes(mla_w_q_up, 1, 2))
    wk, wvt = _pack_kv_up(mla_w_kv_up)
    f1 = (bf(ffn1_w1), bf(ffn1_w3), bf(ffn1_w2))
    f2 = (bf(ffn2_w1), bf(ffn2_w3), bf(ffn2_w2))
    wa, wb, wo = bf(w_branch_a), bf(w_branch_b), bf(w_out)
    g_ffn1, g_mix, g_ffn2 = row(norm_ffn1), row(norm_mix), row(norm_ffn2)
    og, qlg, kvlg = row(gdn_out_gain), row(mla_q_lat_gain), row(mla_kv_lat_gain)
    qnc, kn = col(mla_q_norm), row(mla_k_norm)
    a_log, dt_bias = col(gdn_a_log), col(gdn_dt_bias)

    for l in range(L):
        x = _ffn(x, mod, g_ffn1, *f1, l, 0)
        qkv, zs, gates, lat, ba = _mixin(x, mod, g_mix, wqkv, wz, wg, wlat, wba, gdn_conv, l)
        grow, gcol = _gdn_gates(ba, a_log, dt_bias, l)
        o_gdn = _gdn(qkv, zs, grow, gcol, og, l)
        qt, k, vt = _mla_prep(lat, cos2, sin2, cost, sint, qlg, kvlg, wqt, wk, wvt, qnc, kn, l)
        o_mla = _attention(qt, k, vt)
        x = _merge(x, mod, o_gdn, o_mla, gates, wa, wb, wo, l)
        x = _ffn(x, mod, g_ffn2, *f2, l, 6)
    return x
```

```python
import functools
import math

import numpy as np
import jax
import jax.numpy as jnp
from jax import lax
from jax.experimental import pallas as pl
from jax.experimental.pallas import tpu as pltpu

F32 = jnp.float32
BF16 = jnp.bfloat16

EPS = 1e-6
N_MOD = 9
GDN_HEADS = 8
GDN_DK = 128
GDN_DV = 128
CONV_K = 4
CHUNK = 64
LOG2_CHUNK = 6
MLA_HEADS = 8
Q_LORA = 384
KV_LORA = 256
QK_NOPE = 128
QK_ROPE = 64
V_HEAD = 128
ROPE_THETA = 10000.0
MLA_QK = QK_NOPE + QK_ROPE

V7X_VMEM_BYTES = 64 * 1024 * 1024
V7X_LANES = 128
V7X_SUBLANES = 8
VMEM_CAP_BYTES = V7X_VMEM_BYTES - 6 * 1024 * 1024

FFN_TOKENS = 512
MIX_TOKENS = 256
GATE_TOKENS = 512
GDN_TOKENS = 256
GDN_GROUP = 4
ATTN_Q = 512
MERGE_TOKENS = 1024
HALO = V7X_SUBLANES


def _params(vmem_bytes, n_axes):
    limit = int(min(VMEM_CAP_BYTES, max(vmem_bytes, 16 * 1024 * 1024)))
    return pltpu.CompilerParams(dimension_semantics=("arbitrary",) * n_axes, vmem_limit_bytes=limit)


def _resident(block_shape, index_map):
    return pl.BlockSpec(block_shape, index_map, pipeline_mode=pl.Buffered(1))


def _dot(a, b):
    return jnp.dot(a, b, preferred_element_type=F32)


def _dot_nt(a, b):
    return lax.dot_general(a, b, (((1,), (1,)), ((), ())), preferred_element_type=F32)


def _sigmoid(x):
    return 1.0 / (1.0 + jnp.exp(-x))


def _silu(x):
    return x * _sigmoid(x)


def _modulate(x, gain, shift, scale):
    y = x * lax.rsqrt(jnp.mean(x * x, axis=-1, keepdims=True) + EPS)
    return (y * gain) * (1.0 + scale) + shift


def _split3(x):
    x1 = x.astype(BF16)
    r1 = x - x1.astype(F32)
    x2 = r1.astype(BF16)
    x3 = (r1 - x2.astype(F32)).astype(BF16)
    return x1, x2, x3


def _mod_kernel(c_ref, w_ref, b_ref, o_ref):
    cond = _silu(c_ref[...]).astype(BF16)
    o_ref[...] = _dot(cond, w_ref[...].astype(BF16)) + b_ref[...]


def _ada_mod(c, ada_w, ada_b):
    L, D, _ = ada_w.shape
    B = c.shape[0]
    ada_b3 = ada_b.reshape(L, N_MOD, 1, D)
    out = pl.pallas_call(
        _mod_kernel,
        grid=(L, N_MOD),
        in_specs=[
            pl.BlockSpec((B, D), lambda l, j: (0, 0)),
            pl.BlockSpec((None, D, D), lambda l, j: (l, 0, j)),
            pl.BlockSpec((None, None, 1, D), lambda l, j: (l, j, 0, 0)),
        ],
        out_specs=pl.BlockSpec((None, None, B, D), lambda l, j: (l, j, 0, 0)),
        out_shape=jax.ShapeDtypeStruct((L, N_MOD, B, D), F32),
        compiler_params=_params(4 * D * D * 4, 2),
        name="ada_mod",
    )(c, ada_w, ada_b3)
    return jnp.transpose(out, (0, 2, 1, 3))


def _rope_kernel(pos_ref, post_ref, freq_ref, sign_ref, freqc_ref, cos_ref, sin_ref, cost_ref, sint_ref):
    ang = pos_ref[...].astype(F32) * freq_ref[...]
    cos_ref[...] = jnp.cos(ang)
    sin_ref[...] = jnp.sin(ang) * sign_ref[...]
    angt = freqc_ref[...] * post_ref[...].astype(F32)
    cost_ref[...] = jnp.cos(angt)
    sint_ref[...] = jnp.sin(angt)


def _rope_tables(positions):
    B, S = positions.shape
    half = QK_ROPE // 2
    inv_freq = ROPE_THETA ** (-jnp.arange(half, dtype=F32) / half)
    freq2 = jnp.concatenate([inv_freq, inv_freq]).reshape(1, QK_ROPE)
    sign2 = jnp.concatenate([-jnp.ones((half,), F32), jnp.ones((half,), F32)]).reshape(1, QK_ROPE)
    ts = min(S, 1024)
    return pl.pallas_call(
        _rope_kernel,
        grid=(B, S // ts),
        in_specs=[
            pl.BlockSpec((None, ts, 1), lambda b, s: (b, s, 0)),
            pl.BlockSpec((None, 1, ts), lambda b, s: (b, 0, s)),
            pl.BlockSpec((1, QK_ROPE), lambda b, s: (0, 0)),
            pl.BlockSpec((1, QK_ROPE), lambda b, s: (0, 0)),
            pl.BlockSpec((half, 1), lambda b, s: (0, 0)),
        ],
        out_specs=[pl.BlockSpec((None, ts, QK_ROPE), lambda b, s: (b, s, 0))] * 2
        + [pl.BlockSpec((None, half, ts), lambda b, s: (b, 0, s))] * 2,
        out_shape=[jax.ShapeDtypeStruct((B, S, QK_ROPE), F32)] * 2
        + [jax.ShapeDtypeStruct((B, half, S), F32)] * 2,
        compiler_params=_params(8 * ts * V7X_LANES * 4, 2),
        name="rope_tables",
    )(positions.reshape(B, S, 1), positions.reshape(B, 1, S), freq2, sign2, inv_freq.reshape(half, 1))


def _ffn_kernel(x_ref, mod_ref, gain_ref, w1_ref, w3_ref, w2_ref, o_ref, *, row):
    x = x_ref[...]
    mod = mod_ref[...]
    h = _modulate(x, gain_ref[...], mod[row:row + 1], mod[row + 1:row + 2]).astype(BF16)
    a = _dot(h, w1_ref[...])
    b = _dot(h, w3_ref[...])
    g = (_silu(a) * b).astype(BF16)
    y = _dot(g, w2_ref[...])
    o_ref[...] = x + (0.5 * (1.0 + mod[row + 2:row + 3])) * y


def _ffn(x, mod, gain, w1, w3, w2, l, row):
    B, S, D = x.shape
    F = w1.shape[-1]
    tm = min(S, FFN_TOKENS)
    vmem = 3 * D * F * 2 + 4 * tm * D * 4 + 3 * tm * F * 4 + 2 * tm * D * 4
    return pl.pallas_call(
        functools.partial(_ffn_kernel, row=row),
        grid=(B, S // tm),
        in_specs=[
            pl.BlockSpec((None, tm, D), lambda b, s: (b, s, 0)),
            pl.BlockSpec((None, None, N_MOD, D), lambda b, s: (l, b, 0, 0)),
            pl.BlockSpec((None, 1, D), lambda b, s: (l, 0, 0)),
            _resident((None, D, F), lambda b, s: (l, 0, 0)),
            _resident((None, D, F), lambda b, s: (l, 0, 0)),
            _resident((None, F, D), lambda b, s: (l, 0, 0)),
        ],
        out_specs=pl.BlockSpec((None, tm, D), lambda b, s: (b, s, 0)),
        out_shape=jax.ShapeDtypeStruct((B, S, D), F32),
        compiler_params=_params(vmem, 2),
        name="ffn",
    )(x, mod, gain, w1, w3, w2)


def _mixin_kernel(x_ref, mod_ref, gain_ref, wqkv_ref, wz_ref, wg_ref, wlat_ref, wba_ref, conv_ref,
                  qkv_ref, zs_ref, gates_ref, lat_ref, ba_ref, pbuf_ref, *, tm):
    s = pl.program_id(1)
    x = x_ref[...]
    mod = mod_ref[...]
    h = _modulate(x, gain_ref[...], mod[3:4], mod[4:5]).astype(BF16)

    @pl.when(s == 0)
    def _():
        pbuf_ref[0:HALO, :] = jnp.zeros((HALO, pbuf_ref.shape[1]), F32)

    pbuf_ref[HALO:HALO + tm, :] = _dot(h, wqkv_ref[...])
    conv = conv_ref[...]
    n_qk = 2 * GDN_HEADS
    for cb in range(pbuf_ref.shape[1] // V7X_LANES):
        c0 = cb * V7X_LANES
        acc = None
        for j in range(CONV_K):
            r0 = HALO - (CONV_K - 1) + j
            term = conv[j:j + 1, c0:c0 + V7X_LANES] * pbuf_ref[r0:r0 + tm, c0:c0 + V7X_LANES]
            acc = term if acc is None else acc + term
        y = _silu(acc)
        if cb < n_qk:
            y = y * lax.rsqrt(jnp.sum(y * y, axis=-1, keepdims=True) + EPS)
        qkv_ref[cb] = y.astype(BF16)
    pbuf_ref[0:HALO, :] = pbuf_ref[tm:tm + HALO, :]

    z = _silu(_dot(h, wz_ref[...]))
    for hh in range(GDN_HEADS):
        zs_ref[hh] = z[:, hh * GDN_DV:(hh + 1) * GDN_DV].astype(BF16)
    gates_ref[...] = _sigmoid(_dot(h, wg_ref[...])).astype(BF16)
    lat_ref[...] = _dot(h, wlat_ref[...])
    ba_ref[...] = _dot_nt(wba_ref[...], h)


def _mixin(x, mod, gain, wqkv, wz, wg, wlat, wba, conv, l):
    B, S, D = x.shape
    C = wqkv.shape[-1]
    nz = wz.shape[-1]
    ng = wg.shape[-1]
    nl = wlat.shape[-1]
    nb = wba.shape[1]
    tm = min(S, MIX_TOKENS)
    ncb = C // V7X_LANES
    vmem = ((C + nz + ng + nl) * D * 2 + 2 * tm * D * 4 + 2 * tm * (C + nz + ng) * 2 + 2 * tm * nl * 4
            + (tm + HALO) * C * 4 + tm * C * 4 + tm * (nz + ng) * 4 + 4 * 1024 * 1024)
    return pl.pallas_call(
        functools.partial(_mixin_kernel, tm=tm),
        grid=(B, S // tm),
        in_specs=[
            pl.BlockSpec((None, tm, D), lambda b, s: (b, s, 0)),
            pl.BlockSpec((None, None, N_MOD, D), lambda b, s: (l, b, 0, 0)),
            pl.BlockSpec((None, 1, D), lambda b, s: (l, 0, 0)),
            _resident((None, D, C), lambda b, s: (l, 0, 0)),
            _resident((None, D, nz), lambda b, s: (l, 0, 0)),
            _resident((None, D, ng), lambda b, s: (l, 0, 0)),
            _resident((None, D, nl), lambda b, s: (l, 0, 0)),
            _resident((None, nb, D), lambda b, s: (l, 0, 0)),
            pl.BlockSpec((None, CONV_K, C), lambda b, s: (l, 0, 0)),
        ],
        out_specs=[
            pl.BlockSpec((None, ncb, tm, V7X_LANES), lambda b, s: (b, 0, s, 0)),
            pl.BlockSpec((None, GDN_HEADS, tm, GDN_DV), lambda b, s: (b, 0, s, 0)),
            pl.BlockSpec((None, tm, ng), lambda b, s: (b, s, 0)),
            pl.BlockSpec((None, tm, nl), lambda b, s: (b, s, 0)),
            pl.BlockSpec((None, nb, tm), lambda b, s: (b, 0, s)),
        ],
        out_shape=[
            jax.ShapeDtypeStruct((B, ncb, S, V7X_LANES), BF16),
            jax.ShapeDtypeStruct((B, GDN_HEADS, S, GDN_DV), BF16),
            jax.ShapeDtypeStruct((B, S, ng), BF16),
            jax.ShapeDtypeStruct((B, S, nl), F32),
            jax.ShapeDtypeStruct((B, nb, S), F32),
        ],
        scratch_shapes=[pltpu.VMEM((tm + HALO, C), F32)],
        compiler_params=_params(vmem, 2),
        name="mixin",
    )(x, mod, gain, wqkv, wz, wg, wlat, wba, conv)


def _gate_kernel(ba_ref, alog_ref, dtb_ref, grow_ref, gcol_ref, *, tm):
    H = GDN_HEADS
    ba = ba_ref[...]
    beta = _sigmoid(ba[0:H])
    xa = ba[H:2 * H] + dtb_ref[...]
    softplus = jnp.maximum(xa, 0.0) + jnp.log(1.0 + jnp.exp(-jnp.abs(xa)))
    g = -jnp.exp(alog_ref[...]) * softplus

    ii = lax.broadcasted_iota(jnp.int32, (tm, tm), 0)
    jj = lax.broadcasted_iota(jnp.int32, (tm, tm), 1)
    same = (ii >> LOG2_CHUNK) == (jj >> LOG2_CHUNK)
    upper =jnp.where(same & (ii <= jj), 1.0, 0.0).astype(BF16)
    block = jnp.where(same, 1.0, 0.0).astype(BF16)
    eye = jnp.where(ii == jj, 1.0, 0.0).astype(BF16)
    gam = None
    glast = None
    for part in _split3(g):
        t1 = _dot(part, upper)
        t2 = _dot(part, block)
        gam = t1 if gam is None else gam + t1
        glast = t2 if glast is None else glast + t2
    egam = jnp.exp(gam)
    ekd = jnp.exp(glast - gam)
    egl = jnp.exp(glast)
    zero = jnp.zeros_like(gam[0:1])
    rows = []
    for hh in range(H):
        r = jnp.concatenate([beta[hh:hh + 1], gam[hh:hh + 1], egam[hh:hh + 1], ekd[hh:hh + 1],
                             egl[hh:hh + 1], zero, zero, zero], axis=0)
        grow_ref[hh] = r
        rows.append(r)
    allrows = jnp.concatenate(rows, axis=0)
    cols = None
    for part in _split3(allrows):
        t = _dot_nt(eye, part)
        cols = t if cols is None else cols + t
    for hh in range(H):
        gcol_ref[hh] = cols[:, 8 * hh:8 * hh + 8]


def _gdn_gates(ba, a_log, dt_bias, l):
    B, nb, S = ba.shape
    H = GDN_HEADS
    tm = min(S, GATE_TOKENS)
    vmem = 8 * tm * tm * 4 + 4 * H * tm * V7X_LANES * 4
    return pl.pallas_call(
        functools.partial(_gate_kernel, tm=tm),
        grid=(B, S // tm),
        in_specs=[
            pl.BlockSpec((None, nb, tm), lambda b, s: (b, 0, s)),
            pl.BlockSpec((None, H, 1), lambda b, s: (l, 0, 0)),
            pl.BlockSpec((None, H, 1), lambda b, s: (l, 0, 0)),
        ],
        out_specs=[
            pl.BlockSpec((None, H, 8, tm), lambda b, s: (b, 0, 0, s)),
            pl.BlockSpec((None, H, tm, 8), lambda b, s: (b, 0, s, 0)),
        ],
        out_shape=[
            jax.ShapeDtypeStruct((B, H, 8, S), F32),
            jax.ShapeDtypeStruct((B, H, S, 8), F32),
        ],
        compiler_params=_params(vmem, 2),
        name="gdn_gates",
    )(ba, a_log, dt_bias)


def _gdn_kernel(q_ref, k_ref, v_ref, zs_ref, grow_ref, gcol_ref, og_ref, o_ref, state_ref, *, tc, G):
    s = pl.program_id(2)

    @pl.when(s == 0)
    def _():
        state_ref[...] = jnp.zeros(state_ref.shape, F32)

    heads = range(G)
    n_chunks = tc // CHUNK
    scale = GDN_DK ** -0.5
    ii = lax.broadcasted_iota(jnp.int32, (tc, tc), 0)
    jj = lax.broadcasted_iota(jnp.int32, (tc, tc), 1)
    same = (ii >> LOG2_CHUNK) == (jj >> LOG2_CHUNK)
    causal = same & (ii >= jj)
    strict = same & (ii > jj)
    lane = lax.broadcasted_iota(jnp.int32, (GDN_DK, tc), 1) >> LOG2_CHUNK

    gc = [gcol_ref[g] for g in heads]
    gr = [grow_ref[g] for g in heads]
    k = [k_ref[g] for g in heads]
    kf = [k[g].astype(F32) for g in heads]
    kb = [kf[g] * gc[g][:, 0:1] for g in heads]
    decay = [jnp.where(causal, jnp.exp(jnp.minimum(gc[g][:, 1:2] - gr[g][1:2, :], 0.0)), 0.0) for g in heads]

    akq = [_dot_nt(jnp.concatenate([kb[g].astype(BF16), q_ref[g]], axis=0), k[g]) for g in heads]
    m = [jnp.where(strict, akq[g][:tc] * decay[g], 0.0) for g in heads]
    qk = [(akq[g][tc:] * (decay[g] * scale)).astype(BF16) for g in heads]

    rlane = lax.broadcasted_iota(jnp.int32, (CHUNK, tc), 1)
    rrow = lax.broadcasted_iota(jnp.int32, (CHUNK, tc), 0)
    eye_row = jnp.where((rlane & (CHUNK - 1)) == rrow, 1.0, 0.0)
    rblk = rlane >> LOG2_CHUNK

    def blockdiag(x_row):
        return jnp.concatenate([jnp.where(rblk == c, x_row, 0.0) for c in range(n_chunks)], axis=0).astype(BF16)

    def side_by_side(x):
        acc = x[0:CHUNK]
        for c in range(1, n_chunks):
            acc = acc + x[c * CHUNK:(c + 1) * CHUNK]
        return acc

    m_row = [side_by_side(m[g]) for g in heads]
    p = [_dot(m_row[g].astype(BF16), blockdiag(m_row[g])) for g in heads]
    t_row = [eye_row - m_row[g] for g in heads]
    for _ in range(LOG2_CHUNK - 2):
        r = [_dot(jnp.concatenate([p[g], t_row[g]], axis=0).astype(BF16), blockdiag(p[g])) for g in heads]
        p = [r[g][:CHUNK] for g in heads]
        t_row = [t_row[g] + r[g][CHUNK:] for g in heads]
    t_row = [t_row[g] + _dot(t_row[g].astype(BF16), blockdiag(p[g])) for g in heads]

    rhs = [jnp.concatenate([v_ref[g].astype(F32) * gc[g][:, 0:1], kb[g] * gc[g][:, 2:3]], axis=1).astype(BF16)
           for g in heads]
    uw = [_dot(blockdiag(t_row[g]), rhs[g]).astype(BF16) for g in heads]

    kdt = [(kf[g] * gc[g][:, 3:4]).T for g in heads]
    pre_lhs = [jnp.concatenate([jnp.where(lane == c, kdt[g], 0.0).astype(BF16) for c in range(n_chunks)]
                               + [qk[g]], axis=0) for g in heads]
    pre = [_dot(pre_lhs[g], uw[g]) for g in heads]
    qd = [q_ref[g].astype(F32) * (gc[g][:, 2:3] * scale) for g in heads]
    base = n_chunks * GDN_DK
    qmod = [(qd[g] - pre[g][base:, GDN_DV:]).astype(BF16) for g in heads]

    states = [state_ref[g] for g in heads]
    outs = [[] for _ in heads]
    for c in range(n_chunks):
        r0 = c * CHUNK
        for g in heads:
            kblk = pre[g][c * GDN_DK:(c + 1) * GDN_DK]
            lhs = jnp.concatenate([kblk[:, GDN_DV:].astype(BF16), qmod[g][r0:r0 + CHUNK]], axis=0)
            r = _dot(lhs, states[g].astype(BF16))
            outs[g].append(r[GDN_DK:] + pre[g][base + r0:base + r0 + CHUNK, :GDN_DV])
            states[g] = states[g] * gr[g][4:5, r0:r0 + 1] - r[:GDN_DK] + kblk[:, :GDN_DV]
    for g in heads:
        state_ref[g] = states[g]
        o = jnp.concatenate(outs[g], axis=0)
        o = o * lax.rsqrt(jnp.mean(o * o, axis=-1, keepdims=True) + EPS) * og_ref[...]
        o_ref[:, g * GDN_DV:(g + 1) * GDN_DV] = (o * zs_ref[g].astype(F32)).astype(BF16)


def _gdn(qkv, zs, grow, gcol, out_gain, l):
    B, _, S, _ = qkv.shape
    H = GDN_HEADS
    G = GDN_GROUP
    tc = min(S, GDN_TOKENS)
    vmem = G * (24 * tc * tc * 4 + 24 * tc * V7X_LANES * 4)
    return pl.pallas_call(
        functools.partial(_gdn_kernel, tc=tc, G=G),
        grid=(B, H // G, S // tc),
        in_specs=[
            pl.BlockSpec((None, G, tc, GDN_DK), lambda b, h, s: (b, h, s, 0)),
            pl.BlockSpec((None, G, tc, GDN_DK), lambda b, h, s: (b, H // G + h, s, 0)),
            pl.BlockSpec((None, G, tc, GDN_DV), lambda b, h, s: (b, 2 * H // G + h, s, 0)),
            pl.BlockSpec((None, G, tc, GDN_DV), lambda b, h, s: (b, h, s, 0)),
            pl.BlockSpec((None, G, 8, tc), lambda b, h, s: (b, h, 0, s)),
            pl.BlockSpec((None, G, tc, 8), lambda b, h, s: (b, h, s, 0)),
            pl.BlockSpec((None, 1, GDN_DV), lambda b, h, s: (l, 0, 0)),
        ],
        out_specs=pl.BlockSpec((None, tc, G * GDN_DV), lambda b, h, s: (b, s, h)),
        out_shape=jax.ShapeDtypeStruct((B, S, H * GDN_DV), BF16),
        scratch_shapes=[pltpu.VMEM((G, GDN_DK, GDN_DV), F32)],
        compiler_params=_params(vmem, 3),
        name="gdn",
    )(qkv, qkv, qkv, zs, grow, gcol, out_gain)


def _swap_halves(y):
    half = y.shape[-1] // 2
    return jnp.concatenate([y[:, half:], y[:, :half]], axis=1)


def _mla_prep_kernel(lat_ref, cos_ref, sin_ref, cost_ref, sint_ref, qg_ref, kvg_ref, wqt_ref, wk_ref, wvt_ref,
                     qnc_ref, kn_ref, qt_ref, k_ref, vt_ref):
    H = MLA_HEADS
    half = QK_ROPE // 2
    lat = lat_ref[...]
    ql = lat[:, :Q_LORA]
    kvl = lat[:, Q_LORA:Q_LORA + KV_LORA]
    kpe = lat[:, Q_LORA + KV_LORA:Q_LORA + KV_LORA + QK_ROPE]
    qln = (ql * lax.rsqrt(jnp.mean(ql * ql, axis=-1, keepdims=True) + EPS) * qg_ref[...]).astype(BF16)
    kvn = (kvl * lax.rsqrt(jnp.mean(kvl * kvl, axis=-1, keepdims=True) + EPS) * kvg_ref[...]).astype(BF16)
    qt = _dot_nt(wqt_ref[...], qln)
    knope = _dot(kvn, wk_ref[...])
    vt = _dot_nt(wvt_ref[...], kvn)

    cost = cost_ref[...]
    sint = sint_ref[...]
    qnc = qnc_ref[...]
    scale = MLA_QK ** -0.5 * math.log2(math.e)
    for hh in range(H):
        blk = qt[hh * MLA_QK:(hh + 1) * MLA_QK]
        r = lax.rsqrt(jnp.sum(blk * blk, axis=0, keepdims=True) * (1.0 / MLA_QK) + EPS) * scale
        y = blk * qnc * r
        y1 = y[QK_NOPE:QK_NOPE + half]
        y2 = y[QK_NOPE + half:]
        qt_ref[hh, 0:QK_NOPE, :] = y[:QK_NOPE].astype(BF16)
        qt_ref[hh, QK_NOPE:QK_NOPE + half, :] = (y1 * cost - y2 * sint).astype(BF16)
        qt_ref[hh, QK_NOPE + half:MLA_QK, :] = (y2 * cost + y1 * sint).astype(BF16)
        vt_ref[hh] = vt[hh * V_HEAD:(hh + 1) * V_HEAD].astype(BF16)

    kn = kn_ref[...]
    yk = kpe * kn[:, QK_NOPE:]
    kr = yk * cos_ref[...] + _swap_halves(yk) * sin_ref[...]
    kpe_ss = jnp.sum(kpe * kpe, axis=-1, keepdims=True)
    for hh in range(H):
        k_nope = knope[:, hh * QK_NOPE:(hh + 1) * QK_NOPE]
        ss = jnp.sum(k_nope * k_nope, axis=-1, keepdims=True) + kpe_ss
        r = lax.rsqrt(ss * (1.0 / MLA_QK) + EPS)
        k_ref[hh, :, 0:QK_NOPE] = (k_nope * r * kn[:, :QK_NOPE]).astype(BF16)
        k_ref[hh, :, QK_NOPE:MLA_QK] = (kr * r).astype(BF16)


def _mla_prep(lat, cos2, sin2, cost, sint, q_lat_gain, kv_lat_gain, wqt, wk, wvt, q_norm_col, k_norm, l):
    B, S, nl = lat.shape
    H = MLA_HEADS
    half = QK_ROPE // 2
    tm = min(S, ATTN_Q)
    nq = wqt.shape[1]
    nk = wk.shape[-1]
    nv = wvt.shape[1]
    vmem = ((nq * Q_LORA + (nk + nv) * KV_LORA) * 2 + 2 * tm * nl * 4 + 3 * tm * (nq + nk + nv) * 4
            + 4 * H * tm * (256 + MLA_QK + V_HEAD) * 2)
    return pl.pallas_call(
        _mla_prep_kernel,
        grid=(B, S // tm),
        in_specs=[
            pl.BlockSpec((None, tm, nl), lambda b, s: (b, s, 0)),
            pl.BlockSpec((None, tm, QK_ROPE), lambda b, s: (b, s, 0)),
            pl.BlockSpec((None, tm, QK_ROPE), lambda b, s: (b, s, 0)),
            pl.BlockSpec((None, half, tm), lambda b, s: (b, 0, s)),
            pl.BlockSpec((None, half, tm), lambda b, s: (b, 0, s)),
            pl.BlockSpec((None, 1, Q_LORA), lambda b, s: (l, 0, 0)),
            pl.BlockSpec((None, 1, KV_LORA), lambda b, s: (l, 0, 0)),
            _resident((None, nq, Q_LORA), lambda b, s: (l, 0, 0)),
            _resident((None, KV_LORA, nk), lambda b, s: (l, 0, 0)),
            _resident((None, nv, KV_LORA), lambda b, s: (l, 0, 0)),
            pl.BlockSpec((None, MLA_QK, 1), lambda b, s: (l, 0, 0)),
            pl.BlockSpec((None, 1, MLA_QK), lambda b, s: (l, 0, 0)),
        ],
        out_specs=[
            pl.BlockSpec((None, H, MLA_QK, tm), lambda b, s: (b, 0, 0, s)),
            pl.BlockSpec((None, H, tm, MLA_QK), lambda b, s: (b, 0, s, 0)),
            pl.BlockSpec((None, H, None, V_HEAD, tm), lambda b, s: (b, 0, s, 0, 0)),
        ],
        out_shape=[
            jax.ShapeDtypeStruct((B, H, MLA_QK, S), BF16),
            jax.ShapeDtypeStruct((B, H, S, MLA_QK), BF16),
            jax.ShapeDtypeStruct((B, H, S // tm, V_HEAD, tm), BF16),
        ],
        compiler_params=_params(vmem, 2),
        name="mla_prep",
    )(lat, cos2, sin2, cost, sint, q_lat_gain, kv_lat_gain, wqt, wk, wvt, q_norm_col, k_norm)


def _attn_kernel(qt_ref, k_ref, vt_ref, o_ref, *, tq):
    i = pl.program_id(2)
    qt = qt_ref[...]

    def scores(j):
        return _dot(k_ref[pl.ds(pl.multiple_of(j * tq, tq), tq), :], qt)

    def update(st, mx, j, m, l, acc):
        m_new = jnp.maximum(m, mx)
        alpha = jnp.exp2(m - m_new)
        p = jnp.exp2(st - m_new)
        l = alpha * l + jnp.sum(p, axis=0, keepdims=True)
        return m_new, l, alpha * acc + _dot(vt_ref[j], p.astype(BF16))

    def body(j, carry):
        st, mx, m, l, acc = carry
        st_next = scores(j + 1)
        mx_next = jnp.max(st_next, axis=0, keepdims=True)
        m, l, acc = update(st, mx, j, m, l, acc)
        return st_next, mx_next, m, l, acc

    st0 = scores(0)
    init = (st0, jnp.max(st0, axis=0, keepdims=True), jnp.full((1, tq), -jnp.inf, F32),
            jnp.zeros((1, tq), F32), jnp.zeros((V_HEAD, tq), F32))
    st, _, m, l, acc = lax.fori_loop(0, i, body, init)
    kpos = lax.broadcasted_iota(jnp.int32, (tq, tq), 0)
    qpos = lax.broadcasted_iota(jnp.int32, (tq, tq), 1)
    st = jnp.where(kpos <= qpos, st, -jnp.inf)
    m, l, acc = update(st, jnp.max(st, axis=0, keepdims=True), i, m, l, acc)
    o_ref[...] = (acc / l).T.astype(BF16)


def _attention(qt, k, vt):
    B, H, S, _ = k.shape
    tq = vt.shape[-1]
    vmem = 6 * tq * tq * 4 + 4 * S * (256 + V_HEAD) * 2 + 4 * tq * 256 * 2 + 8 * tq * V7X_LANES * 4
    return pl.pallas_call(
        functools.partial(_attn_kernel, tq=tq),
        grid=(B, H, S // tq),
        in_specs=[
            pl.BlockSpec((None, None, MLA_QK, tq), lambda b, h, i: (b, h, 0, i)),
            pl.BlockSpec((None, None, S, MLA_QK), lambda b, h, i: (b, h, 0, 0)),
            pl.BlockSpec((None, None, S // tq, V_HEAD, tq), lambda b, h, i: (b, h, 0, 0, 0)),
        ],
        out_specs=pl.BlockSpec((None, tq, V_HEAD), lambda b, h, i: (b, i, h)),
        out_shape=jax.ShapeDtypeStruct((B, S, H * V_HEAD), BF16),
        compiler_params=_params(vmem, 3),
        name="mla_attention",
    )(qt, k, vt)


def _merge_kernel(x_ref, mod_ref, oa_ref, ob_ref, gates_ref, wa_ref, wb_ref, wo_ref, o_ref):
    D = x_ref.shape[-1]
    mod = mod_ref[...]
    gates = gates_ref[...].astype(F32)
    ya = _dot(oa_ref[...], wa_ref[...])
    yb = _dot(ob_ref[...], wb_ref[...])
    y = (gates[:, :D] * ya + gates[:, D:] * yb).astype(BF16)
    o_ref[...] = x_ref[...] + (1.0 + mod[5:6]) * _dot(y, wo_ref[...])


def _merge(x, mod, oa, ob, gates, wa, wb, wo, l):
    B, S, D = x.shape
    tm = min(S, MERGE_TOKENS)
    na = oa.shape[-1]
    nbv = ob.shape[-1]
    vmem = (na + nbv + D) * D * 2 + 4 * tm * D * 4 + 2 * tm * (na + nbv + 2 * D) * 2 + 4 * tm * D * 4
    return pl.pallas_call(
        _merge_kernel,
        grid=(B, S // tm),
        in_specs=[
            pl.BlockSpec((None, tm, D), lambda b, s: (b, s, 0)),
            pl.BlockSpec((None, None, N_MOD, D), lambda b, s: (l, b, 0, 0)),
            pl.BlockSpec((None, tm, na), lambda b, s: (b, s, 0)),
            pl.BlockSpec((None, tm, nbv), lambda b, s: (b, s, 0)),
            pl.BlockSpec((None, tm, 2 * D), lambda b, s: (b, s, 0)),
            _resident((None, na, D), lambda b, s: (l, 0, 0)),
            _resident((None, nbv, D), lambda b, s: (l, 0, 0)),
            _resident((None, D, D), lambda b, s: (l, 0, 0)),
        ],
        out_specs=pl.BlockSpec((None, tm, D), lambda b, s: (b, s, 0)),
        out_shape=jax.ShapeDtypeStruct((B, S, D), F32),
        compiler_params=_params(vmem, 2),
        name="merge",
    )(x, mod, oa, ob, gates, wa, wb, wo)


def _pack_w_in(w_in):
    conv_c = 2 * GDN_HEADS * GDN_DK + GDN_HEADS * GDN_DV
    gdn_v = GDN_HEADS * GDN_DV
    D = w_in.shape[1]
    sizes = (conv_c, gdn_v, GDN_HEADS, GDN_HEADS, Q_LORA, KV_LORA, QK_ROPE, D, D)
    offs = np.cumsum((0,) + sizes)
    qkv, z, b_raw, a_raw, q_lat, kv_lat, k_pe, gate_a, gate_b = (
        w_in[:, :, int(offs[i]):int(offs[i + 1])] for i in range(len(sizes)))
    n_lat = Q_LORA + KV_LORA + QK_ROPE
    pad = (-n_lat) % V7X_LANES
    lat = jnp.concatenate([q_lat, kv_lat, k_pe, jnp.zeros(w_in.shape[:2] + (pad,), w_in.dtype)], axis=-1)
    ba = jnp.swapaxes(jnp.concatenate([b_raw, a_raw], axis=-1), 1, 2)
    return (qkv.astype(BF16), z.astype(BF16), jnp.concatenate([gate_a, gate_b], axis=-1).astype(BF16),
            lat.astype(BF16), ba.astype(BF16))


def _pack_kv_up(w_kv_up):
    L = w_kv_up.shape[0]
    w = w_kv_up.reshape(L, KV_LORA, MLA_HEADS, QK_NOPE + V_HEAD)
    wk = w[..., :QK_NOPE].reshape(L, KV_LORA, MLA_HEADS * QK_NOPE)
    wv = w[..., QK_NOPE:].reshape(L, KV_LORA, MLA_HEADS * V_HEAD)
    return wk.astype(BF16), jnp.swapaxes(wv, 1, 2).astype(BF16)


def kernel(x, c, positions, ada_w, ada_b, norm_ffn1, ffn1_w1, ffn1_w3, ffn1_w2, norm_mix, w_in, gdn_conv, gdn_a_log, gdn_dt_bias, gdn_out_gain, mla_q_lat_gain, mla_kv_lat_gain, mla_w_q_up, mla_w_kv_up, mla_q_norm, mla_k_norm, w_branch_a, w_branch_b, w_out, norm_ffn2, ffn2_w1, ffn2_w3, ffn2_w2):
    L = ada_w.shape[0]
    S = x.shape[1]
    assert S % CHUNK == 0 and S % min(S, GDN_TOKENS) == 0 and S % min(S, FFN_TOKENS) == 0
    assert S % min(S, MERGE_TOKENS) == 0 and S % min(S, ATTN_Q) == 0

    mod = _ada_mod(c, ada_w, ada_b)
    cos2, sin2, cost, sint = _rope_tables(positions)

    row = lambda a: a.reshape(L, 1, a.shape[-1])
    col = lambda a: a.reshape(L, a.shape[-1], 1)
    bf = lambda a: a.astype(BF16)
    wqkv, wz, wg, wlat, wba = _pack_w_in(w_in)
    wqt = bf(jnp.swapaxes(mla_w_q_up, 1, 2))
    wk, wvt = _pack_kv_up(mla_w_kv_up)
    f1 = (bf(ffn1_w1), bf(ffn1_w3), bf(ffn1_w2))
    f2 = (bf(ffn2_w1), bf(ffn2_w3), bf(ffn2_w2))
    wa, wb, wo = bf(w_branch_a), bf(w_branch_b), bf(w_out)
    g_ffn1, g_mix, g_ffn2 = row(norm_ffn1), row(norm_mix), row(norm_ffn2)
    og, qlg, kvlg = row(gdn_out_gain), row(mla_q_lat_gain), row(mla_kv_lat_gain)
    qnc, kn = col(mla_q_norm), row(mla_k_norm)
    a_log, dt_bias = col(gdn_a_log), col(gdn_dt_bias)

    for l in range(L):
        x = _ffn(x, mod, g_ffn1, *f1, l, 0)
        qkv, zs, gates, lat, ba = _mixin(x, mod, g_mix, wqkv, wz, wg, wlat, wba, gdn_conv, l)
        grow, gcol = _gdn_gates(ba, a_log, dt_bias, l)
        o_gdn = _gdn(qkv, zs, grow, gcol, og, l)
        qt, k, vt = _mla_prep(lat, cos2, sin2, cost, sint, qlg, kvlg, wqt, wk, wvt, qnc, kn, l)
        o_mla = _attention(qt, k, vt)
        x = _merge(x, mod, o_gdn, o_mla, gates, wa, wb, wo, l)
        x = _ffn(x, mod, g_ffn2, *f2, l, 6)
    return x
```

```python
import functools
import math

import numpy as np
import jax
import jax.numpy as jnp
from jax import lax
from jax.experimental import pallas as pl
from jax.experimental.pallas import tpu as pltpu

F32 = jnp.float32
BF16 = jnp.bfloat16

EPS = 1e-6
N_MOD = 9
GDN_HEADS = 8
GDN_DK = 128
GDN_DV = 128
CONV_K = 4
CHUNK = 64
LOG2_CHUNK = 6
MLA_HEADS = 8
Q_LORA = 384
KV_LORA = 256
QK_NOPE = 128
QK_ROPE = 64
V_HEAD = 128
ROPE_THETA = 10000.0
MLA_QK = QK_NOPE + QK_ROPE

V7X_VMEM_BYTES = 64 * 1024 * 1024
V7X_LANES = 128
V7X_SUBLANES = 8
VMEM_CAP_BYTES = V7X_VMEM_BYTES - 6 * 1024 * 1024

FFN_TOKENS = 512
MIX_TOKENS = 256
GATE_TOKENS = 512
GDN_TOKENS = 256
GDN_GROUP = 8
ATTN_Q = 512
MERGE_TOKENS = 1024
HALO = V7X_SUBLANES


def _params(vmem_bytes, n_axes):
    limit = int(min(VMEM_CAP_BYTES, max(vmem_bytes, 16 * 1024 * 1024)))
    return pltpu.CompilerParams(dimension_semantics=("arbitrary",) * n_axes, vmem_limit_bytes=limit)


def _resident(block_shape, index_map):
    return pl.BlockSpec(block_shape, index_map, pipeline_mode=pl.Buffered(1))


def _dot(a, b):
    return jnp.dot(a, b, preferred_element_type=F32)


def _dot_nt(a, b):
    return lax.dot_general(a, b, (((1,), (1,)), ((), ())), preferred_element_type=F32)


def _sigmoid(x):
    return 1.0 / (1.0 + jnp.exp(-x))


def _silu(x):
    return x * _sigmoid(x)


def _modulate(x, gain, shift, scale):
    y = x * lax.rsqrt(jnp.mean(x * x, axis=-1, keepdims=True) + EPS)
    return (y * gain) * (1.0 + scale) + shift


def _split3(x):
    x1 = x.astype(BF16)
    r1 = x - x1.astype(F32)
    x2 = r1.astype(BF16)
    x3 = (r1 - x2.astype(F32)).astype(BF16)
    return x1, x2, x3


def _mod_kernel(c_ref, w_ref, b_ref, o_ref):
    cond = _silu(c_ref[...]).astype(BF16)
    o_ref[...] = _dot(cond, w_ref[...].astype(BF16)) + b_ref[...]


def _ada_mod(c, ada_w, ada_b):
    L, D, _ = ada_w.shape
    B = c.shape[0]
    ada_b3 = ada_b.reshape(L, N_MOD, 1, D)
    out = pl.pallas_call(
        _mod_kernel,
        grid=(L, N_MOD),
        in_specs=[
            pl.BlockSpec((B, D), lambda l, j: (0, 0)),
            pl.BlockSpec((None, D, D), lambda l, j: (l, 0, j)),
            pl.BlockSpec((None, None, 1, D), lambda l, j: (l, j, 0, 0)),
        ],
        out_specs=pl.BlockSpec((None, None, B, D), lambda l, j: (l, j, 0, 0)),
        out_shape=jax.ShapeDtypeStruct((L, N_MOD, B, D), F32),
        compiler_params=_params(4 * D * D * 4, 2),
        name="ada_mod",
    )(c, ada_w, ada_b3)
    return jnp.transpose(out, (0, 2, 1, 3))


def _rope_kernel(pos_ref, post_ref, freq_ref, sign_ref, freqc_ref, cos_ref, sin_ref, cost_ref, sint_ref):
    ang = pos_ref[...].astype(F32) * freq_ref[...]
    cos_ref[...] = jnp.cos(ang)
    sin_ref[...] = jnp.sin(ang) * sign_ref[...]
    angt = freqc_ref[...] * post_ref[...].astype(F32)
    cost_ref[...] = jnp.cos(angt)
    sint_ref[...] = jnp.sin(angt)


def _rope_tables(positions):
    B, S = positions.shape
    half = QK_ROPE // 2
    inv_freq = ROPE_THETA ** (-jnp.arange(half, dtype=F32) / half)
    freq2 = jnp.concatenate([inv_freq, inv_freq]).reshape(1, QK_ROPE)
    sign2 = jnp.concatenate([-jnp.ones((half,), F32), jnp.ones((half,), F32)]).reshape(1, QK_ROPE)
    ts = min(S, 1024)
    return pl.pallas_call(
        _rope_kernel,
        grid=(B, S // ts),
        in_specs=[
            pl.BlockSpec((None, ts, 1), lambda b, s: (b, s, 0)),
            pl.BlockSpec((None, 1, ts), lambda b, s: (b, 0, s)),
            pl.BlockSpec((1, QK_ROPE), lambda b, s: (0, 0)),
            pl.BlockSpec((1, QK_ROPE), lambda b, s: (0, 0)),
            pl.BlockSpec((half, 1), lambda b, s: (0, 0)),
        ],
        out_specs=[pl.BlockSpec((None, ts, QK_ROPE), lambda b, s: (b, s, 0))] * 2
        + [pl.BlockSpec((None, half, ts), lambda b, s: (b, 0, s))] * 2,
        out_shape=[jax.ShapeDtypeStruct((B, S, QK_ROPE), F32)] * 2
        + [jax.ShapeDtypeStruct((B, half, S), F32)] * 2,
        compiler_params=_params(8 * ts * V7X_LANES * 4, 2),
        name="rope_tables",
    )(positions.reshape(B, S, 1), positions.reshape(B, 1, S), freq2, sign2, inv_freq.reshape(half, 1))


def _ffn_kernel(x_ref, mod_ref, gain_ref, w1_ref, w3_ref, w2_ref, o_ref, *, row):
    x = x_ref[...]
    mod = mod_ref[...]
    h = _modulate(x, gain_ref[...], mod[row:row + 1], mod[row + 1:row + 2]).astype(BF16)
    a = _dot(h, w1_ref[...])
    b = _dot(h, w3_ref[...])
    g = (_silu(a) * b).astype(BF16)
    y = _dot(g, w2_ref[...])
    o_ref[...] = x + (0.5 * (1.0 + mod[row + 2:row + 3])) * y


def _ffn(x, mod, gain, w1, w3, w2, l, row):
    B, S, D = x.shape
    F = w1.shape[-1]
    tm = min(S, FFN_TOKENS)
    vmem = 3 * D * F * 2 + 4 * tm * D * 4 + 3 * tm * F * 4 + 2 * tm * D * 4
    return pl.pallas_call(
        functools.partial(_ffn_kernel, row=row),
        grid=(B, S // tm),
        in_specs=[
            pl.BlockSpec((None, tm, D), lambda b, s: (b, s, 0)),
            pl.BlockSpec((None, None, N_MOD, D), lambda b, s: (l, b, 0, 0)),
            pl.BlockSpec((None, 1, D), lambda b, s: (l, 0, 0)),
            _resident((None, D, F), lambda b, s: (l, 0, 0)),
            _resident((None, D, F), lambda b, s: (l, 0, 0)),
            _resident((None, F, D), lambda b, s: (l, 0, 0)),
        ],
        out_specs=pl.BlockSpec((None, tm, D), lambda b, s: (b, s, 0)),
        out_shape=jax.ShapeDtypeStruct((B, S, D), F32),
        compiler_params=_params(vmem, 2),
        name="ffn",
    )(x, mod, gain, w1, w3, w2)


def _mixin_kernel(x_ref, mod_ref, gain_ref, wqkv_ref, wz_ref, wg_ref, wlat_ref, wba_ref, conv_ref,
                  qkv_ref, zs_ref, gates_ref, lat_ref, ba_ref, pbuf_ref, *, tm):
    s = pl.program_id(1)
    x = x_ref[...]
    mod = mod_ref[...]
    h = _modulate(x, gain_ref[...], mod[3:4], mod[4:5]).astype(BF16)

    @pl.when(s == 0)
    def _():
        pbuf_ref[0:HALO, :] = jnp.zeros((HALO, pbuf_ref.shape[1]), F32)

    pbuf_ref[HALO:HALO + tm, :] = _dot(h, wqkv_ref[...])
    conv = conv_ref[...]
    n_qk = 2 * GDN_HEADS
    for cb in range(pbuf_ref.shape[1] // V7X_LANES):
        c0 = cb * V7X_LANES
        xs = pbuf_ref[:, c0:c0 + V7X_LANES]
        acc = conv[CONV_K - 1:CONV_K, c0:c0 + V7X_LANES] * xs[HALO:]
        for d in range(1, CONV_K):
            shifted = pltpu.roll(xs, d, axis=0)[HALO:]
            acc = acc + conv[CONV_K - 1 - d:CONV_K - d, c0:c0 + V7X_LANES] * shifted
        y = _silu(acc)
        if cb < n_qk:
            y = y * lax.rsqrt(jnp.sum(y * y, axis=-1, keepdims=True) + EPS)
        qkv_ref[cb] = y.astype(BF16)
    pbuf_ref[0:HALO, :] = pbuf_ref[tm:tm + HALO, :]

    z = _dot(h, wz_ref[...])
    for hh in range(GDN_HEADS):
        zs_ref[hh] = z[:, hh * GDN_DV:(hh + 1) * GDN_DV].astype(BF16)
    gates_ref[...] = _dot(h, wg_ref[...]).astype(BF16)
    lat_ref[...] = _dot(h, wlat_ref[...])
    ba_ref[...] = _dot_nt(wba_ref[...], h)


def _mixin(x, mod, gain, wqkv, wz, wg, wlat, wba, conv, l):
    B, S, D = x.shape
    C = wqkv.shape[-1]
    nz = wz.shape[-1]
    ng = wg.shape[-1]
    nl = wlat.shape[-1]
    nb = wba.shape[1]
    tm = min(S, MIX_TOKENS)
    ncb = C // V7X_LANES
    vmem = ((C + nz + ng + nl) * D * 2 + 2 * tm * D * 4 + 2 * tm * (C + nz + ng) * 2 + 2 * tm * nl * 4
            + (tm + HALO) * C * 4 + tm * C * 4 + tm * (nz + ng) * 4 + 4 * 1024 * 1024)
    return pl.pallas_call(
        functools.partial(_mixin_kernel, tm=tm),
        grid=(B, S // tm),
        in_specs=[
            pl.BlockSpec((None, tm, D), lambda b, s: (b, s, 0)),
            pl.BlockSpec((None, None, N_MOD, D), lambda b, s: (l, b, 0, 0)),
            pl.BlockSpec((None, 1, D), lambda b, s: (l, 0, 0)),
            _resident((None, D, C), lambda b, s: (l, 0, 0)),
            _resident((None, D, nz), lambda b, s: (l, 0, 0)),
            _resident((None, D, ng), lambda b, s: (l, 0, 0)),
            _resident((None, D, nl), lambda b, s: (l, 0, 0)),
            _resident((None, nb, D), lambda b, s: (l, 0, 0)),
            pl.BlockSpec((None, CONV_K, C), lambda b, s: (l, 0, 0)),
        ],
        out_specs=[
            pl.BlockSpec((None, ncb, tm, V7X_LANES), lambda b, s: (b, 0, s, 0)),
            pl.BlockSpec((None, GDN_HEADS, tm, GDN_DV), lambda b, s: (b, 0, s, 0)),
            pl.BlockSpec((None, tm, ng), lambda b, s: (b, s, 0)),
            pl.BlockSpec((None, tm, nl), lambda b, s: (b, s, 0)),
            pl.BlockSpec((None, nb, tm), lambda b, s: (b, 0, s)),
        ],
        out_shape=[
            jax.ShapeDtypeStruct((B, ncb, S, V7X_LANES), BF16),
            jax.ShapeDtypeStruct((B, GDN_HEADS, S, GDN_DV), BF16),
            jax.ShapeDtypeStruct((B, S, ng), BF16),
            jax.ShapeDtypeStruct((B, S, nl), F32),
            jax.ShapeDtypeStruct((B, nb, S), F32),
        ],
        scratch_shapes=[pltpu.VMEM((tm + HALO, C), F32)],
        compiler_params=_params(vmem, 2),
        name="mixin",
    )(x, mod, gain, wqkv, wz, wg, wlat, wba, conv)


def _gate_kernel(ba_ref, alog_ref, dtb_ref, grow_ref, gcol_ref, *, tm):
    H = GDN_HEADS
    ba = ba_ref[...]
    beta = _sigmoid(ba[0:H])
    xa = ba[H:2 * H] + dtb_ref[...]
    softplus = jnp.maximum(xa, 0.0) + jnp.log(1.0 + jnp.exp(-jnp.abs(xa)))
    g = -jnp.exp(alog_ref[...]) * softplus

    ii = lax.broadcasted_iota(jnp.int32, (tm, tm), 0)
    jj = lax.broadcasted_iota(jnp.int32, (tm, tm), 1)
    same = (ii >> LOG2_CHUNK) == (jj >> LOG2_CHUNK)
    upper =jnp.where(same & (ii <= jj), 1.0, 0.0).astype(BF16)
    block = jnp.where(same, 1.0, 0.0).astype(BF16)
    eye = jnp.where(ii == jj, 1.0, 0.0).astype(BF16)
    gam = None
    glast = None
    for part in _split3(g):
        t1 = _dot(part, upper)
        t2 = _dot(part, block)
        gam = t1 if gam is None else gam + t1
        glast = t2 if glast is None else glast + t2
    egam = jnp.exp(gam)
    ekd = jnp.exp(glast - gam)
    egl = jnp.exp(glast)
    zero = jnp.zeros_like(gam[0:1])
    rows = []
    for hh in range(H):
        r = jnp.concatenate([beta[hh:hh + 1], gam[hh:hh + 1], egam[hh:hh + 1], ekd[hh:hh + 1],
                             egl[hh:hh + 1], zero, zero, zero], axis=0)
        grow_ref[hh] = r
        rows.append(r)
    allrows = jnp.concatenate(rows, axis=0)
    cols = None
    for part in _split3(allrows):
        t = _dot_nt(eye, part)
        cols = t if cols is None else cols + t
    for hh in range(H):
        gcol_ref[hh] = cols[:, 8 * hh:8 * hh + 8]


def _gdn_gates(ba, a_log, dt_bias, l):
    B, nb, S = ba.shape
    H = GDN_HEADS
    tm = min(S, GATE_TOKENS)
    vmem = 8 * tm * tm * 4 + 4 * H * tm * V7X_LANES * 4
    return pl.pallas_call(
        functools.partial(_gate_kernel, tm=tm),
        grid=(B, S // tm),
        in_specs=[
            pl.BlockSpec((None, nb, tm), lambda b, s: (b, 0, s)),
            pl.BlockSpec((None, H, 1), lambda b, s: (l, 0, 0)),
            pl.BlockSpec((None, H, 1), lambda b, s: (l, 0, 0)),
        ],
        out_specs=[
            pl.BlockSpec((None, H, 8, tm), lambda b, s: (b, 0, 0, s)),
            pl.BlockSpec((None, H, tm, 8), lambda b, s: (b, 0, s, 0)),
        ],
        out_shape=[
            jax.ShapeDtypeStruct((B, H, 8, S), F32),
            jax.ShapeDtypeStruct((B, H, S, 8), F32),
        ],
        compiler_params=_params(vmem, 2),
        name="gdn_gates",
    )(ba, a_log, dt_bias)


def _gdn_kernel(q_ref, k_ref, v_ref, zs_ref, grow_ref, gcol_ref, og_ref, o_ref, state_ref, *, tc, G):
    s = pl.program_id(2)

    @pl.when(s == 0)
    def _():
        state_ref[...] = jnp.zeros(state_ref.shape, F32)

    heads = range(G)
    n_chunks = tc // CHUNK
    scale = GDN_DK ** -0.5
    ii = lax.broadcasted_iota(jnp.int32, (tc, tc), 0)
    jj = lax.broadcasted_iota(jnp.int32, (tc, tc), 1)
    same = (ii >> LOG2_CHUNK) == (jj >> LOG2_CHUNK)
    causal = same & (ii >= jj)
    strict = same & (ii > jj)
    lane = lax.broadcasted_iota(jnp.int32, (GDN_DK, tc), 1) >> LOG2_CHUNK

    gc = [gcol_ref[g] for g in heads]
    gr = [grow_ref[g] for g in heads]
    k = [k_ref[g] for g in heads]
    kf = [k[g].astype(F32) for g in heads]
    kb = [kf[g] * gc[g][:, 0:1] for g in heads]
    decay = [jnp.where(causal, jnp.exp(jnp.minimum(gc[g][:, 1:2] - gr[g][1:2, :], 0.0)), 0.0) for g in heads]

    akq = [_dot_nt(jnp.concatenate([kb[g].astype(BF16), q_ref[g]], axis=0), k[g]) for g in heads]
    m = [jnp.where(strict, akq[g][:tc] * decay[g], 0.0) for g in heads]
    qk = [(akq[g][tc:] * (decay[g] * scale)).astype(BF16) for g in heads]

    rlane = lax.broadcasted_iota(jnp.int32, (CHUNK, tc), 1)
    rrow = lax.broadcasted_iota(jnp.int32, (CHUNK, tc), 0)
    eye_row = jnp.where((rlane & (CHUNK - 1)) == rrow, 1.0, 0.0)
    rblk = rlane >> LOG2_CHUNK

    def blockdiag(x_row):
        return jnp.concatenate([jnp.where(rblk == c, x_row, 0.0) for c in range(n_chunks)], axis=0).astype(BF16)

    def side_by_side(x):
        acc = x[0:CHUNK]
        for c in range(1, n_chunks):
            acc = acc + x[c * CHUNK:(c + 1) * CHUNK]
        return acc

    m_row = [side_by_side(m[g]) for g in heads]
    p = [_dot(m_row[g].astype(BF16), blockdiag(m_row[g])) for g in heads]
    t_row = [eye_row - m_row[g] for g in heads]
    for _ in range(LOG2_CHUNK - 2):
        r = [_dot(jnp.concatenate([p[g], t_row[g]], axis=0).astype(BF16), blockdiag(p[g])) for g in heads]
        p = [r[g][:CHUNK] for g in heads]
        t_row = [t_row[g] + r[g][CHUNK:] for g in heads]
    t_row = [t_row[g] + _dot(t_row[g].astype(BF16), blockdiag(p[g])) for g in heads]

    rhs = [jnp.concatenate([v_ref[g].astype(F32) * gc[g][:, 0:1], kb[g] * gc[g][:, 2:3]], axis=1).astype(BF16)
           for g in heads]
    uw = [_dot(blockdiag(t_row[g]), rhs[g]).astype(BF16) for g in heads]

    kdt = [(kf[g] * gc[g][:, 3:4]).T for g in heads]
    pre_lhs = [jnp.concatenate([jnp.where(lane == c, kdt[g], 0.0).astype(BF16) for c in range(n_chunks)]
                               + [qk[g]], axis=0) for g in heads]
    pre = [_dot(pre_lhs[g], uw[g]) for g in heads]
    qd = [q_ref[g].astype(F32) * (gc[g][:, 2:3] * scale) for g in heads]
    base = n_chunks * GDN_DK
    qmod = [(qd[g] - pre[g][base:, GDN_DV:]).astype(BF16) for g in heads]

    states = [state_ref[g] for g in heads]
    outs = [[] for _ in heads]
    for c in range(n_chunks):
        r0 = c * CHUNK
        for g in heads:
            kblk = pre[g][c * GDN_DK:(c + 1) * GDN_DK]
            lhs = jnp.concatenate([kblk[:, GDN_DV:].astype(BF16), qmod[g][r0:r0 + CHUNK]], axis=0)
            r = _dot(lhs, states[g].astype(BF16))
            outs[g].append(r[GDN_DK:] + pre[g][base + r0:base + r0 + CHUNK, :GDN_DV])
            states[g] = states[g] * gr[g][4:5, r0:r0 + 1] - r[:GDN_DK] + kblk[:, :GDN_DV]
    for g in heads:
        state_ref[g] = states[g]
        o = jnp.concatenate(outs[g], axis=0)
        o = o * lax.rsqrt(jnp.mean(o * o, axis=-1, keepdims=True) + EPS) * og_ref[...]
        o_ref[:, g * GDN_DV:(g + 1) * GDN_DV] = (o * _silu(zs_ref[g].astype(F32))).astype(BF16)


def _gdn(qkv, zs, grow, gcol, out_gain, l):
    B, _, S, _ = qkv.shape
    H = GDN_HEADS
    G = GDN_GROUP
    tc = min(S, GDN_TOKENS)
    vmem = G * (24 * tc * tc * 4 + 24 * tc * V7X_LANES * 4)
    return pl.pallas_call(
        functools.partial(_gdn_kernel, tc=tc, G=G),
        grid=(B, H // G, S // tc),
        in_specs=[
            pl.BlockSpec((None, G, tc, GDN_DK), lambda b, h, s: (b, h, s, 0)),
            pl.BlockSpec((None, G, tc, GDN_DK), lambda b, h, s: (b, H // G + h, s, 0)),
            pl.BlockSpec((None, G, tc, GDN_DV), lambda b, h, s: (b, 2 * H // G + h, s, 0)),
            pl.BlockSpec((None, G, tc, GDN_DV), lambda b, h, s: (b, h, s, 0)),
            pl.BlockSpec((None, G, 8, tc), lambda b, h, s: (b, h, 0, s)),
            pl.BlockSpec((None, G, tc, 8), lambda b, h, s: (b, h, s, 0)),
            pl.BlockSpec((None, 1, GDN_DV), lambda b, h, s: (l, 0, 0)),
        ],
        out_specs=pl.BlockSpec((None, tc, G * GDN_DV), lambda b, h, s: (b, s, h)),
        out_shape=jax.ShapeDtypeStruct((B, S, H * GDN_DV), BF16),
        scratch_shapes=[pltpu.VMEM((G, GDN_DK, GDN_DV), F32)],
        compiler_params=_params(vmem, 3),
        name="gdn",
    )(qkv, qkv, qkv, zs, grow, gcol, out_gain)


def _swap_halves(y):
    half = y.shape[-1] // 2
    return jnp.concatenate([y[:, half:], y[:, :half]], axis=1)


def _mla_prep_kernel(lat_ref, cos_ref, sin_ref, cost_ref, sint_ref, qg_ref, kvg_ref, wqt_ref, wk_ref, wvt_ref,
                     qnc_ref, kn_ref, qt_ref, k_ref, vt_ref):
    H = MLA_HEADS
    half = QK_ROPE // 2
    lat = lat_ref[...]
    ql = lat[:, :Q_LORA]
    kvl = lat[:, Q_LORA:Q_LORA + KV_LORA]
    kpe = lat[:, Q_LORA + KV_LORA:Q_LORA + KV_LORA + QK_ROPE]
    qln = (ql * lax.rsqrt(jnp.mean(ql * ql, axis=-1, keepdims=True) + EPS) * qg_ref[...]).astype(BF16)
    kvn = (kvl * lax.rsqrt(jnp.mean(kvl * kvl, axis=-1, keepdims=True) + EPS) * kvg_ref[...]).astype(BF16)
    qt = _dot_nt(wqt_ref[...], qln)
    knope = _dot(kvn, wk_ref[...])
    vt = _dot_nt(wvt_ref[...], kvn)

    cost = cost_ref[...]
    sint = sint_ref[...]
    qnc = qnc_ref[...]
    scale = MLA_QK ** -0.5 * math.log2(math.e)
    for hh in range(H):
        blk = qt[hh * MLA_QK:(hh + 1) * MLA_QK]
        r = lax.rsqrt(jnp.sum(blk * blk, axis=0, keepdims=True) * (1.0 / MLA_QK) + EPS) * scale
        y = blk * qnc * r
        y1 = y[QK_NOPE:QK_NOPE + half]
        y2 = y[QK_NOPE + half:]
        qt_ref[hh, 0:QK_NOPE, :] = y[:QK_NOPE].astype(BF16)
        qt_ref[hh, QK_NOPE:QK_NOPE + half, :] = (y1 * cost - y2 * sint).astype(BF16)
        qt_ref[hh, QK_NOPE + half:MLA_QK, :] = (y2 * cost + y1 * sint).astype(BF16)
        vt_ref[hh] = vt[hh * V_HEAD:(hh + 1) * V_HEAD].astype(BF16)

    kn = kn_ref[...]
    yk = kpe * kn[:, QK_NOPE:]
    kr = yk * cos_ref[...] + _swap_halves(yk) * sin_ref[...]
    kpe_ss = jnp.sum(kpe * kpe, axis=-1, keepdims=True)
    for hh in range(H):
        k_nope = knope[:, hh * QK_NOPE:(hh + 1) * QK_NOPE]
        ss = jnp.sum(k_nope * k_nope, axis=-1, keepdims=True) + kpe_ss
        r = lax.rsqrt(ss * (1.0 / MLA_QK) + EPS)
        k_ref[hh, :, 0:QK_NOPE] = (k_nope * r * kn[:, :QK_NOPE]).astype(BF16)
        k_ref[hh, :, QK_NOPE:MLA_QK] = (kr * r).astype(BF16)


def _mla_prep(lat, cos2, sin2, cost, sint, q_lat_gain, kv_lat_gain, wqt, wk, wvt, q_norm_col, k_norm, l):
    B, S, nl = lat.shape
    H = MLA_HEADS
    half = QK_ROPE // 2
    tm = min(S, ATTN_Q)
    nq = wqt.shape[1]
    nk = wk.shape[-1]
    nv = wvt.shape[1]
    vmem = ((nq * Q_LORA + (nk + nv) * KV_LORA) * 2 + 2 * tm * nl * 4 + 3 * tm * (nq + nk + nv) * 4
            + 4 * H * tm * (256 + MLA_QK + V_HEAD) * 2)
    return pl.pallas_call(
        _mla_prep_kernel,
        grid=(B, S // tm),
        in_specs=[
            pl.BlockSpec((None, tm, nl), lambda b, s: (b, s, 0)),
            pl.BlockSpec((None, tm, QK_ROPE), lambda b, s: (b, s, 0)),
            pl.BlockSpec((None, tm, QK_ROPE), lambda b, s: (b, s, 0)),
            pl.BlockSpec((None, half, tm), lambda b, s: (b, 0, s)),
            pl.BlockSpec((None, half, tm), lambda b, s: (b, 0, s)),
            pl.BlockSpec((None, 1, Q_LORA), lambda b, s: (l, 0, 0)),
            pl.BlockSpec((None, 1, KV_LORA), lambda b, s: (l, 0, 0)),
            _resident((None, nq, Q_LORA), lambda b, s: (l, 0, 0)),
            _resident((None, KV_LORA, nk), lambda b, s: (l, 0, 0)),
            _resident((None, nv, KV_LORA), lambda b, s: (l, 0, 0)),
            pl.BlockSpec((None, MLA_QK, 1), lambda b, s: (l, 0, 0)),
            pl.BlockSpec((None, 1, MLA_QK), lambda b, s: (l, 0, 0)),
        ],
        out_specs=[
            pl.BlockSpec((None, H, MLA_QK, tm), lambda b, s: (b, 0, 0, s)),
            pl.BlockSpec((None, H, tm, MLA_QK), lambda b, s: (b, 0, s, 0)),
            pl.BlockSpec((None, H, None, V_HEAD, tm), lambda b, s: (b, 0, s, 0, 0)),
        ],
        out_shape=[
            jax.ShapeDtypeStruct((B, H, MLA_QK, S), BF16),
            jax.ShapeDtypeStruct((B, H, S, MLA_QK), BF16),
            jax.ShapeDtypeStruct((B, H, S // tm, V_HEAD, tm), BF16),
        ],
        compiler_params=_params(vmem, 2),
        name="mla_prep",
    )(lat, cos2, sin2, cost, sint, q_lat_gain, kv_lat_gain, wqt, wk, wvt, q_norm_col, k_norm)


def _attn_kernel(qt_ref, k_ref, vt_ref, o_ref, *, tq):
    i = pl.program_id(2)
    qt = qt_ref[...]

    def scores(j):
        return _dot(k_ref[pl.ds(pl.multiple_of(j * tq, tq), tq), :], qt)

    def update(st, mx, j, m, l, acc):
        m_new = jnp.maximum(m, mx)
        alpha = jnp.exp2(m - m_new)
        p = jnp.exp2(st - m_new)
        l = alpha * l + jnp.sum(p, axis=0, keepdims=True)
        return m_new, l, alpha * acc + _dot(vt_ref[j], p.astype(BF16))

    def body(j, carry):
        st, mx, m, l, acc = carry
        st_next = scores(j + 1)
        mx_next = jnp.max(st_next, axis=0, keepdims=True)
        m, l, acc = update(st, mx, j, m, l, acc)
        return st_next, mx_next, m, l, acc

    st0 = scores(0)
    init = (st0, jnp.max(st0, axis=0, keepdims=True), jnp.full((1, tq), -jnp.inf, F32),
            jnp.zeros((1, tq), F32), jnp.zeros((V_HEAD, tq), F32))
    st, _, m, l, acc = lax.fori_loop(0, i, body, init)
    kpos = lax.broadcasted_iota(jnp.int32, (tq, tq), 0)
    qpos = lax.broadcasted_iota(jnp.int32, (tq, tq), 1)
    st = jnp.where(kpos <= qpos, st, -jnp.inf)
    m, l, acc = update(st, jnp.max(st, axis=0, keepdims=True), i, m, l, acc)
    o_ref[...] = (acc / l).T.astype(BF16)


def _attention(qt, k, vt):
    B, H, S, _ = k.shape
    tq = vt.shape[-1]
    vmem = 6 * tq * tq * 4 + 4 * S * (256 + V_HEAD) * 2 + 4 * tq * 256 * 2 + 8 * tq * V7X_LANES * 4
    return pl.pallas_call(
        functools.partial(_attn_kernel, tq=tq),
        grid=(B, H, S // tq),
        in_specs=[
            pl.BlockSpec((None, None, MLA_QK, tq), lambda b, h, i: (b, h, 0, i)),
            pl.BlockSpec((None, None, S, MLA_QK), lambda b, h, i: (b, h, 0, 0)),
            pl.BlockSpec((None, None, S // tq, V_HEAD, tq), lambda b, h, i: (b, h, 0, 0, 0)),
        ],
        out_specs=pl.BlockSpec((None, tq, V_HEAD), lambda b, h, i: (b, i, h)),
        out_shape=jax.ShapeDtypeStruct((B, S, H * V_HEAD), BF16),
        compiler_params=_params(vmem, 3),
        name="mla_attention",
    )(qt, k, vt)


def _merge_kernel(x_ref, mod_ref, oa_ref, ob_ref, gates_ref, wa_ref, wb_ref, wo_ref, o_ref):
    D = x_ref.shape[-1]
    mod = mod_ref[...]
    gates = _sigmoid(gates_ref[...].astype(F32))
    ya =_dot(oa_ref[...], wa_ref[...])
    yb = _dot(ob_ref[...], wb_ref[...])
    y = (gates[:, :D] * ya + gates[:, D:] * yb).astype(BF16)
    o_ref[...] = x_ref[...] + (1.0 + mod[5:6]) * _dot(y, wo_ref[...])


def _merge(x, mod, oa, ob, gates, wa, wb, wo, l):
    B, S, D = x.shape
    tm = min(S, MERGE_TOKENS)
    na = oa.shape[-1]
    nbv = ob.shape[-1]
    vmem = (na + nbv + D) * D * 2 + 4 * tm * D * 4 + 2 * tm * (na + nbv + 2 * D) * 2 + 4 * tm * D * 4
    return pl.pallas_call(
        _merge_kernel,
        grid=(B, S // tm),
        in_specs=[
            pl.BlockSpec((None, tm, D), lambda b, s: (b, s, 0)),
            pl.BlockSpec((None, None, N_MOD, D), lambda b, s: (l, b, 0, 0)),
            pl.BlockSpec((None, tm, na), lambda b, s: (b, s, 0)),
            pl.BlockSpec((None, tm, nbv), lambda b, s: (b, s, 0)),
            pl.BlockSpec((None, tm, 2 * D), lambda b, s: (b, s, 0)),
            _resident((None, na, D), lambda b, s: (l, 0, 0)),
            _resident((None, nbv, D), lambda b, s: (l, 0, 0)),
            _resident((None, D, D), lambda b, s: (l, 0, 0)),
        ],
        out_specs=pl.BlockSpec((None, tm, D), lambda b, s: (b, s, 0)),
        out_shape=jax.ShapeDtypeStruct((B, S, D), F32),
        compiler_params=_params(vmem, 2),
        name="merge",
    )(x, mod, oa, ob, gates, wa, wb, wo)


def _pack_w_in(w_in):
    conv_c = 2 * GDN_HEADS * GDN_DK + GDN_HEADS * GDN_DV
    gdn_v = GDN_HEADS * GDN_DV
    D = w_in.shape[1]
    sizes = (conv_c, gdn_v, GDN_HEADS, GDN_HEADS, Q_LORA, KV_LORA, QK_ROPE, D, D)
    offs = np.cumsum((0,) + sizes)
    qkv, z, b_raw, a_raw, q_lat, kv_lat, k_pe, gate_a, gate_b = (
        w_in[:, :, int(offs[i]):int(offs[i + 1])] for i in range(len(sizes)))
    n_lat = Q_LORA + KV_LORA + QK_ROPE
    pad = (-n_lat) % V7X_LANES
    lat = jnp.concatenate([q_lat, kv_lat, k_pe, jnp.zeros(w_in.shape[:2] + (pad,), w_in.dtype)], axis=-1)
    ba = jnp.swapaxes(jnp.concatenate([b_raw, a_raw], axis=-1), 1, 2)
    return (qkv.astype(BF16), z.astype(BF16), jnp.concatenate([gate_a, gate_b], axis=-1).astype(BF16),
            lat.astype(BF16), ba.astype(BF16))


def _pack_kv_up(w_kv_up):
    L = w_kv_up.shape[0]
    w = w_kv_up.reshape(L, KV_LORA, MLA_HEADS, QK_NOPE + V_HEAD)
    wk = w[..., :QK_NOPE].reshape(L, KV_LORA, MLA_HEADS * QK_NOPE)
    wv = w[..., QK_NOPE:].reshape(L, KV_LORA, MLA_HEADS * V_HEAD)
    return wk.astype(BF16), jnp.swapaxes(wv, 1, 2).astype(BF16)


def kernel(x, c, positions, ada_w, ada_b, norm_ffn1, ffn1_w1, ffn1_w3, ffn1_w2, norm_mix, w_in, gdn_conv, gdn_a_log, gdn_dt_bias, gdn_out_gain, mla_q_lat_gain, mla_kv_lat_gain, mla_w_q_up, mla_w_kv_up, mla_q_norm, mla_k_norm, w_branch_a, w_branch_b, w_out, norm_ffn2, ffn2_w1, ffn2_w3, ffn2_w2):
    L = ada_w.shape[0]
    S = x.shape[1]
    assert S % CHUNK == 0 and S % min(S, GDN_TOKENS) == 0 and S % min(S, FFN_TOKENS) == 0
    assert S % min(S, MERGE_TOKENS) == 0 and S % min(S, ATTN_Q) == 0

    mod = _ada_mod(c, ada_w, ada_b)
    cos2, sin2, cost, sint = _rope_tables(positions)

    row = lambda a: a.reshape(L, 1, a.shape[-1])
    col = lambda a: a.reshape(L, a.shape[-1], 1)
    bf = lambda a: a.astype(BF16)
    wqkv, wz, wg, wlat, wba = _pack_w_in(w_in)
    wqt = bf(jnp.swapaxes(mla_w_q_up, 1, 2))
    wk, wvt = _pack_kv_up(mla_w_kv_up)
    f1 = (bf(ffn1_w1), bf(ffn1_w3), bf(ffn1_w2))
    f2 = (bf(ffn2_w1), bf(ffn2_w3), bf(ffn2_w2))
    wa, wb, wo = bf(w_branch_a), bf(w_branch_b), bf(w_out)
    g_ffn1, g_mix, g_ffn2 = row(norm_ffn1), row(norm_mix), row(norm_ffn2)
    og, qlg, kvlg = row(gdn_out_gain), row(mla_q_lat_gain), row(mla_kv_lat_gain)
    qnc, kn = col(mla_q_norm), row(mla_k_norm)
    a_log, dt_bias = col(gdn_a_log), col(gdn_dt_bias)

    for l in range(L):
        x = _ffn(x, mod, g_ffn1, *f1, l, 0)
        qkv, zs, gates, lat, ba = _mixin(x, mod, g_mix, wqkv, wz, wg, wlat, wba, gdn_conv, l)
        grow, gcol = _gdn_gates(ba, a_log, dt_bias, l)
        o_gdn = _gdn(qkv, zs, grow, gcol, og, l)
        qt, k, vt = _mla_prep(lat, cos2, sin2, cost, sint, qlg, kvlg, wqt, wk, wvt, qnc, kn, l)
        o_mla = _attention(qt, k, vt)
        x = _merge(x, mod, o_gdn, o_mla, gates, wa, wb, wo, l)
        x = _ffn(x, mod, g_ffn2, *f2, l, 6)
    return x
```

```python
import functools
import math

import numpy as np
import jax
import jax.numpy as jnp
from jax import lax
from jax.experimental import pallas as pl
from jax.experimental.pallas import tpu as pltpu

F32 = jnp.float32
BF16 = jnp.bfloat16

EPS = 1e-6
N_MOD = 9
GDN_HEADS = 8
GDN_DK = 128
GDN_DV = 128
CONV_K = 4
CHUNK = 64
LOG2_CHUNK = 6
MLA_HEADS = 8
Q_LORA = 384
KV_LORA = 256
QK_NOPE = 128
QK_ROPE = 64
V_HEAD = 128
ROPE_THETA = 10000.0
MLA_QK = QK_NOPE + QK_ROPE

V7X_VMEM_BYTES = 64 * 1024 * 1024
V7X_LANES = 128
V7X_SUBLANES = 8
VMEM_CAP_BYTES = V7X_VMEM_BYTES - 6 * 1024 * 1024

FFN_TOKENS = 512
MIX_TOKENS = 256
GATE_TOKENS = 512
GDN_TOKENS = 256
GDN_GROUP = 8
ATTN_Q = 512
MERGE_TOKENS = 1024
HALO = V7X_SUBLANES


def _params(vmem_bytes, n_axes):
    limit = int(min(VMEM_CAP_BYTES, max(vmem_bytes, 16 * 1024 * 1024)))
    return pltpu.CompilerParams(dimension_semantics=("arbitrary",) * n_axes, vmem_limit_bytes=limit)


def _resident(block_shape, index_map):
    return pl.BlockSpec(block_shape, index_map, pipeline_mode=pl.Buffered(1))


def _dot(a, b):
    return jnp.dot(a, b, preferred_element_type=F32)


def _dot_nt(a, b):
    return lax.dot_general(a, b, (((1,), (1,)), ((), ())), preferred_element_type=F32)


def _sigmoid(x):
    return 1.0 / (1.0 + jnp.exp(-x))


def _silu(x):
    return x * _sigmoid(x)


def _modulate(x, gain, shift, scale):
    y = x * lax.rsqrt(jnp.mean(x * x, axis=-1, keepdims=True) + EPS)
    return (y * gain) * (1.0 + scale) + shift


def _split3(x):
    x1 = x.astype(BF16)
    r1 = x - x1.astype(F32)
    x2 = r1.astype(BF16)
    x3 = (r1 - x2.astype(F32)).astype(BF16)
    return x1, x2, x3


def _mod_kernel(c_ref, w_ref, b_ref, o_ref):
    cond = _silu(c_ref[...]).astype(BF16)
    o_ref[...] = _dot(cond, w_ref[...].astype(BF16)) + b_ref[...]


def _ada_mod(c, ada_w, ada_b):
    L, D, _ = ada_w.shape
    B = c.shape[0]
    ada_b3 = ada_b.reshape(L, N_MOD, 1, D)
    out = pl.pallas_call(
        _mod_kernel,
        grid=(L, N_MOD),
        in_specs=[
            pl.BlockSpec((B, D), lambda l, j: (0, 0)),
            pl.BlockSpec((None, D, D), lambda l, j: (l, 0, j)),
            pl.BlockSpec((None, None, 1, D), lambda l, j: (l, j, 0, 0)),
        ],
        out_specs=pl.BlockSpec((None, None, B, D), lambda l, j: (l, j, 0, 0)),
        out_shape=jax.ShapeDtypeStruct((L, N_MOD, B, D), F32),
        compiler_params=_params(4 * D * D * 4, 2),
        name="ada_mod",
    )(c, ada_w, ada_b3)
    return jnp.transpose(out, (0, 2, 1, 3))


def _rope_kernel(pos_ref, post_ref, freq_ref, sign_ref, freqc_ref, cos_ref, sin_ref, cost_ref, sint_ref):
    ang = pos_ref[...].astype(F32) * freq_ref[...]
    cos_ref[...] = jnp.cos(ang)
    sin_ref[...] = jnp.sin(ang) * sign_ref[...]
    angt = freqc_ref[...] * post_ref[...].astype(F32)
    cost_ref[...] = jnp.cos(angt)
    sint_ref[...] = jnp.sin(angt)


def _rope_tables(positions):
    B, S = positions.shape
    half = QK_ROPE // 2
    inv_freq = ROPE_THETA ** (-jnp.arange(half, dtype=F32) / half)
    freq2 = jnp.concatenate([inv_freq, inv_freq]).reshape(1, QK_ROPE)
    sign2 = jnp.concatenate([-jnp.ones((half,), F32), jnp.ones((half,), F32)]).reshape(1, QK_ROPE)
    ts = min(S, 1024)
    return pl.pallas_call(
        _rope_kernel,
        grid=(B, S // ts),
        in_specs=[
            pl.BlockSpec((None, ts, 1), lambda b, s: (b, s, 0)),
            pl.BlockSpec((None, 1, ts), lambda b, s: (b, 0, s)),
            pl.BlockSpec((1, QK_ROPE), lambda b, s: (0, 0)),
            pl.BlockSpec((1, QK_ROPE), lambda b, s: (0, 0)),
            pl.BlockSpec((half, 1), lambda b, s: (0, 0)),
        ],
        out_specs=[pl.BlockSpec((None, ts, QK_ROPE), lambda b, s: (b, s, 0))] * 2
        + [pl.BlockSpec((None, half, ts), lambda b, s: (b, 0, s))] * 2,
        out_shape=[jax.ShapeDtypeStruct((B, S, QK_ROPE), F32)] * 2
        + [jax.ShapeDtypeStruct((B, half, S), F32)] * 2,
        compiler_params=_params(8 * ts * V7X_LANES * 4, 2),
        name="rope_tables",
    )(positions.reshape(B, S, 1), positions.reshape(B, 1, S), freq2, sign2, inv_freq.reshape(half, 1))


def _ffn_kernel(x_ref, mod_ref, gain_ref, w1_ref, w3_ref, w2_ref, o_ref, *, row):
    x = x_ref[...]
    mod = mod_ref[...]
    h = _modulate(x, gain_ref[...], mod[row:row + 1], mod[row + 1:row + 2]).astype(BF16)
    a = _dot(h, w1_ref[...])
    b = _dot(h, w3_ref[...])
    g = (_silu(a) * b).astype(BF16)
    y = _dot(g, w2_ref[...])
    o_ref[...] = x + (0.5 * (1.0 + mod[row + 2:row + 3])) * y


def _ffn(x, mod, gain, w1, w3, w2, l, row):
    B, S, D = x.shape
    F = w1.shape[-1]
    tm = min(S, FFN_TOKENS)
    vmem = 3 * D * F * 2 + 4 * tm * D * 4 + 3 * tm * F * 4 + 2 * tm * D * 4
    return pl.pallas_call(
        functools.partial(_ffn_kernel, row=row),
        grid=(B, S // tm),
        in_specs=[
            pl.BlockSpec((None, tm, D), lambda b, s: (b, s, 0)),
            pl.BlockSpec((None, None, N_MOD, D), lambda b, s: (l, b, 0, 0)),
            pl.BlockSpec((None, 1, D), lambda b, s: (l, 0, 0)),
            _resident((None, D, F), lambda b, s: (l, 0, 0)),
            _resident((None, D, F), lambda b, s: (l, 0, 0)),
            _resident((None, F, D), lambda b, s: (l, 0, 0)),
        ],
        out_specs=pl.BlockSpec((None, tm, D), lambda b, s: (b, s, 0)),
        out_shape=jax.ShapeDtypeStruct((B, S, D), F32),
        compiler_params=_params(vmem, 2),
        name="ffn",
    )(x, mod, gain, w1, w3, w2)


def _mixin_kernel(x_ref, mod_ref, gain_ref, wqkv_ref, wz_ref, wg_ref, wlat_ref, wba_ref, conv_ref,
                  qkv_ref, zs_ref, gates_ref, lat_ref, ba_ref, pbuf_ref, *, tm):
    s = pl.program_id(1)
    x = x_ref[...]
    mod = mod_ref[...]
    h = _modulate(x, gain_ref[...], mod[3:4], mod[4:5]).astype(BF16)

    @pl.when(s == 0)
    def _():
        pbuf_ref[0:HALO, :] = jnp.zeros((HALO, pbuf_ref.shape[1]), F32)

    pbuf_ref[HALO:HALO + tm, :] = _dot(h, wqkv_ref[...])
    conv = conv_ref[...]
    n_qk = 2 * GDN_HEADS
    for cb in range(pbuf_ref.shape[1] // V7X_LANES):
        c0 = cb * V7X_LANES
        xs = pbuf_ref[:, c0:c0 + V7X_LANES]
        acc = conv[CONV_K - 1:CONV_K, c0:c0 + V7X_LANES] * xs[HALO:]
        for d in range(1, CONV_K):
            shifted = pltpu.roll(xs, d, axis=0)[HALO:]
            acc = acc + conv[CONV_K - 1 - d:CONV_K - d, c0:c0 + V7X_LANES] * shifted
        y = _silu(acc)
        if cb < n_qk:
            y = y * lax.rsqrt(jnp.sum(y * y, axis=-1, keepdims=True) + EPS)
        qkv_ref[cb] = y.astype(BF16)
    pbuf_ref[0:HALO, :] = pbuf_ref[tm:tm + HALO, :]

    z = _dot(h, wz_ref[...])
    for hh in range(GDN_HEADS):
        zs_ref[hh] = z[:, hh * GDN_DV:(hh + 1) * GDN_DV].astype(BF16)
    gates_ref[...] = _dot(h, wg_ref[...]).astype(BF16)
    lat_ref[...] = _dot(h, wlat_ref[...])
    ba_ref[...] = _dot_nt(wba_ref[...], h)


def _mixin(x, mod, gain, wqkv, wz, wg, wlat, wba, conv, l):
    B, S, D = x.shape
    C = wqkv.shape[-1]
    nz = wz.shape[-1]
    ng = wg.shape[-1]
    nl = wlat.shape[-1]
    nb = wba.shape[1]
    tm = min(S, MIX_TOKENS)
    ncb = C // V7X_LANES
    vmem = ((C + nz + ng + nl) * D * 2 + 2 * tm * D * 4 + 2 * tm * (C + nz + ng) * 2 + 2 * tm * nl * 4
            + (tm + HALO) * C * 4 + tm * C * 4 + tm * (nz + ng) * 4 + 4 * 1024 * 1024)
    return pl.pallas_call(
        functools.partial(_mixin_kernel, tm=tm),
        grid=(B, S // tm),
        in_specs=[
            pl.BlockSpec((None, tm, D), lambda b, s: (b, s, 0)),
            pl.BlockSpec((None, None, N_MOD, D), lambda b, s: (l, b, 0, 0)),
            pl.BlockSpec((None, 1, D), lambda b, s: (l, 0, 0)),
            _resident((None, D, C), lambda b, s: (l, 0, 0)),
            _resident((None, D, nz), lambda b, s: (l, 0, 0)),
            _resident((None, D, ng), lambda b, s: (l, 0, 0)),
            _resident((None, D, nl), lambda b, s: (l, 0, 0)),
            _resident((None, nb, D), lambda b, s: (l, 0, 0)),
            pl.BlockSpec((None, CONV_K, C), lambda b, s: (l, 0, 0)),
        ],
        out_specs=[
            pl.BlockSpec((None, ncb, tm, V7X_LANES), lambda b, s: (b, 0, s, 0)),
            pl.BlockSpec((None, GDN_HEADS, tm, GDN_DV), lambda b, s: (b, 0, s, 0)),
            pl.BlockSpec((None, tm, ng), lambda b, s: (b, s, 0)),
            pl.BlockSpec((None, tm, nl), lambda b, s: (b, s, 0)),
            pl.BlockSpec((None, nb, tm), lambda b, s: (b, 0, s)),
        ],
        out_shape=[
            jax.ShapeDtypeStruct((B, ncb, S, V7X_LANES), BF16),
            jax.ShapeDtypeStruct((B, GDN_HEADS, S, GDN_DV), BF16),
            jax.ShapeDtypeStruct((B, S, ng), BF16),
            jax.ShapeDtypeStruct((B, S, nl), F32),
            jax.ShapeDtypeStruct((B, nb, S), F32),
        ],
        scratch_shapes=[pltpu.VMEM((tm + HALO, C), F32)],
        compiler_params=_params(vmem, 2),
        name="mixin",
    )(x, mod, gain, wqkv, wz, wg, wlat, wba, conv)


def _gate_kernel(ba_ref, alog_ref, dtb_ref, grow_ref, gcol_ref, *, tm):
    H = GDN_HEADS
    ba = ba_ref[...]
    beta = _sigmoid(ba[0:H])
    xa = ba[H:2 * H] + dtb_ref[...]
    softplus = jnp.maximum(xa, 0.0) + jnp.log(1.0 + jnp.exp(-jnp.abs(xa)))
    g = -jnp.exp(alog_ref[...]) * softplus

    ii = lax.broadcasted_iota(jnp.int32, (tm, tm), 0)
    jj = lax.broadcasted_iota(jnp.int32, (tm, tm), 1)
    same = (ii >> LOG2_CHUNK) == (jj >> LOG2_CHUNK)
    upper =jnp.where(same & (ii <= jj), 1.0, 0.0).astype(BF16)
    block = jnp.where(same, 1.0, 0.0).astype(BF16)
    eye = jnp.where(ii == jj, 1.0, 0.0).astype(BF16)
    gam = None
    glast = None
    for part in _split3(g):
        t1 = _dot(part, upper)
        t2 = _dot(part, block)
        gam = t1 if gam is None else gam + t1
        glast = t2 if glast is None else glast + t2
    egam = jnp.exp(gam)
    ekd = jnp.exp(glast - gam)
    egl = jnp.exp(glast)
    zero = jnp.zeros_like(gam[0:1])
    rows = []
    for hh in range(H):
        r = jnp.concatenate([beta[hh:hh + 1], gam[hh:hh + 1], egam[hh:hh + 1], ekd[hh:hh + 1],
                             egl[hh:hh + 1], zero, zero, zero], axis=0)
        grow_ref[hh] = r
        rows.append(r)
    allrows = jnp.concatenate(rows, axis=0)
    cols = None
    for part in _split3(allrows):
        t = _dot_nt(eye, part)
        cols = t if cols is None else cols + t
    for hh in range(H):
        gcol_ref[hh] = cols[:, 8 * hh:8 * hh + 8]


def _gdn_gates(ba, a_log, dt_bias, l):
    B, nb, S = ba.shape
    H = GDN_HEADS
    tm = min(S, GATE_TOKENS)
    vmem = 8 * tm * tm * 4 + 4 * H * tm * V7X_LANES * 4
    return pl.pallas_call(
        functools.partial(_gate_kernel, tm=tm),
        grid=(B, S // tm),
        in_specs=[
            pl.BlockSpec((None, nb, tm), lambda b, s: (b, 0, s)),
            pl.BlockSpec((None, H, 1), lambda b, s: (l, 0, 0)),
            pl.BlockSpec((None, H, 1), lambda b, s: (l, 0, 0)),
        ],
        out_specs=[
            pl.BlockSpec((None, H, 8, tm), lambda b, s: (b, 0, 0, s)),
            pl.BlockSpec((None, H, tm, 8), lambda b, s: (b, 0, s, 0)),
        ],
        out_shape=[
            jax.ShapeDtypeStruct((B, H, 8, S), F32),
            jax.ShapeDtypeStruct((B, H, S, 8), F32),
        ],
        compiler_params=_params(vmem, 2),
        name="gdn_gates",
    )(ba, a_log, dt_bias)


def _gdn_kernel(q_ref, k_ref, v_ref, zs_ref, grow_ref, gcol_ref, og_ref, o_ref, state_ref, *, tc, G):
    s = pl.program_id(2)

    @pl.when(s == 0)
    def _():
        state_ref[...] = jnp.zeros(state_ref.shape, F32)

    heads = range(G)
    n_chunks = tc // CHUNK
    scale = GDN_DK ** -0.5
    ii = lax.broadcasted_iota(jnp.int32, (tc, tc), 0)
    jj = lax.broadcasted_iota(jnp.int32, (tc, tc), 1)
    same = (ii >> LOG2_CHUNK) == (jj >> LOG2_CHUNK)
    causal = same & (ii >= jj)
    strict = same & (ii > jj)
    lane = lax.broadcasted_iota(jnp.int32, (GDN_DK, tc), 1) >> LOG2_CHUNK

    gc = [gcol_ref[g] for g in heads]
    gr = [grow_ref[g] for g in heads]
    k = [k_ref[g] for g in heads]
    kf = [k[g].astype(F32) for g in heads]
    kb = [kf[g] * gc[g][:, 0:1] for g in heads]
    decay = [jnp.where(causal, jnp.exp(jnp.minimum(gc[g][:, 1:2] - gr[g][1:2, :], 0.0)), 0.0) for g in heads]

    akq = [_dot_nt(jnp.concatenate([kb[g].astype(BF16), q_ref[g]], axis=0), k[g]) for g in heads]
    m = [jnp.where(strict, akq[g][:tc] * decay[g], 0.0) for g in heads]
    qk = [(akq[g][tc:] * (decay[g] * scale)).astype(BF16) for g in heads]

    rlane = lax.broadcasted_iota(jnp.int32, (CHUNK, tc), 1)
    rrow = lax.broadcasted_iota(jnp.int32, (CHUNK, tc), 0)
    eye_row = jnp.where((rlane & (CHUNK - 1)) == rrow, 1.0, 0.0)
    rblk = rlane >> LOG2_CHUNK

    def blockdiag(x_row):
        return jnp.concatenate([jnp.where(rblk == c, x_row, 0.0) for c in range(n_chunks)], axis=0).astype(BF16)

    def side_by_side(x):
        acc = x[0:CHUNK]
        for c in range(1, n_chunks):
            acc = acc + x[c * CHUNK:(c + 1) * CHUNK]
        return acc

    m_row = [side_by_side(m[g]) for g in heads]
    p = [_dot(m_row[g].astype(BF16), blockdiag(m_row[g])) for g in heads]
    t_row = [eye_row - m_row[g] for g in heads]
    for _ in range(LOG2_CHUNK - 2):
        r = [_dot(jnp.concatenate([p[g], t_row[g]], axis=0).astype(BF16), blockdiag(p[g])) for g in heads]
        p = [r[g][:CHUNK] for g in heads]
        t_row = [t_row[g] + r[g][CHUNK:] for g in heads]
    t_row = [t_row[g] + _dot(t_row[g].astype(BF16), blockdiag(p[g])) for g in heads]

    rhs = [jnp.concatenate([v_ref[g].astype(F32) * gc[g][:, 0:1], kb[g] * gc[g][:, 2:3]], axis=1).astype(BF16)
           for g in heads]
    uw = [_dot(blockdiag(t_row[g]), rhs[g]).astype(BF16) for g in heads]

    kdt = [(kf[g] * gc[g][:, 3:4]).T for g in heads]
    pre_lhs = [jnp.concatenate([jnp.where(lane == c, kdt[g], 0.0).astype(BF16) for c in range(n_chunks)]
                               + [qk[g]], axis=0) for g in heads]
    pre = [_dot(pre_lhs[g], uw[g]) for g in heads]
    qd = [q_ref[g].astype(F32) * (gc[g][:, 2:3] * scale) for g in heads]
    base = n_chunks * GDN_DK
    qmod = [(qd[g] - pre[g][base:, GDN_DV:]).astype(BF16) for g in heads]

    states = [state_ref[g] for g in heads]
    outs = [[] for _ in heads]
    for c in range(n_chunks):
        r0 = c * CHUNK
        for g in heads:
            kblk = pre[g][c * GDN_DK:(c + 1) * GDN_DK]
            lhs = jnp.concatenate([kblk[:, GDN_DV:].astype(BF16), qmod[g][r0:r0 + CHUNK]], axis=0)
            r = _dot(lhs, states[g].astype(BF16))
            outs[g].append(r[GDN_DK:] + pre[g][base + r0:base + r0 + CHUNK, :GDN_DV])
            states[g] = states[g] * gr[g][4:5, r0:r0 + 1] - r[:GDN_DK] + kblk[:, :GDN_DV]
    for g in heads:
        state_ref[g] = states[g]
        o = jnp.concatenate(outs[g], axis=0)
        o = o * lax.rsqrt(jnp.mean(o * o, axis=-1, keepdims=True) + EPS) * og_ref[...]
        o_ref[:, g * GDN_DV:(g + 1) * GDN_DV] = (o * _silu(zs_ref[g].astype(F32))).astype(BF16)


def _gdn(qkv, zs, grow, gcol, out_gain, l):
    B, _, S, _ = qkv.shape
    H = GDN_HEADS
    G = GDN_GROUP
    tc = min(S, GDN_TOKENS)
    vmem = G * (24 * tc * tc * 4 + 24 * tc * V7X_LANES * 4)
    return pl.pallas_call(
        functools.partial(_gdn_kernel, tc=tc, G=G),
        grid=(B, H // G, S // tc),
        in_specs=[
            pl.BlockSpec((None, G, tc, GDN_DK), lambda b, h, s: (b, h, s, 0)),
            pl.BlockSpec((None, G, tc, GDN_DK), lambda b, h, s: (b, H // G + h, s, 0)),
            pl.BlockSpec((None, G, tc, GDN_DV), lambda b, h, s: (b, 2 * H // G + h, s, 0)),
            pl.BlockSpec((None, G, tc, GDN_DV), lambda b, h, s: (b, h, s, 0)),
            pl.BlockSpec((None, G, 8, tc), lambda b, h, s: (b, h, 0, s)),
            pl.BlockSpec((None, G, tc, 8), lambda b, h, s: (b, h, s, 0)),
            pl.BlockSpec((None, 1, GDN_DV), lambda b, h, s: (l, 0, 0)),
        ],
        out_specs=pl.BlockSpec((None, tc, G * GDN_DV), lambda b, h, s: (b, s, h)),
        out_shape=jax.ShapeDtypeStruct((B, S, H * GDN_DV), BF16),
        scratch_shapes=[pltpu.VMEM((G, GDN_DK, GDN_DV), F32)],
        compiler_params=_params(vmem, 3),
        name="gdn",
    )(qkv, qkv, qkv, zs, grow, gcol, out_gain)


def _swap_halves(y):
    half = y.shape[-1] // 2
    return jnp.concatenate([y[:, half:], y[:, :half]], axis=1)


def _mla_prep_kernel(lat_ref, cos_ref, sin_ref, cost_ref, sint_ref, qg_ref, kvg_ref, wqt_ref, wk_ref, wvt_ref,
                     qnc_ref, kn_ref, qt_ref, k_ref, vt_ref):
    H = MLA_HEADS
    half = QK_ROPE // 2
    lat = lat_ref[...]
    ql = lat[:, :Q_LORA]
    kvl = lat[:, Q_LORA:Q_LORA + KV_LORA]
    kpe = lat[:, Q_LORA + KV_LORA:Q_LORA + KV_LORA + QK_ROPE]
    qln = (ql * lax.rsqrt(jnp.mean(ql * ql, axis=-1, keepdims=True) + EPS) * qg_ref[...]).astype(BF16)
    kvn = (kvl * lax.rsqrt(jnp.mean(kvl * kvl, axis=-1, keepdims=True) + EPS) * kvg_ref[...]).astype(BF16)
    qt = _dot_nt(wqt_ref[...], qln)
    knope = _dot(kvn, wk_ref[...])
    vt = _dot_nt(wvt_ref[...], kvn)

    cost = cost_ref[...]
    sint = sint_ref[...]
    qnc = qnc_ref[...]
    scale = MLA_QK ** -0.5 * math.log2(math.e)
    for hh in range(H):
        blk = qt[hh * MLA_QK:(hh + 1) * MLA_QK]
        r = lax.rsqrt(jnp.sum(blk * blk, axis=0, keepdims=True) * (1.0 / MLA_QK) + EPS) * scale
        y = blk * qnc * r
        y1 = y[QK_NOPE:QK_NOPE + half]
        y2 = y[QK_NOPE + half:]
        qt_ref[hh, 0:QK_NOPE, :] = y[:QK_NOPE].astype(BF16)
        qt_ref[hh, QK_NOPE:QK_NOPE + half, :] = (y1 * cost - y2 * sint).astype(BF16)
        qt_ref[hh, QK_NOPE + half:MLA_QK, :] = (y2 * cost + y1 * sint).astype(BF16)
        vt_ref[hh] = vt[hh * V_HEAD:(hh + 1) * V_HEAD].astype(BF16)

    kn = kn_ref[...]
    yk = kpe * kn[:, QK_NOPE:]
    kr = yk * cos_ref[...] + _swap_halves(yk) * sin_ref[...]
    kpe_ss = jnp.sum(kpe * kpe, axis=-1, keepdims=True)
    for hh in range(H):
        k_nope = knope[:, hh * QK_NOPE:(hh + 1) * QK_NOPE]
        ss = jnp.sum(k_nope * k_nope, axis=-1, keepdims=True) + kpe_ss
        r = lax.rsqrt(ss * (1.0 / MLA_QK) + EPS)
        k_ref[hh, :, 0:QK_NOPE] = (k_nope * r * kn[:, :QK_NOPE]).astype(BF16)
        k_ref[hh, :, QK_NOPE:MLA_QK] = (kr * r).astype(BF16)


def _mla_prep(lat, cos2, sin2, cost, sint, q_lat_gain, kv_lat_gain, wqt, wk, wvt, q_norm_col, k_norm, l):
    B, S, nl = lat.shape
    H = MLA_HEADS
    half = QK_ROPE // 2
    tm = min(S, ATTN_Q)
    nq = wqt.shape[1]
    nk = wk.shape[-1]
    nv = wvt.shape[1]
    vmem = ((nq * Q_LORA + (nk + nv) * KV_LORA) * 2 + 2 * tm * nl * 4 + 3 * tm * (nq + nk + nv) * 4
            + 4 * H * tm * (256 + MLA_QK + V_HEAD) * 2)
    return pl.pallas_call(
        _mla_prep_kernel,
        grid=(B, S // tm),
        in_specs=[
            pl.BlockSpec((None, tm, nl), lambda b, s: (b, s, 0)),
            pl.BlockSpec((None, tm, QK_ROPE), lambda b, s: (b, s, 0)),
            pl.BlockSpec((None, tm, QK_ROPE), lambda b, s: (b, s, 0)),
            pl.BlockSpec((None, half, tm), lambda b, s: (b, 0, s)),
            pl.BlockSpec((None, half, tm), lambda b, s: (b, 0, s)),
            pl.BlockSpec((None, 1, Q_LORA), lambda b, s: (l, 0, 0)),
            pl.BlockSpec((None, 1, KV_LORA), lambda b, s: (l, 0, 0)),
            _resident((None, nq, Q_LORA), lambda b, s: (l, 0, 0)),
            _resident((None, KV_LORA, nk), lambda b, s: (l, 0, 0)),
            _resident((None, nv, KV_LORA), lambda b, s: (l, 0, 0)),
            pl.BlockSpec((None, MLA_QK, 1), lambda b, s: (l, 0, 0)),
            pl.BlockSpec((None, 1, MLA_QK), lambda b, s: (l, 0, 0)),
        ],
        out_specs=[
            pl.BlockSpec((None, H, MLA_QK, tm), lambda b, s: (b, 0, 0, s)),
            pl.BlockSpec((None, H, tm, MLA_QK), lambda b, s: (b, 0, s, 0)),
            pl.BlockSpec((None, H, None, V_HEAD, tm), lambda b, s: (b, 0, s, 0, 0)),
        ],
        out_shape=[
            jax.ShapeDtypeStruct((B, H, MLA_QK, S), BF16),
            jax.ShapeDtypeStruct((B, H, S, MLA_QK), BF16),
            jax.ShapeDtypeStruct((B, H, S // tm, V_HEAD, tm), BF16),
        ],
        compiler_params=_params(vmem, 2),
        name="mla_prep",
    )(lat, cos2, sin2, cost, sint, q_lat_gain, kv_lat_gain, wqt, wk, wvt, q_norm_col, k_norm)


def _attn_kernel(qt_ref, k_ref, vt_ref, o_ref, sa_ref, sb_ref, mxa_ref, mxb_ref, m_ref, l_ref, acc_ref, *, tq):
    i = pl.program_id(2)
    qt = qt_ref[...]

    def scores(j, s_ref, mx_ref):
        st = _dot(k_ref[pl.ds(pl.multiple_of(j * tq, tq), tq), :], qt)
        s_ref[...] = st
        mx_ref[...] = jnp.max(st, axis=0, keepdims=True)

    def update(j, s_ref, mx_ref, masked):
        st = s_ref[...]
        mx = mx_ref[...]
        if masked:
            kpos = lax.broadcasted_iota(jnp.int32, (tq, tq), 0)
            qpos = lax.broadcasted_iota(jnp.int32, (tq, tq), 1)
            st = jnp.where(kpos <= qpos, st, -jnp.inf)
            mx = jnp.max(st, axis=0, keepdims=True)
        m = m_ref[...]
        m_new = jnp.maximum(m, mx)
        alpha = jnp.exp2(m - m_new)
        p = jnp.exp2(st - m_new)
        m_ref[...] = m_new
        l_ref[...] = alpha * l_ref[...] + jnp.sum(p, axis=0, keepdims=True)
        acc_ref[...] = alpha * acc_ref[...] + _dot(vt_ref[j], p.astype(BF16))

    m_ref[...] = jnp.full(m_ref.shape, -jnp.inf, F32)
    l_ref[...] = jnp.zeros(l_ref.shape, F32)
    acc_ref[...] = jnp.zeros(acc_ref.shape, F32)
    scores(0, sa_ref, mxa_ref)

    def body(t, _):
        scores(2 * t + 1, sb_ref, mxb_ref)
        update(2 * t, sa_ref, mxa_ref, False)
        scores(2 * t + 2, sa_ref, mxa_ref)
        update(2 * t + 1, sb_ref, mxb_ref, False)
        return 0

    lax.fori_loop(0, i // 2, body, 0)

    @pl.when(i % 2 == 0)
    def _():
        update(i, sa_ref, mxa_ref, True)

    @pl.when(i % 2 == 1)
    def _():
        scores(i, sb_ref, mxb_ref)
        update(i - 1, sa_ref, mxa_ref, False)
        update(i, sb_ref, mxb_ref, True)

    o_ref[...] = (acc_ref[...] / l_ref[...]).T.astype(BF16)


def _attention(qt, k, vt):
    B, H, S, _ = k.shape
    tq = vt.shape[-1]
    vmem = 8 * tq * tq * 4 + 4 * S * (256 + V_HEAD) * 2 + 4 * tq * 256 * 2 + 8 * tq * V7X_LANES * 4
    return pl.pallas_call(
        functools.partial(_attn_kernel, tq=tq),
        grid=(B, H, S // tq),
        in_specs=[
            pl.BlockSpec((None, None, MLA_QK, tq), lambda b, h, i: (b, h, 0, i)),
            pl.BlockSpec((None, None, S, MLA_QK), lambda b, h, i: (b, h, 0, 0)),
            pl.BlockSpec((None, None, S // tq, V_HEAD, tq), lambda b, h, i: (b, h, 0, 0, 0)),
        ],
        out_specs=pl.BlockSpec((None, tq, V_HEAD), lambda b, h, i: (b, i, h)),
        out_shape=jax.ShapeDtypeStruct((B, S, H * V_HEAD), BF16),
        scratch_shapes=[pltpu.VMEM((tq, tq), F32), pltpu.VMEM((tq, tq), F32),
                        pltpu.VMEM((1, tq), F32), pltpu.VMEM((1, tq), F32),
                        pltpu.VMEM((1, tq), F32), pltpu.VMEM((1, tq), F32), pltpu.VMEM((V_HEAD, tq), F32)],
        compiler_params=_params(vmem, 3),
        name="mla_attention",
    )(qt, k, vt)


def _merge_kernel(x_ref, mod_ref, oa_ref, ob_ref, gates_ref, wa_ref, wb_ref, wo_ref, o_ref):
    D = x_ref.shape[-1]
    mod = mod_ref[...]
    gates = _sigmoid(gates_ref[...].astype(F32))
    ya =_dot(oa_ref[...], wa_ref[...])
    yb = _dot(ob_ref[...], wb_ref[...])
    y = (gates[:, :D] * ya + gates[:, D:] * yb).astype(BF16)
    o_ref[...] = x_ref[...] + (1.0 + mod[5:6]) * _dot(y, wo_ref[...])


def _merge(x, mod, oa, ob, gates, wa, wb, wo, l):
    B, S, D = x.shape
    tm = min(S, MERGE_TOKENS)
    na = oa.shape[-1]
    nbv = ob.shape[-1]
    vmem = (na + nbv + D) * D * 2 + 4 * tm * D * 4 + 2 * tm * (na + nbv + 2 * D) * 2 + 4 * tm * D * 4
    return pl.pallas_call(
        _merge_kernel,
        grid=(B, S // tm),
        in_specs=[
            pl.BlockSpec((None, tm, D), lambda b, s: (b, s, 0)),
            pl.BlockSpec((None, None, N_MOD, D), lambda b, s: (l, b, 0, 0)),
            pl.BlockSpec((None, tm, na), lambda b, s: (b, s, 0)),
            pl.BlockSpec((None, tm, nbv), lambda b, s: (b, s, 0)),
            pl.BlockSpec((None, tm, 2 * D), lambda b, s: (b, s, 0)),
            _resident((None, na, D), lambda b, s: (l, 0, 0)),
            _resident((None, nbv, D), lambda b, s: (l, 0, 0)),
            _resident((None, D, D), lambda b, s: (l, 0, 0)),
        ],
        out_specs=pl.BlockSpec((None, tm, D), lambda b, s: (b, s, 0)),
        out_shape=jax.ShapeDtypeStruct((B, S, D), F32),
        compiler_params=_params(vmem, 2),
        name="merge",
    )(x, mod, oa, ob, gates, wa, wb, wo)


def _pack_w_in(w_in):
    conv_c = 2 * GDN_HEADS * GDN_DK + GDN_HEADS * GDN_DV
    gdn_v = GDN_HEADS * GDN_DV
    D = w_in.shape[1]
    sizes = (conv_c, gdn_v, GDN_HEADS, GDN_HEADS, Q_LORA, KV_LORA, QK_ROPE, D, D)
    offs = np.cumsum((0,) + sizes)
    qkv, z, b_raw, a_raw, q_lat, kv_lat, k_pe, gate_a, gate_b = (
        w_in[:, :, int(offs[i]):int(offs[i + 1])] for i in range(len(sizes)))
    n_lat = Q_LORA + KV_LORA + QK_ROPE
    pad = (-n_lat) % V7X_LANES
    lat = jnp.concatenate([q_lat, kv_lat, k_pe, jnp.zeros(w_in.shape[:2] + (pad,), w_in.dtype)], axis=-1)
    ba = jnp.swapaxes(jnp.concatenate([b_raw, a_raw], axis=-1), 1, 2)
    return (qkv.astype(BF16), z.astype(BF16), jnp.concatenate([gate_a, gate_b], axis=-1).astype(BF16),
            lat.astype(BF16), ba.astype(BF16))


def _pack_kv_up(w_kv_up):
    L = w_kv_up.shape[0]
    w = w_kv_up.reshape(L, KV_LORA, MLA_HEADS, QK_NOPE + V_HEAD)
    wk = w[..., :QK_NOPE].reshape(L, KV_LORA, MLA_HEADS * QK_NOPE)
    wv = w[..., QK_NOPE:].reshape(L, KV_LORA, MLA_HEADS * V_HEAD)
    return wk.astype(BF16), jnp.swapaxes(wv, 1, 2).astype(BF16)


def kernel(x, c, positions, ada_w, ada_b, norm_ffn1, ffn1_w1, ffn1_w3, ffn1_w2, norm_mix, w_in, gdn_conv, gdn_a_log, gdn_dt_bias, gdn_out_gain, mla_q_lat_gain, mla_kv_lat_gain, mla_w_q_up, mla_w_kv_up, mla_q_norm, mla_k_norm, w_branch_a, w_branch_b, w_out, norm_ffn2, ffn2_w1, ffn2_w3, ffn2_w2):
    L = ada_w.shape[0]
    S = x.shape[1]
    assert S % CHUNK == 0 and S % min(S, GDN_TOKENS) == 0 and S % min(S, FFN_TOKENS) == 0
    assert S % min(S, MERGE_TOKENS) == 0 and S % min(S, ATTN_Q) == 0

    mod = _ada_mod(c, ada_w, ada_b)
    cos2, sin2, cost, sint = _rope_tables(positions)

    row = lambda a: a.reshape(L, 1, a.shape[-1])
    col = lambda a: a.reshape(L, a.shape[-1], 1)
    bf = lambda a: a.astype(BF16)
    wqkv, wz, wg, wlat, wba = _pack_w_in(w_in)
    wqt = bf(jnp.swapaxes(mla_w_q_up, 1, 2))
    wk, wvt = _pack_kv_up(mla_w_kv_up)
    f1 = (bf(ffn1_w1), bf(ffn1_w3), bf(ffn1_w2))
    f2 = (bf(ffn2_w1), bf(ffn2_w3), bf(ffn2_w2))
    wa, wb, wo = bf(w_branch_a), bf(w_branch_b), bf(w_out)
    g_ffn1, g_mix, g_ffn2 = row(norm_ffn1), row(norm_mix), row(norm_ffn2)
    og, qlg, kvlg = row(gdn_out_gain), row(mla_q_lat_gain), row(mla_kv_lat_gain)
    qnc, kn = col(mla_q_norm), row(mla_k_norm)
    a_log, dt_bias = col(gdn_a_log), col(gdn_dt_bias)

    for l in range(L):
        x = _ffn(x, mod, g_ffn1, *f1, l, 0)
        qkv, zs, gates, lat, ba = _mixin(x, mod, g_mix, wqkv, wz, wg, wlat, wba, gdn_conv, l)
        grow, gcol = _gdn_gates(ba, a_log, dt_bias, l)
        o_gdn = _gdn(qkv, zs, grow, gcol, og, l)
        qt, k, vt = _mla_prep(lat, cos2, sin2, cost, sint, qlg, kvlg, wqt, wk, wvt, qnc, kn, l)
        o_mla = _attention(qt, k, vt)
        x = _merge(x, mod, o_gdn, o_mla, gates, wa, wb, wo, l)
        x = _ffn(x, mod, g_ffn2, *f2, l, 6)
    return x
```

```python
import functools
import math

import numpy as np
import jax
import jax.numpy as jnp
from jax import lax
from jax.experimental import pallas as pl
from jax.experimental.pallas import tpu as pltpu

F32 = jnp.float32
BF16 = jnp.bfloat16

EPS = 1e-6
N_MOD = 9
GDN_HEADS = 8
GDN_DK = 128
GDN_DV = 128
CONV_K = 4
CHUNK = 64
LOG2_CHUNK = 6
MLA_HEADS = 8
Q_LORA = 384
KV_LORA = 256
QK_NOPE = 128
QK_ROPE = 64
V_HEAD = 128
ROPE_THETA = 10000.0
MLA_QK = QK_NOPE + QK_ROPE

V7X_VMEM_BYTES = 64 * 1024 * 1024
V7X_LANES = 128
V7X_SUBLANES = 8
VMEM_CAP_BYTES = V7X_VMEM_BYTES - 6 * 1024 * 1024

FFN_TOKENS = 512
MIX_TOKENS = 256
GATE_TOKENS = 512
GDN_TOKENS = 256
GDN_GROUP = 8
ATTN_Q = 512
MERGE_TOKENS = 1024
HALO = V7X_SUBLANES


def _params(vmem_bytes, n_axes):
    limit = int(min(VMEM_CAP_BYTES, max(vmem_bytes, 16 * 1024 * 1024)))
    return pltpu.CompilerParams(dimension_semantics=("arbitrary",) * n_axes, vmem_limit_bytes=limit)


def _resident(block_shape, index_map):
    return pl.BlockSpec(block_shape, index_map, pipeline_mode=pl.Buffered(1))


def _dot(a, b):
    return jnp.dot(a, b, preferred_element_type=F32)


def _dot_nt(a, b):
    return lax.dot_general(a, b, (((1,), (1,)), ((), ())), preferred_element_type=F32)


def _sigmoid(x):
    return 1.0 / (1.0 + jnp.exp(-x))


def _silu(x):
    return x * _sigmoid(x)


def _modulate(x, gain, shift, scale):
    y = x * lax.rsqrt(jnp.mean(x * x, axis=-1, keepdims=True) + EPS)
    return (y * gain) * (1.0 + scale) + shift


def _split3(x):
    x1 = x.astype(BF16)
    r1 = x - x1.astype(F32)
    x2 = r1.astype(BF16)
    x3 = (r1 - x2.astype(F32)).astype(BF16)
    return x1, x2, x3


def _mod_kernel(c_ref, w_ref, b_ref, o_ref):
    cond = _silu(c_ref[...]).astype(BF16)
    o_ref[...] = _dot(cond, w_ref[...].astype(BF16)) + b_ref[...]


def _ada_mod(c, ada_w, ada_b):
    L, D, _ = ada_w.shape
    B = c.shape[0]
    ada_b3 = ada_b.reshape(L, N_MOD, 1, D)
    out = pl.pallas_call(
        _mod_kernel,
        grid=(L, N_MOD),
        in_specs=[
            pl.BlockSpec((B, D), lambda l, j: (0, 0)),
            pl.BlockSpec((None, D, D), lambda l, j: (l, 0, j)),
            pl.BlockSpec((None, None, 1, D), lambda l, j: (l, j, 0, 0)),
        ],
        out_specs=pl.BlockSpec((None, None, B, D), lambda l, j: (l, j, 0, 0)),
        out_shape=jax.ShapeDtypeStruct((L, N_MOD, B, D), F32),
        compiler_params=_params(4 * D * D * 4, 2),
        name="ada_mod",
    )(c, ada_w, ada_b3)
    return jnp.transpose(out, (0, 2, 1, 3))


def _rope_kernel(pos_ref, post_ref, freq_ref, sign_ref, freqc_ref, cos_ref, sin_ref, cost_ref, sint_ref):
    ang = pos_ref[...].astype(F32) * freq_ref[...]
    cos_ref[...] = jnp.cos(ang)
    sin_ref[...] = jnp.sin(ang) * sign_ref[...]
    angt = freqc_ref[...] * post_ref[...].astype(F32)
    cost_ref[...] = jnp.cos(angt)
    sint_ref[...] = jnp.sin(angt)


def _rope_tables(positions):
    B, S = positions.shape
    half = QK_ROPE // 2
    inv_freq = ROPE_THETA ** (-jnp.arange(half, dtype=F32) / half)
    freq2 = jnp.concatenate([inv_freq, inv_freq]).reshape(1, QK_ROPE)
    sign2 = jnp.concatenate([-jnp.ones((half,), F32), jnp.ones((half,), F32)]).reshape(1, QK_ROPE)
    ts = min(S, 1024)
    return pl.pallas_call(
        _rope_kernel,
        grid=(B, S // ts),
        in_specs=[
            pl.BlockSpec((None, ts, 1), lambda b, s: (b, s, 0)),
            pl.BlockSpec((None, 1, ts), lambda b, s: (b, 0, s)),
            pl.BlockSpec((1, QK_ROPE), lambda b, s: (0, 0)),
            pl.BlockSpec((1, QK_ROPE), lambda b, s: (0, 0)),
            pl.BlockSpec((half, 1), lambda b, s: (0, 0)),
        ],
        out_specs=[pl.BlockSpec((None, ts, QK_ROPE), lambda b, s: (b, s, 0))] * 2
        + [pl.BlockSpec((None, half, ts), lambda b, s: (b, 0, s))] * 2,
        out_shape=[jax.ShapeDtypeStruct((B, S, QK_ROPE), F32)] * 2
        + [jax.ShapeDtypeStruct((B, half, S), F32)] * 2,
        compiler_params=_params(8 * ts * V7X_LANES * 4, 2),
        name="rope_tables",
    )(positions.reshape(B, S, 1), positions.reshape(B, 1, S), freq2, sign2, inv_freq.reshape(half, 1))


def _ffn_kernel(x_ref, mod_ref, gain_ref, w1_ref, w3_ref, w2_ref, o_ref, *, row):
    x = x_ref[...]
    mod = mod_ref[...]
    h = _modulate(x, gain_ref[...], mod[row:row + 1], mod[row + 1:row + 2]).astype(BF16)
    a = _dot(h, w1_ref[...])
    b = _dot(h, w3_ref[...])
    g = (_silu(a) * b).astype(BF16)
    y = _dot(g, w2_ref[...])
    o_ref[...] = x + (0.5 * (1.0 + mod[row + 2:row + 3])) * y


def _ffn(x, mod, gain, w1, w3, w2, l, row):
    B, S, D = x.shape
    F = w1.shape[-1]
    tm = min(S, FFN_TOKENS)
    vmem = 3 * D * F * 2 + 4 * tm * D * 4 + 3 * tm * F * 4 + 2 * tm * D * 4
    return pl.pallas_call(
        functools.partial(_ffn_kernel, row=row),
        grid=(B, S // tm),
        in_specs=[
            pl.BlockSpec((None, tm, D), lambda b, s: (b, s, 0)),
            pl.BlockSpec((None, None, N_MOD, D), lambda b, s: (l, b, 0, 0)),
            pl.BlockSpec((None, 1, D), lambda b, s: (l, 0, 0)),
            _resident((None, D, F), lambda b, s: (l, 0, 0)),
            _resident((None, D, F), lambda b, s: (l, 0, 0)),
            _resident((None, F, D), lambda b, s: (l, 0, 0)),
        ],
        out_specs=pl.BlockSpec((None, tm, D), lambda b, s: (b, s, 0)),
        out_shape=jax.ShapeDtypeStruct((B, S, D), F32),
        compiler_params=_params(vmem, 2),
        name="ffn",
    )(x, mod, gain, w1, w3, w2)


def _mixin_kernel(x_ref, mod_ref, gain_ref, wqkv_ref, wz_ref, wg_ref, wlat_ref, wba_ref, conv_ref,
                  qkv_ref, zs_ref, gates_ref, lat_ref, ba_ref, pbuf_ref, *, tm):
    s = pl.program_id(1)
    x = x_ref[...]
    mod = mod_ref[...]
    h = _modulate(x, gain_ref[...], mod[3:4], mod[4:5]).astype(BF16)

    @pl.when(s == 0)
    def _():
        pbuf_ref[0:HALO, :] = jnp.zeros((HALO, pbuf_ref.shape[1]), F32)

    pbuf_ref[HALO:HALO + tm, :] = _dot(h, wqkv_ref[...])
    conv = conv_ref[...]
    n_qk = 2 * GDN_HEADS
    for cb in range(pbuf_ref.shape[1] // V7X_LANES):
        c0 = cb * V7X_LANES
        xs = pbuf_ref[:, c0:c0 + V7X_LANES]
        acc = conv[CONV_K - 1:CONV_K, c0:c0 + V7X_LANES] * xs[HALO:]
        for d in range(1, CONV_K):
            shifted = pltpu.roll(xs, d, axis=0)[HALO:]
            acc = acc + conv[CONV_K - 1 - d:CONV_K - d, c0:c0 + V7X_LANES] * shifted
        y = _silu(acc)
        if cb < n_qk:
            y = y * lax.rsqrt(jnp.sum(y * y, axis=-1, keepdims=True) + EPS)
        qkv_ref[cb] = y.astype(BF16)
    pbuf_ref[0:HALO, :] = pbuf_ref[tm:tm + HALO, :]

    z = _dot(h, wz_ref[...])
    for hh in range(GDN_HEADS):
        zs_ref[hh] = z[:, hh * GDN_DV:(hh + 1) * GDN_DV].astype(BF16)
    gates_ref[...] = _dot(h, wg_ref[...]).astype(BF16)
    lat_ref[...] = _dot(h, wlat_ref[...])
    ba_ref[...] = _dot_nt(wba_ref[...], h)


def _mixin(x, mod, gain, wqkv, wz, wg, wlat, wba, conv, l):
    B, S, D = x.shape
    C = wqkv.shape[-1]
    nz = wz.shape[-1]
    ng = wg.shape[-1]
    nl = wlat.shape[-1]
    nb = wba.shape[1]
    tm = min(S, MIX_TOKENS)
    ncb = C // V7X_LANES
    vmem = ((C + nz + ng + nl) * D * 2 + 2 * tm * D * 4 + 2 * tm * (C + nz + ng) * 2 + 2 * tm * nl * 4
            + (tm + HALO) * C * 4 + tm * C * 4 + tm * (nz + ng) * 4 + 4 * 1024 * 1024)
    return pl.pallas_call(
        functools.partial(_mixin_kernel, tm=tm),
        grid=(B, S // tm),
        in_specs=[
            pl.BlockSpec((None, tm, D), lambda b, s: (b, s, 0)),
            pl.BlockSpec((None, None, N_MOD, D), lambda b, s: (l, b, 0, 0)),
            pl.BlockSpec((None, 1, D), lambda b, s: (l, 0, 0)),
            _resident((None, D, C), lambda b, s: (l, 0, 0)),
            _resident((None, D, nz), lambda b, s: (l, 0, 0)),
            _resident((None, D, ng), lambda b, s: (l, 0, 0)),
            _resident((None, D, nl), lambda b, s: (l, 0, 0)),
            _resident((None, nb, D), lambda b, s: (l, 0, 0)),
            pl.BlockSpec((None, CONV_K, C), lambda b, s: (l, 0, 0)),
        ],
        out_specs=[
            pl.BlockSpec((None, ncb, tm, V7X_LANES), lambda b, s: (b, 0, s, 0)),
            pl.BlockSpec((None, GDN_HEADS, tm, GDN_DV), lambda b, s: (b, 0, s, 0)),
            pl.BlockSpec((None, tm, ng), lambda b, s: (b, s, 0)),
            pl.BlockSpec((None, tm, nl), lambda b, s: (b, s, 0)),
            pl.BlockSpec((None, nb, tm), lambda b, s: (b, 0, s)),
        ],
        out_shape=[
            jax.ShapeDtypeStruct((B, ncb, S, V7X_LANES), BF16),
            jax.ShapeDtypeStruct((B, GDN_HEADS, S, GDN_DV), BF16),
            jax.ShapeDtypeStruct((B, S, ng), BF16),
            jax.ShapeDtypeStruct((B, S, nl), F32),
            jax.ShapeDtypeStruct((B, nb, S), F32),
        ],
        scratch_shapes=[pltpu.VMEM((tm + HALO, C), F32)],
        compiler_params=_params(vmem, 2),
        name="mixin",
    )(x, mod, gain, wqkv, wz, wg, wlat, wba, conv)


def _gate_kernel(ba_ref, alog_ref, dtb_ref, grow_ref, gcol_ref, *, tm):
    H = GDN_HEADS
    ba = ba_ref[...]
    beta = _sigmoid(ba[0:H])
    xa = ba[H:2 * H] + dtb_ref[...]
    softplus = jnp.maximum(xa, 0.0) + jnp.log(1.0 + jnp.exp(-jnp.abs(xa)))
    g = -jnp.exp(alog_ref[...]) * softplus

    ii = lax.broadcasted_iota(jnp.int32, (tm, tm), 0)
    jj = lax.broadcasted_iota(jnp.int32, (tm, tm), 1)
    same = (ii >> LOG2_CHUNK) == (jj >> LOG2_CHUNK)
    upper =jnp.where(same & (ii <= jj), 1.0, 0.0).astype(BF16)
    block = jnp.where(same, 1.0, 0.0).astype(BF16)
    eye = jnp.where(ii == jj, 1.0, 0.0).astype(BF16)
    gam = None
    glast = None
    for part in _split3(g):
        t1 = _dot(part, upper)
        t2 = _dot(part, block)
        gam = t1 if gam is None else gam + t1
        glast = t2 if glast is None else glast + t2
    egam = jnp.exp(gam)
    ekd = jnp.exp(glast - gam)
    egl = jnp.exp(glast)
    zero = jnp.zeros_like(gam[0:1])
    rows = []
    for hh in range(H):
        r = jnp.concatenate([beta[hh:hh + 1], gam[hh:hh + 1], egam[hh:hh + 1], ekd[hh:hh + 1],
                             egl[hh:hh + 1], zero, zero, zero], axis=0)
        grow_ref[hh] = r
        rows.append(r)
    allrows = jnp.concatenate(rows, axis=0)
    cols = None
    for part in _split3(allrows):
        t = _dot_nt(eye, part)
        cols = t if cols is None else cols + t
    for hh in range(H):
        gcol_ref[hh] = cols[:, 8 * hh:8 * hh + 8]


def _gdn_gates(ba, a_log, dt_bias, l):
    B, nb, S = ba.shape
    H = GDN_HEADS
    tm = min(S, GATE_TOKENS)
    vmem = 8 * tm * tm * 4 + 4 * H * tm * V7X_LANES * 4
    return pl.pallas_call(
        functools.partial(_gate_kernel, tm=tm),
        grid=(B, S // tm),
        in_specs=[
            pl.BlockSpec((None, nb, tm), lambda b, s: (b, 0, s)),
            pl.BlockSpec((None, H, 1), lambda b, s: (l, 0, 0)),
            pl.BlockSpec((None, H, 1), lambda b, s: (l, 0, 0)),
        ],
        out_specs=[
            pl.BlockSpec((None, H, 8, tm), lambda b, s: (b, 0, 0, s)),
            pl.BlockSpec((None, H, tm, 8), lambda b, s: (b, 0, s, 0)),
        ],
        out_shape=[
            jax.ShapeDtypeStruct((B, H, 8, S), F32),
            jax.ShapeDtypeStruct((B, H, S, 8), F32),
        ],
        compiler_params=_params(vmem, 2),
        name="gdn_gates",
    )(ba, a_log, dt_bias)


def _gdn_kernel(q_ref, k_ref, v_ref, zs_ref, grow_ref, gcol_ref, og_ref, o_ref, state_ref, *, tc, G):
    s = pl.program_id(2)

    @pl.when(s == 0)
    def _():
        state_ref[...] = jnp.zeros(state_ref.shape, F32)

    heads = range(G)
    n_chunks = tc // CHUNK
    scale = GDN_DK ** -0.5
    ii = lax.broadcasted_iota(jnp.int32, (tc, tc), 0)
    jj = lax.broadcasted_iota(jnp.int32, (tc, tc), 1)
    same = (ii >> LOG2_CHUNK) == (jj >> LOG2_CHUNK)
    causal = same & (ii >= jj)
    strict = same & (ii > jj)
    lane = lax.broadcasted_iota(jnp.int32, (GDN_DK, tc), 1) >> LOG2_CHUNK

    gc = [gcol_ref[g] for g in heads]
    gr = [grow_ref[g] for g in heads]
    k = [k_ref[g] for g in heads]
    kf = [k[g].astype(F32) for g in heads]
    kb = [kf[g] * gc[g][:, 0:1] for g in heads]
    decay = [jnp.where(causal, jnp.exp(jnp.minimum(gc[g][:, 1:2] - gr[g][1:2, :], 0.0)), 0.0) for g in heads]

    akq = [_dot_nt(jnp.concatenate([kb[g].astype(BF16), q_ref[g]], axis=0), k[g]) for g in heads]
    m = [jnp.where(strict, akq[g][:tc] * decay[g], 0.0) for g in heads]
    qk = [(akq[g][tc:] * (decay[g] * scale)).astype(BF16) for g in heads]

    rlane = lax.broadcasted_iota(jnp.int32, (CHUNK, tc), 1)
    rrow = lax.broadcasted_iota(jnp.int32, (CHUNK, tc), 0)
    eye_row = jnp.where((rlane & (CHUNK - 1)) == rrow, 1.0, 0.0)
    rblk = rlane >> LOG2_CHUNK

    def blockdiag(x_row):
        return jnp.concatenate([jnp.where(rblk == c, x_row, 0.0) for c in range(n_chunks)], axis=0).astype(BF16)

    def side_by_side(x):
        acc = x[0:CHUNK]
        for c in range(1, n_chunks):
            acc = acc + x[c * CHUNK:(c + 1) * CHUNK]
        return acc

    m_row = [side_by_side(m[g]) for g in heads]
    p = [_dot(m_row[g].astype(BF16), blockdiag(m_row[g])) for g in heads]
    t_row = [eye_row - m_row[g] for g in heads]
    for _ in range(LOG2_CHUNK - 2):
        r = [_dot(jnp.concatenate([p[g], t_row[g]], axis=0).astype(BF16), blockdiag(p[g])) for g in heads]
        p = [r[g][:CHUNK] for g in heads]
        t_row = [t_row[g] + r[g][CHUNK:] for g in heads]
    t_row = [t_row[g] + _dot(t_row[g].astype(BF16), blockdiag(p[g])) for g in heads]

    rhs = [jnp.concatenate([v_ref[g].astype(F32) * gc[g][:, 0:1], kb[g] * gc[g][:, 2:3]], axis=1).astype(BF16)
           for g in heads]
    uw = [_dot(blockdiag(t_row[g]), rhs[g]).astype(BF16) for g in heads]

    kdt = [(kf[g] * gc[g][:, 3:4]).T for g in heads]
    pre_lhs = [jnp.concatenate([jnp.where(lane == c, kdt[g], 0.0).astype(BF16) for c in range(n_chunks)]
                               + [qk[g]], axis=0) for g in heads]
    pre = [_dot(pre_lhs[g], uw[g]) for g in heads]
    qd = [q_ref[g].astype(F32) * (gc[g][:, 2:3] * scale) for g in heads]
    base = n_chunks * GDN_DK
    qmod = [(qd[g] - pre[g][base:, GDN_DV:]).astype(BF16) for g in heads]

    states = [state_ref[g] for g in heads]
    outs = [[] for _ in heads]
    for c in range(n_chunks):
        r0 = c * CHUNK
        for g in heads:
            kblk = pre[g][c * GDN_DK:(c + 1) * GDN_DK]
            lhs = jnp.concatenate([kblk[:, GDN_DV:].astype(BF16), qmod[g][r0:r0 + CHUNK]], axis=0)
            r = _dot(lhs, states[g].astype(BF16))
            outs[g].append(r[GDN_DK:] + pre[g][base + r0:base + r0 + CHUNK, :GDN_DV])
            states[g] = states[g] * gr[g][4:5, r0:r0 + 1] - r[:GDN_DK] + kblk[:, :GDN_DV]
    for g in heads:
        state_ref[g] = states[g]
        o = jnp.concatenate(outs[g], axis=0)
        o = o * lax.rsqrt(jnp.mean(o * o, axis=-1, keepdims=True) + EPS) * og_ref[...]
        o_ref[:, g * GDN_DV:(g + 1) * GDN_DV] = (o * _silu(zs_ref[g].astype(F32))).astype(BF16)


def _gdn(qkv, zs, grow, gcol, out_gain, l):
    B, _, S, _ = qkv.shape
    H = GDN_HEADS
    G = GDN_GROUP
    tc = min(S, GDN_TOKENS)
    vmem = G * (24 * tc * tc * 4 + 24 * tc * V7X_LANES * 4)
    return pl.pallas_call(
        functools.partial(_gdn_kernel, tc=tc, G=G),
        grid=(B, H // G, S // tc),
        in_specs=[
            pl.BlockSpec((None, G, tc, GDN_DK), lambda b, h, s: (b, h, s, 0)),
            pl.BlockSpec((None, G, tc, GDN_DK), lambda b, h, s: (b, H // G + h, s, 0)),
            pl.BlockSpec((None, G, tc, GDN_DV), lambda b, h, s: (b, 2 * H // G + h, s, 0)),
            pl.BlockSpec((None, G, tc, GDN_DV), lambda b, h, s: (b, h, s, 0)),
            pl.BlockSpec((None, G, 8, tc), lambda b, h, s: (b, h, 0, s)),
            pl.BlockSpec((None, G, tc, 8), lambda b, h, s: (b, h, s, 0)),
            pl.BlockSpec((None, 1, GDN_DV), lambda b, h, s: (l, 0, 0)),
        ],
        out_specs=pl.BlockSpec((None, tc, G * GDN_DV), lambda b, h, s: (b, s, h)),
        out_shape=jax.ShapeDtypeStruct((B, S, H * GDN_DV), BF16),
        scratch_shapes=[pltpu.VMEM((G, GDN_DK, GDN_DV), F32)],
        compiler_params=_params(vmem, 3),
        name="gdn",
    )(qkv, qkv, qkv, zs, grow, gcol, out_gain)


def _swap_halves(y):
    half = y.shape[-1] // 2
    return jnp.concatenate([y[:, half:], y[:, :half]], axis=1)


def _mla_prep_kernel(lat_ref, cos_ref, sin_ref, cost_ref, sint_ref, qg_ref, kvg_ref, wqt_ref, wk_ref, wvt_ref,
                     qnc_ref, kn_ref, qt_ref, k_ref, vt_ref):
    H = MLA_HEADS
    half = QK_ROPE // 2
    lat = lat_ref[...]
    ql = lat[:, :Q_LORA]
    kvl = lat[:, Q_LORA:Q_LORA + KV_LORA]
    kpe = lat[:, Q_LORA + KV_LORA:Q_LORA + KV_LORA + QK_ROPE]
    qln = (ql * lax.rsqrt(jnp.mean(ql * ql, axis=-1, keepdims=True) + EPS) * qg_ref[...]).astype(BF16)
    kvn = (kvl * lax.rsqrt(jnp.mean(kvl * kvl, axis=-1, keepdims=True) + EPS) * kvg_ref[...]).astype(BF16)
    qt = _dot_nt(wqt_ref[...], qln)
    knope = _dot(kvn, wk_ref[...])
    vt = _dot_nt(wvt_ref[...], kvn)

    cost = cost_ref[...]
    sint = sint_ref[...]
    qnc = qnc_ref[...]
    scale = MLA_QK ** -0.5 * math.log2(math.e)
    for hh in range(H):
        blk = qt[hh * MLA_QK:(hh + 1) * MLA_QK]
        r = lax.rsqrt(jnp.sum(blk * blk, axis=0, keepdims=True) * (1.0 / MLA_QK) + EPS) * scale
        y = blk * qnc * r
        y1 = y[QK_NOPE:QK_NOPE + half]
        y2 = y[QK_NOPE + half:]
        qt_ref[hh, 0:QK_NOPE, :] = y[:QK_NOPE].astype(BF16)
        qt_ref[hh, QK_NOPE:QK_NOPE + half, :] = (y1 * cost - y2 * sint).astype(BF16)
        qt_ref[hh, QK_NOPE + half:MLA_QK, :] = (y2 * cost + y1 * sint).astype(BF16)
        vt_ref[hh] = vt[hh * V_HEAD:(hh + 1) * V_HEAD].astype(BF16)

    kn = kn_ref[...]
    yk = kpe * kn[:, QK_NOPE:]
    kr = yk * cos_ref[...] + _swap_halves(yk) * sin_ref[...]
    kpe_ss = jnp.sum(kpe * kpe, axis=-1, keepdims=True)
    for hh in range(H):
        k_nope = knope[:, hh * QK_NOPE:(hh + 1) * QK_NOPE]
        ss = jnp.sum(k_nope * k_nope, axis=-1, keepdims=True) + kpe_ss
        r = lax.rsqrt(ss * (1.0 / MLA_QK) + EPS)
        k_ref[hh, :, 0:QK_NOPE] = (k_nope * r * kn[:, :QK_NOPE]).astype(BF16)
        k_ref[hh, :, QK_NOPE:MLA_QK] = (kr * r).astype(BF16)


def _mla_prep(lat, cos2, sin2, cost, sint, q_lat_gain, kv_lat_gain, wqt, wk, wvt, q_norm_col, k_norm, l):
    B, S, nl = lat.shape
    H = MLA_HEADS
    half = QK_ROPE // 2
    tm = min(S, ATTN_Q)
    nq = wqt.shape[1]
    nk = wk.shape[-1]
    nv = wvt.shape[1]
    vmem = ((nq * Q_LORA + (nk + nv) * KV_LORA) * 2 + 2 * tm * nl * 4 + 3 * tm * (nq + nk + nv) * 4
            + 4 * H * tm * (256 + MLA_QK + V_HEAD) * 2)
    return pl.pallas_call(
        _mla_prep_kernel,
        grid=(B, S // tm),
        in_specs=[
            pl.BlockSpec((None, tm, nl), lambda b, s: (b, s, 0)),
            pl.BlockSpec((None, tm, QK_ROPE), lambda b, s: (b, s, 0)),
            pl.BlockSpec((None, tm, QK_ROPE), lambda b, s: (b, s, 0)),
            pl.BlockSpec((None, half, tm), lambda b, s: (b, 0, s)),
            pl.BlockSpec((None, half, tm), lambda b, s: (b, 0, s)),
            pl.BlockSpec((None, 1, Q_LORA), lambda b, s: (l, 0, 0)),
            pl.BlockSpec((None, 1, KV_LORA), lambda b, s: (l, 0, 0)),
            _resident((None, nq, Q_LORA), lambda b, s: (l, 0, 0)),
            _resident((None, KV_LORA, nk), lambda b, s: (l, 0, 0)),
            _resident((None, nv, KV_LORA), lambda b, s: (l, 0, 0)),
            pl.BlockSpec((None, MLA_QK, 1), lambda b, s: (l, 0, 0)),
            pl.BlockSpec((None, 1, MLA_QK), lambda b, s: (l, 0, 0)),
        ],
        out_specs=[
            pl.BlockSpec((None, H, MLA_QK, tm), lambda b, s: (b, 0, 0, s)),
            pl.BlockSpec((None, H, tm, MLA_QK), lambda b, s: (b, 0, s, 0)),
            pl.BlockSpec((None, H, None, V_HEAD, tm), lambda b, s: (b, 0, s, 0, 0)),
        ],
        out_shape=[
            jax.ShapeDtypeStruct((B, H, MLA_QK, S), BF16),
            jax.ShapeDtypeStruct((B, H, S, MLA_QK), BF16),
            jax.ShapeDtypeStruct((B, H, S // tm, V_HEAD, tm), BF16),
        ],
        compiler_params=_params(vmem, 2),
        name="mla_prep",
    )(lat, cos2, sin2, cost, sint, q_lat_gain, kv_lat_gain, wqt, wk, wvt, q_norm_col, k_norm)


def _attn_kernel(qt_ref, k_ref, vt_ref, o_ref, sa_ref, sb_ref, mxa_ref, mxb_ref, m_ref, l_ref, acc_ref, *, tq):
    p = pl.program_id(2)
    bufs = ((sa_ref, mxa_ref), (sb_ref, mxb_ref))

    def scores(qt, j, buf):
        st = _dot(k_ref[pl.ds(pl.multiple_of(j * tq, tq), tq), :], qt)
        buf[0][...] = st
        buf[1][...] = jnp.max(st, axis=0, keepdims=True)

    def reset(w):
        m_ref[w] = jnp.full(m_ref.shape[1:], -jnp.inf, F32)
        l_ref[w] = jnp.zeros(l_ref.shape[1:], F32)
        acc_ref[w] = jnp.zeros(acc_ref.shape[1:], F32)

    def update(w, j, buf, masked=False):
        st = buf[0][...]
        mx = buf[1][...]
        if masked:
            kpos = lax.broadcasted_iota(jnp.int32, (tq, tq), 0)
            qpos = lax.broadcasted_iota(jnp.int32, (tq, tq), 1)
            st = jnp.where(kpos <= qpos, st, -jnp.inf)
            mx = jnp.max(st, axis=0, keepdims=True)
        m = m_ref[w]
        m_new = jnp.maximum(m, mx)
        alpha = jnp.exp2(m - m_new)
        pr = jnp.exp2(st - m_new)
        m_ref[w] = m_new
        l_ref[w] = alpha * l_ref[w] + jnp.sum(pr, axis=0, keepdims=True)
        acc_ref[w] = alpha * acc_ref[w] + _dot(vt_ref[j], pr.astype(BF16))

    def full_blocks(w, qt, cur, nxt):
        def body(t, _):
            scores(qt, 2 * t + 1, nxt)
            update(w, 2 * t, cur)
            scores(qt, 2 * t + 2, cur)
            update(w, 2 * t + 1, nxt)
            return 0
        lax.fori_loop(0, p, body, 0)

    def finish(w):
        o_ref[w * tq:(w + 1) * tq, :] = (acc_ref[w] / l_ref[w]).T.astype(BF16)

    qt0 = qt_ref[:, 0:tq]
    qt1 = qt_ref[:, tq:2 * tq]
    reset(0)
    scores(qt0, 0, bufs[0])
    full_blocks(0, qt0, bufs[0], bufs[1])
    reset(1)
    scores(qt1, 0, bufs[1])
    update(0, 2 * p, bufs[0], masked=True)
    finish(0)
    full_blocks(1, qt1, bufs[1], bufs[0])
    scores(qt1, 2 * p + 1, bufs[0])
    update(1, 2 * p, bufs[1])
    update(1, 2 * p + 1, bufs[0], masked=True)
    finish(1)


def _attention(qt, k, vt):
    B, H, S, _ = k.shape
    tq = vt.shape[-1]
    assert (S // tq) % 2 == 0, "q blocks are taken in pairs"
    vmem = 8 * tq * tq * 4 + 4 * S * (256 + V_HEAD) * 2 + 8 * tq * 256 * 2 + 16 * tq * V7X_LANES * 4
    return pl.pallas_call(
        functools.partial(_attn_kernel, tq=tq),
        grid=(B, H, S // (2 * tq)),
        in_specs=[
            pl.BlockSpec((None, None, MLA_QK, 2 * tq), lambda b, h, i: (b, h, 0, i)),
            pl.BlockSpec((None, None, S, MLA_QK), lambda b, h, i: (b, h, 0, 0)),
            pl.BlockSpec((None, None, S // tq, V_HEAD, tq), lambda b, h, i: (b, h, 0, 0, 0)),
        ],
        out_specs=pl.BlockSpec((None, 2 * tq, V_HEAD), lambda b, h, i: (b, i, h)),
        out_shape=jax.ShapeDtypeStruct((B, S, H * V_HEAD), BF16),
        scratch_shapes=[pltpu.VMEM((tq, tq), F32), pltpu.VMEM((tq, tq), F32),
                        pltpu.VMEM((1, tq), F32), pltpu.VMEM((1, tq), F32),
                        pltpu.VMEM((2, 1, tq), F32), pltpu.VMEM((2, 1, tq), F32),
                        pltpu.VMEM((2, V_HEAD, tq), F32)],
        compiler_params=_params(vmem, 3),
        name="mla_attention",
    )(qt, k, vt)


def _merge_kernel(x_ref, mod_ref, oa_ref, ob_ref, gates_ref, wa_ref, wb_ref, wo_ref, o_ref):
    D = x_ref.shape[-1]
    mod = mod_ref[...]
    gates = _sigmoid(gates_ref[...].astype(F32))
    ya =_dot(oa_ref[...], wa_ref[...])
    yb = _dot(ob_ref[...], wb_ref[...])
    y = (gates[:, :D] * ya + gates[:, D:] * yb).astype(BF16)
    o_ref[...] = x_ref[...] + (1.0 + mod[5:6]) * _dot(y, wo_ref[...])


def _merge(x, mod, oa, ob, gates, wa, wb, wo, l):
    B, S, D = x.shape
    tm = min(S, MERGE_TOKENS)
    na = oa.shape[-1]
    nbv = ob.shape[-1]
    vmem = (na + nbv + D) * D * 2 + 4 * tm * D * 4 + 2 * tm * (na + nbv + 2 * D) * 2 + 4 * tm * D * 4
    return pl.pallas_call(
        _merge_kernel,
        grid=(B, S // tm),
        in_specs=[
            pl.BlockSpec((None, tm, D), lambda b, s: (b, s, 0)),
            pl.BlockSpec((None, None, N_MOD, D), lambda b, s: (l, b, 0, 0)),
            pl.BlockSpec((None, tm, na), lambda b, s: (b, s, 0)),
            pl.BlockSpec((None, tm, nbv), lambda b, s: (b, s, 0)),
            pl.BlockSpec((None, tm, 2 * D), lambda b, s: (b, s, 0)),
            _resident((None, na, D), lambda b, s: (l, 0, 0)),
            _resident((None, nbv, D), lambda b, s: (l, 0, 0)),
            _resident((None, D, D), lambda b, s: (l, 0, 0)),
        ],
        out_specs=pl.BlockSpec((None, tm, D), lambda b, s: (b, s, 0)),
        out_shape=jax.ShapeDtypeStruct((B, S, D), F32),
        compiler_params=_params(vmem, 2),
        name="merge",
    )(x, mod, oa, ob, gates, wa, wb, wo)


def _pack_w_in(w_in):
    conv_c = 2 * GDN_HEADS * GDN_DK + GDN_HEADS * GDN_DV
    gdn_v = GDN_HEADS * GDN_DV
    D = w_in.shape[1]
    sizes = (conv_c, gdn_v, GDN_HEADS, GDN_HEADS, Q_LORA, KV_LORA, QK_ROPE, D, D)
    offs = np.cumsum((0,) + sizes)
    qkv, z, b_raw, a_raw, q_lat, kv_lat, k_pe, gate_a, gate_b = (
        w_in[:, :, int(offs[i]):int(offs[i + 1])] for i in range(len(sizes)))
    n_lat = Q_LORA + KV_LORA + QK_ROPE
    pad = (-n_lat) % V7X_LANES
    lat = jnp.concatenate([q_lat, kv_lat, k_pe, jnp.zeros(w_in.shape[:2] + (pad,), w_in.dtype)], axis=-1)
    ba = jnp.swapaxes(jnp.concatenate([b_raw, a_raw], axis=-1), 1, 2)
    return (qkv.astype(BF16), z.astype(BF16), jnp.concatenate([gate_a, gate_b], axis=-1).astype(BF16),
            lat.astype(BF16), ba.astype(BF16))


def _pack_kv_up(w_kv_up):
    L = w_kv_up.shape[0]
    w = w_kv_up.reshape(L, KV_LORA, MLA_HEADS, QK_NOPE + V_HEAD)
    wk = w[..., :QK_NOPE].reshape(L, KV_LORA, MLA_HEADS * QK_NOPE)
    wv = w[..., QK_NOPE:].reshape(L, KV_LORA, MLA_HEADS * V_HEAD)
    return wk.astype(BF16), jnp.swapaxes(wv, 1, 2).astype(BF16)


def kernel(x, c, positions, ada_w, ada_b, norm_ffn1, ffn1_w1, ffn1_w3, ffn1_w2, norm_mix, w_in, gdn_conv, gdn_a_log, gdn_dt_bias, gdn_out_gain, mla_q_lat_gain, mla_kv_lat_gain, mla_w_q_up, mla_w_kv_up, mla_q_norm, mla_k_norm, w_branch_a, w_branch_b, w_out, norm_ffn2, ffn2_w1, ffn2_w3, ffn2_w2):
    L = ada_w.shape[0]
    S = x.shape[1]
    assert S % CHUNK == 0 and S % min(S, GDN_TOKENS) == 0 and S % min(S, FFN_TOKENS) == 0
    assert S % min(S, MERGE_TOKENS) == 0 and S % min(S, ATTN_Q) == 0

    mod = _ada_mod(c, ada_w, ada_b)
    cos2, sin2, cost, sint = _rope_tables(positions)

    row = lambda a: a.reshape(L, 1, a.shape[-1])
    col = lambda a: a.reshape(L, a.shape[-1], 1)
    bf = lambda a: a.astype(BF16)
    wqkv, wz, wg, wlat, wba = _pack_w_in(w_in)
    wqt = bf(jnp.swapaxes(mla_w_q_up, 1, 2))
    wk, wvt = _pack_kv_up(mla_w_kv_up)
    f1 = (bf(ffn1_w1), bf(ffn1_w3), bf(ffn1_w2))
    f2 = (bf(ffn2_w1), bf(ffn2_w3), bf(ffn2_w2))
    wa, wb, wo = bf(w_branch_a), bf(w_branch_b), bf(w_out)
    g_ffn1, g_mix, g_ffn2 = row(norm_ffn1), row(norm_mix), row(norm_ffn2)
    og, qlg, kvlg = row(gdn_out_gain), row(mla_q_lat_gain), row(mla_kv_lat_gain)
    qnc, kn = col(mla_q_norm), row(mla_k_norm)
    a_log, dt_bias = col(gdn_a_log), col(gdn_dt_bias)

    for l in range(L):
        x = _ffn(x, mod, g_ffn1, *f1, l, 0)
        qkv, zs, gates, lat, ba = _mixin(x, mod, g_mix, wqkv, wz, wg, wlat, wba, gdn_conv, l)
        grow, gcol = _gdn_gates(ba, a_log, dt_bias, l)
        o_gdn = _gdn(qkv, zs, grow, gcol, og, l)
        qt, k, vt = _mla_prep(lat, cos2, sin2, cost, sint, qlg, kvlg, wqt, wk, wvt, qnc, kn, l)
        o_mla = _attention(qt, k, vt)
        x = _merge(x, mod, o_gdn, o_mla, gates, wa, wb, wo, l)
        x = _ffn(x, mod, g_ffn2, *f2, l, 6)
    return x
```

```python
import functools
import math

import numpy as np
import jax
import jax.numpy as jnp
from jax import lax
from jax.experimental import pallas as pl
from jax.experimental.pallas import tpu as pltpu

F32 = jnp.float32
BF16 = jnp.bfloat16

EPS = 1e-6
N_MOD = 9
GDN_HEADS = 8
GDN_DK = 128
GDN_DV = 128
CONV_K = 4
CHUNK = 64
LOG2_CHUNK = 6
MLA_HEADS = 8
Q_LORA = 384
KV_LORA = 256
QK_NOPE = 128
QK_ROPE = 64
V_HEAD = 128
ROPE_THETA = 10000.0
MLA_QK = QK_NOPE + QK_ROPE

V7X_VMEM_BYTES = 64 * 1024 * 1024
V7X_LANES = 128
V7X_SUBLANES = 8
VMEM_CAP_BYTES = V7X_VMEM_BYTES - 6 * 1024 * 1024

FFN_TOKENS = 512
MIX_TOKENS = 256
GATE_TOKENS = 512
GDN_TOKENS = 256
GDN_GROUP = 8
ATTN_Q = 512
MERGE_TOKENS = 1024
HALO = V7X_SUBLANES


def _params(vmem_bytes, n_axes):
    limit = int(min(VMEM_CAP_BYTES, max(vmem_bytes, 16 * 1024 * 1024)))
    return pltpu.CompilerParams(dimension_semantics=("arbitrary",) * n_axes, vmem_limit_bytes=limit)


def _resident(block_shape, index_map):
    return pl.BlockSpec(block_shape, index_map, pipeline_mode=pl.Buffered(1))


def _dot(a, b):
    return jnp.dot(a, b, preferred_element_type=F32)


def _dot_nt(a, b):
    return lax.dot_general(a, b, (((1,), (1,)), ((), ())), preferred_element_type=F32)


def _sigmoid(x):
    return 0.5 * jnp.tanh(0.5 * x) + 0.5


def _silu(x):
    hx = 0.5 * x
    return hx * jnp.tanh(hx) + hx


def _modulate(x, gain, shift, scale):
    y = x * lax.rsqrt(jnp.mean(x * x, axis=-1, keepdims=True) + EPS)
    return (y * gain) * (1.0 + scale) + shift


def _split3(x):
    x1 = x.astype(BF16)
    r1 = x - x1.astype(F32)
    x2 = r1.astype(BF16)
    x3 = (r1 - x2.astype(F32)).astype(BF16)
    return x1, x2, x3


def _mod_kernel(c_ref, w_ref, b_ref, o_ref):
    cond = _silu(c_ref[...]).astype(BF16)
    o_ref[...] = _dot(cond, w_ref[...].astype(BF16)) + b_ref[...]


def _ada_mod(c, ada_w, ada_b):
    L, D, _ = ada_w.shape
    B = c.shape[0]
    ada_b3 = ada_b.reshape(L, N_MOD, 1, D)
    out = pl.pallas_call(
        _mod_kernel,
        grid=(L, N_MOD),
        in_specs=[
            pl.BlockSpec((B, D), lambda l, j: (0, 0)),
            pl.BlockSpec((None, D, D), lambda l, j: (l, 0, j)),
            pl.BlockSpec((None, None, 1, D), lambda l, j: (l, j, 0, 0)),
        ],
        out_specs=pl.BlockSpec((None, None, B, D), lambda l, j: (l, j, 0, 0)),
        out_shape=jax.ShapeDtypeStruct((L, N_MOD, B, D), F32),
        compiler_params=_params(4 * D * D * 4, 2),
        name="ada_mod",
    )(c, ada_w, ada_b3)
    return jnp.transpose(out, (0, 2, 1, 3))


def _rope_kernel(pos_ref, post_ref, freq_ref, sign_ref, freqc_ref, cos_ref, sin_ref, cost_ref, sint_ref):
    ang = pos_ref[...].astype(F32) * freq_ref[...]
    cos_ref[...] = jnp.cos(ang)
    sin_ref[...] = jnp.sin(ang) * sign_ref[...]
    angt = freqc_ref[...] * post_ref[...].astype(F32)
    cost_ref[...] = jnp.cos(angt)
    sint_ref[...] = jnp.sin(angt)


def _rope_tables(positions):
    B, S = positions.shape
    half = QK_ROPE // 2
    inv_freq = ROPE_THETA ** (-jnp.arange(half, dtype=F32) / half)
    freq2 = jnp.concatenate([inv_freq, inv_freq]).reshape(1, QK_ROPE)
    sign2 = jnp.concatenate([-jnp.ones((half,), F32), jnp.ones((half,), F32)]).reshape(1, QK_ROPE)
    ts = min(S, 1024)
    return pl.pallas_call(
        _rope_kernel,
        grid=(B, S // ts),
        in_specs=[
            pl.BlockSpec((None, ts, 1), lambda b, s: (b, s, 0)),
            pl.BlockSpec((None, 1, ts), lambda b, s: (b, 0, s)),
            pl.BlockSpec((1, QK_ROPE), lambda b, s: (0, 0)),
            pl.BlockSpec((1, QK_ROPE), lambda b, s: (0, 0)),
            pl.BlockSpec((half, 1), lambda b, s: (0, 0)),
        ],
        out_specs=[pl.BlockSpec((None, ts, QK_ROPE), lambda b, s: (b, s, 0))] * 2
        + [pl.BlockSpec((None, half, ts), lambda b, s: (b, 0, s))] * 2,
        out_shape=[jax.ShapeDtypeStruct((B, S, QK_ROPE), F32)] * 2
        + [jax.ShapeDtypeStruct((B, half, S), F32)] * 2,
        compiler_params=_params(8 * ts * V7X_LANES * 4, 2),
        name="rope_tables",
    )(positions.reshape(B, S, 1), positions.reshape(B, 1, S), freq2, sign2, inv_freq.reshape(half, 1))


def _ffn_kernel(x_ref, mod_ref, gain_ref, w1_ref, w3_ref, w2_ref, o_ref, *, row):
    x = x_ref[...]
    mod = mod_ref[...]
    h = _modulate(x, gain_ref[...], mod[row:row + 1], mod[row + 1:row + 2]).astype(BF16)
    a = _dot(h, w1_ref[...])
    b = _dot(h, w3_ref[...])
    g = (_silu(a) * b).astype(BF16)
    y = _dot(g, w2_ref[...])
    o_ref[...] = x + (0.5 * (1.0 + mod[row + 2:row + 3])) * y


def _ffn(x, mod, gain, w1, w3, w2, l, row):
    B, S, D = x.shape
    F = w1.shape[-1]
    tm = min(S, FFN_TOKENS)
    vmem = 3 * D * F * 2 + 4 * tm * D * 4 + 3 * tm * F * 4 + 2 * tm * D * 4
    return pl.pallas_call(
        functools.partial(_ffn_kernel, row=row),
        grid=(B, S // tm),
        in_specs=[
            pl.BlockSpec((None, tm, D), lambda b, s: (b, s, 0)),
            pl.BlockSpec((None, None, N_MOD, D), lambda b, s: (l, b, 0, 0)),
            pl.BlockSpec((None, 1, D), lambda b, s: (l, 0, 0)),
            _resident((None, D, F), lambda b, s: (l, 0, 0)),
            _resident((None, D, F), lambda b, s: (l, 0, 0)),
            _resident((None, F, D), lambda b, s: (l, 0, 0)),
        ],
        out_specs=pl.BlockSpec((None, tm, D), lambda b, s: (b, s, 0)),
        out_shape=jax.ShapeDtypeStruct((B, S, D), F32),
        compiler_params=_params(vmem, 2),
        name="ffn",
    )(x, mod, gain, w1, w3, w2)


def _mixin_kernel(x_ref, mod_ref, gain_ref, wall_ref, wba_ref, conv_ref,
                  qkv_ref, zs_ref, gates_ref, lat_ref, ba_ref, pbuf_ref, *, tm):
    @pl.when(pl.program_id(1) == 0)
    def _():
        pbuf_ref[0:HALO, :] = jnp.zeros((HALO, pbuf_ref.shape[1]), F32)

    x = x_ref[...]
    mod = mod_ref[...]
    h = _modulate(x, gain_ref[...], mod[3:4], mod[4:5]).astype(BF16)

    C = pbuf_ref.shape[1]
    nz = zs_ref.shape[0] * zs_ref.shape[2]
    ng = gates_ref.shape[1]
    proj = _dot(h, wall_ref[...])
    pbuf_ref[HALO:HALO + tm, :] = proj[:, :C]
    conv = conv_ref[...]
    n_qk = 2 * GDN_HEADS
    for cb in range(pbuf_ref.shape[1] // V7X_LANES):
        c0 = cb * V7X_LANES
        xs = pbuf_ref[:, c0:c0 + V7X_LANES]
        acc = conv[CONV_K - 1:CONV_K, c0:c0 + V7X_LANES] * xs[HALO:]
        for d in range(1, CONV_K):
            shifted = pltpu.roll(xs, d, axis=0)[HALO:]
            acc = acc + conv[CONV_K - 1 - d:CONV_K - d, c0:c0 + V7X_LANES] * shifted
        y = _silu(acc)
        if cb < n_qk:
            y = y * lax.rsqrt(jnp.sum(y * y, axis=-1, keepdims=True) + EPS)
        qkv_ref[cb] = y.astype(BF16)
    pbuf_ref[0:HALO, :] = pbuf_ref[tm:tm + HALO, :]

    for hh in range(GDN_HEADS):
        zs_ref[hh] = proj[:, C + hh * GDN_DV:C + (hh + 1) * GDN_DV].astype(BF16)
    gates_ref[...] = proj[:, C + nz:C + nz + ng].astype(BF16)
    lat_ref[...] = proj[:, C + nz + ng:]
    ba_ref[...] = _dot_nt(wba_ref[...], h)


def _mixin(x, mod, gain, wall, wba, conv, l):
    B, S, D = x.shape
    C = conv.shape[-1]
    nz = GDN_HEADS * GDN_DV
    ng = 2 * D
    nl = wall.shape[-1] - C - nz - ng
    nb = wba.shape[1]
    tm = min(S, MIX_TOKENS)
    ncb = C // V7X_LANES
    vmem = ((C + nz + ng + nl) * D * 2 + 2 * tm * D * 4 + 2 * tm * (C + nz + ng) * 2 + 2 * tm * nl * 4
            + (tm + HALO) * C * 4 + tm * C * 4 + tm * (nz + ng) * 4 + 4 * 1024 * 1024)
    return pl.pallas_call(
        functools.partial(_mixin_kernel, tm=tm),
        grid=(B, S // tm),
        in_specs=[
            pl.BlockSpec((None, tm, D), lambda b, s: (b, s, 0)),
            pl.BlockSpec((None, None, N_MOD, D), lambda b, s: (l, b, 0, 0)),
            pl.BlockSpec((None, 1, D), lambda b, s: (l, 0, 0)),
            _resident((None, D, C + nz + ng + nl), lambda b, s: (l, 0, 0)),
            _resident((None, nb, D), lambda b, s: (l, 0, 0)),
            pl.BlockSpec((None, CONV_K, C), lambda b, s: (l, 0, 0)),
        ],
        out_specs=[
            pl.BlockSpec((None, ncb, tm, V7X_LANES), lambda b, s: (b, 0, s, 0)),
            pl.BlockSpec((None, GDN_HEADS, tm, GDN_DV), lambda b, s: (b, 0, s, 0)),
            pl.BlockSpec((None, tm, ng), lambda b, s: (b, s, 0)),
            pl.BlockSpec((None, tm, nl), lambda b, s: (b, s, 0)),
            pl.BlockSpec((None, nb, tm), lambda b, s: (b, 0, s)),
        ],
        out_shape=[
            jax.ShapeDtypeStruct((B, ncb, S, V7X_LANES), BF16),
            jax.ShapeDtypeStruct((B, GDN_HEADS, S, GDN_DV), BF16),
            jax.ShapeDtypeStruct((B, S, ng), BF16),
            jax.ShapeDtypeStruct((B, S, nl), F32),
            jax.ShapeDtypeStruct((B, nb, S), F32),
        ],
        scratch_shapes=[pltpu.VMEM((tm + HALO, C), F32)],
        compiler_params=_params(vmem, 2),
        name="mixin",
    )(x, mod, gain, wall, wba, conv)


def _gate_kernel(ba_ref, alog_ref, dtb_ref, grow_ref, gcol_ref, *, tm):
    H = GDN_HEADS
    ba = ba_ref[...]
    beta = _sigmoid(ba[0:H])
    xa = ba[H:2 * H] + dtb_ref[...]
    softplus = jnp.maximum(xa, 0.0) + jnp.log(1.0 + jnp.exp(-jnp.abs(xa)))
    g = -jnp.exp(alog_ref[...]) * softplus

    ii = lax.broadcasted_iota(jnp.int32, (tm, tm), 0)
    jj = lax.broadcasted_iota(jnp.int32, (tm, tm), 1)
    same = (ii >> LOG2_CHUNK) == (jj >> LOG2_CHUNK)
    upper =jnp.where(same & (ii <= jj), 1.0, 0.0).astype(BF16)
    block = jnp.where(same, 1.0, 0.0).astype(BF16)
    eye = jnp.where(ii == jj, 1.0, 0.0).astype(BF16)
    gam = None
    glast = None
    for part in _split3(g):
        t1 = _dot(part, upper)
        t2 = _dot(part, block)
        gam = t1 if gam is None else gam + t1
        glast = t2 if glast is None else glast + t2
    egam = jnp.exp(gam)
    ekd = jnp.exp(glast - gam)
    egl = jnp.exp(glast)
    zero = jnp.zeros_like(gam[0:1])
    rows = []
    for hh in range(H):
        r = jnp.concatenate([beta[hh:hh + 1], gam[hh:hh + 1], egam[hh:hh + 1], ekd[hh:hh + 1],
                             egl[hh:hh + 1], zero, zero, zero], axis=0)
        grow_ref[hh] = r
        rows.append(r)
    allrows = jnp.concatenate(rows, axis=0)
    cols = None
    for part in _split3(allrows):
        t = _dot_nt(eye, part)
        cols = t if cols is None else cols + t
    for hh in range(H):
        gcol_ref[hh] = cols[:, 8 * hh:8 * hh + 8]


def _gdn_gates(ba, a_log, dt_bias, l):
    B, nb, S = ba.shape
    H = GDN_HEADS
    tm = min(S, GATE_TOKENS)
    vmem = 8 * tm * tm * 4 + 4 * H * tm * V7X_LANES * 4
    return pl.pallas_call(
        functools.partial(_gate_kernel, tm=tm),
        grid=(B, S // tm),
        in_specs=[
            pl.BlockSpec((None, nb, tm), lambda b, s: (b, 0, s)),
            pl.BlockSpec((None, H, 1), lambda b, s: (l, 0, 0)),
            pl.BlockSpec((None, H, 1), lambda b, s: (l, 0, 0)),
        ],
        out_specs=[
            pl.BlockSpec((None, H, 8, tm), lambda b, s: (b, 0, 0, s)),
            pl.BlockSpec((None, H, tm, 8), lambda b, s: (b, 0, s, 0)),
        ],
        out_shape=[
            jax.ShapeDtypeStruct((B, H, 8, S), F32),
            jax.ShapeDtypeStruct((B, H, S, 8), F32),
        ],
        compiler_params=_params(vmem, 2),
        name="gdn_gates",
    )(ba, a_log, dt_bias)


def _gdn_kernel(q_ref, k_ref, v_ref, zs_ref, grow_ref, gcol_ref, og_ref, o_ref, state_ref, *, tc, G):
    s = pl.program_id(2)

    @pl.when(s == 0)
    def _():
        state_ref[...] = jnp.zeros(state_ref.shape, F32)

    heads = range(G)
    n_chunks = tc // CHUNK
    scale = GDN_DK ** -0.5
    ii = lax.broadcasted_iota(jnp.int32, (tc, tc), 0)
    jj = lax.broadcasted_iota(jnp.int32, (tc, tc), 1)
    same = (ii >> LOG2_CHUNK) == (jj >> LOG2_CHUNK)
    causal = same & (ii >= jj)
    strict = same & (ii > jj)
    lane = lax.broadcasted_iota(jnp.int32, (GDN_DK, tc), 1) >> LOG2_CHUNK

    gc = [gcol_ref[g] for g in heads]
    gr = [grow_ref[g] for g in heads]
    k = [k_ref[g] for g in heads]
    kf = [k[g].astype(F32) for g in heads]
    kb = [kf[g] * gc[g][:, 0:1] for g in heads]
    decay = [jnp.where(causal, jnp.exp(jnp.minimum(gc[g][:, 1:2] - gr[g][1:2, :], 0.0)), 0.0) for g in heads]

    akq = [_dot_nt(jnp.concatenate([kb[g].astype(BF16), q_ref[g]], axis=0), k[g]) for g in heads]
    m = [jnp.where(strict, akq[g][:tc] * decay[g], 0.0) for g in heads]
    qk = [(akq[g][tc:] * (decay[g] * scale)).astype(BF16) for g in heads]

    rlane = lax.broadcasted_iota(jnp.int32, (CHUNK, tc), 1)
    rrow = lax.broadcasted_iota(jnp.int32, (CHUNK, tc), 0)
    eye_row = jnp.where((rlane & (CHUNK - 1)) == rrow, 1.0, 0.0)
    rblk = rlane >> LOG2_CHUNK

    def blockdiag(x_row):
        return jnp.concatenate([jnp.where(rblk == c, x_row, 0.0) for c in range(n_chunks)], axis=0).astype(BF16)

    def side_by_side(x):
        acc = x[0:CHUNK]
        for c in range(1, n_chunks):
            acc = acc + x[c * CHUNK:(c + 1) * CHUNK]
        return acc

    m_row = [side_by_side(m[g]) for g in heads]
    p = [_dot(m_row[g].astype(BF16), blockdiag(m_row[g])) for g in heads]
    t_row = [eye_row - m_row[g] for g in heads]
    for _ in range(LOG2_CHUNK - 2):
        r = [_dot(jnp.concatenate([p[g], t_row[g]], axis=0).astype(BF16), blockdiag(p[g])) for g in heads]
        p = [r[g][:CHUNK] for g in heads]
        t_row = [t_row[g] + r[g][CHUNK:] for g in heads]
    t_row = [t_row[g] + _dot(t_row[g].astype(BF16), blockdiag(p[g])) for g in heads]

    rhs = [jnp.concatenate([v_ref[g].astype(F32) * gc[g][:, 0:1], kb[g] * gc[g][:, 2:3]], axis=1).astype(BF16)
           for g in heads]
    uw = [_dot(blockdiag(t_row[g]), rhs[g]).astype(BF16) for g in heads]

    kdt = [(kf[g] * gc[g][:, 3:4]).T for g in heads]
    pre_lhs = [jnp.concatenate([jnp.where(lane == c, kdt[g], 0.0).astype(BF16) for c in range(n_chunks)]
                               + [qk[g]], axis=0) for g in heads]
    pre = [_dot(pre_lhs[g], uw[g]) for g in heads]
    qd = [q_ref[g].astype(F32) * (gc[g][:, 2:3] * scale) for g in heads]
    base = n_chunks * GDN_DK
    qmod = [(qd[g] - pre[g][base:, GDN_DV:]).astype(BF16) for g in heads]

    states = [state_ref[g] for g in heads]
    outs = [[] for _ in heads]
    for c in range(n_chunks):
        r0 = c * CHUNK
        for g in heads:
            kblk = pre[g][c * GDN_DK:(c + 1) * GDN_DK]
            lhs = jnp.concatenate([kblk[:, GDN_DV:].astype(BF16), qmod[g][r0:r0 + CHUNK]], axis=0)
            r = _dot(lhs, states[g].astype(BF16))
            outs[g].append(r[GDN_DK:] + pre[g][base + r0:base + r0 + CHUNK, :GDN_DV])
            states[g] = states[g] * gr[g][4:5, r0:r0 + 1] - r[:GDN_DK] + kblk[:, :GDN_DV]
    for g in heads:
        state_ref[g] = states[g]
        o = jnp.concatenate(outs[g], axis=0)
        o = o * lax.rsqrt(jnp.mean(o * o, axis=-1, keepdims=True) + EPS) * og_ref[...]
        o_ref[:, g * GDN_DV:(g + 1) * GDN_DV] = (o * _silu(zs_ref[g].astype(F32))).astype(BF16)


def _gdn(qkv, zs, grow, gcol, out_gain, l):
    B, _, S, _ = qkv.shape
    H = GDN_HEADS
    G = GDN_GROUP
    tc = min(S, GDN_TOKENS)
    vmem = G * (24 * tc * tc * 4 + 24 * tc * V7X_LANES * 4)
    return pl.pallas_call(
        functools.partial(_gdn_kernel, tc=tc, G=G),
        grid=(B, H // G, S // tc),
        in_specs=[
            pl.BlockSpec((None, G, tc, GDN_DK), lambda b, h, s: (b, h, s, 0)),
            pl.BlockSpec((None, G, tc, GDN_DK), lambda b, h, s: (b, H // G + h, s, 0)),
            pl.BlockSpec((None, G, tc, GDN_DV), lambda b, h, s: (b, 2 * H // G + h, s, 0)),
            pl.BlockSpec((None, G, tc, GDN_DV), lambda b, h, s: (b, h, s, 0)),
            pl.BlockSpec((None, G, 8, tc), lambda b, h, s: (b, h, 0, s)),
            pl.BlockSpec((None, G, tc, 8), lambda b, h, s: (b, h, s, 0)),
            pl.BlockSpec((None, 1, GDN_DV), lambda b, h, s: (l, 0, 0)),
        ],
        out_specs=pl.BlockSpec((None, tc, G * GDN_DV), lambda b, h, s: (b, s, h)),
        out_shape=jax.ShapeDtypeStruct((B, S, H * GDN_DV), BF16),
        scratch_shapes=[pltpu.VMEM((G, GDN_DK, GDN_DV), F32)],
        compiler_params=_params(vmem, 3),
        name="gdn",
    )(qkv, qkv, qkv, zs, grow, gcol, out_gain)


def _swap_halves(y):
    half = y.shape[-1] // 2
    return jnp.concatenate([y[:, half:], y[:, :half]], axis=1)


def _mla_prep_kernel(lat_ref, cos_ref, sin_ref, cost_ref, sint_ref, qg_ref, kvg_ref, wqt_ref, wk_ref, wvt_ref,
                     qnc_ref, kn_ref, qt_ref, k_ref, vt_ref):
    H = MLA_HEADS
    half = QK_ROPE // 2
    lat = lat_ref[...]
    ql = lat[:, :Q_LORA]
    kvl = lat[:, Q_LORA:Q_LORA + KV_LORA]
    kpe = lat[:, Q_LORA + KV_LORA:Q_LORA + KV_LORA + QK_ROPE]
    qln = (ql * lax.rsqrt(jnp.mean(ql * ql, axis=-1, keepdims=True) + EPS) * qg_ref[...]).astype(BF16)
    kvn = (kvl * lax.rsqrt(jnp.mean(kvl * kvl, axis=-1, keepdims=True) + EPS) * kvg_ref[...]).astype(BF16)
    qt = _dot_nt(wqt_ref[...], qln)
    knope = _dot(kvn, wk_ref[...])
    vt = _dot_nt(wvt_ref[...], kvn)

    cost = cost_ref[...]
    sint = sint_ref[...]
    qnc = qnc_ref[...]
    scale = MLA_QK ** -0.5 * math.log2(math.e)
    for hh in range(H):
        blk = qt[hh * MLA_QK:(hh + 1) * MLA_QK]
        r = lax.rsqrt(jnp.sum(blk * blk, axis=0, keepdims=True) * (1.0 / MLA_QK) + EPS) * scale
        y = blk * qnc * r
        y1 = y[QK_NOPE:QK_NOPE + half]
        y2 = y[QK_NOPE + half:]
        qt_ref[hh, 0:QK_NOPE, :] = y[:QK_NOPE].astype(BF16)
        qt_ref[hh, QK_NOPE:QK_NOPE + half, :] = (y1 * cost - y2 * sint).astype(BF16)
        qt_ref[hh, QK_NOPE + half:MLA_QK, :] = (y2 * cost + y1 * sint).astype(BF16)
        vt_ref[hh] = vt[hh * V_HEAD:(hh + 1) * V_HEAD].astype(BF16)

    kn = kn_ref[...]
    yk = kpe * kn[:, QK_NOPE:]
    kr = yk * cos_ref[...] + _swap_halves(yk) * sin_ref[...]
    kpe_ss = jnp.sum(kpe * kpe, axis=-1, keepdims=True)
    for hh in range(H):
        k_nope = knope[:, hh * QK_NOPE:(hh + 1) * QK_NOPE]
        ss = jnp.sum(k_nope * k_nope, axis=-1, keepdims=True) + kpe_ss
        r = lax.rsqrt(ss * (1.0 / MLA_QK) + EPS)
        k_ref[hh, :, 0:QK_NOPE] = (k_nope * r * kn[:, :QK_NOPE]).astype(BF16)
        k_ref[hh, :, QK_NOPE:MLA_QK] = (kr * r).astype(BF16)


def _mla_prep(lat, cos2, sin2, cost, sint, q_lat_gain, kv_lat_gain, wqt, wk, wvt, q_norm_col, k_norm, l):
    B, S, nl = lat.shape
    H = MLA_HEADS
    half = QK_ROPE // 2
    tm = min(S, ATTN_Q)
    nq = wqt.shape[1]
    nk = wk.shape[-1]
    nv = wvt.shape[1]
    vmem = ((nq * Q_LORA + (nk + nv) * KV_LORA) * 2 + 2 * tm * nl * 4 + 3 * tm * (nq + nk + nv) * 4
            + 4 * H * tm * (256 + MLA_QK + V_HEAD) * 2)
    return pl.pallas_call(
        _mla_prep_kernel,
        grid=(B, S // tm),
        in_specs=[
            pl.BlockSpec((None, tm, nl), lambda b, s: (b, s, 0)),
            pl.BlockSpec((None, tm, QK_ROPE), lambda b, s: (b, s, 0)),
            pl.BlockSpec((None, tm, QK_ROPE), lambda b, s: (b, s, 0)),
            pl.BlockSpec((None, half, tm), lambda b, s: (b, 0, s)),
            pl.BlockSpec((None, half, tm), lambda b, s: (b, 0, s)),
            pl.BlockSpec((None, 1, Q_LORA), lambda b, s: (l, 0, 0)),
            pl.BlockSpec((None, 1, KV_LORA), lambda b, s: (l, 0, 0)),
            _resident((None, nq, Q_LORA), lambda b, s: (l, 0, 0)),
            _resident((None, KV_LORA, nk), lambda b, s: (l, 0, 0)),
            _resident((None, nv, KV_LORA), lambda b, s: (l, 0, 0)),
            pl.BlockSpec((None, MLA_QK, 1), lambda b, s: (l, 0, 0)),
            pl.BlockSpec((None, 1, MLA_QK), lambda b, s: (l, 0, 0)),
        ],
        out_specs=[
            pl.BlockSpec((None, H, MLA_QK, tm), lambda b, s: (b, 0, 0, s)),
            pl.BlockSpec((None, H, tm, MLA_QK), lambda b, s: (b, 0, s, 0)),
            pl.BlockSpec((None, H, None, V_HEAD, tm), lambda b, s: (b, 0, s, 0, 0)),
        ],
        out_shape=[
            jax.ShapeDtypeStruct((B, H, MLA_QK, S), BF16),
            jax.ShapeDtypeStruct((B, H, S, MLA_QK), BF16),
            jax.ShapeDtypeStruct((B, H, S // tm, V_HEAD, tm), BF16),
        ],
        compiler_params=_params(vmem, 2),
        name="mla_prep",
    )(lat, cos2, sin2, cost, sint, q_lat_gain, kv_lat_gain, wqt, wk, wvt, q_norm_col, k_norm)


def _attn_kernel(qt_ref, k_ref, vt_ref, o_ref, sa_ref, sb_ref, mxa_ref, mxb_ref, m_ref, l_ref, acc_ref, *, tq):
    p = pl.program_id(2)
    bufs = ((sa_ref, mxa_ref), (sb_ref, mxb_ref))

    def scores(qt, j, buf):
        st = _dot(k_ref[pl.ds(pl.multiple_of(j * tq, tq), tq), :], qt)
        buf[0][...] = st
        buf[1][...] = jnp.max(st, axis=0, keepdims=True)

    def reset(w):
        m_ref[w] = jnp.full(m_ref.shape[1:], -jnp.inf, F32)
        l_ref[w] = jnp.zeros(l_ref.shape[1:], F32)
        acc_ref[w] = jnp.zeros(acc_ref.shape[1:], F32)

    def update(w, j, buf, masked=False):
        st = buf[0][...]
        mx = buf[1][...]
        if masked:
            kpos = lax.broadcasted_iota(jnp.int32, (tq, tq), 0)
            qpos = lax.broadcasted_iota(jnp.int32, (tq, tq), 1)
            st = jnp.where(kpos <= qpos, st, -jnp.inf)
            mx = jnp.max(st, axis=0, keepdims=True)
        m = m_ref[w]
        m_new = jnp.maximum(m, mx)
        alpha = jnp.exp2(m - m_new)
        pr = jnp.exp2(st - m_new)
        m_ref[w] = m_new
        l_ref[w] = alpha * l_ref[w] + jnp.sum(pr, axis=0, keepdims=True)
        acc_ref[w] = alpha * acc_ref[w] + _dot(vt_ref[j], pr.astype(BF16))

    def full_blocks(w, qt, cur, nxt):
        def body(t, _):
            scores(qt, 2 * t + 1, nxt)
            update(w, 2 * t, cur)
            scores(qt, 2 * t + 2, cur)
            update(w, 2 * t + 1, nxt)
            return 0
        lax.fori_loop(0, p, body, 0)

    def finish(w):
        o_ref[w * tq:(w + 1) * tq, :] = (acc_ref[w] / l_ref[w]).T.astype(BF16)

    qt0 = qt_ref[:, 0:tq]
    qt1 = qt_ref[:, tq:2 * tq]
    reset(0)
    scores(qt0, 0, bufs[0])
    full_blocks(0, qt0, bufs[0], bufs[1])
    reset(1)
    scores(qt1, 0, bufs[1])
    update(0, 2 * p, bufs[0], masked=True)
    finish(0)
    full_blocks(1, qt1, bufs[1], bufs[0])
    scores(qt1, 2 * p + 1, bufs[0])
    update(1, 2 * p, bufs[1])
    update(1, 2 * p + 1, bufs[0], masked=True)
    finish(1)


def _attention(qt, k, vt):
    B, H, S, _ = k.shape
    tq = vt.shape[-1]
    assert (S // tq) % 2 == 0, "q blocks are taken in pairs"
    vmem = 8 * tq * tq * 4 + 4 * S * (256 + V_HEAD) * 2 + 8 * tq * 256 * 2 + 16 * tq * V7X_LANES * 4
    return pl.pallas_call(
        functools.partial(_attn_kernel, tq=tq),
        grid=(B, H, S // (2 * tq)),
        in_specs=[
            pl.BlockSpec((None, None, MLA_QK, 2 * tq), lambda b, h, i: (b, h, 0, i)),
            pl.BlockSpec((None, None, S, MLA_QK), lambda b, h, i: (b, h, 0, 0)),
            pl.BlockSpec((None, None, S // tq, V_HEAD, tq), lambda b, h, i: (b, h, 0, 0, 0)),
        ],
        out_specs=pl.BlockSpec((None, 2 * tq, V_HEAD), lambda b, h, i: (b, i, h)),
        out_shape=jax.ShapeDtypeStruct((B, S, H * V_HEAD), BF16),
        scratch_shapes=[pltpu.VMEM((tq, tq), F32), pltpu.VMEM((tq, tq), F32),
                        pltpu.VMEM((1, tq), F32), pltpu.VMEM((1, tq), F32),
                        pltpu.VMEM((2, 1, tq), F32), pltpu.VMEM((2, 1, tq), F32),
                        pltpu.VMEM((2, V_HEAD, tq), F32)],
        compiler_params=_params(vmem, 3),
        name="mla_attention",
    )(qt, k, vt)


def _merge_kernel(x_ref, mod_ref, oa_ref, ob_ref, gates_ref, wa_ref, wb_ref, wo_ref, o_ref):
    D = x_ref.shape[-1]
    mod = mod_ref[...]
    gates = _sigmoid(gates_ref[...].astype(F32))
    ya =_dot(oa_ref[...], wa_ref[...])
    yb = _dot(ob_ref[...], wb_ref[...])
    y = (gates[:, :D] * ya + gates[:, D:] * yb).astype(BF16)
    o_ref[...] = x_ref[...] + (1.0 + mod[5:6]) * _dot(y, wo_ref[...])


def _merge(x, mod, oa, ob, gates, wa, wb, wo, l):
    B, S, D = x.shape
    tm = min(S, MERGE_TOKENS)
    na = oa.shape[-1]
    nbv = ob.shape[-1]
    vmem = (na + nbv + D) * D * 2 + 4 * tm * D * 4 + 2 * tm * (na + nbv + 2 * D) * 2 + 4 * tm * D * 4
    return pl.pallas_call(
        _merge_kernel,
        grid=(B, S // tm),
        in_specs=[
            pl.BlockSpec((None, tm, D), lambda b, s: (b, s, 0)),
            pl.BlockSpec((None, None, N_MOD, D), lambda b, s: (l, b, 0, 0)),
            pl.BlockSpec((None, tm, na), lambda b, s: (b, s, 0)),
            pl.BlockSpec((None, tm, nbv), lambda b, s: (b, s, 0)),
            pl.BlockSpec((None, tm, 2 * D), lambda b, s: (b, s, 0)),
            _resident((None, na, D), lambda b, s: (l, 0, 0)),
            _resident((None, nbv, D), lambda b, s: (l, 0, 0)),
            _resident((None, D, D), lambda b, s: (l, 0, 0)),
        ],
        out_specs=pl.BlockSpec((None, tm, D), lambda b, s: (b, s, 0)),
        out_shape=jax.ShapeDtypeStruct((B, S, D), F32),
        compiler_params=_params(vmem, 2),
        name="merge",
    )(x, mod, oa, ob, gates, wa, wb, wo)


def _pack_w_in(w_in):
    conv_c = 2 * GDN_HEADS * GDN_DK + GDN_HEADS * GDN_DV
    gdn_v = GDN_HEADS * GDN_DV
    D = w_in.shape[1]
    sizes = (conv_c, gdn_v, GDN_HEADS, GDN_HEADS, Q_LORA, KV_LORA, QK_ROPE, D, D)
    offs = np.cumsum((0,) + sizes)
    qkv, z, b_raw, a_raw, q_lat, kv_lat, k_pe, gate_a, gate_b = (
        w_in[:, :, int(offs[i]):int(offs[i + 1])] for i in range(len(sizes)))
    n_lat = Q_LORA + KV_LORA + QK_ROPE
    pad = (-n_lat) % V7X_LANES
    lat = jnp.concatenate([q_lat, kv_lat, k_pe, jnp.zeros(w_in.shape[:2] + (pad,), w_in.dtype)], axis=-1)
    ba = jnp.swapaxes(jnp.concatenate([b_raw, a_raw], axis=-1), 1, 2)
    wall = jnp.concatenate([qkv, z, gate_a, gate_b, lat], axis=-1)
    return wall.astype(BF16), ba.astype(BF16)


def _pack_kv_up(w_kv_up):
    L = w_kv_up.shape[0]
    w = w_kv_up.reshape(L, KV_LORA, MLA_HEADS, QK_NOPE + V_HEAD)
    wk = w[..., :QK_NOPE].reshape(L, KV_LORA, MLA_HEADS * QK_NOPE)
    wv = w[..., QK_NOPE:].reshape(L, KV_LORA, MLA_HEADS * V_HEAD)
    return wk.astype(BF16), jnp.swapaxes(wv, 1, 2).astype(BF16)


def kernel(x, c, positions, ada_w, ada_b, norm_ffn1, ffn1_w1, ffn1_w3, ffn1_w2, norm_mix, w_in, gdn_conv, gdn_a_log, gdn_dt_bias, gdn_out_gain, mla_q_lat_gain, mla_kv_lat_gain, mla_w_q_up, mla_w_kv_up, mla_q_norm, mla_k_norm, w_branch_a, w_branch_b, w_out, norm_ffn2, ffn2_w1, ffn2_w3, ffn2_w2):
    L = ada_w.shape[0]
    S = x.shape[1]
    assert S % CHUNK == 0 and S % min(S, GDN_TOKENS) == 0 and S % min(S, FFN_TOKENS) == 0
    assert S % min(S, MERGE_TOKENS) == 0 and S % min(S, ATTN_Q) == 0

    mod = _ada_mod(c, ada_w, ada_b)
    cos2, sin2, cost, sint = _rope_tables(positions)

    row = lambda a: a.reshape(L, 1, a.shape[-1])
    col = lambda a: a.reshape(L, a.shape[-1], 1)
    bf = lambda a: a.astype(BF16)
    wall, wba = _pack_w_in(w_in)
    wqt = bf(jnp.swapaxes(mla_w_q_up, 1, 2))
    wk, wvt = _pack_kv_up(mla_w_kv_up)
    f1 = (bf(ffn1_w1), bf(ffn1_w3), bf(ffn1_w2))
    f2 = (bf(ffn2_w1), bf(ffn2_w3), bf(ffn2_w2))
    wa, wb, wo = bf(w_branch_a), bf(w_branch_b), bf(w_out)
    g_ffn1, g_mix, g_ffn2 = row(norm_ffn1), row(norm_mix), row(norm_ffn2)
    og, qlg, kvlg = row(gdn_out_gain), row(mla_q_lat_gain), row(mla_kv_lat_gain)
    qnc, kn = col(mla_q_norm), row(mla_k_norm)
    a_log, dt_bias = col(gdn_a_log), col(gdn_dt_bias)

    for l in range(L):
        x = _ffn(x, mod, g_ffn1, *f1, l, 0)
        qkv, zs, gates, lat, ba = _mixin(x, mod, g_mix, wall, wba, gdn_conv, l)
        grow, gcol = _gdn_gates(ba, a_log, dt_bias, l)
        o_gdn = _gdn(qkv, zs, grow, gcol, og, l)
        qt, k, vt = _mla_prep(lat, cos2, sin2, cost, sint, qlg, kvlg, wqt, wk, wvt, qnc, kn, l)
        o_mla = _attention(qt, k, vt)
        x = _merge(x, mod, o_gdn, o_mla, gates, wa, wb, wo, l)
        x = _ffn(x, mod, g_ffn2, *f2, l, 6)
    return x
```

```python
import functools
import math

import numpy as np
import jax
import jax.numpy as jnp
from jax import lax
from jax.experimental import pallas as pl
from jax.experimental.pallas import tpu as pltpu

F32 = jnp.float32
BF16 = jnp.bfloat16

EPS = 1e-6
N_MOD = 9
GDN_HEADS = 8
GDN_DK = 128
GDN_DV = 128
CONV_K = 4
CHUNK = 64
LOG2_CHUNK = 6
MLA_HEADS = 8
Q_LORA = 384
KV_LORA = 256
QK_NOPE = 128
QK_ROPE = 64
V_HEAD = 128
ROPE_THETA = 10000.0
MLA_QK = QK_NOPE + QK_ROPE

V7X_VMEM_BYTES = 64 * 1024 * 1024
V7X_LANES = 128
V7X_SUBLANES = 8
VMEM_CAP_BYTES = V7X_VMEM_BYTES - 6 * 1024 * 1024

FFN_TOKENS = 512
MIX_TOKENS = 256
GATE_TOKENS = 512
GDN_TOKENS = 256
GDN_GROUP = 8
ATTN_Q = 512
ATTN_CHAIN = 8
MERGE_TOKENS = 1024
HALO = V7X_SUBLANES


def _params(vmem_bytes, n_axes):
    limit = int(min(VMEM_CAP_BYTES, max(vmem_bytes, 16 * 1024 * 1024)))
    return pltpu.CompilerParams(dimension_semantics=("arbitrary",) * n_axes, vmem_limit_bytes=limit)


def _resident(block_shape, index_map):
    return pl.BlockSpec(block_shape, index_map, pipeline_mode=pl.Buffered(1))


def _dot(a, b):
    return jnp.dot(a, b, preferred_element_type=F32)


def _dot_nt(a, b):
    return lax.dot_general(a, b, (((1,), (1,)), ((), ())), preferred_element_type=F32)


def _sigmoid(x):
    return 0.5 * jnp.tanh(0.5 * x) + 0.5


def _silu(x):
    hx = 0.5 * x
    return hx * jnp.tanh(hx) + hx


def _modulate(x, gain, shift, scale):
    y = x * lax.rsqrt(jnp.mean(x * x, axis=-1, keepdims=True) + EPS)
    return (y * gain) * (1.0 + scale) + shift


def _split3(x):
    x1 = x.astype(BF16)
    r1 = x - x1.astype(F32)
    x2 = r1.astype(BF16)
    x3 = (r1 - x2.astype(F32)).astype(BF16)
    return x1, x2, x3


def _mod_kernel(c_ref, w_ref, b_ref, o_ref):
    cond = _silu(c_ref[...]).astype(BF16)
    o_ref[...] = _dot(cond, w_ref[...].astype(BF16)) + b_ref[...]


def _ada_mod(c, ada_w, ada_b):
    L, D, _ = ada_w.shape
    B = c.shape[0]
    ada_b3 = ada_b.reshape(L, N_MOD, 1, D)
    out = pl.pallas_call(
        _mod_kernel,
        grid=(L, N_MOD),
        in_specs=[
            pl.BlockSpec((B, D), lambda l, j: (0, 0)),
            pl.BlockSpec((None, D, D), lambda l, j: (l, 0, j)),
            pl.BlockSpec((None, None, 1, D), lambda l, j: (l, j, 0, 0)),
        ],
        out_specs=pl.BlockSpec((None, None, B, D), lambda l, j: (l, j, 0, 0)),
        out_shape=jax.ShapeDtypeStruct((L, N_MOD, B, D), F32),
        compiler_params=_params(4 * D * D * 4, 2),
        name="ada_mod",
    )(c, ada_w, ada_b3)
    return jnp.transpose(out, (0, 2, 1, 3))


def _rope_kernel(pos_ref, post_ref, freq_ref, sign_ref, freqc_ref, cos_ref, sin_ref, cost_ref, sint_ref):
    ang = pos_ref[...].astype(F32) * freq_ref[...]
    cos_ref[...] = jnp.cos(ang)
    sin_ref[...] = jnp.sin(ang) * sign_ref[...]
    angt = freqc_ref[...] * post_ref[...].astype(F32)
    cost_ref[...] = jnp.cos(angt)
    sint_ref[...] = jnp.sin(angt)


def _rope_tables(positions):
    B, S = positions.shape
    half = QK_ROPE // 2
    inv_freq = ROPE_THETA ** (-jnp.arange(half, dtype=F32) / half)
    freq2 = jnp.concatenate([inv_freq, inv_freq]).reshape(1, QK_ROPE)
    sign2 = jnp.concatenate([-jnp.ones((half,), F32), jnp.ones((half,), F32)]).reshape(1, QK_ROPE)
    ts = min(S, 1024)
    return pl.pallas_call(
        _rope_kernel,
        grid=(B, S // ts),
        in_specs=[
            pl.BlockSpec((None, ts, 1), lambda b, s: (b, s, 0)),
            pl.BlockSpec((None, 1, ts), lambda b, s: (b, 0, s)),
            pl.BlockSpec((1, QK_ROPE), lambda b, s: (0, 0)),
            pl.BlockSpec((1, QK_ROPE), lambda b, s: (0, 0)),
            pl.BlockSpec((half, 1), lambda b, s: (0, 0)),
        ],
        out_specs=[pl.BlockSpec((None, ts, QK_ROPE), lambda b, s: (b, s, 0))] * 2
        + [pl.BlockSpec((None, half, ts), lambda b, s: (b, 0, s))] * 2,
        out_shape=[jax.ShapeDtypeStruct((B, S, QK_ROPE), F32)] * 2
        + [jax.ShapeDtypeStruct((B, half, S), F32)] * 2,
        compiler_params=_params(8 * ts * V7X_LANES * 4, 2),
        name="rope_tables",
    )(positions.reshape(B, S, 1), positions.reshape(B, 1, S), freq2, sign2, inv_freq.reshape(half, 1))


def _ffn_kernel(x_ref, mod_ref, gain_ref, w1_ref, w3_ref, w2_ref, o_ref, *, row):
    x = x_ref[...]
    mod = mod_ref[...]
    h = _modulate(x, gain_ref[...], mod[row:row + 1], mod[row + 1:row + 2]).astype(BF16)
    a = _dot(h, w1_ref[...])
    b = _dot(h, w3_ref[...])
    g = (_silu(a) * b).astype(BF16)
    y = _dot(g, w2_ref[...])
    o_ref[...] = x + (0.5 * (1.0 + mod[row + 2:row + 3])) * y


def _ffn(x, mod, gain, w1, w3, w2, l, row):
    B, S, D = x.shape
    F = w1.shape[-1]
    tm = min(S, FFN_TOKENS)
    vmem = 3 * D * F * 2 + 4 * tm * D * 4 + 3 * tm * F * 4 + 2 * tm * D * 4
    return pl.pallas_call(
        functools.partial(_ffn_kernel, row=row),
        grid=(B, S // tm),
        in_specs=[
            pl.BlockSpec((None, tm, D), lambda b, s: (b, s, 0)),
            pl.BlockSpec((None, None, N_MOD, D), lambda b, s: (l, b, 0, 0)),
            pl.BlockSpec((None, 1, D), lambda b, s: (l, 0, 0)),
            _resident((None, D, F), lambda b, s: (l, 0, 0)),
            _resident((None, D, F), lambda b, s: (l, 0, 0)),
            _resident((None, F, D), lambda b, s: (l, 0, 0)),
        ],
        out_specs=pl.BlockSpec((None, tm, D), lambda b, s: (b, s, 0)),
        out_shape=jax.ShapeDtypeStruct((B, S, D), F32),
        compiler_params=_params(vmem, 2),
        name="ffn",
    )(x, mod, gain, w1, w3, w2)


def _mixin_kernel(x_ref, mod_ref, gain_ref, wall_ref, wba_ref, conv_ref,
                  qkv_ref, zs_ref, gates_ref, lat_ref, ba_ref, pbuf_ref, *, tm):
    @pl.when(pl.program_id(1) == 0)
    def _():
        pbuf_ref[0:HALO, :] = jnp.zeros((HALO, pbuf_ref.shape[1]), F32)

    x = x_ref[...]
    mod = mod_ref[...]
    h = _modulate(x, gain_ref[...], mod[3:4], mod[4:5]).astype(BF16)

    C = pbuf_ref.shape[1]
    nz = zs_ref.shape[0] * zs_ref.shape[2]
    ng = gates_ref.shape[1]
    proj = _dot(h, wall_ref[...])
    pbuf_ref[HALO:HALO + tm, :] = proj[:, :C]
    conv = conv_ref[...]
    n_qk = 2 * GDN_HEADS
    for cb in range(pbuf_ref.shape[1] // V7X_LANES):
        c0 = cb * V7X_LANES
        xs = pbuf_ref[:, c0:c0 + V7X_LANES]
        acc = conv[CONV_K - 1:CONV_K, c0:c0 + V7X_LANES] * xs[HALO:]
        for d in range(1, CONV_K):
            shifted = pltpu.roll(xs, d, axis=0)[HALO:]
            acc = acc + conv[CONV_K - 1 - d:CONV_K - d, c0:c0 + V7X_LANES] * shifted
        y = _silu(acc)
        if cb < n_qk:
            y = y * lax.rsqrt(jnp.sum(y * y, axis=-1, keepdims=True) + EPS)
        qkv_ref[cb] = y.astype(BF16)
    pbuf_ref[0:HALO, :] = pbuf_ref[tm:tm + HALO, :]

    for hh in range(GDN_HEADS):
        zs_ref[hh] = proj[:, C + hh * GDN_DV:C + (hh + 1) * GDN_DV].astype(BF16)
    gates_ref[...] = proj[:, C + nz:C + nz + ng].astype(BF16)
    lat_ref[...] = proj[:, C + nz + ng:]
    ba_ref[...] = _dot_nt(wba_ref[...], h)


def _mixin(x, mod, gain, wall, wba, conv, l):
    B, S, D = x.shape
    C = conv.shape[-1]
    nz = GDN_HEADS * GDN_DV
    ng = 2 * D
    nl = wall.shape[-1] - C - nz - ng
    nb = wba.shape[1]
    tm = min(S, MIX_TOKENS)
    ncb = C // V7X_LANES
    vmem = ((C + nz + ng + nl) * D * 2 + 2 * tm * D * 4 + 2 * tm * (C + nz + ng) * 2 + 2 * tm * nl * 4
            + (tm + HALO) * C * 4 + tm * C * 4 + tm * (nz + ng) * 4 + 4 * 1024 * 1024)
    return pl.pallas_call(
        functools.partial(_mixin_kernel, tm=tm),
        grid=(B, S // tm),
        in_specs=[
            pl.BlockSpec((None, tm, D), lambda b, s: (b, s, 0)),
            pl.BlockSpec((None, None, N_MOD, D), lambda b, s: (l, b, 0, 0)),
            pl.BlockSpec((None, 1, D), lambda b, s: (l, 0, 0)),
            _resident((None, D, C + nz + ng + nl), lambda b, s: (l, 0, 0)),
            _resident((None, nb, D), lambda b, s: (l, 0, 0)),
            pl.BlockSpec((None, CONV_K, C), lambda b, s: (l, 0, 0)),
        ],
        out_specs=[
            pl.BlockSpec((None, ncb, tm, V7X_LANES), lambda b, s: (b, 0, s, 0)),
            pl.BlockSpec((None, GDN_HEADS, tm, GDN_DV), lambda b, s: (b, 0, s, 0)),
            pl.BlockSpec((None, tm, ng), lambda b, s: (b, s, 0)),
            pl.BlockSpec((None, tm, nl), lambda b, s: (b, s, 0)),
            pl.BlockSpec((None, nb, tm), lambda b, s: (b, 0, s)),
        ],
        out_shape=[
            jax.ShapeDtypeStruct((B, ncb, S, V7X_LANES), BF16),
            jax.ShapeDtypeStruct((B, GDN_HEADS, S, GDN_DV), BF16),
            jax.ShapeDtypeStruct((B, S, ng), BF16),
            jax.ShapeDtypeStruct((B, S, nl), F32),
            jax.ShapeDtypeStruct((B, nb, S), F32),
        ],
        scratch_shapes=[pltpu.VMEM((tm + HALO, C), F32)],
        compiler_params=_params(vmem, 2),
        name="mixin",
    )(x, mod, gain, wall, wba, conv)


def _gate_kernel(ba_ref, alog_ref, dtb_ref, grow_ref, gcol_ref, *, tm):
    H = GDN_HEADS
    ba = ba_ref[...]
    beta = _sigmoid(ba[0:H])
    xa = ba[H:2 * H] + dtb_ref[...]
    softplus = jnp.maximum(xa, 0.0) + jnp.log(1.0 + jnp.exp(-jnp.abs(xa)))
    g = -jnp.exp(alog_ref[...]) * softplus

    ii = lax.broadcasted_iota(jnp.int32, (tm, tm), 0)
    jj = lax.broadcasted_iota(jnp.int32, (tm, tm), 1)
    same = (ii >> LOG2_CHUNK) == (jj >> LOG2_CHUNK)
    upper =jnp.where(same & (ii <= jj), 1.0, 0.0).astype(BF16)
    block = jnp.where(same, 1.0, 0.0).astype(BF16)
    eye = jnp.where(ii == jj, 1.0, 0.0).astype(BF16)
    gam = None
    glast = None
    for part in _split3(g):
        t1 = _dot(part, upper)
        t2 = _dot(part, block)
        gam = t1 if gam is None else gam + t1
        glast = t2 if glast is None else glast + t2
    egam = jnp.exp(gam)
    ekd = jnp.exp(glast - gam)
    egl = jnp.exp(glast)
    zero = jnp.zeros_like(gam[0:1])
    rows = []
    for hh in range(H):
        r = jnp.concatenate([beta[hh:hh + 1], gam[hh:hh + 1], egam[hh:hh + 1], ekd[hh:hh + 1],
                             egl[hh:hh + 1], zero, zero, zero], axis=0)
        grow_ref[hh] = r
        rows.append(r)
    allrows = jnp.concatenate(rows, axis=0)
    cols = None
    for part in _split3(allrows):
        t = _dot_nt(eye, part)
        cols = t if cols is None else cols + t
    for hh in range(H):
        gcol_ref[hh] = cols[:, 8 * hh:8 * hh + 8]


def _gdn_gates(ba, a_log, dt_bias, l):
    B, nb, S = ba.shape
    H = GDN_HEADS
    tm = min(S, GATE_TOKENS)
    vmem = 8 * tm * tm * 4 + 4 * H * tm * V7X_LANES * 4
    return pl.pallas_call(
        functools.partial(_gate_kernel, tm=tm),
        grid=(B, S // tm),
        in_specs=[
            pl.BlockSpec((None, nb, tm), lambda b, s: (b, 0, s)),
            pl.BlockSpec((None, H, 1), lambda b, s: (l, 0, 0)),
            pl.BlockSpec((None, H, 1), lambda b, s: (l, 0, 0)),
        ],
        out_specs=[
            pl.BlockSpec((None, H, 8, tm), lambda b, s: (b, 0, 0, s)),
            pl.BlockSpec((None, H, tm, 8), lambda b, s: (b, 0, s, 0)),
        ],
        out_shape=[
            jax.ShapeDtypeStruct((B, H, 8, S), F32),
            jax.ShapeDtypeStruct((B, H, S, 8), F32),
        ],
        compiler_params=_params(vmem, 2),
        name="gdn_gates",
    )(ba, a_log, dt_bias)


def _gdn_kernel(q_ref, k_ref, v_ref, zs_ref, grow_ref, gcol_ref, og_ref, o_ref, state_ref, *, tc, G):
    s = pl.program_id(2)

    @pl.when(s == 0)
    def _():
        state_ref[...] = jnp.zeros(state_ref.shape, F32)

    heads = range(G)
    n_chunks = tc // CHUNK
    scale = GDN_DK ** -0.5
    ii = lax.broadcasted_iota(jnp.int32, (tc, tc), 0)
    jj = lax.broadcasted_iota(jnp.int32, (tc, tc), 1)
    same = (ii >> LOG2_CHUNK) == (jj >> LOG2_CHUNK)
    causal = same & (ii >= jj)
    strict = same & (ii > jj)
    lane = lax.broadcasted_iota(jnp.int32, (GDN_DK, tc), 1) >> LOG2_CHUNK

    gc = [gcol_ref[g] for g in heads]
    gr = [grow_ref[g] for g in heads]
    k = [k_ref[g] for g in heads]
    kf = [k[g].astype(F32) for g in heads]
    kb = [kf[g] * gc[g][:, 0:1] for g in heads]
    decay = [jnp.where(causal, jnp.exp(jnp.minimum(gc[g][:, 1:2] - gr[g][1:2, :], 0.0)), 0.0) for g in heads]

    akq = [_dot_nt(jnp.concatenate([kb[g].astype(BF16), q_ref[g]], axis=0), k[g]) for g in heads]
    m = [jnp.where(strict, akq[g][:tc] * decay[g], 0.0) for g in heads]
    qk = [(akq[g][tc:] * (decay[g] * scale)).astype(BF16) for g in heads]

    rlane = lax.broadcasted_iota(jnp.int32, (CHUNK, tc), 1)
    rrow = lax.broadcasted_iota(jnp.int32, (CHUNK, tc), 0)
    eye_row = jnp.where((rlane & (CHUNK - 1)) == rrow, 1.0, 0.0)
    rblk = rlane >> LOG2_CHUNK

    def blockdiag(x_row):
        return jnp.concatenate([jnp.where(rblk == c, x_row, 0.0) for c in range(n_chunks)], axis=0).astype(BF16)

    def side_by_side(x):
        acc = x[0:CHUNK]
        for c in range(1, n_chunks):
            acc = acc + x[c * CHUNK:(c + 1) * CHUNK]
        return acc

    m_row = [side_by_side(m[g]) for g in heads]
    p = [_dot(m_row[g].astype(BF16), blockdiag(m_row[g])) for g in heads]
    t_row = [eye_row - m_row[g] for g in heads]
    for _ in range(LOG2_CHUNK - 2):
        r = [_dot(jnp.concatenate([p[g], t_row[g]], axis=0).astype(BF16), blockdiag(p[g])) for g in heads]
        p = [r[g][:CHUNK] for g in heads]
        t_row = [t_row[g] + r[g][CHUNK:] for g in heads]
    t_row = [t_row[g] + _dot(t_row[g].astype(BF16), blockdiag(p[g])) for g in heads]

    rhs = [jnp.concatenate([v_ref[g].astype(F32) * gc[g][:, 0:1], kb[g] * gc[g][:, 2:3]], axis=1).astype(BF16)
           for g in heads]
    uw = [_dot(blockdiag(t_row[g]), rhs[g]).astype(BF16) for g in heads]

    kdt = [(kf[g] * gc[g][:, 3:4]).T for g in heads]
    pre_lhs = [jnp.concatenate([jnp.where(lane == c, kdt[g], 0.0).astype(BF16) for c in range(n_chunks)]
                               + [qk[g]], axis=0) for g in heads]
    pre = [_dot(pre_lhs[g], uw[g]) for g in heads]
    qd = [q_ref[g].astype(F32) * (gc[g][:, 2:3] * scale) for g in heads]
    base = n_chunks * GDN_DK
    qmod = [(qd[g] - pre[g][base:, GDN_DV:]).astype(BF16) for g in heads]

    states = [state_ref[g] for g in heads]
    outs = [[] for _ in heads]
    for c in range(n_chunks):
        r0 = c * CHUNK
        for g in heads:
            kblk = pre[g][c * GDN_DK:(c + 1) * GDN_DK]
            lhs = jnp.concatenate([kblk[:, GDN_DV:].astype(BF16), qmod[g][r0:r0 + CHUNK]], axis=0)
            r = _dot(lhs, states[g].astype(BF16))
            outs[g].append(r[GDN_DK:] + pre[g][base + r0:base + r0 + CHUNK, :GDN_DV])
            states[g] = states[g] * gr[g][4:5, r0:r0 + 1] - r[:GDN_DK] + kblk[:, :GDN_DV]
    for g in heads:
        state_ref[g] = states[g]
        o = jnp.concatenate(outs[g], axis=0)
        o = o * lax.rsqrt(jnp.mean(o * o, axis=-1, keepdims=True) + EPS) * og_ref[...]
        o_ref[:, g * GDN_DV:(g + 1) * GDN_DV] = (o * _silu(zs_ref[g].astype(F32))).astype(BF16)


def _gdn(qkv, zs, grow, gcol, out_gain, l):
    B, _, S, _ = qkv.shape
    H = GDN_HEADS
    G = GDN_GROUP
    tc = min(S, GDN_TOKENS)
    vmem = G * (24 * tc * tc * 4 + 24 * tc * V7X_LANES * 4)
    return pl.pallas_call(
        functools.partial(_gdn_kernel, tc=tc, G=G),
        grid=(B, H // G, S // tc),
        in_specs=[
            pl.BlockSpec((None, G, tc, GDN_DK), lambda b, h, s: (b, h, s, 0)),
            pl.BlockSpec((None, G, tc, GDN_DK), lambda b, h, s: (b, H // G + h, s, 0)),
            pl.BlockSpec((None, G, tc, GDN_DV), lambda b, h, s: (b, 2 * H // G + h, s, 0)),
            pl.BlockSpec((None, G, tc, GDN_DV), lambda b, h, s: (b, h, s, 0)),
            pl.BlockSpec((None, G, 8, tc), lambda b, h, s: (b, h, 0, s)),
            pl.BlockSpec((None, G, tc, 8), lambda b, h, s: (b, h, s, 0)),
            pl.BlockSpec((None, 1, GDN_DV), lambda b, h, s: (l, 0, 0)),
        ],
        out_specs=pl.BlockSpec((None, tc, G * GDN_DV), lambda b, h, s: (b, s, h)),
        out_shape=jax.ShapeDtypeStruct((B, S, H * GDN_DV), BF16),
        scratch_shapes=[pltpu.VMEM((G, GDN_DK, GDN_DV), F32)],
        compiler_params=_params(vmem, 3),
        name="gdn",
    )(qkv, qkv, qkv, zs, grow, gcol, out_gain)


def _swap_halves(y):
    half = y.shape[-1] // 2
    return jnp.concatenate([y[:, half:], y[:, :half]], axis=1)


def _mla_prep_kernel(lat_ref, cos_ref, sin_ref, cost_ref, sint_ref, qg_ref, kvg_ref, wqt_ref, wk_ref, wvt_ref,
                     qnc_ref, kn_ref, qt_ref, k_ref, vt_ref):
    H = MLA_HEADS
    half = QK_ROPE // 2
    lat = lat_ref[...]
    ql = lat[:, :Q_LORA]
    kvl = lat[:, Q_LORA:Q_LORA + KV_LORA]
    kpe = lat[:, Q_LORA + KV_LORA:Q_LORA + KV_LORA + QK_ROPE]
    qln = (ql * lax.rsqrt(jnp.mean(ql * ql, axis=-1, keepdims=True) + EPS) * qg_ref[...]).astype(BF16)
    kvn = (kvl * lax.rsqrt(jnp.mean(kvl * kvl, axis=-1, keepdims=True) + EPS) * kvg_ref[...]).astype(BF16)
    qt = _dot_nt(wqt_ref[...], qln)
    knope = _dot(kvn, wk_ref[...])
    vt = _dot_nt(wvt_ref[...], kvn)

    cost = cost_ref[...]
    sint = sint_ref[...]
    qnc = qnc_ref[...]
    scale = MLA_QK ** -0.5 * math.log2(math.e)
    for hh in range(H):
        blk = qt[hh * MLA_QK:(hh + 1) * MLA_QK]
        r = lax.rsqrt(jnp.sum(blk * blk, axis=0, keepdims=True) * (1.0 / MLA_QK) + EPS) * scale
        y = blk * qnc * r
        y1 = y[QK_NOPE:QK_NOPE + half]
        y2 = y[QK_NOPE + half:]
        qt_ref[hh, 0:QK_NOPE, :] = y[:QK_NOPE].astype(BF16)
        qt_ref[hh, QK_NOPE:QK_NOPE + half, :] = (y1 * cost - y2 * sint).astype(BF16)
        qt_ref[hh, QK_NOPE + half:MLA_QK, :] = (y2 * cost + y1 * sint).astype(BF16)
        vt_ref[hh] = vt[hh * V_HEAD:(hh + 1) * V_HEAD].astype(BF16)

    kn = kn_ref[...]
    yk = kpe * kn[:, QK_NOPE:]
    kr = yk * cos_ref[...] + _swap_halves(yk) * sin_ref[...]
    kpe_ss = jnp.sum(kpe * kpe, axis=-1, keepdims=True)
    for hh in range(H):
        k_nope = knope[:, hh * QK_NOPE:(hh + 1) * QK_NOPE]
        ss = jnp.sum(k_nope * k_nope, axis=-1, keepdims=True) + kpe_ss
        r = lax.rsqrt(ss * (1.0 / MLA_QK) + EPS)
        k_ref[hh, :, 0:QK_NOPE] = (k_nope * r * kn[:, :QK_NOPE]).astype(BF16)
        k_ref[hh, :, QK_NOPE:MLA_QK] = (kr * r).astype(BF16)


def _mla_prep(lat, cos2, sin2, cost, sint, q_lat_gain, kv_lat_gain, wqt, wk, wvt, q_norm_col, k_norm, l):
    B, S, nl = lat.shape
    H = MLA_HEADS
    half = QK_ROPE // 2
    tm = min(S, ATTN_Q)
    nq = wqt.shape[1]
    nk = wk.shape[-1]
    nv = wvt.shape[1]
    vmem = ((nq * Q_LORA + (nk + nv) * KV_LORA) * 2 + 2 * tm * nl * 4 + 3 * tm * (nq + nk + nv) * 4
            + 4 * H * tm * (256 + MLA_QK + V_HEAD) * 2)
    return pl.pallas_call(
        _mla_prep_kernel,
        grid=(B, S // tm),
        in_specs=[
            pl.BlockSpec((None, tm, nl), lambda b, s: (b, s, 0)),
            pl.BlockSpec((None, tm, QK_ROPE), lambda b, s: (b, s, 0)),
            pl.BlockSpec((None, tm, QK_ROPE), lambda b, s: (b, s, 0)),
            pl.BlockSpec((None, half, tm), lambda b, s: (b, 0, s)),
            pl.BlockSpec((None, half, tm), lambda b, s: (b, 0, s)),
            pl.BlockSpec((None, 1, Q_LORA), lambda b, s: (l, 0, 0)),
            pl.BlockSpec((None, 1, KV_LORA), lambda b, s: (l, 0, 0)),
            _resident((None, nq, Q_LORA), lambda b, s: (l, 0, 0)),
            _resident((None, KV_LORA, nk), lambda b, s: (l, 0, 0)),
            _resident((None, nv, KV_LORA), lambda b, s: (l, 0, 0)),
            pl.BlockSpec((None, MLA_QK, 1), lambda b, s: (l, 0, 0)),
            pl.BlockSpec((None, 1, MLA_QK), lambda b, s: (l, 0, 0)),
        ],
        out_specs=[
            pl.BlockSpec((None, H, MLA_QK, tm), lambda b, s: (b, 0, 0, s)),
            pl.BlockSpec((None, H, tm, MLA_QK), lambda b, s: (b, 0, s, 0)),
            pl.BlockSpec((None, H, None, V_HEAD, tm), lambda b, s: (b, 0, s, 0, 0)),
        ],
        out_shape=[
            jax.ShapeDtypeStruct((B, H, MLA_QK, S), BF16),
            jax.ShapeDtypeStruct((B, H, S, MLA_QK), BF16),
            jax.ShapeDtypeStruct((B, H, S // tm, V_HEAD, tm), BF16),
        ],
        compiler_params=_params(vmem, 2),
        name="mla_prep",
    )(lat, cos2, sin2, cost, sint, q_lat_gain, kv_lat_gain, wqt, wk, wvt, q_norm_col, k_norm)


def _attn_kernel(qt_ref, k_ref, vt_ref, o_ref, sa_ref, sb_ref, mxa_ref, mxb_ref, m_ref, l_ref, acc_ref, *,
                 tq, chain):
    c = pl.program_id(2)
    bufs = ((sa_ref, mxa_ref), (sb_ref, mxb_ref))

    def scores(qt, j, buf):
        st = _dot(k_ref[pl.ds(pl.multiple_of(j * tq, tq), tq), :], qt)
        buf[0][...] = st
        buf[1][...] = jnp.max(st, axis=0, keepdims=True)

    def reset(w):
        m_ref[w] = jnp.full(m_ref.shape[1:], -jnp.inf, F32)
        l_ref[w] = jnp.zeros(l_ref.shape[1:], F32)
        acc_ref[w] = jnp.zeros(acc_ref.shape[1:], F32)

    def update(w, j, buf, masked=False):
        st = buf[0][...]
        mx = buf[1][...]
        if masked:
            kpos = lax.broadcasted_iota(jnp.int32, (tq, tq), 0)
            qpos = lax.broadcasted_iota(jnp.int32, (tq, tq), 1)
            st = jnp.where(kpos <= qpos, st, -jnp.inf)
            mx = jnp.max(st, axis=0, keepdims=True)
        m = m_ref[w]
        m_new = jnp.maximum(m, mx)
        alpha = jnp.exp2(m - m_new)
        pr = jnp.exp2(st - m_new)
        m_ref[w] = m_new
        l_ref[w] = alpha * l_ref[w] + jnp.sum(pr, axis=0, keepdims=True)
        acc_ref[w] = alpha * acc_ref[w] + _dot(vt_ref[j], pr.astype(BF16))

    def full_blocks(w, qt, cur, nxt, trips):
        def body(t, _):
            scores(qt, 2 * t + 1, nxt)
            update(w, 2 * t, cur)
            scores(qt, 2 * t + 2, cur)
            update(w, 2 * t + 1, nxt)
            return 0
        lax.fori_loop(0, trips, body, 0)

    def q_block(u):
        return qt_ref[:, u * tq:(u + 1) * tq]

    cur, nxt = bufs
    reset(0)
    scores(q_block(0), 0, cur)
    for u in range(chain):
        w = u % 2
        i = c * chain + u
        full_blocks(w, q_block(u), cur, nxt, c * (chain // 2) + u // 2)
        if u % 2 == 1:
            scores(q_block(u), i, nxt)
            update(w, i - 1, cur)
            cur, nxt = nxt, cur
        if u + 1 < chain:
            reset(1 - w)
            scores(q_block(u + 1), 0, nxt)
        update(w, i, cur, masked=True)
        o_ref[u * tq:(u + 1) * tq, :] = (acc_ref[w] / l_ref[w]).T.astype(BF16)
        cur, nxt = nxt, cur


def _attention(qt, k, vt):
    B, H, S, _ = k.shape
    tq = vt.shape[-1]
    nq = S // tq
    chain = min(nq, ATTN_CHAIN)
    assert chain % 2 == 0 and nq % chain == 0, "q blocks are taken in even-length runs"
    vmem = (8 * tq * tq * 4 + 4 * S * (256 + V_HEAD) * 2 + 4 * chain * tq * (256 + V_HEAD) * 2
            + 16 * tq * V7X_LANES * 4)
    return pl.pallas_call(
        functools.partial(_attn_kernel, tq=tq, chain=chain),
        grid=(B, H, nq // chain),
        in_specs=[
            pl.BlockSpec((None, None, MLA_QK, chain * tq), lambda b, h, i: (b, h, 0, i)),
            pl.BlockSpec((None, None, S, MLA_QK), lambda b, h, i: (b, h, 0, 0)),
            pl.BlockSpec((None, None, S // tq, V_HEAD, tq), lambda b, h, i: (b, h, 0, 0, 0)),
        ],
        out_specs=pl.BlockSpec((None, chain * tq, V_HEAD), lambda b, h, i: (b, i, h)),
        out_shape=jax.ShapeDtypeStruct((B, S, H * V_HEAD), BF16),
        scratch_shapes=[pltpu.VMEM((tq, tq), F32), pltpu.VMEM((tq, tq), F32),
                        pltpu.VMEM((1, tq), F32), pltpu.VMEM((1, tq), F32),
                        pltpu.VMEM((2, 1, tq), F32), pltpu.VMEM((2, 1, tq), F32),
                        pltpu.VMEM((2, V_HEAD, tq), F32)],
        compiler_params=_params(vmem, 3),
        name="mla_attention",
    )(qt, k, vt)


def _merge_kernel(x_ref, mod_ref, oa_ref, ob_ref, gates_ref, wa_ref, wb_ref, wo_ref, o_ref):
    D = x_ref.shape[-1]
    mod = mod_ref[...]
    gates = _sigmoid(gates_ref[...].astype(F32))
    ya =_dot(oa_ref[...], wa_ref[...])
    yb = _dot(ob_ref[...], wb_ref[...])
    y = (gates[:, :D] * ya + gates[:, D:] * yb).astype(BF16)
    o_ref[...] = x_ref[...] + (1.0 + mod[5:6]) * _dot(y, wo_ref[...])


def _merge(x, mod, oa, ob, gates, wa, wb, wo, l):
    B, S, D = x.shape
    tm = min(S, MERGE_TOKENS)
    na = oa.shape[-1]
    nbv = ob.shape[-1]
    vmem = (na + nbv + D) * D * 2 + 4 * tm * D * 4 + 2 * tm * (na + nbv + 2 * D) * 2 + 4 * tm * D * 4
    return pl.pallas_call(
        _merge_kernel,
        grid=(B, S // tm),
        in_specs=[
            pl.BlockSpec((None, tm, D), lambda b, s: (b, s, 0)),
            pl.BlockSpec((None, None, N_MOD, D), lambda b, s: (l, b, 0, 0)),
            pl.BlockSpec((None, tm, na), lambda b, s: (b, s, 0)),
            pl.BlockSpec((None, tm, nbv), lambda b, s: (b, s, 0)),
            pl.BlockSpec((None, tm, 2 * D), lambda b, s: (b, s, 0)),
            _resident((None, na, D), lambda b, s: (l, 0, 0)),
            _resident((None, nbv, D), lambda b, s: (l, 0, 0)),
            _resident((None, D, D), lambda b, s: (l, 0, 0)),
        ],
        out_specs=pl.BlockSpec((None, tm, D), lambda b, s: (b, s, 0)),
        out_shape=jax.ShapeDtypeStruct((B, S, D), F32),
        compiler_params=_params(vmem, 2),
        name="merge",
    )(x, mod, oa, ob, gates, wa, wb, wo)


def _pack_w_in(w_in):
    conv_c = 2 * GDN_HEADS * GDN_DK + GDN_HEADS * GDN_DV
    gdn_v = GDN_HEADS * GDN_DV
    D = w_in.shape[1]
    sizes = (conv_c, gdn_v, GDN_HEADS, GDN_HEADS, Q_LORA, KV_LORA, QK_ROPE, D, D)
    offs = np.cumsum((0,) + sizes)
    qkv, z, b_raw, a_raw, q_lat, kv_lat, k_pe, gate_a, gate_b = (
        w_in[:, :, int(offs[i]):int(offs[i + 1])] for i in range(len(sizes)))
    n_lat = Q_LORA + KV_LORA + QK_ROPE
    pad = (-n_lat) % V7X_LANES
    lat = jnp.concatenate([q_lat, kv_lat, k_pe, jnp.zeros(w_in.shape[:2] + (pad,), w_in.dtype)], axis=-1)
    ba = jnp.swapaxes(jnp.concatenate([b_raw, a_raw], axis=-1), 1, 2)
    wall = jnp.concatenate([qkv, z, gate_a, gate_b, lat], axis=-1)
    return wall.astype(BF16), ba.astype(BF16)


def _pack_kv_up(w_kv_up):
    L = w_kv_up.shape[0]
    w = w_kv_up.reshape(L, KV_LORA, MLA_HEADS, QK_NOPE + V_HEAD)
    wk = w[..., :QK_NOPE].reshape(L, KV_LORA, MLA_HEADS * QK_NOPE)
    wv = w[..., QK_NOPE:].reshape(L, KV_LORA, MLA_HEADS * V_HEAD)
    return wk.astype(BF16), jnp.swapaxes(wv, 1, 2).astype(BF16)


def kernel(x, c, positions, ada_w, ada_b, norm_ffn1, ffn1_w1, ffn1_w3, ffn1_w2, norm_mix, w_in, gdn_conv, gdn_a_log, gdn_dt_bias, gdn_out_gain, mla_q_lat_gain, mla_kv_lat_gain, mla_w_q_up, mla_w_kv_up, mla_q_norm, mla_k_norm, w_branch_a, w_branch_b, w_out, norm_ffn2, ffn2_w1, ffn2_w3, ffn2_w2):
    L = ada_w.shape[0]
    S = x.shape[1]
    assert S % CHUNK == 0 and S % min(S, GDN_TOKENS) == 0 and S % min(S, FFN_TOKENS) == 0
    assert S % min(S, MERGE_TOKENS) == 0 and S % min(S, ATTN_Q) == 0

    mod = _ada_mod(c, ada_w, ada_b)
    cos2, sin2, cost, sint = _rope_tables(positions)

    row = lambda a: a.reshape(L, 1, a.shape[-1])
    col = lambda a: a.reshape(L, a.shape[-1], 1)
    bf = lambda a: a.astype(BF16)
    wall, wba = _pack_w_in(w_in)
    wqt = bf(jnp.swapaxes(mla_w_q_up, 1, 2))
    wk, wvt = _pack_kv_up(mla_w_kv_up)
    f1 = (bf(ffn1_w1), bf(ffn1_w3), bf(ffn1_w2))
    f2 = (bf(ffn2_w1), bf(ffn2_w3), bf(ffn2_w2))
    wa, wb, wo = bf(w_branch_a), bf(w_branch_b), bf(w_out)
    g_ffn1, g_mix, g_ffn2 = row(norm_ffn1), row(norm_mix), row(norm_ffn2)
    og, qlg, kvlg = row(gdn_out_gain), row(mla_q_lat_gain), row(mla_kv_lat_gain)
    qnc, kn = col(mla_q_norm), row(mla_k_norm)
    a_log, dt_bias = col(gdn_a_log), col(gdn_dt_bias)

    for l in range(L):
        x = _ffn(x, mod, g_ffn1, *f1, l, 0)
        qkv, zs, gates, lat, ba = _mixin(x, mod, g_mix, wall, wba, gdn_conv, l)
        grow, gcol = _gdn_gates(ba, a_log, dt_bias, l)
        o_gdn = _gdn(qkv, zs, grow, gcol, og, l)
        qt, k, vt = _mla_prep(lat, cos2, sin2, cost, sint, qlg, kvlg, wqt, wk, wvt, qnc, kn, l)
        o_mla = _attention(qt, k, vt)
        x = _merge(x, mod, o_gdn, o_mla, gates, wa, wb, wo, l)
        x = _ffn(x, mod, g_ffn2, *f2, l, 6)
    return x
```

```python
import functools
import math

import numpy as np
import jax
import jax.numpy as jnp
from jax import lax
from jax.experimental import pallas as pl
from jax.experimental.pallas import tpu as pltpu

F32 = jnp.float32
BF16 = jnp.bfloat16

EPS = 1e-6
N_MOD = 9
GDN_HEADS = 8
GDN_DK = 128
GDN_DV = 128
CONV_K = 4
CHUNK = 64
LOG2_CHUNK = 6
MLA_HEADS = 8
Q_LORA = 384
KV_LORA = 256
QK_NOPE = 128
QK_ROPE = 64
V_HEAD = 128
ROPE_THETA = 10000.0
MLA_QK = QK_NOPE + QK_ROPE

V7X_VMEM_BYTES = 64 * 1024 * 1024
V7X_LANES = 128
V7X_SUBLANES = 8
VMEM_CAP_BYTES = V7X_VMEM_BYTES - 6 * 1024 * 1024

FFN_TOKENS = 1024
MIX_TOKENS = 256
GATE_TOKENS = 512
GDN_TOKENS = 256
GDN_GROUP = 8
ATTN_Q = 512
ATTN_CHAIN = 8
MERGE_TOKENS = 1024
HALO = V7X_SUBLANES


def _params(vmem_bytes, n_axes):
    limit = int(min(VMEM_CAP_BYTES, max(vmem_bytes, 16 * 1024 * 1024)))
    return pltpu.CompilerParams(dimension_semantics=("arbitrary",) * n_axes, vmem_limit_bytes=limit)


def _resident(block_shape, index_map):
    return pl.BlockSpec(block_shape, index_map, pipeline_mode=pl.Buffered(1))


def _dot(a, b):
    return jnp.dot(a, b, preferred_element_type=F32)


def _dot_nt(a, b):
    return lax.dot_general(a, b, (((1,), (1,)), ((), ())), preferred_element_type=F32)


def _sigmoid(x):
    return 0.5 * jnp.tanh(0.5 * x) + 0.5


def _silu(x):
    hx = 0.5 * x
    return hx * jnp.tanh(hx) + hx


def _modulate(x, gain, shift, scale):
    y = x * lax.rsqrt(jnp.mean(x * x, axis=-1, keepdims=True) + EPS)
    return (y * gain) * (1.0 + scale) + shift


def _split3(x):
    x1 = x.astype(BF16)
    r1 = x - x1.astype(F32)
    x2 = r1.astype(BF16)
    x3 = (r1 - x2.astype(F32)).astype(BF16)
    return x1, x2, x3


def _mod_kernel(c_ref, w_ref, b_ref, o_ref):
    cond = _silu(c_ref[...]).astype(BF16)
    o_ref[...] = _dot(cond, w_ref[...].astype(BF16)) + b_ref[...]


def _ada_mod(c, ada_w, ada_b):
    L, D, _ = ada_w.shape
    B = c.shape[0]
    ada_b3 = ada_b.reshape(L, N_MOD, 1, D)
    out = pl.pallas_call(
        _mod_kernel,
        grid=(L, N_MOD),
        in_specs=[
            pl.BlockSpec((B, D), lambda l, j: (0, 0)),
            pl.BlockSpec((None, D, D), lambda l, j: (l, 0, j)),
            pl.BlockSpec((None, None, 1, D), lambda l, j: (l, j, 0, 0)),
        ],
        out_specs=pl.BlockSpec((None, None, B, D), lambda l, j: (l, j, 0, 0)),
        out_shape=jax.ShapeDtypeStruct((L, N_MOD, B, D), F32),
        compiler_params=_params(4 * D * D * 4, 2),
        name="ada_mod",
    )(c, ada_w, ada_b3)
    return jnp.transpose(out, (0, 2, 1, 3))


def _rope_kernel(pos_ref, post_ref, freq_ref, sign_ref, freqc_ref, cos_ref, sin_ref, cost_ref, sint_ref):
    ang = pos_ref[...].astype(F32) * freq_ref[...]
    cos_ref[...] = jnp.cos(ang)
    sin_ref[...] = jnp.sin(ang) * sign_ref[...]
    angt = freqc_ref[...] * post_ref[...].astype(F32)
    cost_ref[...] = jnp.cos(angt)
    sint_ref[...] = jnp.sin(angt)


def _rope_tables(positions):
    B, S = positions.shape
    half = QK_ROPE // 2
    inv_freq = ROPE_THETA ** (-jnp.arange(half, dtype=F32) / half)
    freq2 = jnp.concatenate([inv_freq, inv_freq]).reshape(1, QK_ROPE)
    sign2 = jnp.concatenate([-jnp.ones((half,), F32), jnp.ones((half,), F32)]).reshape(1, QK_ROPE)
    ts = min(S, 1024)
    return pl.pallas_call(
        _rope_kernel,
        grid=(B, S // ts),
        in_specs=[
            pl.BlockSpec((None, ts, 1), lambda b, s: (b, s, 0)),
            pl.BlockSpec((None, 1, ts), lambda b, s: (b, 0, s)),
            pl.BlockSpec((1, QK_ROPE), lambda b, s: (0, 0)),
            pl.BlockSpec((1, QK_ROPE), lambda b, s: (0, 0)),
            pl.BlockSpec((half, 1), lambda b, s: (0, 0)),
        ],
        out_specs=[pl.BlockSpec((None, ts, QK_ROPE), lambda b, s: (b, s, 0))] * 2
        + [pl.BlockSpec((None, half, ts), lambda b, s: (b, 0, s))] * 2,
        out_shape=[jax.ShapeDtypeStruct((B, S, QK_ROPE), F32)] * 2
        + [jax.ShapeDtypeStruct((B, half, S), F32)] * 2,
        compiler_params=_params(8 * ts * V7X_LANES * 4, 2),
        name="rope_tables",
    )(positions.reshape(B, S, 1), positions.reshape(B, 1, S), freq2, sign2, inv_freq.reshape(half, 1))


def _ffn_kernel(x_ref, mod_ref, gain_ref, w1_ref, w3_ref, w2_ref, o_ref, *, row):
    x = x_ref[...]
    mod = mod_ref[...]
    h = _modulate(x, gain_ref[...], mod[row:row + 1], mod[row + 1:row + 2]).astype(BF16)
    a = _dot(h, w1_ref[...])
    b = _dot(h, w3_ref[...])
    g = (_silu(a) * b).astype(BF16)
    y = _dot(g, w2_ref[...])
    o_ref[...] = x + (0.5 * (1.0 + mod[row + 2:row + 3])) * y


def _ffn(x, mod, gain, w1, w3, w2, l, row):
    B, S, D = x.shape
    F = w1.shape[-1]
    tm = min(S, FFN_TOKENS)
    vmem = 3 * D * F * 2 + 4 * tm * D * 4 + 3 * tm * F * 4 + 2 * tm * D * 4
    return pl.pallas_call(
        functools.partial(_ffn_kernel, row=row),
        grid=(B, S // tm),
        in_specs=[
            pl.BlockSpec((None, tm, D), lambda b, s: (b, s, 0)),
            pl.BlockSpec((None, None, N_MOD, D), lambda b, s: (l, b, 0, 0)),
            pl.BlockSpec((None, 1, D), lambda b, s: (l, 0, 0)),
            _resident((None, D, F), lambda b, s: (l, 0, 0)),
            _resident((None, D, F), lambda b, s: (l, 0, 0)),
            _resident((None, F, D), lambda b, s: (l, 0, 0)),
        ],
        out_specs=pl.BlockSpec((None, tm, D), lambda b, s: (b, s, 0)),
        out_shape=jax.ShapeDtypeStruct((B, S, D), F32),
        compiler_params=_params(vmem, 2),
        name="ffn",
    )(x, mod, gain, w1, w3, w2)


def _mixin_kernel(x_ref, mod_ref, gain_ref, wall_ref, wba_ref, conv_ref,
                  qkv_ref, zs_ref, gates_ref, lat_ref, ba_ref, pbuf_ref, *, tm):
    @pl.when(pl.program_id(1) == 0)
    def _():
        pbuf_ref[0:HALO, :] = jnp.zeros((HALO, pbuf_ref.shape[1]), F32)

    x = x_ref[...]
    mod = mod_ref[...]
    h = _modulate(x, gain_ref[...], mod[3:4], mod[4:5]).astype(BF16)

    C = pbuf_ref.shape[1]
    nz = zs_ref.shape[0] * zs_ref.shape[2]
    ng = gates_ref.shape[1]
    proj = _dot(h, wall_ref[...])
    pbuf_ref[HALO:HALO + tm, :] = proj[:, :C]
    conv = conv_ref[...]
    assert CONV_K == 4
    n_qk = 2 * GDN_HEADS
    for cb in range(pbuf_ref.shape[1] // V7X_LANES):
        c0 = cb * V7X_LANES
        xs = pbuf_ref[:, c0:c0 + V7X_LANES]
        w = [conv[j:j + 1, c0:c0 + V7X_LANES] for j in range(CONV_K)]
        x1 = pltpu.roll(xs, 1, axis=0)
        near = w[3] * xs + w[2] * x1
        far = w[1] * xs + w[0] * x1
        y = _silu(near[HALO:] + pltpu.roll(far, 2, axis=0)[HALO:])
        if cb < n_qk:
            y = y * lax.rsqrt(jnp.sum(y * y, axis=-1, keepdims=True) + EPS)
        qkv_ref[cb] = y.astype(BF16)
    pbuf_ref[0:HALO, :] = pbuf_ref[tm:tm + HALO, :]

    for hh in range(GDN_HEADS):
        zs_ref[hh] = proj[:, C + hh * GDN_DV:C + (hh + 1) * GDN_DV].astype(BF16)
    gates_ref[...] = proj[:, C + nz:C + nz + ng].astype(BF16)
    lat_ref[...] = proj[:, C + nz + ng:]
    ba_ref[...] = _dot_nt(wba_ref[...], h)


def _mixin(x, mod, gain, wall, wba, conv, l):
    B, S, D = x.shape
    C = conv.shape[-1]
    nz = GDN_HEADS * GDN_DV
    ng = 2 * D
    nl = wall.shape[-1] - C - nz - ng
    nb = wba.shape[1]
    tm = min(S, MIX_TOKENS)
    ncb = C // V7X_LANES
    vmem = ((C + nz + ng + nl) * D * 2 + 2 * tm * D * 4 + 2 * tm * (C + nz + ng) * 2 + 2 * tm * nl * 4
            + (tm + HALO) * C * 4 + tm * C * 4 + tm * (nz + ng) * 4 + 4 * 1024 * 1024)
    return pl.pallas_call(
        functools.partial(_mixin_kernel, tm=tm),
        grid=(B, S // tm),
        in_specs=[
            pl.BlockSpec((None, tm, D), lambda b, s: (b, s, 0)),
            pl.BlockSpec((None, None, N_MOD, D), lambda b, s: (l, b, 0, 0)),
            pl.BlockSpec((None, 1, D), lambda b, s: (l, 0, 0)),
            _resident((None, D, C + nz + ng + nl), lambda b, s: (l, 0, 0)),
            _resident((None, nb, D), lambda b, s: (l, 0, 0)),
            pl.BlockSpec((None, CONV_K, C), lambda b, s: (l, 0, 0)),
        ],
        out_specs=[
            pl.BlockSpec((None, ncb, tm, V7X_LANES), lambda b, s: (b, 0, s, 0)),
            pl.BlockSpec((None, GDN_HEADS, tm, GDN_DV), lambda b, s: (b, 0, s, 0)),
            pl.BlockSpec((None, tm, ng), lambda b, s: (b, s, 0)),
            pl.BlockSpec((None, tm, nl), lambda b, s: (b, s, 0)),
            pl.BlockSpec((None, nb, tm), lambda b, s: (b, 0, s)),
        ],
        out_shape=[
            jax.ShapeDtypeStruct((B, ncb, S, V7X_LANES), BF16),
            jax.ShapeDtypeStruct((B, GDN_HEADS, S, GDN_DV), BF16),
            jax.ShapeDtypeStruct((B, S, ng), BF16),
            jax.ShapeDtypeStruct((B, S, nl), F32),
            jax.ShapeDtypeStruct((B, nb, S), F32),
        ],
        scratch_shapes=[pltpu.VMEM((tm + HALO, C), F32)],
        compiler_params=_params(vmem, 2),
        name="mixin",
    )(x, mod, gain, wall, wba, conv)


def _gate_kernel(ba_ref, alog_ref, dtb_ref, grow_ref, gcol_ref, *, tm):
    H = GDN_HEADS
    ba = ba_ref[...]
    beta = _sigmoid(ba[0:H])
    xa = ba[H:2 * H] + dtb_ref[...]
    softplus = jnp.maximum(xa, 0.0) + jnp.log(1.0 + jnp.exp(-jnp.abs(xa)))
    g = -jnp.exp(alog_ref[...]) * softplus

    ii = lax.broadcasted_iota(jnp.int32, (tm, tm), 0)
    jj = lax.broadcasted_iota(jnp.int32, (tm, tm), 1)
    same = (ii >> LOG2_CHUNK) == (jj >> LOG2_CHUNK)
    upper =jnp.where(same & (ii <= jj), 1.0, 0.0).astype(BF16)
    block = jnp.where(same, 1.0, 0.0).astype(BF16)
    eye = jnp.where(ii == jj, 1.0, 0.0).astype(BF16)
    gam = None
    glast = None
    for part in _split3(g):
        t1 = _dot(part, upper)
        t2 = _dot(part, block)
        gam = t1 if gam is None else gam + t1
        glast = t2 if glast is None else glast + t2
    egam = jnp.exp(gam)
    ekd = jnp.exp(glast - gam)
    egl = jnp.exp(glast)
    zero = jnp.zeros_like(gam[0:1])
    rows = []
    for hh in range(H):
        r = jnp.concatenate([beta[hh:hh + 1], gam[hh:hh + 1], egam[hh:hh + 1], ekd[hh:hh + 1],
                             egl[hh:hh + 1], zero, zero, zero], axis=0)
        grow_ref[hh] = r
        rows.append(r)
    allrows = jnp.concatenate(rows, axis=0)
    cols = None
    for part in _split3(allrows):
        t = _dot_nt(eye, part)
        cols = t if cols is None else cols + t
    for hh in range(H):
        gcol_ref[hh] = cols[:, 8 * hh:8 * hh + 8]


def _gdn_gates(ba, a_log, dt_bias, l):
    B, nb, S = ba.shape
    H = GDN_HEADS
    tm = min(S, GATE_TOKENS)
    vmem = 8 * tm * tm * 4 + 4 * H * tm * V7X_LANES * 4
    return pl.pallas_call(
        functools.partial(_gate_kernel, tm=tm),
        grid=(B, S // tm),
        in_specs=[
            pl.BlockSpec((None, nb, tm), lambda b, s: (b, 0, s)),
            pl.BlockSpec((None, H, 1), lambda b, s: (l, 0, 0)),
            pl.BlockSpec((None, H, 1), lambda b, s: (l, 0, 0)),
        ],
        out_specs=[
            pl.BlockSpec((None, H, 8, tm), lambda b, s: (b, 0, 0, s)),
            pl.BlockSpec((None, H, tm, 8), lambda b, s: (b, 0, s, 0)),
        ],
        out_shape=[
            jax.ShapeDtypeStruct((B, H, 8, S), F32),
            jax.ShapeDtypeStruct((B, H, S, 8), F32),
        ],
        compiler_params=_params(vmem, 2),
        name="gdn_gates",
    )(ba, a_log, dt_bias)


def _gdn_kernel(q_ref, k_ref, v_ref, zs_ref, grow_ref, gcol_ref, og_ref, o_ref, state_ref, *, tc, G):
    s = pl.program_id(2)

    @pl.when(s == 0)
    def _():
        state_ref[...] = jnp.zeros(state_ref.shape, F32)

    heads = range(G)
    n_chunks = tc // CHUNK
    scale = GDN_DK ** -0.5
    ii = lax.broadcasted_iota(jnp.int32, (tc, tc), 0)
    jj = lax.broadcasted_iota(jnp.int32, (tc, tc), 1)
    same = (ii >> LOG2_CHUNK) == (jj >> LOG2_CHUNK)
    causal = same & (ii >= jj)
    strict = same & (ii > jj)
    lane = lax.broadcasted_iota(jnp.int32, (GDN_DK, tc), 1) >> LOG2_CHUNK

    gc = [gcol_ref[g] for g in heads]
    gr = [grow_ref[g] for g in heads]
    k = [k_ref[g] for g in heads]
    kf = [k[g].astype(F32) for g in heads]
    kb = [kf[g] * gc[g][:, 0:1] for g in heads]
    decay = [jnp.where(causal, jnp.exp(jnp.minimum(gc[g][:, 1:2] - gr[g][1:2, :], 0.0)), 0.0) for g in heads]

    akq = [_dot_nt(jnp.concatenate([kb[g].astype(BF16), q_ref[g]], axis=0), k[g]) for g in heads]
    m = [jnp.where(strict, akq[g][:tc] * decay[g], 0.0) for g in heads]
    qk = [(akq[g][tc:] * (decay[g] * scale)).astype(BF16) for g in heads]

    rlane = lax.broadcasted_iota(jnp.int32, (CHUNK, tc), 1)
    rrow = lax.broadcasted_iota(jnp.int32, (CHUNK, tc), 0)
    eye_row = jnp.where((rlane & (CHUNK - 1)) == rrow, 1.0, 0.0)
    rblk = rlane >> LOG2_CHUNK

    def blockdiag(x_row):
        return jnp.concatenate([jnp.where(rblk == c, x_row, 0.0) for c in range(n_chunks)], axis=0).astype(BF16)

    def side_by_side(x):
        acc = x[0:CHUNK]
        for c in range(1, n_chunks):
            acc = acc + x[c * CHUNK:(c + 1) * CHUNK]
        return acc

    m_row = [side_by_side(m[g]) for g in heads]
    p = [_dot(m_row[g].astype(BF16), blockdiag(m_row[g])) for g in heads]
    t_row = [eye_row - m_row[g] for g in heads]
    for _ in range(LOG2_CHUNK - 2):
        r = [_dot(jnp.concatenate([p[g], t_row[g]], axis=0).astype(BF16), blockdiag(p[g])) for g in heads]
        p = [r[g][:CHUNK] for g in heads]
        t_row = [t_row[g] + r[g][CHUNK:] for g in heads]
    t_row = [t_row[g] + _dot(t_row[g].astype(BF16), blockdiag(p[g])) for g in heads]

    rhs = [jnp.concatenate([v_ref[g].astype(F32) * gc[g][:, 0:1], kb[g] * gc[g][:, 2:3]], axis=1).astype(BF16)
           for g in heads]
    uw = [_dot(blockdiag(t_row[g]), rhs[g]).astype(BF16) for g in heads]

    kdt = [(kf[g] * gc[g][:, 3:4]).T for g in heads]
    pre_lhs = [jnp.concatenate([jnp.where(lane == c, kdt[g], 0.0).astype(BF16) for c in range(n_chunks)]
                               + [qk[g]], axis=0) for g in heads]
    pre = [_dot(pre_lhs[g], uw[g]) for g in heads]
    qd = [q_ref[g].astype(F32) * (gc[g][:, 2:3] * scale) for g in heads]
    base = n_chunks * GDN_DK
    qmod = [(qd[g] - pre[g][base:, GDN_DV:]).astype(BF16) for g in heads]

    states = [state_ref[g] for g in heads]
    outs = [[] for _ in heads]
    for c in range(n_chunks):
        r0 = c * CHUNK
        for g in heads:
            kblk = pre[g][c * GDN_DK:(c + 1) * GDN_DK]
            lhs = jnp.concatenate([kblk[:, GDN_DV:].astype(BF16), qmod[g][r0:r0 + CHUNK]], axis=0)
            r = _dot(lhs, states[g].astype(BF16))
            outs[g].append(r[GDN_DK:] + pre[g][base + r0:base + r0 + CHUNK, :GDN_DV])
            states[g] = states[g] * gr[g][4:5, r0:r0 + 1] - r[:GDN_DK] + kblk[:, :GDN_DV]
    for g in heads:
        state_ref[g] = states[g]
        o = jnp.concatenate(outs[g], axis=0)
        o = o * lax.rsqrt(jnp.mean(o * o, axis=-1, keepdims=True) + EPS) * og_ref[...]
        o_ref[:, g * GDN_DV:(g + 1) * GDN_DV] = (o * _silu(zs_ref[g].astype(F32))).astype(BF16)


def _gdn(qkv, zs, grow, gcol, out_gain, l):
    B, _, S, _ = qkv.shape
    H = GDN_HEADS
    G = GDN_GROUP
    tc = min(S, GDN_TOKENS)
    vmem = G * (24 * tc * tc * 4 + 24 * tc * V7X_LANES * 4)
    return pl.pallas_call(
        functools.partial(_gdn_kernel, tc=tc, G=G),
        grid=(B, H // G, S // tc),
        in_specs=[
            pl.BlockSpec((None, G, tc, GDN_DK), lambda b, h, s: (b, h, s, 0)),
            pl.BlockSpec((None, G, tc, GDN_DK), lambda b, h, s: (b, H // G + h, s, 0)),
            pl.BlockSpec((None, G, tc, GDN_DV), lambda b, h, s: (b, 2 * H // G + h, s, 0)),
            pl.BlockSpec((None, G, tc, GDN_DV), lambda b, h, s: (b, h, s, 0)),
            pl.BlockSpec((None, G, 8, tc), lambda b, h, s: (b, h, 0, s)),
            pl.BlockSpec((None, G, tc, 8), lambda b, h, s: (b, h, s, 0)),
            pl.BlockSpec((None, 1, GDN_DV), lambda b, h, s: (l, 0, 0)),
        ],
        out_specs=pl.BlockSpec((None, tc, G * GDN_DV), lambda b, h, s: (b, s, h)),
        out_shape=jax.ShapeDtypeStruct((B, S, H * GDN_DV), BF16),
        scratch_shapes=[pltpu.VMEM((G, GDN_DK, GDN_DV), F32)],
        compiler_params=_params(vmem, 3),
        name="gdn",
    )(qkv, qkv, qkv, zs, grow, gcol, out_gain)


def _swap_halves(y):
    half = y.shape[-1] // 2
    return jnp.concatenate([y[:, half:], y[:, :half]], axis=1)


def _mla_prep_kernel(lat_ref, cos_ref, sin_ref, cost_ref, sint_ref, qg_ref, kvg_ref, wqt_ref, wk_ref, wvt_ref,
                     qnc_ref, kn_ref, qt_ref, k_ref, vt_ref):
    H = MLA_HEADS
    half = QK_ROPE // 2
    lat = lat_ref[...]
    ql = lat[:, :Q_LORA]
    kvl = lat[:, Q_LORA:Q_LORA + KV_LORA]
    kpe = lat[:, Q_LORA + KV_LORA:Q_LORA + KV_LORA + QK_ROPE]
    qln = (ql * lax.rsqrt(jnp.mean(ql * ql, axis=-1, keepdims=True) + EPS) * qg_ref[...]).astype(BF16)
    kvn = (kvl * lax.rsqrt(jnp.mean(kvl * kvl, axis=-1, keepdims=True) + EPS) * kvg_ref[...]).astype(BF16)
    qt = _dot_nt(wqt_ref[...], qln)
    knope = _dot(kvn, wk_ref[...])
    vt = _dot_nt(wvt_ref[...], kvn)

    cost = cost_ref[...]
    sint = sint_ref[...]
    qnc = qnc_ref[...]
    scale = MLA_QK ** -0.5 * math.log2(math.e)
    for hh in range(H):
        blk = qt[hh * MLA_QK:(hh + 1) * MLA_QK]
        r = lax.rsqrt(jnp.sum(blk * blk, axis=0, keepdims=True) * (1.0 / MLA_QK) + EPS) * scale
        y = blk * qnc * r
        y1 = y[QK_NOPE:QK_NOPE + half]
        y2 = y[QK_NOPE + half:]
        qt_ref[hh, 0:QK_NOPE, :] = y[:QK_NOPE].astype(BF16)
        qt_ref[hh, QK_NOPE:QK_NOPE + half, :] = (y1 * cost - y2 * sint).astype(BF16)
        qt_ref[hh, QK_NOPE + half:MLA_QK, :] = (y2 * cost + y1 * sint).astype(BF16)
        vt_ref[hh] = vt[hh * V_HEAD:(hh + 1) * V_HEAD].astype(BF16)

    kn = kn_ref[...]
    yk = kpe * kn[:, QK_NOPE:]
    kr = yk * cos_ref[...] + _swap_halves(yk) * sin_ref[...]
    kpe_ss = jnp.sum(kpe * kpe, axis=-1, keepdims=True)
    for hh in range(H):
        k_nope = knope[:, hh * QK_NOPE:(hh + 1) * QK_NOPE]
        ss = jnp.sum(k_nope * k_nope, axis=-1, keepdims=True) + kpe_ss
        r = lax.rsqrt(ss * (1.0 / MLA_QK) + EPS)
        k_ref[hh, :, 0:QK_NOPE] = (k_nope * r * kn[:, :QK_NOPE]).astype(BF16)
        k_ref[hh, :, QK_NOPE:MLA_QK] = (kr * r).astype(BF16)


def _mla_prep(lat, cos2, sin2, cost, sint, q_lat_gain, kv_lat_gain, wqt, wk, wvt, q_norm_col, k_norm, l):
    B, S, nl = lat.shape
    H = MLA_HEADS
    half = QK_ROPE // 2
    tm = min(S, ATTN_Q)
    nq = wqt.shape[1]
    nk = wk.shape[-1]
    nv = wvt.shape[1]
    vmem = ((nq * Q_LORA + (nk + nv) * KV_LORA) * 2 + 2 * tm * nl * 4 + 3 * tm * (nq + nk + nv) * 4
            + 4 * H * tm * (256 + MLA_QK + V_HEAD) * 2)
    return pl.pallas_call(
        _mla_prep_kernel,
        grid=(B, S // tm),
        in_specs=[
            pl.BlockSpec((None, tm, nl), lambda b, s: (b, s, 0)),
            pl.BlockSpec((None, tm, QK_ROPE), lambda b, s: (b, s, 0)),
            pl.BlockSpec((None, tm, QK_ROPE), lambda b, s: (b, s, 0)),
            pl.BlockSpec((None, half, tm), lambda b, s: (b, 0, s)),
            pl.BlockSpec((None, half, tm), lambda b, s: (b, 0, s)),
            pl.BlockSpec((None, 1, Q_LORA), lambda b, s: (l, 0, 0)),
            pl.BlockSpec((None, 1, KV_LORA), lambda b, s: (l, 0, 0)),
            _resident((None, nq, Q_LORA), lambda b, s: (l, 0, 0)),
            _resident((None, KV_LORA, nk), lambda b, s: (l, 0, 0)),
            _resident((None, nv, KV_LORA), lambda b, s: (l, 0, 0)),
            pl.BlockSpec((None, MLA_QK, 1), lambda b, s: (l, 0, 0)),
            pl.BlockSpec((None, 1, MLA_QK), lambda b, s: (l, 0, 0)),
        ],
        out_specs=[
            pl.BlockSpec((None, H, MLA_QK, tm), lambda b, s: (b, 0, 0, s)),
            pl.BlockSpec((None, H, tm, MLA_QK), lambda b, s: (b, 0, s, 0)),
            pl.BlockSpec((None, H, None, V_HEAD, tm), lambda b, s: (b, 0, s, 0, 0)),
        ],
        out_shape=[
            jax.ShapeDtypeStruct((B, H, MLA_QK, S), BF16),
            jax.ShapeDtypeStruct((B, H, S, MLA_QK), BF16),
            jax.ShapeDtypeStruct((B, H, S // tm, V_HEAD, tm), BF16),
        ],
        compiler_params=_params(vmem, 2),
        name="mla_prep",
    )(lat, cos2, sin2, cost, sint, q_lat_gain, kv_lat_gain, wqt, wk, wvt, q_norm_col, k_norm)


def _attn_kernel(qt_ref, k_ref, vt_ref, o_ref, sa_ref, sb_ref, mxa_ref, mxb_ref, m_ref, l_ref, acc_ref, *,
                 tq, chain):
    c = pl.program_id(2)
    bufs = ((sa_ref, mxa_ref), (sb_ref, mxb_ref))

    def scores(qt, j, buf):
        st = _dot(k_ref[pl.ds(pl.multiple_of(j * tq, tq), tq), :], qt)
        buf[0][...] = st
        buf[1][...] = jnp.max(st, axis=0, keepdims=True)

    def reset(w):
        m_ref[w] = jnp.full(m_ref.shape[1:], -jnp.inf, F32)
        l_ref[w] = jnp.zeros(l_ref.shape[1:], F32)
        acc_ref[w] = jnp.zeros(acc_ref.shape[1:], F32)

    def update(w, j, buf, masked=False):
        st = buf[0][...]
        mx = buf[1][...]
        if masked:
            kpos = lax.broadcasted_iota(jnp.int32, (tq, tq), 0)
            qpos = lax.broadcasted_iota(jnp.int32, (tq, tq), 1)
            st = jnp.where(kpos <= qpos, st, -jnp.inf)
            mx = jnp.max(st, axis=0, keepdims=True)
        m = m_ref[w]
        m_new = jnp.maximum(m, mx)
        alpha = jnp.exp2(m - m_new)
        pr = jnp.exp2(st - m_new)
        m_ref[w] = m_new
        l_ref[w] = alpha * l_ref[w] + jnp.sum(pr, axis=0, keepdims=True)
        acc_ref[w] = alpha * acc_ref[w] + _dot(vt_ref[j], pr.astype(BF16))

    def full_blocks(w, qt, cur, nxt, trips):
        def body(t, _):
            scores(qt, 2 * t + 1, nxt)
            update(w, 2 * t, cur)
            scores(qt, 2 * t + 2, cur)
            update(w, 2 * t + 1, nxt)
            return 0
        lax.fori_loop(0, trips, body, 0)

    def q_block(u):
        return qt_ref[:, u * tq:(u + 1) * tq]

    cur, nxt = bufs
    reset(0)
    scores(q_block(0), 0, cur)
    for u in range(chain):
        w = u % 2
        i = c * chain + u
        full_blocks(w, q_block(u), cur, nxt, c * (chain // 2) + u // 2)
        if u % 2 == 1:
            scores(q_block(u), i, nxt)
            update(w, i - 1, cur)
            cur, nxt = nxt, cur
        if u + 1 < chain:
            reset(1 - w)
            scores(q_block(u + 1), 0, nxt)
        update(w, i, cur, masked=True)
        o_ref[u * tq:(u + 1) * tq, :] = (acc_ref[w] / l_ref[w]).T.astype(BF16)
        cur, nxt = nxt, cur


def _attention(qt, k, vt):
    B, H, S, _ = k.shape
    tq = vt.shape[-1]
    nq = S // tq
    chain = min(nq, ATTN_CHAIN)
    assert chain % 2 == 0 and nq % chain == 0, "q blocks are taken in even-length runs"
    vmem = (8 * tq * tq * 4 + 4 * S * (256 + V_HEAD) * 2 + 4 * chain * tq * (256 + V_HEAD) * 2
            + 16 * tq * V7X_LANES * 4)
    return pl.pallas_call(
        functools.partial(_attn_kernel, tq=tq, chain=chain),
        grid=(B, H, nq // chain),
        in_specs=[
            pl.BlockSpec((None, None, MLA_QK, chain * tq), lambda b, h, i: (b, h, 0, i)),
            pl.BlockSpec((None, None, S, MLA_QK), lambda b, h, i: (b, h, 0, 0)),
            pl.BlockSpec((None, None, S // tq, V_HEAD, tq), lambda b, h, i: (b, h, 0, 0, 0)),
        ],
        out_specs=pl.BlockSpec((None, chain * tq, V_HEAD), lambda b, h, i: (b, i, h)),
        out_shape=jax.ShapeDtypeStruct((B, S, H * V_HEAD), BF16),
        scratch_shapes=[pltpu.VMEM((tq, tq), F32), pltpu.VMEM((tq, tq), F32),
                        pltpu.VMEM((1, tq), F32), pltpu.VMEM((1, tq), F32),
                        pltpu.VMEM((2, 1, tq), F32), pltpu.VMEM((2, 1, tq), F32),
                        pltpu.VMEM((2, V_HEAD, tq), F32)],
        compiler_params=_params(vmem, 3),
        name="mla_attention",
    )(qt, k, vt)


def _merge_kernel(x_ref, mod_ref, oa_ref, ob_ref, gates_ref, wa_ref, wb_ref, wo_ref, o_ref):
    D = x_ref.shape[-1]
    mod = mod_ref[...]
    gates = _sigmoid(gates_ref[...].astype(F32))
    ya =_dot(oa_ref[...], wa_ref[...])
    yb = _dot(ob_ref[...], wb_ref[...])
    y = (gates[:, :D] * ya + gates[:, D:] * yb).astype(BF16)
    o_ref[...] = x_ref[...] + (1.0 + mod[5:6]) * _dot(y, wo_ref[...])


def _merge(x, mod, oa, ob, gates, wa, wb, wo, l):
    B, S, D = x.shape
    tm = min(S, MERGE_TOKENS)
    na = oa.shape[-1]
    nbv = ob.shape[-1]
    vmem = (na + nbv + D) * D * 2 + 4 * tm * D * 4 + 2 * tm * (na + nbv + 2 * D) * 2 + 4 * tm * D * 4
    return pl.pallas_call(
        _merge_kernel,
        grid=(B, S // tm),
        in_specs=[
            pl.BlockSpec((None, tm, D), lambda b, s: (b, s, 0)),
            pl.BlockSpec((None, None, N_MOD, D), lambda b, s: (l, b, 0, 0)),
            pl.BlockSpec((None, tm, na), lambda b, s: (b, s, 0)),
            pl.BlockSpec((None, tm, nbv), lambda b, s: (b, s, 0)),
            pl.BlockSpec((None, tm, 2 * D), lambda b, s: (b, s, 0)),
            _resident((None, na, D), lambda b, s: (l, 0, 0)),
            _resident((None, nbv, D), lambda b, s: (l, 0, 0)),
            _resident((None, D, D), lambda b, s: (l, 0, 0)),
        ],
        out_specs=pl.BlockSpec((None, tm, D), lambda b, s: (b, s, 0)),
        out_shape=jax.ShapeDtypeStruct((B, S, D), F32),
        compiler_params=_params(vmem, 2),
        name="merge",
    )(x, mod, oa, ob, gates, wa, wb, wo)


def _pack_w_in(w_in):
    conv_c = 2 * GDN_HEADS * GDN_DK + GDN_HEADS * GDN_DV
    gdn_v = GDN_HEADS * GDN_DV
    D = w_in.shape[1]
    sizes = (conv_c, gdn_v, GDN_HEADS, GDN_HEADS, Q_LORA, KV_LORA, QK_ROPE, D, D)
    offs = np.cumsum((0,) + sizes)
    qkv, z, b_raw, a_raw, q_lat, kv_lat, k_pe, gate_a, gate_b = (
        w_in[:, :, int(offs[i]):int(offs[i + 1])] for i in range(len(sizes)))
    n_lat = Q_LORA + KV_LORA + QK_ROPE
    pad = (-n_lat) % V7X_LANES
    lat = jnp.concatenate([q_lat, kv_lat, k_pe, jnp.zeros(w_in.shape[:2] + (pad,), w_in.dtype)], axis=-1)
    ba = jnp.swapaxes(jnp.concatenate([b_raw, a_raw], axis=-1), 1, 2)
    wall = jnp.concatenate([qkv, z, gate_a, gate_b, lat], axis=-1)
    return wall.astype(BF16), ba.astype(BF16)


def _pack_kv_up(w_kv_up):
    L = w_kv_up.shape[0]
    w = w_kv_up.reshape(L, KV_LORA, MLA_HEADS, QK_NOPE + V_HEAD)
    wk = w[..., :QK_NOPE].reshape(L, KV_LORA, MLA_HEADS * QK_NOPE)
    wv = w[..., QK_NOPE:].reshape(L, KV_LORA, MLA_HEADS * V_HEAD)
    return wk.astype(BF16), jnp.swapaxes(wv, 1, 2).astype(BF16)


def kernel(x, c, positions, ada_w, ada_b, norm_ffn1, ffn1_w1, ffn1_w3, ffn1_w2, norm_mix, w_in, gdn_conv, gdn_a_log, gdn_dt_bias, gdn_out_gain, mla_q_lat_gain, mla_kv_lat_gain, mla_w_q_up, mla_w_kv_up, mla_q_norm, mla_k_norm, w_branch_a, w_branch_b, w_out, norm_ffn2, ffn2_w1, ffn2_w3, ffn2_w2):
    L = ada_w.shape[0]
    S = x.shape[1]
    assert S % CHUNK == 0 and S % min(S, GDN_TOKENS) == 0 and S % min(S, FFN_TOKENS) == 0
    assert S % min(S, MERGE_TOKENS) == 0 and S % min(S, ATTN_Q) == 0

    mod = _ada_mod(c, ada_w, ada_b)
    cos2, sin2, cost, sint = _rope_tables(positions)

    row = lambda a: a.reshape(L, 1, a.shape[-1])
    col = lambda a: a.reshape(L, a.shape[-1], 1)
    bf = lambda a: a.astype(BF16)
    wall, wba = _pack_w_in(w_in)
    wqt = bf(jnp.swapaxes(mla_w_q_up, 1, 2))
    wk, wvt = _pack_kv_up(mla_w_kv_up)
    f1 = (bf(ffn1_w1), bf(ffn1_w3), bf(ffn1_w2))
    f2 = (bf(ffn2_w1), bf(ffn2_w3), bf(ffn2_w2))
    wa, wb, wo = bf(w_branch_a), bf(w_branch_b), bf(w_out)
    g_ffn1, g_mix, g_ffn2 = row(norm_ffn1), row(norm_mix), row(norm_ffn2)
    og, qlg, kvlg = row(gdn_out_gain), row(mla_q_lat_gain), row(mla_kv_lat_gain)
    qnc, kn = col(mla_q_norm), row(mla_k_norm)
    a_log, dt_bias = col(gdn_a_log), col(gdn_dt_bias)

    for l in range(L):
        x = _ffn(x, mod, g_ffn1, *f1, l, 0)
        qkv, zs, gates, lat, ba = _mixin(x, mod, g_mix, wall, wba, gdn_conv, l)
        grow, gcol = _gdn_gates(ba, a_log, dt_bias, l)
        o_gdn = _gdn(qkv, zs, grow, gcol, og, l)
        qt, k, vt = _mla_prep(lat, cos2, sin2, cost, sint, qlg, kvlg, wqt, wk, wvt, qnc, kn, l)
        o_mla = _attention(qt, k, vt)
        x = _merge(x, mod, o_gdn, o_mla, gates, wa, wb, wo, l)
        x = _ffn(x, mod, g_ffn2, *f2, l, 6)
    return x
```

```python
import functools
import math

import numpy as np
import jax
import jax.numpy as jnp
from jax import lax
from jax.experimental import pallas as pl
from jax.experimental.pallas import tpu as pltpu

F32 = jnp.float32
BF16 = jnp.bfloat16

EPS = 1e-6
N_MOD = 9
GDN_HEADS = 8
GDN_DK = 128
GDN_DV = 128
CONV_K = 4
CHUNK = 64
LOG2_CHUNK = 6
MLA_HEADS = 8
Q_LORA = 384
KV_LORA = 256
QK_NOPE = 128
QK_ROPE = 64
V_HEAD = 128
ROPE_THETA = 10000.0
MLA_QK = QK_NOPE + QK_ROPE

V7X_VMEM_BYTES = 64 * 1024 * 1024
V7X_LANES = 128
V7X_SUBLANES = 8
VMEM_CAP_BYTES = V7X_VMEM_BYTES - 6 * 1024 * 1024

FFN_TOKENS = 1024
MIX_TOKENS = 256
GATE_TOKENS = 512
GDN_TOKENS = 256
GDN_GROUP = 8
ATTN_Q = 512
ATTN_CHAIN = 8
MERGE_TOKENS = 1024
HALO = V7X_SUBLANES


def _params(vmem_bytes, n_axes):
    limit = int(min(VMEM_CAP_BYTES, max(vmem_bytes, 16 * 1024 * 1024)))
    return pltpu.CompilerParams(dimension_semantics=("arbitrary",) * n_axes, vmem_limit_bytes=limit)


def _resident(block_shape, index_map):
    return pl.BlockSpec(block_shape, index_map, pipeline_mode=pl.Buffered(1))


def _dot(a, b):
    return jnp.dot(a, b, preferred_element_type=F32)


def _dot_nt(a, b):
    return lax.dot_general(a, b, (((1,), (1,)), ((), ())), preferred_element_type=F32)


def _sigmoid(x):
    return 0.5 * jnp.tanh(0.5 * x) + 0.5


def _silu(x):
    hx = 0.5 * x
    return hx * jnp.tanh(hx) + hx


def _modulate(x, gain, shift, scale):
    y = x * lax.rsqrt(jnp.mean(x * x, axis=-1, keepdims=True) + EPS)
    return (y * gain) * (1.0 + scale) + shift


def _split3(x):
    x1 = x.astype(BF16)
    r1 = x - x1.astype(F32)
    x2 = r1.astype(BF16)
    x3 = (r1 - x2.astype(F32)).astype(BF16)
    return x1, x2, x3


def _mod_kernel(c_ref, w_ref, b_ref, o_ref):
    cond = _silu(c_ref[...]).astype(BF16)
    o_ref[...] = _dot(cond, w_ref[...].astype(BF16)) + b_ref[...]


def _ada_mod(c, ada_w, ada_b):
    L, D, _ = ada_w.shape
    B = c.shape[0]
    ada_b3 = ada_b.reshape(L, N_MOD, 1, D)
    out = pl.pallas_call(
        _mod_kernel,
        grid=(L, N_MOD),
        in_specs=[
            pl.BlockSpec((B, D), lambda l, j: (0, 0)),
            pl.BlockSpec((None, D, D), lambda l, j: (l, 0, j)),
            pl.BlockSpec((None, None, 1, D), lambda l, j: (l, j, 0, 0)),
        ],
        out_specs=pl.BlockSpec((None, None, B, D), lambda l, j: (l, j, 0, 0)),
        out_shape=jax.ShapeDtypeStruct((L, N_MOD, B, D), F32),
        compiler_params=_params(4 * D * D * 4, 2),
        name="ada_mod",
    )(c, ada_w, ada_b3)
    return jnp.transpose(out, (0, 2, 1, 3))


def _rope_kernel(pos_ref, post_ref, freq_ref, sign_ref, freqc_ref, cos_ref, sin_ref, cost_ref, sint_ref):
    ang = pos_ref[...].astype(F32) * freq_ref[...]
    cos_ref[...] = jnp.cos(ang)
    sin_ref[...] = jnp.sin(ang) * sign_ref[...]
    angt = freqc_ref[...] * post_ref[...].astype(F32)
    cost_ref[...] = jnp.cos(angt)
    sint_ref[...] = jnp.sin(angt)


def _rope_tables(positions):
    B, S = positions.shape
    half = QK_ROPE // 2
    inv_freq = ROPE_THETA ** (-jnp.arange(half, dtype=F32) / half)
    freq2 = jnp.concatenate([inv_freq, inv_freq]).reshape(1, QK_ROPE)
    sign2 = jnp.concatenate([-jnp.ones((half,), F32), jnp.ones((half,), F32)]).reshape(1, QK_ROPE)
    ts = min(S, 1024)
    return pl.pallas_call(
        _rope_kernel,
        grid=(B, S // ts),
        in_specs=[
            pl.BlockSpec((None, ts, 1), lambda b, s: (b, s, 0)),
            pl.BlockSpec((None, 1, ts), lambda b, s: (b, 0, s)),
            pl.BlockSpec((1, QK_ROPE), lambda b, s: (0, 0)),
            pl.BlockSpec((1, QK_ROPE), lambda b, s: (0, 0)),
            pl.BlockSpec((half, 1), lambda b, s: (0, 0)),
        ],
        out_specs=[pl.BlockSpec((None, ts, QK_ROPE), lambda b, s: (b, s, 0))] * 2
        + [pl.BlockSpec((None, half, ts), lambda b, s: (b, 0, s))] * 2,
        out_shape=[jax.ShapeDtypeStruct((B, S, QK_ROPE), F32)] * 2
        + [jax.ShapeDtypeStruct((B, half, S), F32)] * 2,
        compiler_params=_params(8 * ts * V7X_LANES * 4, 2),
        name="rope_tables",
    )(positions.reshape(B, S, 1), positions.reshape(B, 1, S), freq2, sign2, inv_freq.reshape(half, 1))


def _ffn_kernel(x_ref, mod_ref, gain_ref, w1_ref, w3_ref, w2_ref, o_ref, *, row):
    x = x_ref[...]
    mod = mod_ref[...]
    h = _modulate(x, gain_ref[...], mod[row:row + 1], mod[row + 1:row + 2]).astype(BF16)
    a = _dot(h, w1_ref[...])
    b = _dot(h, w3_ref[...])
    g = (_silu(a) * b).astype(BF16)
    y = _dot(g, w2_ref[...])
    o_ref[...] = x + (0.5 * (1.0 + mod[row + 2:row + 3])) * y


def _ffn(x, mod, gain, w1, w3, w2, l, row):
    B, S, D = x.shape
    F = w1.shape[-1]
    tm = min(S, FFN_TOKENS)
    vmem = 3 * D * F * 2 + 4 * tm * D * 4 + 3 * tm * F * 4 + 2 * tm * D * 4
    return pl.pallas_call(
        functools.partial(_ffn_kernel, row=row),
        grid=(B, S // tm),
        in_specs=[
            pl.BlockSpec((None, tm, D), lambda b, s: (b, s, 0)),
            pl.BlockSpec((None, None, N_MOD, D), lambda b, s: (l, b, 0, 0)),
            pl.BlockSpec((None, 1, D), lambda b, s: (l, 0, 0)),
            _resident((None, D, F), lambda b, s: (l, 0, 0)),
            _resident((None, D, F), lambda b, s: (l, 0, 0)),
            _resident((None, F, D), lambda b, s: (l, 0, 0)),
        ],
        out_specs=pl.BlockSpec((None, tm, D), lambda b, s: (b, s, 0)),
        out_shape=jax.ShapeDtypeStruct((B, S, D), F32),
        compiler_params=_params(vmem, 2),
        name="ffn",
    )(x, mod, gain, w1, w3, w2)


def _mixin_kernel(x_ref, mod_ref, gain_ref, wall_ref, wba_ref, conv_ref,
                  qkv_ref, zs_ref, gates_ref, lat_ref, ba_ref, pbuf_ref, *, tm):
    @pl.when(pl.program_id(1) == 0)
    def _():
        pbuf_ref[0:HALO, :] = jnp.zeros((HALO, pbuf_ref.shape[1]), F32)

    x = x_ref[...]
    mod = mod_ref[...]
    h = _modulate(x, gain_ref[...], mod[3:4], mod[4:5]).astype(BF16)

    C = pbuf_ref.shape[1]
    nz = zs_ref.shape[0] * zs_ref.shape[2]
    ng = gates_ref.shape[1]
    proj = _dot(h, wall_ref[...])
    pbuf_ref[HALO:HALO + tm, :] = proj[:, :C]
    conv = conv_ref[...]
    assert CONV_K == 4
    n_qk = 2 * GDN_HEADS
    for cb in range(pbuf_ref.shape[1] // V7X_LANES):
        c0 = cb * V7X_LANES
        xs = pbuf_ref[:, c0:c0 + V7X_LANES]
        w = [conv[j:j + 1, c0:c0 + V7X_LANES] for j in range(CONV_K)]
        x1 = pltpu.roll(xs, 1, axis=0)
        near = w[3] * xs + w[2] * x1
        far = w[1] * xs + w[0] * x1
        y = _silu(near[HALO:] + pltpu.roll(far, 2, axis=0)[HALO:])
        if cb < n_qk:
            y = y * lax.rsqrt(jnp.sum(y * y, axis=-1, keepdims=True) + EPS)
        qkv_ref[cb] = y.astype(BF16)
    pbuf_ref[0:HALO, :] = pbuf_ref[tm:tm + HALO, :]

    for hh in range(GDN_HEADS):
        zs_ref[hh] = proj[:, C + hh * GDN_DV:C + (hh + 1) * GDN_DV].astype(BF16)
    gates_ref[...] = proj[:, C + nz:C + nz + ng].astype(BF16)
    lat_ref[...] = proj[:, C + nz + ng:]
    ba_ref[...] = _dot_nt(wba_ref[...], h)


def _mixin(x, mod, gain, wall, wba, conv, l):
    B, S, D = x.shape
    C = conv.shape[-1]
    nz = GDN_HEADS * GDN_DV
    ng = 2 * D
    nl = wall.shape[-1] - C - nz - ng
    nb = wba.shape[1]
    tm = min(S, MIX_TOKENS)
    ncb = C // V7X_LANES
    vmem = ((C + nz + ng + nl) * D * 2 + 2 * tm * D * 4 + 2 * tm * (C + nz + ng) * 2 + 2 * tm * nl * 4
            + (tm + HALO) * C * 4 + tm * C * 4 + tm * (nz + ng) * 4 + 4 * 1024 * 1024)
    return pl.pallas_call(
        functools.partial(_mixin_kernel, tm=tm),
        grid=(B, S // tm),
        in_specs=[
            pl.BlockSpec((None, tm, D), lambda b, s: (b, s, 0)),
            pl.BlockSpec((None, None, N_MOD, D), lambda b, s: (l, b, 0, 0)),
            pl.BlockSpec((None, 1, D), lambda b, s: (l, 0, 0)),
            _resident((None, D, C + nz + ng + nl), lambda b, s: (l, 0, 0)),
            _resident((None, nb, D), lambda b, s: (l, 0, 0)),
            pl.BlockSpec((None, CONV_K, C), lambda b, s: (l, 0, 0)),
        ],
        out_specs=[
            pl.BlockSpec((None, ncb, tm, V7X_LANES), lambda b, s: (b, 0, s, 0)),
            pl.BlockSpec((None, GDN_HEADS, tm, GDN_DV), lambda b, s: (b, 0, s, 0)),
            pl.BlockSpec((None, tm, ng), lambda b, s: (b, s, 0)),
            pl.BlockSpec((None, tm, nl), lambda b, s: (b, s, 0)),
            pl.BlockSpec((None, nb, tm), lambda b, s: (b, 0, s)),
        ],
        out_shape=[
            jax.ShapeDtypeStruct((B, ncb, S, V7X_LANES), BF16),
            jax.ShapeDtypeStruct((B, GDN_HEADS, S, GDN_DV), BF16),
            jax.ShapeDtypeStruct((B, S, ng), BF16),
            jax.ShapeDtypeStruct((B, S, nl), F32),
            jax.ShapeDtypeStruct((B, nb, S), F32),
        ],
        scratch_shapes=[pltpu.VMEM((tm + HALO, C), F32)],
        compiler_params=_params(vmem, 2),
        name="mixin",
    )(x, mod, gain, wall, wba, conv)


def _gate_kernel(ba_ref, alog_ref, dtb_ref, grow_ref, gcol_ref, *, tm):
    H = GDN_HEADS
    ba = ba_ref[...]
    beta = _sigmoid(ba[0:H])
    xa = ba[H:2 * H] + dtb_ref[...]
    softplus = jnp.maximum(xa, 0.0) + jnp.log(1.0 + jnp.exp(-jnp.abs(xa)))
    g = -jnp.exp(alog_ref[...]) * softplus

    ii = lax.broadcasted_iota(jnp.int32, (tm, tm), 0)
    jj = lax.broadcasted_iota(jnp.int32, (tm, tm), 1)
    same = (ii >> LOG2_CHUNK) == (jj >> LOG2_CHUNK)
    upper =jnp.where(same & (ii <= jj), 1.0, 0.0).astype(BF16)
    block = jnp.where(same, 1.0, 0.0).astype(BF16)
    eye = jnp.where(ii == jj, 1.0, 0.0).astype(BF16)
    gam = None
    glast = None
    for part in _split3(g):
        t1 = _dot(part, upper)
        t2 = _dot(part, block)
        gam = t1 if gam is None else gam + t1
        glast = t2 if glast is None else glast + t2
    egam = jnp.exp(gam)
    ekd = jnp.exp(glast - gam)
    egl = jnp.exp(glast)
    zero = jnp.zeros_like(gam[0:1])
    rows = []
    for hh in range(H):
        r = jnp.concatenate([beta[hh:hh + 1], gam[hh:hh + 1], egam[hh:hh + 1], ekd[hh:hh + 1],
                             egl[hh:hh + 1], zero, zero, zero], axis=0)
        grow_ref[hh] = r
        rows.append(r)
    allrows = jnp.concatenate(rows, axis=0)
    cols = None
    for part in _split3(allrows):
        t = _dot_nt(eye, part)
        cols = t if cols is None else cols + t
    for hh in range(H):
        gcol_ref[hh] = cols[:, 8 * hh:8 * hh + 8]


def _gdn_gates(ba, a_log, dt_bias, l):
    B, nb, S = ba.shape
    H = GDN_HEADS
    tm = min(S, GATE_TOKENS)
    vmem = 8 * tm * tm * 4 + 4 * H * tm * V7X_LANES * 4
    return pl.pallas_call(
        functools.partial(_gate_kernel, tm=tm),
        grid=(B, S // tm),
        in_specs=[
            pl.BlockSpec((None, nb, tm), lambda b, s: (b, 0, s)),
            pl.BlockSpec((None, H, 1), lambda b, s: (l, 0, 0)),
            pl.BlockSpec((None, H, 1), lambda b, s: (l, 0, 0)),
        ],
        out_specs=[
            pl.BlockSpec((None, H, 8, tm), lambda b, s: (b, 0, 0, s)),
            pl.BlockSpec((None, H, tm, 8), lambda b, s: (b, 0, s, 0)),
        ],
        out_shape=[
            jax.ShapeDtypeStruct((B, H, 8, S), F32),
            jax.ShapeDtypeStruct((B, H, S, 8), F32),
        ],
        compiler_params=_params(vmem, 2),
        name="gdn_gates",
    )(ba, a_log, dt_bias)


def _gdn_kernel(q_ref, k_ref, v_ref, zs_ref, grow_ref, gcol_ref, og_ref, o_ref, state_ref, *, tc, G):
    s = pl.program_id(2)

    @pl.when(s == 0)
    def _():
        state_ref[...] = jnp.zeros(state_ref.shape, F32)

    heads = range(G)
    n_chunks = tc // CHUNK
    scale = GDN_DK ** -0.5
    ii = lax.broadcasted_iota(jnp.int32, (tc, tc), 0)
    jj = lax.broadcasted_iota(jnp.int32, (tc, tc), 1)
    same = (ii >> LOG2_CHUNK) == (jj >> LOG2_CHUNK)
    causal = same & (ii >= jj)
    strict = same & (ii > jj)
    lane = lax.broadcasted_iota(jnp.int32, (GDN_DK, tc), 1) >> LOG2_CHUNK

    gc = [gcol_ref[g] for g in heads]
    gr = [grow_ref[g] for g in heads]
    k = [k_ref[g] for g in heads]
    kf = [k[g].astype(F32) for g in heads]
    kb = [kf[g] * gc[g][:, 0:1] for g in heads]
    decay = [jnp.where(causal, jnp.exp(jnp.minimum(gc[g][:, 1:2] - gr[g][1:2, :], 0.0)), 0.0) for g in heads]

    akq = [_dot_nt(jnp.concatenate([kb[g].astype(BF16), q_ref[g]], axis=0), k[g]) for g in heads]
    m = [jnp.where(strict, akq[g][:tc] * decay[g], 0.0) for g in heads]
    qk = [(akq[g][tc:] * (decay[g] * scale)).astype(BF16) for g in heads]

    rlane = lax.broadcasted_iota(jnp.int32, (CHUNK, tc), 1)
    rrow = lax.broadcasted_iota(jnp.int32, (CHUNK, tc), 0)
    eye_row = jnp.where((rlane & (CHUNK - 1)) == rrow, 1.0, 0.0)
    rblk = rlane >> LOG2_CHUNK

    def blockdiag(x_row):
        return jnp.concatenate([jnp.where(rblk == c, x_row, 0.0) for c in range(n_chunks)], axis=0).astype(BF16)

    def side_by_side(x):
        acc = x[0:CHUNK]
        for c in range(1, n_chunks):
            acc = acc + x[c * CHUNK:(c + 1) * CHUNK]
        return acc

    m_row = [side_by_side(m[g]) for g in heads]
    p = [_dot(m_row[g].astype(BF16), blockdiag(m_row[g])) for g in heads]
    t_row = [eye_row - m_row[g] for g in heads]
    for _ in range(LOG2_CHUNK - 2):
        r = [_dot(jnp.concatenate([p[g], t_row[g]], axis=0).astype(BF16), blockdiag(p[g])) for g in heads]
        p = [r[g][:CHUNK] for g in heads]
        t_row = [t_row[g] + r[g][CHUNK:] for g in heads]
    t_row = [t_row[g] + _dot(t_row[g].astype(BF16), blockdiag(p[g])) for g in heads]

    rhs = [jnp.concatenate([v_ref[g].astype(F32) * gc[g][:, 0:1], kb[g] * gc[g][:, 2:3]], axis=1).astype(BF16)
           for g in heads]
    uw = [_dot(blockdiag(t_row[g]), rhs[g]).astype(BF16) for g in heads]

    kdt = [(kf[g] * gc[g][:, 3:4]).T for g in heads]
    pre_lhs = [jnp.concatenate([jnp.where(lane == c, kdt[g], 0.0).astype(BF16) for c in range(n_chunks)]
                               + [qk[g]], axis=0) for g in heads]
    pre = [_dot(pre_lhs[g], uw[g]) for g in heads]
    qd = [q_ref[g].astype(F32) * (gc[g][:, 2:3] * scale) for g in heads]
    base = n_chunks * GDN_DK
    qmod = [(qd[g] - pre[g][base:, GDN_DV:]).astype(BF16) for g in heads]

    states = [state_ref[g] for g in heads]
    outs = [[] for _ in heads]
    for c in range(n_chunks):
        r0 = c * CHUNK
        for g in heads:
            kblk = pre[g][c * GDN_DK:(c + 1) * GDN_DK]
            lhs = jnp.concatenate([kblk[:, GDN_DV:].astype(BF16), qmod[g][r0:r0 + CHUNK]], axis=0)
            r = _dot(lhs, states[g].astype(BF16))
            outs[g].append(r[GDN_DK:] + pre[g][base + r0:base + r0 + CHUNK, :GDN_DV])
            states[g] = states[g] * gr[g][4:5, r0:r0 + 1] - r[:GDN_DK] + kblk[:, :GDN_DV]
    for g in heads:
        state_ref[g] = states[g]
        o = jnp.concatenate(outs[g], axis=0)
        o = o * lax.rsqrt(jnp.mean(o * o, axis=-1, keepdims=True) + EPS) * og_ref[...]
        o_ref[:, g * GDN_DV:(g + 1) * GDN_DV] = (o * _silu(zs_ref[g].astype(F32))).astype(BF16)


def _gdn(qkv, zs, grow, gcol, out_gain, l):
    B, _, S, _ = qkv.shape
    H = GDN_HEADS
    G = GDN_GROUP
    tc = min(S, GDN_TOKENS)
    vmem = G * (24 * tc * tc * 4 + 24 * tc * V7X_LANES * 4)
    return pl.pallas_call(
        functools.partial(_gdn_kernel, tc=tc, G=G),
        grid=(B, H // G, S // tc),
        in_specs=[
            pl.BlockSpec((None, G, tc, GDN_DK), lambda b, h, s: (b, h, s, 0)),
            pl.BlockSpec((None, G, tc, GDN_DK), lambda b, h, s: (b, H // G + h, s, 0)),
            pl.BlockSpec((None, G, tc, GDN_DV), lambda b, h, s: (b, 2 * H // G + h, s, 0)),
            pl.BlockSpec((None, G, tc, GDN_DV), lambda b, h, s: (b, h, s, 0)),
            pl.BlockSpec((None, G, 8, tc), lambda b, h, s: (b, h, 0, s)),
            pl.BlockSpec((None, G, tc, 8), lambda b, h, s: (b, h, s, 0)),
            pl.BlockSpec((None, 1, GDN_DV), lambda b, h, s: (l, 0, 0)),
        ],
        out_specs=pl.BlockSpec((None, tc, G * GDN_DV), lambda b, h, s: (b, s, h)),
        out_shape=jax.ShapeDtypeStruct((B, S, H * GDN_DV), BF16),
        scratch_shapes=[pltpu.VMEM((G, GDN_DK, GDN_DV), F32)],
        compiler_params=_params(vmem, 3),
        name="gdn",
    )(qkv, qkv, qkv, zs, grow, gcol, out_gain)


def _swap_halves(y):
    half = y.shape[-1] // 2
    return jnp.concatenate([y[:, half:], y[:, :half]], axis=1)


def _mla_prep_kernel(lat_ref, cos_ref, sin_ref, cost_ref, sint_ref, qg_ref, kvg_ref, wqt_ref, wk_ref, wvt_ref,
                     qnc_ref, kn_ref, qt_ref, k_ref, vt_ref):
    H = MLA_HEADS
    half = QK_ROPE // 2
    lat = lat_ref[...]
    ql = lat[:, :Q_LORA]
    kvl = lat[:, Q_LORA:Q_LORA + KV_LORA]
    kpe = lat[:, Q_LORA + KV_LORA:Q_LORA + KV_LORA + QK_ROPE]
    qln = (ql * lax.rsqrt(jnp.mean(ql * ql, axis=-1, keepdims=True) + EPS) * qg_ref[...]).astype(BF16)
    kvn = (kvl * lax.rsqrt(jnp.mean(kvl * kvl, axis=-1, keepdims=True) + EPS) * kvg_ref[...]).astype(BF16)
    qt = _dot_nt(wqt_ref[...], qln)
    knope = _dot(kvn, wk_ref[...])
    vt = _dot_nt(wvt_ref[...], kvn)

    cost = cost_ref[...]
    sint = sint_ref[...]
    qnc = qnc_ref[...]
    scale = MLA_QK ** -0.5 * math.log2(math.e)
    for hh in range(H):
        blk = qt[hh * MLA_QK:(hh + 1) * MLA_QK]
        r = lax.rsqrt(jnp.sum(blk * blk, axis=0, keepdims=True) * (1.0 / MLA_QK) + EPS) * scale
        y = blk * qnc * r
        y1 = y[QK_NOPE:QK_NOPE + half]
        y2 = y[QK_NOPE + half:]
        qt_ref[hh, 0:QK_NOPE, :] = y[:QK_NOPE].astype(BF16)
        qt_ref[hh, QK_NOPE:QK_NOPE + half, :] = (y1 * cost - y2 * sint).astype(BF16)
        qt_ref[hh, QK_NOPE + half:MLA_QK, :] = (y2 * cost + y1 * sint).astype(BF16)
        vt_ref[hh] = vt[hh * V_HEAD:(hh + 1) * V_HEAD].astype(BF16)

    kn = kn_ref[...]
    yk = kpe * kn[:, QK_NOPE:]
    kr = yk * cos_ref[...] + _swap_halves(yk) * sin_ref[...]
    kpe_ss = jnp.sum(kpe * kpe, axis=-1, keepdims=True)
    for hh in range(H):
        k_nope = knope[:, hh * QK_NOPE:(hh + 1) * QK_NOPE]
        ss = jnp.sum(k_nope * k_nope, axis=-1, keepdims=True) + kpe_ss
        r = lax.rsqrt(ss * (1.0 / MLA_QK) + EPS)
        k_ref[hh, :, 0:QK_NOPE] = (k_nope * r * kn[:, :QK_NOPE]).astype(BF16)
        k_ref[hh, :, QK_NOPE:MLA_QK] = (kr * r).astype(BF16)


def _mla_prep(lat, cos2, sin2, cost, sint, q_lat_gain, kv_lat_gain, wqt, wk, wvt, q_norm_col, k_norm, l):
    B, S, nl = lat.shape
    H = MLA_HEADS
    half = QK_ROPE // 2
    tm = min(S, ATTN_Q)
    nq = wqt.shape[1]
    nk = wk.shape[-1]
    nv = wvt.shape[1]
    vmem = ((nq * Q_LORA + (nk + nv) * KV_LORA) * 2 + 2 * tm * nl * 4 + 3 * tm * (nq + nk + nv) * 4
            + 4 * H * tm * (256 + MLA_QK + V_HEAD) * 2)
    return pl.pallas_call(
        _mla_prep_kernel,
        grid=(B, S // tm),
        in_specs=[
            pl.BlockSpec((None, tm, nl), lambda b, s: (b, s, 0)),
            pl.BlockSpec((None, tm, QK_ROPE), lambda b, s: (b, s, 0)),
            pl.BlockSpec((None, tm, QK_ROPE), lambda b, s: (b, s, 0)),
            pl.BlockSpec((None, half, tm), lambda b, s: (b, 0, s)),
            pl.BlockSpec((None, half, tm), lambda b, s: (b, 0, s)),
            pl.BlockSpec((None, 1, Q_LORA), lambda b, s: (l, 0, 0)),
            pl.BlockSpec((None, 1, KV_LORA), lambda b, s: (l, 0, 0)),
            _resident((None, nq, Q_LORA), lambda b, s: (l, 0, 0)),
            _resident((None, KV_LORA, nk), lambda b, s: (l, 0, 0)),
            _resident((None, nv, KV_LORA), lambda b, s: (l, 0, 0)),
            pl.BlockSpec((None, MLA_QK, 1), lambda b, s: (l, 0, 0)),
            pl.BlockSpec((None, 1, MLA_QK), lambda b, s: (l, 0, 0)),
        ],
        out_specs=[
            pl.BlockSpec((None, H, MLA_QK, tm), lambda b, s: (b, 0, 0, s)),
            pl.BlockSpec((None, H, tm, MLA_QK), lambda b, s: (b, 0, s, 0)),
            pl.BlockSpec((None, H, None, V_HEAD, tm), lambda b, s: (b, 0, s, 0, 0)),
        ],
        out_shape=[
            jax.ShapeDtypeStruct((B, H, MLA_QK, S), BF16),
            jax.ShapeDtypeStruct((B, H, S, MLA_QK), BF16),
            jax.ShapeDtypeStruct((B, H, S // tm, V_HEAD, tm), BF16),
        ],
        compiler_params=_params(vmem, 2),
        name="mla_prep",
    )(lat, cos2, sin2, cost, sint, q_lat_gain, kv_lat_gain, wqt, wk, wvt, q_norm_col, k_norm)


def _attn_kernel(qt_ref, k_ref, vt_ref, o_ref, sa_ref, sb_ref, mxa_ref, mxb_ref, m_ref, l_ref, acc_ref, *,
                 tq, chain, single):
    c = 0 if single else pl.program_id(2)
    bufs = ((sa_ref, mxa_ref), (sb_ref, mxb_ref))

    def scores(qt, j, buf):
        start = j * tq if isinstance(j, int) else pl.multiple_of(j * tq, tq)
        st = _dot(k_ref[pl.ds(start, tq), :], qt)
        buf[0][...] = st
        buf[1][...] = jnp.max(st, axis=0, keepdims=True)

    def reset(w):
        m_ref[w] = jnp.full(m_ref.shape[1:], -jnp.inf, F32)
        l_ref[w] = jnp.zeros(l_ref.shape[1:], F32)
        acc_ref[w] = jnp.zeros(acc_ref.shape[1:], F32)

    def update(w, j, buf, masked=False):
        st = buf[0][...]
        mx = buf[1][...]
        if masked:
            kpos = lax.broadcasted_iota(jnp.int32, (tq, tq), 0)
            qpos = lax.broadcasted_iota(jnp.int32, (tq, tq), 1)
            st = jnp.where(kpos <= qpos, st, -jnp.inf)
            mx = jnp.max(st, axis=0, keepdims=True)
        m = m_ref[w]
        m_new = jnp.maximum(m, mx)
        alpha = jnp.exp2(m - m_new)
        pr = jnp.exp2(st - m_new)
        m_ref[w] = m_new
        l_ref[w] = alpha * l_ref[w] + jnp.sum(pr, axis=0, keepdims=True)
        acc_ref[w] = alpha * acc_ref[w] + _dot(vt_ref[j], pr.astype(BF16))

    def full_blocks(w, qt, cur, nxt, trips):
        def body(t, _):
            scores(qt, 2 * t + 1, nxt)
            update(w, 2 * t, cur)
            scores(qt, 2 * t + 2, cur)
            update(w, 2 * t + 1, nxt)
            return 0
        if isinstance(trips, int):
            for t in range(trips):
                body(t, 0)
        else:
            lax.fori_loop(0, trips, body, 0)

    def q_block(u):
        return qt_ref[:, u * tq:(u + 1) * tq]

    cur, nxt = bufs
    reset(0)
    scores(q_block(0), 0, cur)
    for u in range(chain):
        w = u % 2
        i = c * chain + u
        full_blocks(w, q_block(u), cur, nxt, c * (chain // 2) + u // 2)
        if u % 2 == 1:
            scores(q_block(u), i, nxt)
            update(w, i - 1, cur)
            cur, nxt = nxt, cur
        if u + 1 < chain:
            reset(1 - w)
            scores(q_block(u + 1), 0, nxt)
        update(w, i, cur, masked=True)
        o_ref[u * tq:(u + 1) * tq, :] = (acc_ref[w] / l_ref[w]).T.astype(BF16)
        cur, nxt = nxt, cur


def _attention(qt, k, vt):
    B, H, S, _ = k.shape
    tq = vt.shape[-1]
    nq = S // tq
    chain = min(nq, ATTN_CHAIN)
    assert chain % 2 == 0 and nq % chain == 0, "q blocks are taken in even-length runs"
    vmem = (8 * tq * tq * 4 + 4 * S * (256 + V_HEAD) * 2 + 4 * chain * tq * (256 + V_HEAD) * 2
            + 16 * tq * V7X_LANES * 4)
    return pl.pallas_call(
        functools.partial(_attn_kernel, tq=tq, chain=chain, single=(nq == chain)),
        grid=(B, H, nq // chain),
        in_specs=[
            pl.BlockSpec((None, None, MLA_QK, chain * tq), lambda b, h, i: (b, h, 0, i)),
            pl.BlockSpec((None, None, S, MLA_QK), lambda b, h, i: (b, h, 0, 0)),
            pl.BlockSpec((None, None, S // tq, V_HEAD, tq), lambda b, h, i: (b, h, 0, 0, 0)),
        ],
        out_specs=pl.BlockSpec((None, chain * tq, V_HEAD), lambda b, h, i: (b, i, h)),
        out_shape=jax.ShapeDtypeStruct((B, S, H * V_HEAD), BF16),
        scratch_shapes=[pltpu.VMEM((tq, tq), F32), pltpu.VMEM((tq, tq), F32),
                        pltpu.VMEM((1, tq), F32), pltpu.VMEM((1, tq), F32),
                        pltpu.VMEM((2, 1, tq), F32), pltpu.VMEM((2, 1, tq), F32),
                        pltpu.VMEM((2, V_HEAD, tq), F32)],
        compiler_params=_params(vmem, 3),
        name="mla_attention",
    )(qt, k, vt)


def _merge_kernel(x_ref, mod_ref, oa_ref, ob_ref, gates_ref, wa_ref, wb_ref, wo_ref, o_ref):
    D = x_ref.shape[-1]
    mod = mod_ref[...]
    gates = _sigmoid(gates_ref[...].astype(F32))
    ya =_dot(oa_ref[...], wa_ref[...])
    yb = _dot(ob_ref[...], wb_ref[...])
    y = (gates[:, :D] * ya + gates[:, D:] * yb).astype(BF16)
    o_ref[...] = x_ref[...] + (1.0 + mod[5:6]) * _dot(y, wo_ref[...])


def _merge(x, mod, oa, ob, gates, wa, wb, wo, l):
    B, S, D = x.shape
    tm = min(S, MERGE_TOKENS)
    na = oa.shape[-1]
    nbv = ob.shape[-1]
    vmem = (na + nbv + D) * D * 2 + 4 * tm * D * 4 + 2 * tm * (na + nbv + 2 * D) * 2 + 4 * tm * D * 4
    return pl.pallas_call(
        _merge_kernel,
        grid=(B, S // tm),
        in_specs=[
            pl.BlockSpec((None, tm, D), lambda b, s: (b, s, 0)),
            pl.BlockSpec((None, None, N_MOD, D), lambda b, s: (l, b, 0, 0)),
            pl.BlockSpec((None, tm, na), lambda b, s: (b, s, 0)),
            pl.BlockSpec((None, tm, nbv), lambda b, s: (b, s, 0)),
            pl.BlockSpec((None, tm, 2 * D), lambda b, s: (b, s, 0)),
            _resident((None, na, D), lambda b, s: (l, 0, 0)),
            _resident((None, nbv, D), lambda b, s: (l, 0, 0)),
            _resident((None, D, D), lambda b, s: (l, 0, 0)),
        ],
        out_specs=pl.BlockSpec((None, tm, D), lambda b, s: (b, s, 0)),
        out_shape=jax.ShapeDtypeStruct((B, S, D), F32),
        compiler_params=_params(vmem, 2),
        name="merge",
    )(x, mod, oa, ob, gates, wa, wb, wo)


def _pack_w_in(w_in):
    conv_c = 2 * GDN_HEADS * GDN_DK + GDN_HEADS * GDN_DV
    gdn_v = GDN_HEADS * GDN_DV
    D = w_in.shape[1]
    sizes = (conv_c, gdn_v, GDN_HEADS, GDN_HEADS, Q_LORA, KV_LORA, QK_ROPE, D, D)
    offs = np.cumsum((0,) + sizes)
    qkv, z, b_raw, a_raw, q_lat, kv_lat, k_pe, gate_a, gate_b = (
        w_in[:, :, int(offs[i]):int(offs[i + 1])] for i in range(len(sizes)))
    n_lat = Q_LORA + KV_LORA + QK_ROPE
    pad = (-n_lat) % V7X_LANES
    lat = jnp.concatenate([q_lat, kv_lat, k_pe, jnp.zeros(w_in.shape[:2] + (pad,), w_in.dtype)], axis=-1)
    ba = jnp.swapaxes(jnp.concatenate([b_raw, a_raw], axis=-1), 1, 2)
    wall = jnp.concatenate([qkv, z, gate_a, gate_b, lat], axis=-1)
    return wall.astype(BF16), ba.astype(BF16)


def _pack_kv_up(w_kv_up):
    L = w_kv_up.shape[0]
    w = w_kv_up.reshape(L, KV_LORA, MLA_HEADS, QK_NOPE + V_HEAD)
    wk = w[..., :QK_NOPE].reshape(L, KV_LORA, MLA_HEADS * QK_NOPE)
    wv = w[..., QK_NOPE:].reshape(L, KV_LORA, MLA_HEADS * V_HEAD)
    return wk.astype(BF16), jnp.swapaxes(wv, 1, 2).astype(BF16)


def kernel(x, c, positions, ada_w, ada_b, norm_ffn1, ffn1_w1, ffn1_w3, ffn1_w2, norm_mix, w_in, gdn_conv, gdn_a_log, gdn_dt_bias, gdn_out_gain, mla_q_lat_gain, mla_kv_lat_gain, mla_w_q_up, mla_w_kv_up, mla_q_norm, mla_k_norm, w_branch_a, w_branch_b, w_out, norm_ffn2, ffn2_w1, ffn2_w3, ffn2_w2):
    L = ada_w.shape[0]
    S = x.shape[1]
    assert S % CHUNK == 0 and S % min(S, GDN_TOKENS) == 0 and S % min(S, FFN_TOKENS) == 0
    assert S % min(S, MERGE_TOKENS) == 0 and S % min(S, ATTN_Q) == 0

    mod = _ada_mod(c, ada_w, ada_b)
    cos2, sin2, cost, sint = _rope_tables(positions)

    row = lambda a: a.reshape(L, 1, a.shape[-1])
    col = lambda a: a.reshape(L, a.shape[-1], 1)
    bf = lambda a: a.astype(BF16)
    wall, wba = _pack_w_in(w_in)
    wqt = bf(jnp.swapaxes(mla_w_q_up, 1, 2))
    wk, wvt = _pack_kv_up(mla_w_kv_up)
    f1 = (bf(ffn1_w1), bf(ffn1_w3), bf(ffn1_w2))
    f2 = (bf(ffn2_w1), bf(ffn2_w3), bf(ffn2_w2))
    wa, wb, wo = bf(w_branch_a), bf(w_branch_b), bf(w_out)
    g_ffn1, g_mix, g_ffn2 = row(norm_ffn1), row(norm_mix), row(norm_ffn2)
    og, qlg, kvlg = row(gdn_out_gain), row(mla_q_lat_gain), row(mla_kv_lat_gain)
    qnc, kn = col(mla_q_norm), row(mla_k_norm)
    a_log, dt_bias = col(gdn_a_log), col(gdn_dt_bias)

    for l in range(L):
        x = _ffn(x, mod, g_ffn1, *f1, l, 0)
        qkv, zs, gates, lat, ba = _mixin(x, mod, g_mix, wall, wba, gdn_conv, l)
        grow, gcol = _gdn_gates(ba, a_log, dt_bias, l)
        o_gdn = _gdn(qkv, zs, grow, gcol, og, l)
        qt, k, vt = _mla_prep(lat, cos2, sin2, cost, sint, qlg, kvlg, wqt, wk, wvt, qnc, kn, l)
        o_mla = _attention(qt, k, vt)
        x = _merge(x, mod, o_gdn, o_mla, gates, wa, wb, wo, l)
        x = _ffn(x, mod, g_ffn2, *f2, l, 6)
    return x
```

```python
import functools
import math

import numpy as np
import jax
import jax.numpy as jnp
from jax import lax
from jax.experimental import pallas as pl
from jax.experimental.pallas import tpu as pltpu

F32 = jnp.float32
BF16 = jnp.bfloat16

EPS = 1e-6
N_MOD = 9
GDN_HEADS = 8
GDN_DK = 128
GDN_DV = 128
CONV_K = 4
CHUNK = 64
LOG2_CHUNK = 6
MLA_HEADS = 8
Q_LORA = 384
KV_LORA = 256
QK_NOPE = 128
QK_ROPE = 64
V_HEAD = 128
ROPE_THETA = 10000.0
MLA_QK = QK_NOPE + QK_ROPE

V7X_VMEM_BYTES = 64 * 1024 * 1024
V7X_LANES = 128
V7X_SUBLANES = 8
VMEM_CAP_BYTES = V7X_VMEM_BYTES - 6 * 1024 * 1024

FFN_TOKENS = 1024
MIX_TOKENS = 256
GATE_TOKENS = 512
GDN_TOKENS = 256
GDN_GROUP = 8
GDN_BATCH = 2
ATTN_Q = 512
ATTN_CHAIN = 8
MERGE_TOKENS = 1024
HALO = V7X_SUBLANES


def _params(vmem_bytes, n_axes):
    limit = int(min(VMEM_CAP_BYTES, max(vmem_bytes, 16 * 1024 * 1024)))
    return pltpu.CompilerParams(dimension_semantics=("arbitrary",) * n_axes, vmem_limit_bytes=limit)


def _resident(block_shape, index_map):
    return pl.BlockSpec(block_shape, index_map, pipeline_mode=pl.Buffered(1))


def _dot(a, b):
    return jnp.dot(a, b, preferred_element_type=F32)


def _dot_nt(a, b):
    return lax.dot_general(a, b, (((1,), (1,)), ((), ())), preferred_element_type=F32)


def _sigmoid(x):
    return 0.5 * jnp.tanh(0.5 * x) + 0.5


def _silu(x):
    hx = 0.5 * x
    return hx * jnp.tanh(hx) + hx


def _modulate(x, gain, shift, scale):
    y = x * lax.rsqrt(jnp.mean(x * x, axis=-1, keepdims=True) + EPS)
    return (y * gain) * (1.0 + scale) + shift


def _split3(x):
    x1 = x.astype(BF16)
    r1 = x - x1.astype(F32)
    x2 = r1.astype(BF16)
    x3 = (r1 - x2.astype(F32)).astype(BF16)
    return x1, x2, x3


def _mod_kernel(c_ref, w_ref, b_ref, o_ref):
    cond = _silu(c_ref[...]).astype(BF16)
    o_ref[...] = _dot(cond, w_ref[...].astype(BF16)) + b_ref[...]


def _ada_mod(c, ada_w, ada_b):
    L, D, _ = ada_w.shape
    B = c.shape[0]
    ada_b3 = ada_b.reshape(L, N_MOD, 1, D)
    out = pl.pallas_call(
        _mod_kernel,
        grid=(L, N_MOD),
        in_specs=[
            pl.BlockSpec((B, D), lambda l, j: (0, 0)),
            pl.BlockSpec((None, D, D), lambda l, j: (l, 0, j)),
            pl.BlockSpec((None, None, 1, D), lambda l, j: (l, j, 0, 0)),
        ],
        out_specs=pl.BlockSpec((None, None, B, D), lambda l, j: (l, j, 0, 0)),
        out_shape=jax.ShapeDtypeStruct((L, N_MOD, B, D), F32),
        compiler_params=_params(4 * D * D * 4, 2),
        name="ada_mod",
    )(c, ada_w, ada_b3)
    return jnp.transpose(out, (0, 2, 1, 3))


def _rope_kernel(pos_ref, post_ref, freq_ref, sign_ref, freqc_ref, cos_ref, sin_ref, cost_ref, sint_ref):
    ang = pos_ref[...].astype(F32) * freq_ref[...]
    cos_ref[...] = jnp.cos(ang)
    sin_ref[...] = jnp.sin(ang) * sign_ref[...]
    angt = freqc_ref[...] * post_ref[...].astype(F32)
    cost_ref[...] = jnp.cos(angt)
    sint_ref[...] = jnp.sin(angt)


def _rope_tables(positions):
    B, S = positions.shape
    half = QK_ROPE // 2
    inv_freq = ROPE_THETA ** (-jnp.arange(half, dtype=F32) / half)
    freq2 = jnp.concatenate([inv_freq, inv_freq]).reshape(1, QK_ROPE)
    sign2 = jnp.concatenate([-jnp.ones((half,), F32), jnp.ones((half,), F32)]).reshape(1, QK_ROPE)
    ts = min(S, 1024)
    return pl.pallas_call(
        _rope_kernel,
        grid=(B, S // ts),
        in_specs=[
            pl.BlockSpec((None, ts, 1), lambda b, s: (b, s, 0)),
            pl.BlockSpec((None, 1, ts), lambda b, s: (b, 0, s)),
            pl.BlockSpec((1, QK_ROPE), lambda b, s: (0, 0)),
            pl.BlockSpec((1, QK_ROPE), lambda b, s: (0, 0)),
            pl.BlockSpec((half, 1), lambda b, s: (0, 0)),
        ],
        out_specs=[pl.BlockSpec((None, ts, QK_ROPE), lambda b, s: (b, s, 0))] * 2
        + [pl.BlockSpec((None, half, ts), lambda b, s: (b, 0, s))] * 2,
        out_shape=[jax.ShapeDtypeStruct((B, S, QK_ROPE), F32)] * 2
        + [jax.ShapeDtypeStruct((B, half, S), F32)] * 2,
        compiler_params=_params(8 * ts * V7X_LANES * 4, 2),
        name="rope_tables",
    )(positions.reshape(B, S, 1), positions.reshape(B, 1, S), freq2, sign2, inv_freq.reshape(half, 1))


def _ffn_kernel(x_ref, mod_ref, gain_ref, w1_ref, w3_ref, w2_ref, o_ref, *, row):
    x = x_ref[...]
    mod = mod_ref[...]
    h = _modulate(x, gain_ref[...], mod[row:row + 1], mod[row + 1:row + 2]).astype(BF16)
    a = _dot(h, w1_ref[...])
    b = _dot(h, w3_ref[...])
    g = (_silu(a) * b).astype(BF16)
    y = _dot(g, w2_ref[...])
    o_ref[...] = x + (0.5 * (1.0 + mod[row + 2:row + 3])) * y


def _ffn(x, mod, gain, w1, w3, w2, l, row):
    B, S, D = x.shape
    F = w1.shape[-1]
    tm = min(S, FFN_TOKENS)
    vmem = 3 * D * F * 2 + 4 * tm * D * 4 + 3 * tm * F * 4 + 2 * tm * D * 4
    return pl.pallas_call(
        functools.partial(_ffn_kernel, row=row),
        grid=(B, S // tm),
        in_specs=[
            pl.BlockSpec((None, tm, D), lambda b, s: (b, s, 0)),
            pl.BlockSpec((None, None, N_MOD, D), lambda b, s: (l, b, 0, 0)),
            pl.BlockSpec((None, 1, D), lambda b, s: (l, 0, 0)),
            _resident((None, D, F), lambda b, s: (l, 0, 0)),
            _resident((None, D, F), lambda b, s: (l, 0, 0)),
            _resident((None, F, D), lambda b, s: (l, 0, 0)),
        ],
        out_specs=pl.BlockSpec((None, tm, D), lambda b, s: (b, s, 0)),
        out_shape=jax.ShapeDtypeStruct((B, S, D), F32),
        compiler_params=_params(vmem, 2),
        name="ffn",
    )(x, mod, gain, w1, w3, w2)


def _mixin_kernel(x_ref, mod_ref, gain_ref, wall_ref, wba_ref, conv_ref,
                  qkv_ref, zs_ref, gates_ref, lat_ref, ba_ref, pbuf_ref, *, tm):
    @pl.when(pl.program_id(1) == 0)
    def _():
        pbuf_ref[0:HALO, :] = jnp.zeros((HALO, pbuf_ref.shape[1]), F32)

    x = x_ref[...]
    mod = mod_ref[...]
    h = _modulate(x, gain_ref[...], mod[3:4], mod[4:5]).astype(BF16)

    C = pbuf_ref.shape[1]
    nz = zs_ref.shape[0] * zs_ref.shape[2]
    ng = gates_ref.shape[1]
    proj = _dot(h, wall_ref[...])
    pbuf_ref[HALO:HALO + tm, :] = proj[:, :C]
    conv = conv_ref[...]
    assert CONV_K == 4
    n_qk = 2 * GDN_HEADS
    for cb in range(pbuf_ref.shape[1] // V7X_LANES):
        c0 = cb * V7X_LANES
        xs = pbuf_ref[:, c0:c0 + V7X_LANES]
        w = [conv[j:j + 1, c0:c0 + V7X_LANES] for j in range(CONV_K)]
        x1 = pltpu.roll(xs, 1, axis=0)
        near = w[3] * xs + w[2] * x1
        far = w[1] * xs + w[0] * x1
        y = _silu(near[HALO:] + pltpu.roll(far, 2, axis=0)[HALO:])
        if cb < n_qk:
            y = y * lax.rsqrt(jnp.sum(y * y, axis=-1, keepdims=True) + EPS)
        qkv_ref[cb] = y.astype(BF16)
    pbuf_ref[0:HALO, :] = pbuf_ref[tm:tm + HALO, :]

    for hh in range(GDN_HEADS):
        zs_ref[hh] = proj[:, C + hh * GDN_DV:C + (hh + 1) * GDN_DV].astype(BF16)
    gates_ref[...] = proj[:, C + nz:C + nz + ng].astype(BF16)
    lat_ref[...] = proj[:, C + nz + ng:]
    ba_ref[...] = _dot_nt(wba_ref[...], h)


def _mixin(x, mod, gain, wall, wba, conv, l):
    B, S, D = x.shape
    C = conv.shape[-1]
    nz = GDN_HEADS * GDN_DV
    ng = 2 * D
    nl = wall.shape[-1] - C - nz - ng
    nb = wba.shape[1]
    tm = min(S, MIX_TOKENS)
    ncb = C // V7X_LANES
    vmem = ((C + nz + ng + nl) * D * 2 + 2 * tm * D * 4 + 2 * tm * (C + nz + ng) * 2 + 2 * tm * nl * 4
            + (tm + HALO) * C * 4 + tm * C * 4 + tm * (nz + ng) * 4 + 4 * 1024 * 1024)
    return pl.pallas_call(
        functools.partial(_mixin_kernel, tm=tm),
        grid=(B, S // tm),
        in_specs=[
            pl.BlockSpec((None, tm, D), lambda b, s: (b, s, 0)),
            pl.BlockSpec((None, None, N_MOD, D), lambda b, s: (l, b, 0, 0)),
            pl.BlockSpec((None, 1, D), lambda b, s: (l, 0, 0)),
            _resident((None, D, C + nz + ng + nl), lambda b, s: (l, 0, 0)),
            _resident((None, nb, D), lambda b, s: (l, 0, 0)),
            pl.BlockSpec((None, CONV_K, C), lambda b, s: (l, 0, 0)),
        ],
        out_specs=[
            pl.BlockSpec((None, ncb, tm, V7X_LANES), lambda b, s: (b, 0, s, 0)),
            pl.BlockSpec((None, GDN_HEADS, tm, GDN_DV), lambda b, s: (b, 0, s, 0)),
            pl.BlockSpec((None, tm, ng), lambda b, s: (b, s, 0)),
            pl.BlockSpec((None, tm, nl), lambda b, s: (b, s, 0)),
            pl.BlockSpec((None, nb, tm), lambda b, s: (b, 0, s)),
        ],
        out_shape=[
            jax.ShapeDtypeStruct((B, ncb, S, V7X_LANES), BF16),
            jax.ShapeDtypeStruct((B, GDN_HEADS, S, GDN_DV), BF16),
            jax.ShapeDtypeStruct((B, S, ng), BF16),
            jax.ShapeDtypeStruct((B, S, nl), F32),
            jax.ShapeDtypeStruct((B, nb, S), F32),
        ],
        scratch_shapes=[pltpu.VMEM((tm + HALO, C), F32)],
        compiler_params=_params(vmem, 2),
        name="mixin",
    )(x, mod, gain, wall, wba, conv)


def _gate_kernel(ba_ref, alog_ref, dtb_ref, grow_ref, gcol_ref, *, tm):
    H = GDN_HEADS
    ba = ba_ref[...]
    beta = _sigmoid(ba[0:H])
    xa = ba[H:2 * H] + dtb_ref[...]
    softplus = jnp.maximum(xa, 0.0) + jnp.log(1.0 + jnp.exp(-jnp.abs(xa)))
    g = -jnp.exp(alog_ref[...]) * softplus

    ii = lax.broadcasted_iota(jnp.int32, (tm, tm), 0)
    jj = lax.broadcasted_iota(jnp.int32, (tm, tm), 1)
    same = (ii >> LOG2_CHUNK) == (jj >> LOG2_CHUNK)
    upper =jnp.where(same & (ii <= jj), 1.0, 0.0).astype(BF16)
    block = jnp.where(same, 1.0, 0.0).astype(BF16)
    eye = jnp.where(ii == jj, 1.0, 0.0).astype(BF16)
    gam = None
    glast = None
    for part in _split3(g):
        t1 = _dot(part, upper)
        t2 = _dot(part, block)
        gam = t1 if gam is None else gam + t1
        glast = t2 if glast is None else glast + t2
    egam = jnp.exp(gam)
    ekd = jnp.exp(glast - gam)
    egl = jnp.exp(glast)
    zero = jnp.zeros_like(gam[0:1])
    rows = []
    for hh in range(H):
        r = jnp.concatenate([beta[hh:hh + 1], gam[hh:hh + 1], egam[hh:hh + 1], ekd[hh:hh + 1],
                             egl[hh:hh + 1], zero, zero, zero], axis=0)
        grow_ref[hh] = r
        rows.append(r)
    allrows = jnp.concatenate(rows, axis=0)
    cols = None
    for part in _split3(allrows):
        t = _dot_nt(eye, part)
        cols = t if cols is None else cols + t
    for hh in range(H):
        gcol_ref[hh] = cols[:, 8 * hh:8 * hh + 8]


def _gdn_gates(ba, a_log, dt_bias, l):
    B, nb, S = ba.shape
    H = GDN_HEADS
    tm = min(S, GATE_TOKENS)
    vmem = 8 * tm * tm * 4 + 4 * H * tm * V7X_LANES * 4
    return pl.pallas_call(
        functools.partial(_gate_kernel, tm=tm),
        grid=(B, S // tm),
        in_specs=[
            pl.BlockSpec((None, nb, tm), lambda b, s: (b, 0, s)),
            pl.BlockSpec((None, H, 1), lambda b, s: (l, 0, 0)),
            pl.BlockSpec((None, H, 1), lambda b, s: (l, 0, 0)),
        ],
        out_specs=[
            pl.BlockSpec((None, H, 8, tm), lambda b, s: (b, 0, 0, s)),
            pl.BlockSpec((None, H, tm, 8), lambda b, s: (b, 0, s, 0)),
        ],
        out_shape=[
            jax.ShapeDtypeStruct((B, H, 8, S), F32),
            jax.ShapeDtypeStruct((B, H, S, 8), F32),
        ],
        compiler_params=_params(vmem, 2),
        name="gdn_gates",
    )(ba, a_log, dt_bias)


def _gdn_kernel(q_ref, k_ref, v_ref, zs_ref, grow_ref, gcol_ref, og_ref, o_ref, state_ref, *, tc, G, NB):
    s = pl.program_id(2)

    @pl.when(s == 0)
    def _():
        state_ref[...] = jnp.zeros(state_ref.shape, F32)

    heads = range(NB * G)

    def at(ref, n):
        return ref[n // G, n % G]

    n_chunks = tc // CHUNK
    scale = GDN_DK ** -0.5
    ii = lax.broadcasted_iota(jnp.int32, (tc, tc), 0)
    jj = lax.broadcasted_iota(jnp.int32, (tc, tc), 1)
    same = (ii >> LOG2_CHUNK) == (jj >> LOG2_CHUNK)
    causal = same & (ii >= jj)
    strict = same & (ii > jj)
    lane = lax.broadcasted_iota(jnp.int32, (GDN_DK, tc), 1) >> LOG2_CHUNK

    gc = [at(gcol_ref, g) for g in heads]
    gr = [at(grow_ref, g) for g in heads]
    k = [at(k_ref, g) for g in heads]
    kf = [k[g].astype(F32) for g in heads]
    kb = [kf[g] * gc[g][:, 0:1] for g in heads]
    decay = [jnp.where(causal, jnp.exp(jnp.minimum(gc[g][:, 1:2] - gr[g][1:2, :], 0.0)), 0.0) for g in heads]

    akq = [_dot_nt(jnp.concatenate([kb[g].astype(BF16), at(q_ref, g)], axis=0), k[g]) for g in heads]
    m = [jnp.where(strict, akq[g][:tc] * decay[g], 0.0) for g in heads]
    qk = [(akq[g][tc:] * (decay[g] * scale)).astype(BF16) for g in heads]

    rlane = lax.broadcasted_iota(jnp.int32, (CHUNK, tc), 1)
    rrow = lax.broadcasted_iota(jnp.int32, (CHUNK, tc), 0)
    eye_row = jnp.where((rlane & (CHUNK - 1)) == rrow, 1.0, 0.0)
    rblk = rlane >> LOG2_CHUNK

    def blockdiag(x_row):
        return jnp.concatenate([jnp.where(rblk == c, x_row, 0.0) for c in range(n_chunks)], axis=0).astype(BF16)

    def side_by_side(x):
        acc = x[0:CHUNK]
        for c in range(1, n_chunks):
            acc = acc + x[c * CHUNK:(c + 1) * CHUNK]
        return acc

    m_row = [side_by_side(m[g]) for g in heads]
    p = [_dot(m_row[g].astype(BF16), blockdiag(m_row[g])) for g in heads]
    t_row = [eye_row - m_row[g] for g in heads]
    for _ in range(LOG2_CHUNK - 2):
        r = [_dot(jnp.concatenate([p[g], t_row[g]], axis=0).astype(BF16), blockdiag(p[g])) for g in heads]
        p = [r[g][:CHUNK] for g in heads]
        t_row = [t_row[g] + r[g][CHUNK:] for g in heads]
    t_row = [t_row[g] + _dot(t_row[g].astype(BF16), blockdiag(p[g])) for g in heads]

    rhs = [jnp.concatenate([at(v_ref, g).astype(F32) * gc[g][:, 0:1], kb[g] * gc[g][:, 2:3]],
                           axis=1).astype(BF16) for g in heads]
    uw = [_dot(blockdiag(t_row[g]), rhs[g]).astype(BF16) for g in heads]

    kdt = [(kf[g] * gc[g][:, 3:4]).T for g in heads]
    pre_lhs = [jnp.concatenate([jnp.where(lane == c, kdt[g], 0.0).astype(BF16) for c in range(n_chunks)]
                               + [qk[g]], axis=0) for g in heads]
    pre = [_dot(pre_lhs[g], uw[g]) for g in heads]
    qd = [at(q_ref, g).astype(F32) * (gc[g][:, 2:3] * scale) for g in heads]
    base = n_chunks * GDN_DK
    qmod = [(qd[g] - pre[g][base:, GDN_DV:]).astype(BF16) for g in heads]

    states = [at(state_ref, g) for g in heads]
    outs = [[] for _ in heads]
    for c in range(n_chunks):
        r0 = c * CHUNK
        for g in heads:
            kblk = pre[g][c * GDN_DK:(c + 1) * GDN_DK]
            lhs = jnp.concatenate([kblk[:, GDN_DV:].astype(BF16), qmod[g][r0:r0 + CHUNK]], axis=0)
            r = _dot(lhs, states[g].astype(BF16))
            outs[g].append(r[GDN_DK:] + pre[g][base + r0:base + r0 + CHUNK, :GDN_DV])
            states[g] = states[g] * gr[g][4:5, r0:r0 + 1] - r[:GDN_DK] + kblk[:, :GDN_DV]
    for g in heads:
        bb, hh = g // G, g % G
        state_ref[bb, hh] = states[g]
        o = jnp.concatenate(outs[g], axis=0)
        o = o * lax.rsqrt(jnp.mean(o * o, axis=-1, keepdims=True) + EPS) * og_ref[...]
        o_ref[bb, :, hh * GDN_DV:(hh + 1) * GDN_DV] = (o * _silu(zs_ref[bb, hh].astype(F32))).astype(BF16)


def _gdn(qkv, zs, grow, gcol, out_gain, l):
    B, _, S, _ = qkv.shape
    H = GDN_HEADS
    G = GDN_GROUP
    NB = math.gcd(B, GDN_BATCH)
    tc = min(S, GDN_TOKENS)
    vmem = NB * G * (24 * tc * tc * 4 + 24 * tc * V7X_LANES * 4)
    return pl.pallas_call(
        functools.partial(_gdn_kernel, tc=tc, G=G, NB=NB),
        grid=(B // NB, H // G, S // tc),
        in_specs=[
            pl.BlockSpec((NB, G, tc, GDN_DK), lambda b, h, s: (b, h, s, 0)),
            pl.BlockSpec((NB, G, tc, GDN_DK), lambda b, h, s: (b, H // G + h, s, 0)),
            pl.BlockSpec((NB, G, tc, GDN_DV), lambda b, h, s: (b, 2 * H // G + h, s, 0)),
            pl.BlockSpec((NB, G, tc, GDN_DV), lambda b, h, s: (b, h, s, 0)),
            pl.BlockSpec((NB, G, 8, tc), lambda b, h, s: (b, h, 0, s)),
            pl.BlockSpec((NB, G, tc, 8), lambda b, h, s: (b, h, s, 0)),
            pl.BlockSpec((None, 1, GDN_DV), lambda b, h, s: (l, 0, 0)),
        ],
        out_specs=pl.BlockSpec((NB, tc, G * GDN_DV), lambda b, h, s: (b, s, h)),
        out_shape=jax.ShapeDtypeStruct((B, S, H * GDN_DV), BF16),
        scratch_shapes=[pltpu.VMEM((NB, G, GDN_DK, GDN_DV), F32)],
        compiler_params=_params(vmem, 3),
        name="gdn",
    )(qkv, qkv, qkv, zs, grow, gcol, out_gain)


def _swap_halves(y):
    half = y.shape[-1] // 2
    return jnp.concatenate([y[:, half:], y[:, :half]], axis=1)


def _mla_prep_kernel(lat_ref, cos_ref, sin_ref, cost_ref, sint_ref, qg_ref, kvg_ref, wqt_ref, wk_ref, wvt_ref,
                     qnc_ref, kn_ref, qt_ref, k_ref, vt_ref):
    H = MLA_HEADS
    half = QK_ROPE // 2
    lat = lat_ref[...]
    ql = lat[:, :Q_LORA]
    kvl = lat[:, Q_LORA:Q_LORA + KV_LORA]
    kpe = lat[:, Q_LORA + KV_LORA:Q_LORA + KV_LORA + QK_ROPE]
    qln = (ql * lax.rsqrt(jnp.mean(ql * ql, axis=-1, keepdims=True) + EPS) * qg_ref[...]).astype(BF16)
    kvn = (kvl * lax.rsqrt(jnp.mean(kvl * kvl, axis=-1, keepdims=True) + EPS) * kvg_ref[...]).astype(BF16)
    qt = _dot_nt(wqt_ref[...], qln)
    knope = _dot(kvn, wk_ref[...])
    vt = _dot_nt(wvt_ref[...], kvn)

    cost = cost_ref[...]
    sint = sint_ref[...]
    qnc = qnc_ref[...]
    scale = MLA_QK ** -0.5 * math.log2(math.e)
    for hh in range(H):
        blk = qt[hh * MLA_QK:(hh + 1) * MLA_QK]
        r = lax.rsqrt(jnp.sum(blk * blk, axis=0, keepdims=True) * (1.0 / MLA_QK) + EPS) * scale
        y = blk * qnc * r
        y1 = y[QK_NOPE:QK_NOPE + half]
        y2 = y[QK_NOPE + half:]
        qt_ref[hh, 0:QK_NOPE, :] = y[:QK_NOPE].astype(BF16)
        qt_ref[hh, QK_NOPE:QK_NOPE + half, :] = (y1 * cost - y2 * sint).astype(BF16)
        qt_ref[hh, QK_NOPE + half:MLA_QK, :] = (y2 * cost + y1 * sint).astype(BF16)
        vt_ref[hh] = vt[hh * V_HEAD:(hh + 1) * V_HEAD].astype(BF16)

    kn = kn_ref[...]
    yk = kpe * kn[:, QK_NOPE:]
    kr = yk * cos_ref[...] + _swap_halves(yk) * sin_ref[...]
    kpe_ss = jnp.sum(kpe * kpe, axis=-1, keepdims=True)
    for hh in range(H):
        k_nope = knope[:, hh * QK_NOPE:(hh + 1) * QK_NOPE]
        ss = jnp.sum(k_nope * k_nope, axis=-1, keepdims=True) + kpe_ss
        r = lax.rsqrt(ss * (1.0 / MLA_QK) + EPS)
        k_ref[hh, :, 0:QK_NOPE] = (k_nope * r * kn[:, :QK_NOPE]).astype(BF16)
        k_ref[hh, :, QK_NOPE:MLA_QK] = (kr * r).astype(BF16)


def _mla_prep(lat, cos2, sin2, cost, sint, q_lat_gain, kv_lat_gain, wqt, wk, wvt, q_norm_col, k_norm, l):
    B, S, nl = lat.shape
    H = MLA_HEADS
    half = QK_ROPE // 2
    tm = min(S, ATTN_Q)
    nq = wqt.shape[1]
    nk = wk.shape[-1]
    nv = wvt.shape[1]
    vmem = ((nq * Q_LORA + (nk + nv) * KV_LORA) * 2 + 2 * tm * nl * 4 + 3 * tm * (nq + nk + nv) * 4
            + 4 * H * tm * (256 + MLA_QK + V_HEAD) * 2)
    return pl.pallas_call(
        _mla_prep_kernel,
        grid=(B, S // tm),
        in_specs=[
            pl.BlockSpec((None, tm, nl), lambda b, s: (b, s, 0)),
            pl.BlockSpec((None, tm, QK_ROPE), lambda b, s: (b, s, 0)),
            pl.BlockSpec((None, tm, QK_ROPE), lambda b, s: (b, s, 0)),
            pl.BlockSpec((None, half, tm), lambda b, s: (b, 0, s)),
            pl.BlockSpec((None, half, tm), lambda b, s: (b, 0, s)),
            pl.BlockSpec((None, 1, Q_LORA), lambda b, s: (l, 0, 0)),
            pl.BlockSpec((None, 1, KV_LORA), lambda b, s: (l, 0, 0)),
            _resident((None, nq, Q_LORA), lambda b, s: (l, 0, 0)),
            _resident((None, KV_LORA, nk), lambda b, s: (l, 0, 0)),
            _resident((None, nv, KV_LORA), lambda b, s: (l, 0, 0)),
            pl.BlockSpec((None, MLA_QK, 1), lambda b, s: (l, 0, 0)),
            pl.BlockSpec((None, 1, MLA_QK), lambda b, s: (l, 0, 0)),
        ],
        out_specs=[
            pl.BlockSpec((None, H, MLA_QK, tm), lambda b, s: (b, 0, 0, s)),
            pl.BlockSpec((None, H, tm, MLA_QK), lambda b, s: (b, 0, s, 0)),
            pl.BlockSpec((None, H, None, V_HEAD, tm), lambda b, s: (b, 0, s, 0, 0)),
        ],
        out_shape=[
            jax.ShapeDtypeStruct((B, H, MLA_QK, S), BF16),
            jax.ShapeDtypeStruct((B, H, S, MLA_QK), BF16),
            jax.ShapeDtypeStruct((B, H, S // tm, V_HEAD, tm), BF16),
        ],
        compiler_params=_params(vmem, 2),
        name="mla_prep",
    )(lat, cos2, sin2, cost, sint, q_lat_gain, kv_lat_gain, wqt, wk, wvt, q_norm_col, k_norm)


def _attn_kernel(qt_ref, k_ref, vt_ref, o_ref, sa_ref, sb_ref, mxa_ref, mxb_ref, m_ref, l_ref, acc_ref, *,
                 tq, chain, single):
    c = 0 if single else pl.program_id(2)
    bufs = ((sa_ref, mxa_ref), (sb_ref, mxb_ref))

    def scores(qt, j, buf):
        start = j * tq if isinstance(j, int) else pl.multiple_of(j * tq, tq)
        st = _dot(k_ref[pl.ds(start, tq), :], qt)
        buf[0][...] = st
        buf[1][...] = jnp.max(st, axis=0, keepdims=True)

    def reset(w):
        m_ref[w] = jnp.full(m_ref.shape[1:], -jnp.inf, F32)
        l_ref[w] = jnp.zeros(l_ref.shape[1:], F32)
        acc_ref[w] = jnp.zeros(acc_ref.shape[1:], F32)

    def update(w, j, buf, masked=False):
        st = buf[0][...]
        mx = buf[1][...]
        if masked:
            kpos = lax.broadcasted_iota(jnp.int32, (tq, tq), 0)
            qpos = lax.broadcasted_iota(jnp.int32, (tq, tq), 1)
            st = jnp.where(kpos <= qpos, st, -jnp.inf)
            mx = jnp.max(st, axis=0, keepdims=True)
        m = m_ref[w]
        m_new = jnp.maximum(m, mx)
        alpha = jnp.exp2(m - m_new)
        pr = jnp.exp2(st - m_new)
        m_ref[w] = m_new
        l_ref[w] = alpha * l_ref[w] + jnp.sum(pr, axis=0, keepdims=True)
        acc_ref[w] = alpha * acc_ref[w] + _dot(vt_ref[j], pr.astype(BF16))

    def full_blocks(w, qt, cur, nxt, trips):
        def body(t, _):
            scores(qt, 2 * t + 1, nxt)
            update(w, 2 * t, cur)
            scores(qt, 2 * t + 2, cur)
            update(w, 2 * t + 1, nxt)
            return 0
        if isinstance(trips, int):
            for t in range(trips):
                body(t, 0)
        else:
            lax.fori_loop(0, trips, body, 0)

    def q_block(u):
        return qt_ref[:, u * tq:(u + 1) * tq]

    cur, nxt = bufs
    reset(0)
    scores(q_block(0), 0, cur)
    for u in range(chain):
        w = u % 2
        i = c * chain + u
        full_blocks(w, q_block(u), cur, nxt, c * (chain // 2) + u // 2)
        if u % 2 == 1:
            scores(q_block(u), i, nxt)
            update(w, i - 1, cur)
            cur, nxt = nxt, cur
        if u + 1 < chain:
            reset(1 - w)
            scores(q_block(u + 1), 0, nxt)
        update(w, i, cur, masked=True)
        o_ref[u * tq:(u + 1) * tq, :] = (acc_ref[w] / l_ref[w]).T.astype(BF16)
        cur, nxt = nxt, cur


def _attention(qt, k, vt):
    B, H, S, _ = k.shape
    tq = vt.shape[-1]
    nq = S // tq
    chain = min(nq, ATTN_CHAIN)
    assert chain % 2 == 0 and nq % chain == 0, "q blocks are taken in even-length runs"
    vmem = (8 * tq * tq * 4 + 4 * S * (256 + V_HEAD) * 2 + 4 * chain * tq * (256 + V_HEAD) * 2
            + 16 * tq * V7X_LANES * 4)
    return pl.pallas_call(
        functools.partial(_attn_kernel, tq=tq, chain=chain, single=(nq == chain)),
        grid=(B, H, nq // chain),
        in_specs=[
            pl.BlockSpec((None, None, MLA_QK, chain * tq), lambda b, h, i: (b, h, 0, i)),
            pl.BlockSpec((None, None, S, MLA_QK), lambda b, h, i: (b, h, 0, 0)),
            pl.BlockSpec((None, None, S // tq, V_HEAD, tq), lambda b, h, i: (b, h, 0, 0, 0)),
        ],
        out_specs=pl.BlockSpec((None, chain * tq, V_HEAD), lambda b, h, i: (b, i, h)),
        out_shape=jax.ShapeDtypeStruct((B, S, H * V_HEAD), BF16),
        scratch_shapes=[pltpu.VMEM((tq, tq), F32), pltpu.VMEM((tq, tq), F32),
                        pltpu.VMEM((1, tq), F32), pltpu.VMEM((1, tq), F32),
                        pltpu.VMEM((2, 1, tq), F32), pltpu.VMEM((2, 1, tq), F32),
                        pltpu.VMEM((2, V_HEAD, tq), F32)],
        compiler_params=_params(vmem, 3),
        name="mla_attention",
    )(qt, k, vt)


def _merge_kernel(x_ref, mod_ref, oa_ref, ob_ref, gates_ref, wa_ref, wb_ref, wo_ref, o_ref):
    D = x_ref.shape[-1]
    mod = mod_ref[...]
    gates = _sigmoid(gates_ref[...].astype(F32))
    ya =_dot(oa_ref[...], wa_ref[...])
    yb = _dot(ob_ref[...], wb_ref[...])
    y = (gates[:, :D] * ya + gates[:, D:] * yb).astype(BF16)
    o_ref[...] = x_ref[...] + (1.0 + mod[5:6]) * _dot(y, wo_ref[...])


def _merge(x, mod, oa, ob, gates, wa, wb, wo, l):
    B, S, D = x.shape
    tm = min(S, MERGE_TOKENS)
    na = oa.shape[-1]
    nbv = ob.shape[-1]
    vmem = (na + nbv + D) * D * 2 + 4 * tm * D * 4 + 2 * tm * (na + nbv + 2 * D) * 2 + 4 * tm * D * 4
    return pl.pallas_call(
        _merge_kernel,
        grid=(B, S // tm),
        in_specs=[
            pl.BlockSpec((None, tm, D), lambda b, s: (b, s, 0)),
            pl.BlockSpec((None, None, N_MOD, D), lambda b, s: (l, b, 0, 0)),
            pl.BlockSpec((None, tm, na), lambda b, s: (b, s, 0)),
            pl.BlockSpec((None, tm, nbv), lambda b, s: (b, s, 0)),
            pl.BlockSpec((None, tm, 2 * D), lambda b, s: (b, s, 0)),
            _resident((None, na, D), lambda b, s: (l, 0, 0)),
            _resident((None, nbv, D), lambda b, s: (l, 0, 0)),
            _resident((None, D, D), lambda b, s: (l, 0, 0)),
        ],
        out_specs=pl.BlockSpec((None, tm, D), lambda b, s: (b, s, 0)),
        out_shape=jax.ShapeDtypeStruct((B, S, D), F32),
        compiler_params=_params(vmem, 2),
        name="merge",
    )(x, mod, oa, ob, gates, wa, wb, wo)


def _pack_w_in(w_in):
    conv_c = 2 * GDN_HEADS * GDN_DK + GDN_HEADS * GDN_DV
    gdn_v = GDN_HEADS * GDN_DV
    D = w_in.shape[1]
    sizes = (conv_c, gdn_v, GDN_HEADS, GDN_HEADS, Q_LORA, KV_LORA, QK_ROPE, D, D)
    offs = np.cumsum((0,) + sizes)
    qkv, z, b_raw, a_raw, q_lat, kv_lat, k_pe, gate_a, gate_b = (
        w_in[:, :, int(offs[i]):int(offs[i + 1])] for i in range(len(sizes)))
    n_lat = Q_LORA + KV_LORA + QK_ROPE
    pad = (-n_lat) % V7X_LANES
    lat = jnp.concatenate([q_lat, kv_lat, k_pe, jnp.zeros(w_in.shape[:2] + (pad,), w_in.dtype)], axis=-1)
    ba = jnp.swapaxes(jnp.concatenate([b_raw, a_raw], axis=-1), 1, 2)
    wall = jnp.concatenate([qkv, z, gate_a, gate_b, lat], axis=-1)
    return wall.astype(BF16), ba.astype(BF16)


def _pack_kv_up(w_kv_up):
    L = w_kv_up.shape[0]
    w = w_kv_up.reshape(L, KV_LORA, MLA_HEADS, QK_NOPE + V_HEAD)
    wk = w[..., :QK_NOPE].reshape(L, KV_LORA, MLA_HEADS * QK_NOPE)
    wv = w[..., QK_NOPE:].reshape(L, KV_LORA, MLA_HEADS * V_HEAD)
    return wk.astype(BF16), jnp.swapaxes(wv, 1, 2).astype(BF16)


def kernel(x, c, positions, ada_w, ada_b, norm_ffn1, ffn1_w1, ffn1_w3, ffn1_w2, norm_mix, w_in, gdn_conv, gdn_a_log, gdn_dt_bias, gdn_out_gain, mla_q_lat_gain, mla_kv_lat_gain, mla_w_q_up, mla_w_kv_up, mla_q_norm, mla_k_norm, w_branch_a, w_branch_b, w_out, norm_ffn2, ffn2_w1, ffn2_w3, ffn2_w2):
    L = ada_w.shape[0]
    S = x.shape[1]
    assert S % CHUNK == 0 and S % min(S, GDN_TOKENS) == 0 and S % min(S, FFN_TOKENS) == 0
    assert S % min(S, MERGE_TOKENS) == 0 and S % min(S, ATTN_Q) == 0

    mod = _ada_mod(c, ada_w, ada_b)
    cos2, sin2, cost, sint = _rope_tables(positions)

    row = lambda a: a.reshape(L, 1, a.shape[-1])
    col = lambda a: a.reshape(L, a.shape[-1], 1)
    bf = lambda a: a.astype(BF16)
    wall, wba = _pack_w_in(w_in)
    wqt = bf(jnp.swapaxes(mla_w_q_up, 1, 2))
    wk, wvt = _pack_kv_up(mla_w_kv_up)
    f1 = (bf(ffn1_w1), bf(ffn1_w3), bf(ffn1_w2))
    f2 = (bf(ffn2_w1), bf(ffn2_w3), bf(ffn2_w2))
    wa, wb, wo = bf(w_branch_a), bf(w_branch_b), bf(w_out)
    g_ffn1, g_mix, g_ffn2 = row(norm_ffn1), row(norm_mix), row(norm_ffn2)
    og, qlg, kvlg = row(gdn_out_gain), row(mla_q_lat_gain), row(mla_kv_lat_gain)
    qnc, kn = col(mla_q_norm), row(mla_k_norm)
    a_log, dt_bias = col(gdn_a_log), col(gdn_dt_bias)

    for l in range(L):
        x = _ffn(x, mod, g_ffn1, *f1, l, 0)
        qkv, zs, gates, lat, ba = _mixin(x, mod, g_mix, wall, wba, gdn_conv, l)
        grow, gcol = _gdn_gates(ba, a_log, dt_bias, l)
        o_gdn = _gdn(qkv, zs, grow, gcol, og, l)
        qt, k, vt = _mla_prep(lat, cos2, sin2, cost, sint, qlg, kvlg, wqt, wk, wvt, qnc, kn, l)
        o_mla = _attention(qt, k, vt)
        x = _merge(x, mod, o_gdn, o_mla, gates, wa, wb, wo, l)
        x = _ffn(x, mod, g_ffn2, *f2, l, 6)
    return x
```

```python
import functools
import math

import numpy as np
import jax
import jax.numpy as jnp
from jax import lax
from jax.experimental import pallas as pl
from jax.experimental.pallas import tpu as pltpu

F32 = jnp.float32
BF16 = jnp.bfloat16

EPS = 1e-6
N_MOD = 9
GDN_HEADS = 8
GDN_DK = 128
GDN_DV = 128
CONV_K = 4
CHUNK = 64
LOG2_CHUNK = 6
MLA_HEADS = 8
Q_LORA = 384
KV_LORA = 256
QK_NOPE = 128
QK_ROPE = 64
V_HEAD = 128
ROPE_THETA = 10000.0
MLA_QK = QK_NOPE + QK_ROPE

V7X_VMEM_BYTES = 64 * 1024 * 1024
V7X_LANES = 128
V7X_SUBLANES = 8
VMEM_CAP_BYTES = V7X_VMEM_BYTES - 6 * 1024 * 1024

FFN_TOKENS = 1024
MIX_TOKENS = 256
GATE_TOKENS = 512
GATE_BATCH = 4
GDN_TOKENS = 256
GDN_GROUP = 8
GDN_BATCH = 2
ATTN_Q = 512
ATTN_CHAIN = 8
MERGE_TOKENS = 1024
HALO = V7X_SUBLANES


def _params(vmem_bytes, n_axes):
    limit = int(min(VMEM_CAP_BYTES, max(vmem_bytes, 16 * 1024 * 1024)))
    return pltpu.CompilerParams(dimension_semantics=("arbitrary",) * n_axes, vmem_limit_bytes=limit)


def _resident(block_shape, index_map):
    return pl.BlockSpec(block_shape, index_map, pipeline_mode=pl.Buffered(1))


def _dot(a, b):
    return jnp.dot(a, b, preferred_element_type=F32)


def _dot_nt(a, b):
    return lax.dot_general(a, b, (((1,), (1,)), ((), ())), preferred_element_type=F32)


def _sigmoid(x):
    return 0.5 * jnp.tanh(0.5 * x) + 0.5


def _silu(x):
    hx = 0.5 * x
    return hx * jnp.tanh(hx) + hx


def _modulate(x, gain, shift, scale):
    y = x * lax.rsqrt(jnp.mean(x * x, axis=-1, keepdims=True) + EPS)
    return (y * gain) * (1.0 + scale) + shift


def _split3(x):
    x1 = x.astype(BF16)
    r1 = x - x1.astype(F32)
    x2 = r1.astype(BF16)
    x3 = (r1 - x2.astype(F32)).astype(BF16)
    return x1, x2, x3


def _mod_kernel(c_ref, w_ref, b_ref, o_ref):
    cond = _silu(c_ref[...]).astype(BF16)
    o_ref[...] = _dot(cond, w_ref[...].astype(BF16)) + b_ref[...]


def _ada_mod(c, ada_w, ada_b):
    L, D, _ = ada_w.shape
    B = c.shape[0]
    ada_b3 = ada_b.reshape(L, N_MOD, 1, D)
    out = pl.pallas_call(
        _mod_kernel,
        grid=(L, N_MOD),
        in_specs=[
            pl.BlockSpec((B, D), lambda l, j: (0, 0)),
            pl.BlockSpec((None, D, D), lambda l, j: (l, 0, j)),
            pl.BlockSpec((None, None, 1, D), lambda l, j: (l, j, 0, 0)),
        ],
        out_specs=pl.BlockSpec((None, None, B, D), lambda l, j: (l, j, 0, 0)),
        out_shape=jax.ShapeDtypeStruct((L, N_MOD, B, D), F32),
        compiler_params=_params(4 * D * D * 4, 2),
        name="ada_mod",
    )(c, ada_w, ada_b3)
    return jnp.transpose(out, (0, 2, 1, 3))


def _rope_kernel(pos_ref, post_ref, freq_ref, sign_ref, freqc_ref, cos_ref, sin_ref, cost_ref, sint_ref):
    ang = pos_ref[...].astype(F32) * freq_ref[...]
    cos_ref[...] = jnp.cos(ang)
    sin_ref[...] = jnp.sin(ang) * sign_ref[...]
    angt = freqc_ref[...] * post_ref[...].astype(F32)
    cost_ref[...] = jnp.cos(angt)
    sint_ref[...] = jnp.sin(angt)


def _rope_tables(positions):
    B, S = positions.shape
    half = QK_ROPE // 2
    inv_freq = ROPE_THETA ** (-jnp.arange(half, dtype=F32) / half)
    freq2 = jnp.concatenate([inv_freq, inv_freq]).reshape(1, QK_ROPE)
    sign2 = jnp.concatenate([-jnp.ones((half,), F32), jnp.ones((half,), F32)]).reshape(1, QK_ROPE)
    ts = min(S, 1024)
    return pl.pallas_call(
        _rope_kernel,
        grid=(B, S // ts),
        in_specs=[
            pl.BlockSpec((None, ts, 1), lambda b, s: (b, s, 0)),
            pl.BlockSpec((None, 1, ts), lambda b, s: (b, 0, s)),
            pl.BlockSpec((1, QK_ROPE), lambda b, s: (0, 0)),
            pl.BlockSpec((1, QK_ROPE), lambda b, s: (0, 0)),
            pl.BlockSpec((half, 1), lambda b, s: (0, 0)),
        ],
        out_specs=[pl.BlockSpec((None, ts, QK_ROPE), lambda b, s: (b, s, 0))] * 2
        + [pl.BlockSpec((None, half, ts), lambda b, s: (b, 0, s))] * 2,
        out_shape=[jax.ShapeDtypeStruct((B, S, QK_ROPE), F32)] * 2
        + [jax.ShapeDtypeStruct((B, half, S), F32)] * 2,
        compiler_params=_params(8 * ts * V7X_LANES * 4, 2),
        name="rope_tables",
    )(positions.reshape(B, S, 1), positions.reshape(B, 1, S), freq2, sign2, inv_freq.reshape(half, 1))


def _ffn_kernel(x_ref, mod_ref, gain_ref, w1_ref, w3_ref, w2_ref, o_ref, *, row):
    x = x_ref[...]
    mod = mod_ref[...]
    h = _modulate(x, gain_ref[...], mod[row:row + 1], mod[row + 1:row + 2]).astype(BF16)
    a = _dot(h, w1_ref[...])
    b = _dot(h, w3_ref[...])
    g = (_silu(a) * b).astype(BF16)
    y = _dot(g, w2_ref[...])
    o_ref[...] = x + (0.5 * (1.0 + mod[row + 2:row + 3])) * y


def _ffn(x, mod, gain, w1, w3, w2, l, row):
    B, S, D = x.shape
    F = w1.shape[-1]
    tm = min(S, FFN_TOKENS)
    vmem = 3 * D * F * 2 + 4 * tm * D * 4 + 3 * tm * F * 4 + 2 * tm * D * 4
    return pl.pallas_call(
        functools.partial(_ffn_kernel, row=row),
        grid=(B, S // tm),
        in_specs=[
            pl.BlockSpec((None, tm, D), lambda b, s: (b, s, 0)),
            pl.BlockSpec((None, None, N_MOD, D), lambda b, s: (l, b, 0, 0)),
            pl.BlockSpec((None, 1, D), lambda b, s: (l, 0, 0)),
            _resident((None, D, F), lambda b, s: (l, 0, 0)),
            _resident((None, D, F), lambda b, s: (l, 0, 0)),
            _resident((None, F, D), lambda b, s: (l, 0, 0)),
        ],
        out_specs=pl.BlockSpec((None, tm, D), lambda b, s: (b, s, 0)),
        out_shape=jax.ShapeDtypeStruct((B, S, D), F32),
        compiler_params=_params(vmem, 2),
        name="ffn",
    )(x, mod, gain, w1, w3, w2)


def _mixin_kernel(x_ref, mod_ref, gain_ref, wall_ref, wba_ref, conv_ref,
                  qkv_ref, zs_ref, gates_ref, lat_ref, ba_ref, pbuf_ref, *, tm):
    @pl.when(pl.program_id(1) == 0)
    def _():
        pbuf_ref[0:HALO, :] = jnp.zeros((HALO, pbuf_ref.shape[1]), F32)

    x = x_ref[...]
    mod = mod_ref[...]
    h = _modulate(x, gain_ref[...], mod[3:4], mod[4:5]).astype(BF16)

    C = pbuf_ref.shape[1]
    nz = zs_ref.shape[0] * zs_ref.shape[2]
    ng = gates_ref.shape[1]
    proj = _dot(h, wall_ref[...])
    pbuf_ref[HALO:HALO + tm, :] = proj[:, :C]
    conv = conv_ref[...]
    assert CONV_K == 4
    n_qk = 2 * GDN_HEADS
    for cb in range(pbuf_ref.shape[1] // V7X_LANES):
        c0 = cb * V7X_LANES
        xs = pbuf_ref[:, c0:c0 + V7X_LANES]
        w = [conv[j:j + 1, c0:c0 + V7X_LANES] for j in range(CONV_K)]
        x1 = pltpu.roll(xs, 1, axis=0)
        near = w[3] * xs + w[2] * x1
        far = w[1] * xs + w[0] * x1
        y = _silu(near[HALO:] + pltpu.roll(far, 2, axis=0)[HALO:])
        if cb < n_qk:
            y = y * lax.rsqrt(jnp.sum(y * y, axis=-1, keepdims=True) + EPS)
        qkv_ref[cb] = y.astype(BF16)
    pbuf_ref[0:HALO, :] = pbuf_ref[tm:tm + HALO, :]

    for hh in range(GDN_HEADS):
        zs_ref[hh] = proj[:, C + hh * GDN_DV:C + (hh + 1) * GDN_DV].astype(BF16)
    gates_ref[...] = proj[:, C + nz:C + nz + ng].astype(BF16)
    lat_ref[...] = proj[:, C + nz + ng:]
    ba_ref[...] = _dot_nt(wba_ref[...], h)


def _mixin(x, mod, gain, wall, wba, conv, l):
    B, S, D = x.shape
    C = conv.shape[-1]
    nz = GDN_HEADS * GDN_DV
    ng = 2 * D
    nl = wall.shape[-1] - C - nz - ng
    nb = wba.shape[1]
    tm = min(S, MIX_TOKENS)
    ncb = C // V7X_LANES
    vmem = ((C + nz + ng + nl) * D * 2 + 2 * tm * D * 4 + 2 * tm * (C + nz + ng) * 2 + 2 * tm * nl * 4
            + (tm + HALO) * C * 4 + tm * C * 4 + tm * (nz + ng) * 4 + 4 * 1024 * 1024)
    return pl.pallas_call(
        functools.partial(_mixin_kernel, tm=tm),
        grid=(B, S // tm),
        in_specs=[
            pl.BlockSpec((None, tm, D), lambda b, s: (b, s, 0)),
            pl.BlockSpec((None, None, N_MOD, D), lambda b, s: (l, b, 0, 0)),
            pl.BlockSpec((None, 1, D), lambda b, s: (l, 0, 0)),
            _resident((None, D, C + nz + ng + nl), lambda b, s: (l, 0, 0)),
            _resident((None, nb, D), lambda b, s: (l, 0, 0)),
            pl.BlockSpec((None, CONV_K, C), lambda b, s: (l, 0, 0)),
        ],
        out_specs=[
            pl.BlockSpec((None, ncb, tm, V7X_LANES), lambda b, s: (b, 0, s, 0)),
            pl.BlockSpec((None, GDN_HEADS, tm, GDN_DV), lambda b, s: (b, 0, s, 0)),
            pl.BlockSpec((None, tm, ng), lambda b, s: (b, s, 0)),
            pl.BlockSpec((None, tm, nl), lambda b, s: (b, s, 0)),
            pl.BlockSpec((None, nb, tm), lambda b, s: (b, 0, s)),
        ],
        out_shape=[
            jax.ShapeDtypeStruct((B, ncb, S, V7X_LANES), BF16),
            jax.ShapeDtypeStruct((B, GDN_HEADS, S, GDN_DV), BF16),
            jax.ShapeDtypeStruct((B, S, ng), BF16),
            jax.ShapeDtypeStruct((B, S, nl), F32),
            jax.ShapeDtypeStruct((B, nb, S), F32),
        ],
        scratch_shapes=[pltpu.VMEM((tm + HALO, C), F32)],
        compiler_params=_params(vmem, 2),
        name="mixin",
    )(x, mod, gain, wall, wba, conv)


def _gate_kernel(ba_ref, alog_ref, dtb_ref, grow_ref, gcol_ref, *, tm, NB):
    H = GDN_HEADS
    betas, gs = [], []
    for bb in range(NB):
        ba = ba_ref[bb]
        betas.append(_sigmoid(ba[0:H]))
        xa = ba[H:2 * H] + dtb_ref[...]
        softplus = jnp.maximum(xa, 0.0) + jnp.log(1.0 + jnp.exp(-jnp.abs(xa)))
        gs.append(-jnp.exp(alog_ref[...]) * softplus)
    beta = jnp.concatenate(betas, axis=0)
    g = jnp.concatenate(gs, axis=0)

    ii = lax.broadcasted_iota(jnp.int32, (tm, tm), 0)
    jj = lax.broadcasted_iota(jnp.int32, (tm, tm), 1)
    same = (ii >> LOG2_CHUNK) == (jj >> LOG2_CHUNK)
    upper =jnp.where(same & (ii <= jj), 1.0, 0.0).astype(BF16)
    block = jnp.where(same, 1.0, 0.0).astype(BF16)
    eye = jnp.where(ii == jj, 1.0, 0.0).astype(BF16)
    gam = None
    glast = None
    for part in _split3(g):
        t1 = _dot(part, upper)
        t2 = _dot(part, block)
        gam = t1 if gam is None else gam + t1
        glast = t2 if glast is None else glast + t2
    egam = jnp.exp(gam)
    ekd = jnp.exp(glast - gam)
    egl = jnp.exp(glast)
    zero = jnp.zeros_like(gam[0:1])
    rows = []
    for n in range(NB * H):
        r = jnp.concatenate([beta[n:n + 1], gam[n:n + 1], egam[n:n + 1], ekd[n:n + 1],
                             egl[n:n + 1], zero, zero, zero], axis=0)
        grow_ref[n // H, n % H] = r
        rows.append(r)
    allrows = jnp.concatenate(rows, axis=0)
    cols = None
    for part in _split3(allrows):
        t = _dot_nt(eye, part)
        cols = t if cols is None else cols + t
    for bb in range(NB):
        gcol_ref[bb] = cols[:, 8 * H * bb:8 * H * (bb + 1)]


def _gdn_gates(ba, a_log, dt_bias, l):
    B, nb, S = ba.shape
    H = GDN_HEADS
    NB = math.gcd(B, GATE_BATCH)
    tm = min(S, GATE_TOKENS)
    vmem = 8 * tm * tm * 4 + 6 * NB * H * tm * V7X_LANES * 4
    return pl.pallas_call(
        functools.partial(_gate_kernel, tm=tm, NB=NB),
        grid=(B // NB, S // tm),
        in_specs=[
            pl.BlockSpec((NB, nb, tm), lambda b, s: (b, 0, s)),
            pl.BlockSpec((None, H, 1), lambda b, s: (l, 0, 0)),
            pl.BlockSpec((None, H, 1), lambda b, s: (l, 0, 0)),
        ],
        out_specs=[
            pl.BlockSpec((NB, H, 8, tm), lambda b, s: (b, 0, 0, s)),
            pl.BlockSpec((NB, tm, 8 * H), lambda b, s: (b, s, 0)),
        ],
        out_shape=[
            jax.ShapeDtypeStruct((B, H, 8, S), F32),
            jax.ShapeDtypeStruct((B, S, 8 * H), F32),
        ],
        compiler_params=_params(vmem, 2),
        name="gdn_gates",
    )(ba, a_log, dt_bias)


def _gdn_kernel(q_ref, k_ref, v_ref, zs_ref, grow_ref, gcol_ref, og_ref, o_ref, state_ref, *, tc, G, NB):
    s = pl.program_id(2)

    @pl.when(s == 0)
    def _():
        state_ref[...] = jnp.zeros(state_ref.shape, F32)

    heads = range(NB * G)

    def at(ref, n):
        return ref[n // G, n % G]

    n_chunks = tc // CHUNK
    scale = GDN_DK ** -0.5
    ii = lax.broadcasted_iota(jnp.int32, (tc, tc), 0)
    jj = lax.broadcasted_iota(jnp.int32, (tc, tc), 1)
    same = (ii >> LOG2_CHUNK) == (jj >> LOG2_CHUNK)
    causal = same & (ii >= jj)
    strict = same & (ii > jj)
    lane = lax.broadcasted_iota(jnp.int32, (GDN_DK, tc), 1) >> LOG2_CHUNK

    gc = [gcol_ref[g // G][:, 8 * (g % G):8 * (g % G) + 8] for g in heads]
    gr = [at(grow_ref, g) for g in heads]
    k = [at(k_ref, g) for g in heads]
    kf = [k[g].astype(F32) for g in heads]
    kb = [kf[g] * gc[g][:, 0:1] for g in heads]
    decay = [jnp.where(causal, jnp.exp(jnp.minimum(gc[g][:, 1:2] - gr[g][1:2, :], 0.0)), 0.0) for g in heads]

    akq = [_dot_nt(jnp.concatenate([kb[g].astype(BF16), at(q_ref, g)], axis=0), k[g]) for g in heads]
    m = [jnp.where(strict, akq[g][:tc] * decay[g], 0.0) for g in heads]
    qk = [(akq[g][tc:] * (decay[g] * scale)).astype(BF16) for g in heads]

    rlane = lax.broadcasted_iota(jnp.int32, (CHUNK, tc), 1)
    rrow = lax.broadcasted_iota(jnp.int32, (CHUNK, tc), 0)
    eye_row = jnp.where((rlane & (CHUNK - 1)) == rrow, 1.0, 0.0)
    rblk = rlane >> LOG2_CHUNK

    def blockdiag(x_row):
        return jnp.concatenate([jnp.where(rblk == c, x_row, 0.0) for c in range(n_chunks)], axis=0).astype(BF16)

    def side_by_side(x):
        acc = x[0:CHUNK]
        for c in range(1, n_chunks):
            acc = acc + x[c * CHUNK:(c + 1) * CHUNK]
        return acc

    m_row = [side_by_side(m[g]) for g in heads]
    p = [_dot(m_row[g].astype(BF16), blockdiag(m_row[g])) for g in heads]
    t_row = [eye_row - m_row[g] for g in heads]
    for _ in range(LOG2_CHUNK - 2):
        r = [_dot(jnp.concatenate([p[g], t_row[g]], axis=0).astype(BF16), blockdiag(p[g])) for g in heads]
        p = [r[g][:CHUNK] for g in heads]
        t_row = [t_row[g] + r[g][CHUNK:] for g in heads]
    t_row = [t_row[g] + _dot(t_row[g].astype(BF16), blockdiag(p[g])) for g in heads]

    rhs = [jnp.concatenate([at(v_ref, g).astype(F32) * gc[g][:, 0:1], kb[g] * gc[g][:, 2:3]],
                           axis=1).astype(BF16) for g in heads]
    uw = [_dot(blockdiag(t_row[g]), rhs[g]).astype(BF16) for g in heads]

    kdt = [(kf[g] * gc[g][:, 3:4]).T for g in heads]
    pre_lhs = [jnp.concatenate([jnp.where(lane == c, kdt[g], 0.0).astype(BF16) for c in range(n_chunks)]
                               + [qk[g]], axis=0) for g in heads]
    pre = [_dot(pre_lhs[g], uw[g]) for g in heads]
    qd = [at(q_ref, g).astype(F32) * (gc[g][:, 2:3] * scale) for g in heads]
    base = n_chunks * GDN_DK
    qmod = [(qd[g] - pre[g][base:, GDN_DV:]).astype(BF16) for g in heads]

    states = [at(state_ref, g) for g in heads]
    outs = [[] for _ in heads]
    for c in range(n_chunks):
        r0 = c * CHUNK
        for g in heads:
            kblk = pre[g][c * GDN_DK:(c + 1) * GDN_DK]
            lhs = jnp.concatenate([kblk[:, GDN_DV:].astype(BF16), qmod[g][r0:r0 + CHUNK]], axis=0)
            r = _dot(lhs, states[g].astype(BF16))
            outs[g].append(r[GDN_DK:] + pre[g][base + r0:base + r0 + CHUNK, :GDN_DV])
            states[g] = states[g] * gr[g][4:5, r0:r0 + 1] - r[:GDN_DK] + kblk[:, :GDN_DV]
    for g in heads:
        bb, hh = g // G, g % G
        state_ref[bb, hh] = states[g]
        o = jnp.concatenate(outs[g], axis=0)
        o = o * lax.rsqrt(jnp.mean(o * o, axis=-1, keepdims=True) + EPS) * og_ref[...]
        o_ref[bb, :, hh * GDN_DV:(hh + 1) * GDN_DV] = (o * _silu(zs_ref[bb, hh].astype(F32))).astype(BF16)


def _gdn(qkv, zs, grow, gcol, out_gain, l):
    B, _, S, _ = qkv.shape
    H = GDN_HEADS
    G = GDN_GROUP
    NB = math.gcd(B, GDN_BATCH)
    assert G == H, "the gate columns are read per head from one lane-packed block"
    tc = min(S, GDN_TOKENS)
    vmem = NB * G * (24 * tc * tc * 4 + 24 * tc * V7X_LANES * 4)
    return pl.pallas_call(
        functools.partial(_gdn_kernel, tc=tc, G=G, NB=NB),
        grid=(B // NB, H // G, S // tc),
        in_specs=[
            pl.BlockSpec((NB, G, tc, GDN_DK), lambda b, h, s: (b, h, s, 0)),
            pl.BlockSpec((NB, G, tc, GDN_DK), lambda b, h, s: (b, H // G + h, s, 0)),
            pl.BlockSpec((NB, G, tc, GDN_DV), lambda b, h, s: (b, 2 * H // G + h, s, 0)),
            pl.BlockSpec((NB, G, tc, GDN_DV), lambda b, h, s: (b, h, s, 0)),
            pl.BlockSpec((NB, G, 8, tc), lambda b, h, s: (b, h, 0, s)),
            pl.BlockSpec((NB, tc, 8 * H), lambda b, h, s: (b, s, 0)),
            pl.BlockSpec((None, 1, GDN_DV), lambda b, h, s: (l, 0, 0)),
        ],
        out_specs=pl.BlockSpec((NB, tc, G * GDN_DV), lambda b, h, s: (b, s, h)),
        out_shape=jax.ShapeDtypeStruct((B, S, H * GDN_DV), BF16),
        scratch_shapes=[pltpu.VMEM((NB, G, GDN_DK, GDN_DV), F32)],
        compiler_params=_params(vmem, 3),
        name="gdn",
    )(qkv, qkv, qkv, zs, grow, gcol, out_gain)


def _swap_halves(y):
    half = y.shape[-1] // 2
    return jnp.concatenate([y[:, half:], y[:, :half]], axis=1)


def _mla_prep_kernel(lat_ref, cos_ref, sin_ref, cost_ref, sint_ref, qg_ref, kvg_ref, wqt_ref, wk_ref, wvt_ref,
                     qnc_ref, kn_ref, qt_ref, k_ref, vt_ref):
    H = MLA_HEADS
    half = QK_ROPE // 2
    lat = lat_ref[...]
    ql = lat[:, :Q_LORA]
    kvl = lat[:, Q_LORA:Q_LORA + KV_LORA]
    kpe = lat[:, Q_LORA + KV_LORA:Q_LORA + KV_LORA + QK_ROPE]
    qln = (ql * lax.rsqrt(jnp.mean(ql * ql, axis=-1, keepdims=True) + EPS) * qg_ref[...]).astype(BF16)
    kvn = (kvl * lax.rsqrt(jnp.mean(kvl * kvl, axis=-1, keepdims=True) + EPS) * kvg_ref[...]).astype(BF16)
    qt = _dot_nt(wqt_ref[...], qln)
    knope = _dot(kvn, wk_ref[...])
    vt = _dot_nt(wvt_ref[...], kvn)

    cost = cost_ref[...]
    sint = sint_ref[...]
    qnc = qnc_ref[...]
    scale = MLA_QK ** -0.5 * math.log2(math.e)
    for hh in range(H):
        blk = qt[hh * MLA_QK:(hh + 1) * MLA_QK]
        r = lax.rsqrt(jnp.sum(blk * blk, axis=0, keepdims=True) * (1.0 / MLA_QK) + EPS) * scale
        y = blk * qnc * r
        y1 = y[QK_NOPE:QK_NOPE + half]
        y2 = y[QK_NOPE + half:]
        qt_ref[hh, 0:QK_NOPE, :] = y[:QK_NOPE].astype(BF16)
        qt_ref[hh, QK_NOPE:QK_NOPE + half, :] = (y1 * cost - y2 * sint).astype(BF16)
        qt_ref[hh, QK_NOPE + half:MLA_QK, :] = (y2 * cost + y1 * sint).astype(BF16)
        vt_ref[hh] = vt[hh * V_HEAD:(hh + 1) * V_HEAD].astype(BF16)

    kn = kn_ref[...]
    yk = kpe * kn[:, QK_NOPE:]
    kr = yk * cos_ref[...] + _swap_halves(yk) * sin_ref[...]
    kpe_ss = jnp.sum(kpe * kpe, axis=-1, keepdims=True)
    for hh in range(H):
        k_nope = knope[:, hh * QK_NOPE:(hh + 1) * QK_NOPE]
        ss = jnp.sum(k_nope * k_nope, axis=-1, keepdims=True) + kpe_ss
        r = lax.rsqrt(ss * (1.0 / MLA_QK) + EPS)
        k_ref[hh, :, 0:QK_NOPE] = (k_nope * r * kn[:, :QK_NOPE]).astype(BF16)
        k_ref[hh, :, QK_NOPE:MLA_QK] = (kr * r).astype(BF16)


def _mla_prep(lat, cos2, sin2, cost, sint, q_lat_gain, kv_lat_gain, wqt, wk, wvt, q_norm_col, k_norm, l):
    B, S, nl = lat.shape
    H = MLA_HEADS
    half = QK_ROPE // 2
    tm = min(S, ATTN_Q)
    nq = wqt.shape[1]
    nk = wk.shape[-1]
    nv = wvt.shape[1]
    vmem = ((nq * Q_LORA + (nk + nv) * KV_LORA) * 2 + 2 * tm * nl * 4 + 3 * tm * (nq + nk + nv) * 4
            + 4 * H * tm * (256 + MLA_QK + V_HEAD) * 2)
    return pl.pallas_call(
        _mla_prep_kernel,
        grid=(B, S // tm),
        in_specs=[
            pl.BlockSpec((None, tm, nl), lambda b, s: (b, s, 0)),
            pl.BlockSpec((None, tm, QK_ROPE), lambda b, s: (b, s, 0)),
            pl.BlockSpec((None, tm, QK_ROPE), lambda b, s: (b, s, 0)),
            pl.BlockSpec((None, half, tm), lambda b, s: (b, 0, s)),
            pl.BlockSpec((None, half, tm), lambda b, s: (b, 0, s)),
            pl.BlockSpec((None, 1, Q_LORA), lambda b, s: (l, 0, 0)),
            pl.BlockSpec((None, 1, KV_LORA), lambda b, s: (l, 0, 0)),
            _resident((None, nq, Q_LORA), lambda b, s: (l, 0, 0)),
            _resident((None, KV_LORA, nk), lambda b, s: (l, 0, 0)),
            _resident((None, nv, KV_LORA), lambda b, s: (l, 0, 0)),
            pl.BlockSpec((None, MLA_QK, 1), lambda b, s: (l, 0, 0)),
            pl.BlockSpec((None, 1, MLA_QK), lambda b, s: (l, 0, 0)),
        ],
        out_specs=[
            pl.BlockSpec((None, H, MLA_QK, tm), lambda b, s: (b, 0, 0, s)),
            pl.BlockSpec((None, H, tm, MLA_QK), lambda b, s: (b, 0, s, 0)),
            pl.BlockSpec((None, H, None, V_HEAD, tm), lambda b, s: (b, 0, s, 0, 0)),
        ],
        out_shape=[
            jax.ShapeDtypeStruct((B, H, MLA_QK, S), BF16),
            jax.ShapeDtypeStruct((B, H, S, MLA_QK), BF16),
            jax.ShapeDtypeStruct((B, H, S // tm, V_HEAD, tm), BF16),
        ],
        compiler_params=_params(vmem, 2),
        name="mla_prep",
    )(lat, cos2, sin2, cost, sint, q_lat_gain, kv_lat_gain, wqt, wk, wvt, q_norm_col, k_norm)


def _attn_kernel(qt_ref, k_ref, vt_ref, o_ref, sa_ref, sb_ref, mxa_ref, mxb_ref, m_ref, l_ref, acc_ref, *,
                 tq, chain, single):
    c = 0 if single else pl.program_id(2)
    bufs = ((sa_ref, mxa_ref), (sb_ref, mxb_ref))

    def scores(qt, j, buf):
        start = j * tq if isinstance(j, int) else pl.multiple_of(j * tq, tq)
        st = _dot(k_ref[pl.ds(start, tq), :], qt)
        buf[0][...] = st
        buf[1][...] = jnp.max(st, axis=0, keepdims=True)

    def reset(w):
        m_ref[w] = jnp.full(m_ref.shape[1:], -jnp.inf, F32)
        l_ref[w] = jnp.zeros(l_ref.shape[1:], F32)
        acc_ref[w] = jnp.zeros(acc_ref.shape[1:], F32)

    def update(w, j, buf, masked=False):
        st = buf[0][...]
        mx = buf[1][...]
        if masked:
            kpos = lax.broadcasted_iota(jnp.int32, (tq, tq), 0)
            qpos = lax.broadcasted_iota(jnp.int32, (tq, tq), 1)
            st = jnp.where(kpos <= qpos, st, -jnp.inf)
            mx = jnp.max(st, axis=0, keepdims=True)
        m = m_ref[w]
        m_new = jnp.maximum(m, mx)
        alpha = jnp.exp2(m - m_new)
        pr = jnp.exp2(st - m_new)
        m_ref[w] = m_new
        l_ref[w] = alpha * l_ref[w] + jnp.sum(pr, axis=0, keepdims=True)
        acc_ref[w] = alpha * acc_ref[w] + _dot(vt_ref[j], pr.astype(BF16))

    def full_blocks(w, qt, cur, nxt, trips):
        def body(t, _):
            scores(qt, 2 * t + 1, nxt)
            update(w, 2 * t, cur)
            scores(qt, 2 * t + 2, cur)
            update(w, 2 * t + 1, nxt)
            return 0
        if isinstance(trips, int):
            for t in range(trips):
                body(t, 0)
        else:
            lax.fori_loop(0, trips, body, 0)

    def q_block(u):
        return qt_ref[:, u * tq:(u + 1) * tq]

    cur, nxt = bufs
    reset(0)
    scores(q_block(0), 0, cur)
    for u in range(chain):
        w = u % 2
        i = c * chain + u
        full_blocks(w, q_block(u), cur, nxt, c * (chain // 2) + u // 2)
        if u % 2 == 1:
            scores(q_block(u), i, nxt)
            update(w, i - 1, cur)
            cur, nxt = nxt, cur
        if u + 1 < chain:
            reset(1 - w)
            scores(q_block(u + 1), 0, nxt)
        update(w, i, cur, masked=True)
        o_ref[u * tq:(u + 1) * tq, :] = (acc_ref[w] / l_ref[w]).T.astype(BF16)
        cur, nxt = nxt, cur


def _attention(qt, k, vt):
    B, H, S, _ = k.shape
    tq = vt.shape[-1]
    nq = S // tq
    chain = min(nq, ATTN_CHAIN)
    assert chain % 2 == 0 and nq % chain == 0, "q blocks are taken in even-length runs"
    vmem = (8 * tq * tq * 4 + 4 * S * (256 + V_HEAD) * 2 + 4 * chain * tq * (256 + V_HEAD) * 2
            + 16 * tq * V7X_LANES * 4)
    return pl.pallas_call(
        functools.partial(_attn_kernel, tq=tq, chain=chain, single=(nq == chain)),
        grid=(B, H, nq // chain),
        in_specs=[
            pl.BlockSpec((None, None, MLA_QK, chain * tq), lambda b, h, i: (b, h, 0, i)),
            pl.BlockSpec((None, None, S, MLA_QK), lambda b, h, i: (b, h, 0, 0)),
            pl.BlockSpec((None, None, S // tq, V_HEAD, tq), lambda b, h, i: (b, h, 0, 0, 0)),
        ],
        out_specs=pl.BlockSpec((None, chain * tq, V_HEAD), lambda b, h, i: (b, i, h)),
        out_shape=jax.ShapeDtypeStruct((B, S, H * V_HEAD), BF16),
        scratch_shapes=[pltpu.VMEM((tq, tq), F32), pltpu.VMEM((tq, tq), F32),
                        pltpu.VMEM((1, tq), F32), pltpu.VMEM((1, tq), F32),
                        pltpu.VMEM((2, 1, tq), F32), pltpu.VMEM((2, 1, tq), F32),
                        pltpu.VMEM((2, V_HEAD, tq), F32)],
        compiler_params=_params(vmem, 3),
        name="mla_attention",
    )(qt, k, vt)


def _merge_kernel(x_ref, mod_ref, oa_ref, ob_ref, gates_ref, wa_ref, wb_ref, wo_ref, o_ref):
    D = x_ref.shape[-1]
    mod = mod_ref[...]
    gates = _sigmoid(gates_ref[...].astype(F32))
    ya =_dot(oa_ref[...], wa_ref[...])
    yb = _dot(ob_ref[...], wb_ref[...])
    y = (gates[:, :D] * ya + gates[:, D:] * yb).astype(BF16)
    o_ref[...] = x_ref[...] + (1.0 + mod[5:6]) * _dot(y, wo_ref[...])


def _merge(x, mod, oa, ob, gates, wa, wb, wo, l):
    B, S, D = x.shape
    tm = min(S, MERGE_TOKENS)
    na = oa.shape[-1]
    nbv = ob.shape[-1]
    vmem = (na + nbv + D) * D * 2 + 4 * tm * D * 4 + 2 * tm * (na + nbv + 2 * D) * 2 + 4 * tm * D * 4
    return pl.pallas_call(
        _merge_kernel,
        grid=(B, S // tm),
        in_specs=[
            pl.BlockSpec((None, tm, D), lambda b, s: (b, s, 0)),
            pl.BlockSpec((None, None, N_MOD, D), lambda b, s: (l, b, 0, 0)),
            pl.BlockSpec((None, tm, na), lambda b, s: (b, s, 0)),
            pl.BlockSpec((None, tm, nbv), lambda b, s: (b, s, 0)),
            pl.BlockSpec((None, tm, 2 * D), lambda b, s: (b, s, 0)),
            _resident((None, na, D), lambda b, s: (l, 0, 0)),
            _resident((None, nbv, D), lambda b, s: (l, 0, 0)),
            _resident((None, D, D), lambda b, s: (l, 0, 0)),
        ],
        out_specs=pl.BlockSpec((None, tm, D), lambda b, s: (b, s, 0)),
        out_shape=jax.ShapeDtypeStruct((B, S, D), F32),
        compiler_params=_params(vmem, 2),
        name="merge",
    )(x, mod, oa, ob, gates, wa, wb, wo)


def _pack_w_in(w_in):
    conv_c = 2 * GDN_HEADS * GDN_DK + GDN_HEADS * GDN_DV
    gdn_v = GDN_HEADS * GDN_DV
    D = w_in.shape[1]
    sizes = (conv_c, gdn_v, GDN_HEADS, GDN_HEADS, Q_LORA, KV_LORA, QK_ROPE, D, D)
    offs = np.cumsum((0,) + sizes)
    qkv, z, b_raw, a_raw, q_lat, kv_lat, k_pe, gate_a, gate_b = (
        w_in[:, :, int(offs[i]):int(offs[i + 1])] for i in range(len(sizes)))
    n_lat = Q_LORA + KV_LORA + QK_ROPE
    pad = (-n_lat) % V7X_LANES
    lat = jnp.concatenate([q_lat, kv_lat, k_pe, jnp.zeros(w_in.shape[:2] + (pad,), w_in.dtype)], axis=-1)
    ba = jnp.swapaxes(jnp.concatenate([b_raw, a_raw], axis=-1), 1, 2)
    wall = jnp.concatenate([qkv, z, gate_a, gate_b, lat], axis=-1)
    return wall.astype(BF16), ba.astype(BF16)


def _pack_kv_up(w_kv_up):
    L = w_kv_up.shape[0]
    w = w_kv_up.reshape(L, KV_LORA, MLA_HEADS, QK_NOPE + V_HEAD)
    wk = w[..., :QK_NOPE].reshape(L, KV_LORA, MLA_HEADS * QK_NOPE)
    wv = w[..., QK_NOPE:].reshape(L, KV_LORA, MLA_HEADS * V_HEAD)
    return wk.astype(BF16), jnp.swapaxes(wv, 1, 2).astype(BF16)


def kernel(x, c, positions, ada_w, ada_b, norm_ffn1, ffn1_w1, ffn1_w3, ffn1_w2, norm_mix, w_in, gdn_conv, gdn_a_log, gdn_dt_bias, gdn_out_gain, mla_q_lat_gain, mla_kv_lat_gain, mla_w_q_up, mla_w_kv_up, mla_q_norm, mla_k_norm, w_branch_a, w_branch_b, w_out, norm_ffn2, ffn2_w1, ffn2_w3, ffn2_w2):
    L = ada_w.shape[0]
    S = x.shape[1]
    assert S % CHUNK == 0 and S % min(S, GDN_TOKENS) == 0 and S % min(S, FFN_TOKENS) == 0
    assert S % min(S, MERGE_TOKENS) == 0 and S % min(S, ATTN_Q) == 0

    mod = _ada_mod(c, ada_w, ada_b)
    cos2, sin2, cost, sint = _rope_tables(positions)

    row = lambda a: a.reshape(L, 1, a.shape[-1])
    col = lambda a: a.reshape(L, a.shape[-1], 1)
    bf = lambda a: a.astype(BF16)
    wall, wba = _pack_w_in(w_in)
    wqt = bf(jnp.swapaxes(mla_w_q_up, 1, 2))
    wk, wvt = _pack_kv_up(mla_w_kv_up)
    f1 = (bf(ffn1_w1), bf(ffn1_w3), bf(ffn1_w2))
    f2 = (bf(ffn2_w1), bf(ffn2_w3), bf(ffn2_w2))
    wa, wb, wo = bf(w_branch_a), bf(w_branch_b), bf(w_out)
    g_ffn1, g_mix, g_ffn2 = row(norm_ffn1), row(norm_mix), row(norm_ffn2)
    og, qlg, kvlg = row(gdn_out_gain), row(mla_q_lat_gain), row(mla_kv_lat_gain)
    qnc, kn = col(mla_q_norm), row(mla_k_norm)
    a_log, dt_bias = col(gdn_a_log), col(gdn_dt_bias)

    for l in range(L):
        x = _ffn(x, mod, g_ffn1, *f1, l, 0)
        qkv, zs, gates, lat, ba = _mixin(x, mod, g_mix, wall, wba, gdn_conv, l)
        grow, gcol = _gdn_gates(ba, a_log, dt_bias, l)
        o_gdn = _gdn(qkv, zs, grow, gcol, og, l)
        qt, k, vt = _mla_prep(lat, cos2, sin2, cost, sint, qlg, kvlg, wqt, wk, wvt, qnc, kn, l)
        o_mla = _attention(qt, k, vt)
        x = _merge(x, mod, o_gdn, o_mla, gates, wa, wb, wo, l)
        x = _ffn(x, mod, g_ffn2, *f2, l, 6)
    return x
```

```python
import functools
import math

import numpy as np
import jax
import jax.numpy as jnp
from jax import lax
from jax.experimental import pallas as pl
from jax.experimental.pallas import tpu as pltpu

F32 = jnp.float32
BF16 = jnp.bfloat16

EPS = 1e-6
N_MOD = 9
GDN_HEADS = 8
GDN_DK = 128
GDN_DV = 128
CONV_K = 4
CHUNK = 64
LOG2_CHUNK = 6
MLA_HEADS = 8
Q_LORA = 384
KV_LORA = 256
QK_NOPE = 128
QK_ROPE = 64
V_HEAD = 128
ROPE_THETA = 10000.0
MLA_QK = QK_NOPE + QK_ROPE

V7X_VMEM_BYTES = 64 * 1024 * 1024
V7X_LANES = 128
V7X_SUBLANES = 8
VMEM_CAP_BYTES = V7X_VMEM_BYTES - 6 * 1024 * 1024

FFN_TOKENS = 1024
MIX_TOKENS = 256
GATE_TOKENS = 512
GATE_BATCH = 4
GDN_TOKENS = 256
GDN_GROUP = 8
GDN_BATCH = 4
ATTN_Q = 512
ATTN_CHAIN = 8
MERGE_TOKENS = 1024
HALO = V7X_SUBLANES


def _params(vmem_bytes, n_axes):
    limit = int(min(VMEM_CAP_BYTES, max(vmem_bytes, 16 * 1024 * 1024)))
    return pltpu.CompilerParams(dimension_semantics=("arbitrary",) * n_axes, vmem_limit_bytes=limit)


def _resident(block_shape, index_map):
    return pl.BlockSpec(block_shape, index_map, pipeline_mode=pl.Buffered(1))


def _dot(a, b):
    return jnp.dot(a, b, preferred_element_type=F32)


def _dot_nt(a, b):
    return lax.dot_general(a, b, (((1,), (1,)), ((), ())), preferred_element_type=F32)


def _sigmoid(x):
    return 0.5 * jnp.tanh(0.5 * x) + 0.5


def _silu(x):
    hx = 0.5 * x
    return hx * jnp.tanh(hx) + hx


def _modulate(x, gain, shift, scale):
    y = x * lax.rsqrt(jnp.mean(x * x, axis=-1, keepdims=True) + EPS)
    return (y * gain) * (1.0 + scale) + shift


def _split3(x):
    x1 = x.astype(BF16)
    r1 = x - x1.astype(F32)
    x2 = r1.astype(BF16)
    x3 = (r1 - x2.astype(F32)).astype(BF16)
    return x1, x2, x3


def _mod_kernel(c_ref, w_ref, b_ref, o_ref):
    cond = _silu(c_ref[...]).astype(BF16)
    o_ref[...] = _dot(cond, w_ref[...].astype(BF16)) + b_ref[...]


def _ada_mod(c, ada_w, ada_b):
    L, D, _ = ada_w.shape
    B = c.shape[0]
    ada_b3 = ada_b.reshape(L, N_MOD, 1, D)
    out = pl.pallas_call(
        _mod_kernel,
        grid=(L, N_MOD),
        in_specs=[
            pl.BlockSpec((B, D), lambda l, j: (0, 0)),
            pl.BlockSpec((None, D, D), lambda l, j: (l, 0, j)),
            pl.BlockSpec((None, None, 1, D), lambda l, j: (l, j, 0, 0)),
        ],
        out_specs=pl.BlockSpec((None, None, B, D), lambda l, j: (l, j, 0, 0)),
        out_shape=jax.ShapeDtypeStruct((L, N_MOD, B, D), F32),
        compiler_params=_params(4 * D * D * 4, 2),
        name="ada_mod",
    )(c, ada_w, ada_b3)
    return jnp.transpose(out, (0, 2, 1, 3))


def _rope_kernel(pos_ref, post_ref, freq_ref, sign_ref, freqc_ref, cos_ref, sin_ref, cost_ref, sint_ref):
    ang = pos_ref[...].astype(F32) * freq_ref[...]
    cos_ref[...] = jnp.cos(ang)
    sin_ref[...] = jnp.sin(ang) * sign_ref[...]
    angt = freqc_ref[...] * post_ref[...].astype(F32)
    cost_ref[...] = jnp.cos(angt)
    sint_ref[...] = jnp.sin(angt)


def _rope_tables(positions):
    B, S = positions.shape
    half = QK_ROPE // 2
    inv_freq = ROPE_THETA ** (-jnp.arange(half, dtype=F32) / half)
    freq2 = jnp.concatenate([inv_freq, inv_freq]).reshape(1, QK_ROPE)
    sign2 = jnp.concatenate([-jnp.ones((half,), F32), jnp.ones((half,), F32)]).reshape(1, QK_ROPE)
    ts = min(S, 1024)
    return pl.pallas_call(
        _rope_kernel,
        grid=(B, S // ts),
        in_specs=[
            pl.BlockSpec((None, ts, 1), lambda b, s: (b, s, 0)),
            pl.BlockSpec((None, 1, ts), lambda b, s: (b, 0, s)),
            pl.BlockSpec((1, QK_ROPE), lambda b, s: (0, 0)),
            pl.BlockSpec((1, QK_ROPE), lambda b, s: (0, 0)),
            pl.BlockSpec((half, 1), lambda b, s: (0, 0)),
        ],
        out_specs=[pl.BlockSpec((None, ts, QK_ROPE), lambda b, s: (b, s, 0))] * 2
        + [pl.BlockSpec((None, half, ts), lambda b, s: (b, 0, s))] * 2,
        out_shape=[jax.ShapeDtypeStruct((B, S, QK_ROPE), F32)] * 2
        + [jax.ShapeDtypeStruct((B, half, S), F32)] * 2,
        compiler_params=_params(8 * ts * V7X_LANES * 4, 2),
        name="rope_tables",
    )(positions.reshape(B, S, 1), positions.reshape(B, 1, S), freq2, sign2, inv_freq.reshape(half, 1))


def _ffn_kernel(x_ref, mod_ref, gain_ref, w1_ref, w3_ref, w2_ref, o_ref, *, row):
    x = x_ref[...]
    mod = mod_ref[...]
    h = _modulate(x, gain_ref[...], mod[row:row + 1], mod[row + 1:row + 2]).astype(BF16)
    a = _dot(h, w1_ref[...])
    b = _dot(h, w3_ref[...])
    g = (_silu(a) * b).astype(BF16)
    y = _dot(g, w2_ref[...])
    o_ref[...] = x + (0.5 * (1.0 + mod[row + 2:row + 3])) * y


def _ffn(x, mod, gain, w1, w3, w2, l, row):
    B, S, D = x.shape
    F = w1.shape[-1]
    tm = min(S, FFN_TOKENS)
    vmem = 3 * D * F * 2 + 4 * tm * D * 4 + 3 * tm * F * 4 + 2 * tm * D * 4
    return pl.pallas_call(
        functools.partial(_ffn_kernel, row=row),
        grid=(B, S // tm),
        in_specs=[
            pl.BlockSpec((None, tm, D), lambda b, s: (b, s, 0)),
            pl.BlockSpec((None, None, N_MOD, D), lambda b, s: (l, b, 0, 0)),
            pl.BlockSpec((None, 1, D), lambda b, s: (l, 0, 0)),
            _resident((None, D, F), lambda b, s: (l, 0, 0)),
            _resident((None, D, F), lambda b, s: (l, 0, 0)),
            _resident((None, F, D), lambda b, s: (l, 0, 0)),
        ],
        out_specs=pl.BlockSpec((None, tm, D), lambda b, s: (b, s, 0)),
        out_shape=jax.ShapeDtypeStruct((B, S, D), F32),
        compiler_params=_params(vmem, 2),
        name="ffn",
    )(x, mod, gain, w1, w3, w2)


def _mixin_kernel(x_ref, mod_ref, gain_ref, wall_ref, wba_ref, conv_ref,
                  qkv_ref, zs_ref, gates_ref, lat_ref, ba_ref, pbuf_ref, *, tm):
    @pl.when(pl.program_id(1) == 0)
    def _():
        pbuf_ref[0:HALO, :] = jnp.zeros((HALO, pbuf_ref.shape[1]), F32)

    x = x_ref[...]
    mod = mod_ref[...]
    h = _modulate(x, gain_ref[...], mod[3:4], mod[4:5]).astype(BF16)

    C = pbuf_ref.shape[1]
    nz = zs_ref.shape[0] * zs_ref.shape[2]
    ng = gates_ref.shape[1]
    proj = _dot(h, wall_ref[...])
    pbuf_ref[HALO:HALO + tm, :] = proj[:, :C]
    conv = conv_ref[...]
    assert CONV_K == 4
    n_qk = 2 * GDN_HEADS
    for cb in range(pbuf_ref.shape[1] // V7X_LANES):
        c0 = cb * V7X_LANES
        xs = pbuf_ref[:, c0:c0 + V7X_LANES]
        w = [conv[j:j + 1, c0:c0 + V7X_LANES] for j in range(CONV_K)]
        x1 = pltpu.roll(xs, 1, axis=0)
        near = w[3] * xs + w[2] * x1
        far = w[1] * xs + w[0] * x1
        y = _silu(near[HALO:] + pltpu.roll(far, 2, axis=0)[HALO:])
        if cb < n_qk:
            y = y * lax.rsqrt(jnp.sum(y * y, axis=-1, keepdims=True) + EPS)
        qkv_ref[cb] = y.astype(BF16)
    pbuf_ref[0:HALO, :] = pbuf_ref[tm:tm + HALO, :]

    for hh in range(GDN_HEADS):
        zs_ref[hh] = proj[:, C + hh * GDN_DV:C + (hh + 1) * GDN_DV].astype(BF16)
    gates_ref[...] = proj[:, C + nz:C + nz + ng].astype(BF16)
    lat_ref[...] = proj[:, C + nz + ng:]
    ba_ref[...] = _dot_nt(wba_ref[...], h)


def _mixin(x, mod, gain, wall, wba, conv, l):
    B, S, D = x.shape
    C = conv.shape[-1]
    nz = GDN_HEADS * GDN_DV
    ng = 2 * D
    nl = wall.shape[-1] - C - nz - ng
    nb = wba.shape[1]
    tm = min(S, MIX_TOKENS)
    ncb = C // V7X_LANES
    vmem = ((C + nz + ng + nl) * D * 2 + 2 * tm * D * 4 + 2 * tm * (C + nz + ng) * 2 + 2 * tm * nl * 4
            + (tm + HALO) * C * 4 + tm * C * 4 + tm * (nz + ng) * 4 + 4 * 1024 * 1024)
    return pl.pallas_call(
        functools.partial(_mixin_kernel, tm=tm),
        grid=(B, S // tm),
        in_specs=[
            pl.BlockSpec((None, tm, D), lambda b, s: (b, s, 0)),
            pl.BlockSpec((None, None, N_MOD, D), lambda b, s: (l, b, 0, 0)),
            pl.BlockSpec((None, 1, D), lambda b, s: (l, 0, 0)),
            _resident((None, D, C + nz + ng + nl), lambda b, s: (l, 0, 0)),
            _resident((None, nb, D), lambda b, s: (l, 0, 0)),
            pl.BlockSpec((None, CONV_K, C), lambda b, s: (l, 0, 0)),
        ],
        out_specs=[
            pl.BlockSpec((None, ncb, tm, V7X_LANES), lambda b, s: (b, 0, s, 0)),
            pl.BlockSpec((None, GDN_HEADS, tm, GDN_DV), lambda b, s: (b, 0, s, 0)),
            pl.BlockSpec((None, tm, ng), lambda b, s: (b, s, 0)),
            pl.BlockSpec((None, tm, nl), lambda b, s: (b, s, 0)),
            pl.BlockSpec((None, nb, tm), lambda b, s: (b, 0, s)),
        ],
        out_shape=[
            jax.ShapeDtypeStruct((B, ncb, S, V7X_LANES), BF16),
            jax.ShapeDtypeStruct((B, GDN_HEADS, S, GDN_DV), BF16),
            jax.ShapeDtypeStruct((B, S, ng), BF16),
            jax.ShapeDtypeStruct((B, S, nl), F32),
            jax.ShapeDtypeStruct((B, nb, S), F32),
        ],
        scratch_shapes=[pltpu.VMEM((tm + HALO, C), F32)],
        compiler_params=_params(vmem, 2),
        name="mixin",
    )(x, mod, gain, wall, wba, conv)


def _gate_kernel(ba_ref, alog_ref, dtb_ref, grow_ref, gcol_ref, *, tm, NB):
    H = GDN_HEADS
    betas, gs = [], []
    for bb in range(NB):
        ba = ba_ref[bb]
        betas.append(_sigmoid(ba[0:H]))
        xa = ba[H:2 * H] + dtb_ref[...]
        softplus = jnp.maximum(xa, 0.0) + jnp.log(1.0 + jnp.exp(-jnp.abs(xa)))
        gs.append(-jnp.exp(alog_ref[...]) * softplus)
    beta = jnp.concatenate(betas, axis=0)
    g = jnp.concatenate(gs, axis=0)

    ii = lax.broadcasted_iota(jnp.int32, (tm, tm), 0)
    jj = lax.broadcasted_iota(jnp.int32, (tm, tm), 1)
    same = (ii >> LOG2_CHUNK) == (jj >> LOG2_CHUNK)
    upper =jnp.where(same & (ii <= jj), 1.0, 0.0).astype(BF16)
    block = jnp.where(same, 1.0, 0.0).astype(BF16)
    eye = jnp.where(ii == jj, 1.0, 0.0).astype(BF16)
    gam = None
    glast = None
    for part in _split3(g):
        t1 = _dot(part, upper)
        t2 = _dot(part, block)
        gam = t1 if gam is None else gam + t1
        glast = t2 if glast is None else glast + t2
    egam = jnp.exp(gam)
    ekd = jnp.exp(glast - gam)
    egl = jnp.exp(glast)
    zero = jnp.zeros_like(gam[0:1])
    rows = []
    for n in range(NB * H):
        r = jnp.concatenate([beta[n:n + 1], gam[n:n + 1], egam[n:n + 1], ekd[n:n + 1],
                             egl[n:n + 1], zero, zero, zero], axis=0)
        grow_ref[n // H, n % H] = r
        rows.append(r)
    allrows = jnp.concatenate(rows, axis=0)
    cols = None
    for part in _split3(allrows):
        t = _dot_nt(eye, part)
        cols = t if cols is None else cols + t
    for bb in range(NB):
        gcol_ref[bb] = cols[:, 8 * H * bb:8 * H * (bb + 1)]


def _gdn_gates(ba, a_log, dt_bias, l):
    B, nb, S = ba.shape
    H = GDN_HEADS
    NB = math.gcd(B, GATE_BATCH)
    tm = min(S, GATE_TOKENS)
    vmem = 8 * tm * tm * 4 + 6 * NB * H * tm * V7X_LANES * 4
    return pl.pallas_call(
        functools.partial(_gate_kernel, tm=tm, NB=NB),
        grid=(B // NB, S // tm),
        in_specs=[
            pl.BlockSpec((NB, nb, tm), lambda b, s: (b, 0, s)),
            pl.BlockSpec((None, H, 1), lambda b, s: (l, 0, 0)),
            pl.BlockSpec((None, H, 1), lambda b, s: (l, 0, 0)),
        ],
        out_specs=[
            pl.BlockSpec((NB, H, 8, tm), lambda b, s: (b, 0, 0, s)),
            pl.BlockSpec((NB, tm, 8 * H), lambda b, s: (b, s, 0)),
        ],
        out_shape=[
            jax.ShapeDtypeStruct((B, H, 8, S), F32),
            jax.ShapeDtypeStruct((B, S, 8 * H), F32),
        ],
        compiler_params=_params(vmem, 2),
        name="gdn_gates",
    )(ba, a_log, dt_bias)


def _gdn_kernel(q_ref, k_ref, v_ref, zs_ref, grow_ref, gcol_ref, og_ref, o_ref, state_ref, *, tc, G, NB):
    s = pl.program_id(2)

    @pl.when(s == 0)
    def _():
        state_ref[...] = jnp.zeros(state_ref.shape, F32)

    heads = range(NB * G)

    def at(ref, n):
        return ref[n // G, n % G]

    n_chunks = tc // CHUNK
    scale = GDN_DK ** -0.5
    ii = lax.broadcasted_iota(jnp.int32, (tc, tc), 0)
    jj = lax.broadcasted_iota(jnp.int32, (tc, tc), 1)
    same = (ii >> LOG2_CHUNK) == (jj >> LOG2_CHUNK)
    causal = same & (ii >= jj)
    strict = same & (ii > jj)
    lane = lax.broadcasted_iota(jnp.int32, (GDN_DK, tc), 1) >> LOG2_CHUNK

    gc = [gcol_ref[g // G][:, 8 * (g % G):8 * (g % G) + 8] for g in heads]
    gr = [at(grow_ref, g) for g in heads]
    k = [at(k_ref, g) for g in heads]
    kf = [k[g].astype(F32) for g in heads]
    kb = [kf[g] * gc[g][:, 0:1] for g in heads]
    decay = [jnp.where(causal, jnp.exp(jnp.minimum(gc[g][:, 1:2] - gr[g][1:2, :], 0.0)), 0.0) for g in heads]

    akq = [_dot_nt(jnp.concatenate([kb[g].astype(BF16), at(q_ref, g)], axis=0), k[g]) for g in heads]
    m = [jnp.where(strict, akq[g][:tc] * decay[g], 0.0) for g in heads]
    qk = [(akq[g][tc:] * (decay[g] * scale)).astype(BF16) for g in heads]

    rlane = lax.broadcasted_iota(jnp.int32, (CHUNK, tc), 1)
    rrow = lax.broadcasted_iota(jnp.int32, (CHUNK, tc), 0)
    eye_row = jnp.where((rlane & (CHUNK - 1)) == rrow, 1.0, 0.0)
    rblk = rlane >> LOG2_CHUNK

    def blockdiag(x_row):
        return jnp.concatenate([jnp.where(rblk == c, x_row, 0.0) for c in range(n_chunks)], axis=0).astype(BF16)

    def side_by_side(x):
        acc = x[0:CHUNK]
        for c in range(1, n_chunks):
            acc = acc + x[c * CHUNK:(c + 1) * CHUNK]
        return acc

    m_row = [side_by_side(m[g]) for g in heads]
    p = [_dot(m_row[g].astype(BF16), blockdiag(m_row[g])) for g in heads]
    t_row = [eye_row - m_row[g] for g in heads]
    for _ in range(LOG2_CHUNK - 2):
        r = [_dot(jnp.concatenate([p[g], t_row[g]], axis=0).astype(BF16), blockdiag(p[g])) for g in heads]
        p = [r[g][:CHUNK] for g in heads]
        t_row = [t_row[g] + r[g][CHUNK:] for g in heads]
    t_row = [t_row[g] + _dot(t_row[g].astype(BF16), blockdiag(p[g])) for g in heads]

    rhs = [jnp.concatenate([at(v_ref, g).astype(F32) * gc[g][:, 0:1], kb[g] * gc[g][:, 2:3]],
                           axis=1).astype(BF16) for g in heads]
    uw = [_dot(blockdiag(t_row[g]), rhs[g]).astype(BF16) for g in heads]

    kdt = [(kf[g] * gc[g][:, 3:4]).T for g in heads]
    pre_lhs = [jnp.concatenate([jnp.where(lane == c, kdt[g], 0.0).astype(BF16) for c in range(n_chunks)]
                               + [qk[g]], axis=0) for g in heads]
    pre = [_dot(pre_lhs[g], uw[g]) for g in heads]
    qd = [at(q_ref, g).astype(F32) * (gc[g][:, 2:3] * scale) for g in heads]
    base = n_chunks * GDN_DK
    qmod = [(qd[g] - pre[g][base:, GDN_DV:]).astype(BF16) for g in heads]

    states = [at(state_ref, g) for g in heads]
    outs = [[] for _ in heads]
    for c in range(n_chunks):
        r0 = c * CHUNK
        for g in heads:
            kblk = pre[g][c * GDN_DK:(c + 1) * GDN_DK]
            lhs = jnp.concatenate([kblk[:, GDN_DV:].astype(BF16), qmod[g][r0:r0 + CHUNK]], axis=0)
            r = _dot(lhs, states[g].astype(BF16))
            outs[g].append(r[GDN_DK:] + pre[g][base + r0:base + r0 + CHUNK, :GDN_DV])
            states[g] = states[g] * gr[g][4:5, r0:r0 + 1] - r[:GDN_DK] + kblk[:, :GDN_DV]
    for g in heads:
        bb, hh = g // G, g % G
        state_ref[bb, hh] = states[g]
        o = jnp.concatenate(outs[g], axis=0)
        o = o * lax.rsqrt(jnp.mean(o * o, axis=-1, keepdims=True) + EPS) * og_ref[...]
        o_ref[bb, :, hh * GDN_DV:(hh + 1) * GDN_DV] = (o * _silu(zs_ref[bb, hh].astype(F32))).astype(BF16)


def _gdn(qkv, zs, grow, gcol, out_gain, l):
    B, _, S, _ = qkv.shape
    H = GDN_HEADS
    G = GDN_GROUP
    NB = math.gcd(B, GDN_BATCH)
    assert G == H, "the gate columns are read per head from one lane-packed block"
    tc = min(S, GDN_TOKENS)
    vmem = NB * G * (24 * tc * tc * 4 + 24 * tc * V7X_LANES * 4)
    return pl.pallas_call(
        functools.partial(_gdn_kernel, tc=tc, G=G, NB=NB),
        grid=(B // NB, H // G, S // tc),
        in_specs=[
            pl.BlockSpec((NB, G, tc, GDN_DK), lambda b, h, s: (b, h, s, 0)),
            pl.BlockSpec((NB, G, tc, GDN_DK), lambda b, h, s: (b, H // G + h, s, 0)),
            pl.BlockSpec((NB, G, tc, GDN_DV), lambda b, h, s: (b, 2 * H // G + h, s, 0)),
            pl.BlockSpec((NB, G, tc, GDN_DV), lambda b, h, s: (b, h, s, 0)),
            pl.BlockSpec((NB, G, 8, tc), lambda b, h, s: (b, h, 0, s)),
            pl.BlockSpec((NB, tc, 8 * H), lambda b, h, s: (b, s, 0)),
            pl.BlockSpec((None, 1, GDN_DV), lambda b, h, s: (l, 0, 0)),
        ],
        out_specs=pl.BlockSpec((NB, tc, G * GDN_DV), lambda b, h, s: (b, s, h)),
        out_shape=jax.ShapeDtypeStruct((B, S, H * GDN_DV), BF16),
        scratch_shapes=[pltpu.VMEM((NB, G, GDN_DK, GDN_DV), F32)],
        compiler_params=_params(vmem, 3),
        name="gdn",
    )(qkv, qkv, qkv, zs, grow, gcol, out_gain)


def _swap_halves(y):
    half = y.shape[-1] // 2
    return jnp.concatenate([y[:, half:], y[:, :half]], axis=1)


def _mla_prep_kernel(lat_ref, cos_ref, sin_ref, cost_ref, sint_ref, qg_ref, kvg_ref, wqt_ref, wk_ref, wvt_ref,
                     qnc_ref, kn_ref, qt_ref, k_ref, vt_ref):
    H = MLA_HEADS
    half = QK_ROPE // 2
    lat = lat_ref[...]
    ql = lat[:, :Q_LORA]
    kvl = lat[:, Q_LORA:Q_LORA + KV_LORA]
    kpe = lat[:, Q_LORA + KV_LORA:Q_LORA + KV_LORA + QK_ROPE]
    qln = (ql * lax.rsqrt(jnp.mean(ql * ql, axis=-1, keepdims=True) + EPS) * qg_ref[...]).astype(BF16)
    kvn = (kvl * lax.rsqrt(jnp.mean(kvl * kvl, axis=-1, keepdims=True) + EPS) * kvg_ref[...]).astype(BF16)
    qt = _dot_nt(wqt_ref[...], qln)
    knope = _dot(kvn, wk_ref[...])
    vt = _dot_nt(wvt_ref[...], kvn)

    cost = cost_ref[...]
    sint = sint_ref[...]
    qnc = qnc_ref[...]
    scale = MLA_QK ** -0.5 * math.log2(math.e)
    for hh in range(H):
        blk = qt[hh * MLA_QK:(hh + 1) * MLA_QK]
        r = lax.rsqrt(jnp.sum(blk * blk, axis=0, keepdims=True) * (1.0 / MLA_QK) + EPS) * scale
        y = blk * qnc * r
        y1 = y[QK_NOPE:QK_NOPE + half]
        y2 = y[QK_NOPE + half:]
        qt_ref[hh, 0:QK_NOPE, :] = y[:QK_NOPE].astype(BF16)
        qt_ref[hh, QK_NOPE:QK_NOPE + half, :] = (y1 * cost - y2 * sint).astype(BF16)
        qt_ref[hh, QK_NOPE + half:MLA_QK, :] = (y2 * cost + y1 * sint).astype(BF16)
        vt_ref[hh] = vt[hh * V_HEAD:(hh + 1) * V_HEAD].astype(BF16)

    kn = kn_ref[...]
    yk = kpe * kn[:, QK_NOPE:]
    kr = yk * cos_ref[...] + _swap_halves(yk) * sin_ref[...]
    kpe_ss = jnp.sum(kpe * kpe, axis=-1, keepdims=True)
    for hh in range(H):
        k_nope = knope[:, hh * QK_NOPE:(hh + 1) * QK_NOPE]
        ss = jnp.sum(k_nope * k_nope, axis=-1, keepdims=True) + kpe_ss
        r = lax.rsqrt(ss * (1.0 / MLA_QK) + EPS)
        k_ref[hh, :, 0:QK_NOPE] = (k_nope * r * kn[:, :QK_NOPE]).astype(BF16)
        k_ref[hh, :, QK_NOPE:MLA_QK] = (kr * r).astype(BF16)


def _mla_prep(lat, cos2, sin2, cost, sint, q_lat_gain, kv_lat_gain, wqt, wk, wvt, q_norm_col, k_norm, l):
    B, S, nl = lat.shape
    H = MLA_HEADS
    half = QK_ROPE // 2
    tm = min(S, ATTN_Q)
    nq = wqt.shape[1]
    nk = wk.shape[-1]
    nv = wvt.shape[1]
    vmem = ((nq * Q_LORA + (nk + nv) * KV_LORA) * 2 + 2 * tm * nl * 4 + 3 * tm * (nq + nk + nv) * 4
            + 4 * H * tm * (256 + MLA_QK + V_HEAD) * 2)
    return pl.pallas_call(
        _mla_prep_kernel,
        grid=(B, S // tm),
        in_specs=[
            pl.BlockSpec((None, tm, nl), lambda b, s: (b, s, 0)),
            pl.BlockSpec((None, tm, QK_ROPE), lambda b, s: (b, s, 0)),
            pl.BlockSpec((None, tm, QK_ROPE), lambda b, s: (b, s, 0)),
            pl.BlockSpec((None, half, tm), lambda b, s: (b, 0, s)),
            pl.BlockSpec((None, half, tm), lambda b, s: (b, 0, s)),
            pl.BlockSpec((None, 1, Q_LORA), lambda b, s: (l, 0, 0)),
            pl.BlockSpec((None, 1, KV_LORA), lambda b, s: (l, 0, 0)),
            _resident((None, nq, Q_LORA), lambda b, s: (l, 0, 0)),
            _resident((None, KV_LORA, nk), lambda b, s: (l, 0, 0)),
            _resident((None, nv, KV_LORA), lambda b, s: (l, 0, 0)),
            pl.BlockSpec((None, MLA_QK, 1), lambda b, s: (l, 0, 0)),
            pl.BlockSpec((None, 1, MLA_QK), lambda b, s: (l, 0, 0)),
        ],
        out_specs=[
            pl.BlockSpec((None, H, MLA_QK, tm), lambda b, s: (b, 0, 0, s)),
            pl.BlockSpec((None, H, tm, MLA_QK), lambda b, s: (b, 0, s, 0)),
            pl.BlockSpec((None, H, None, V_HEAD, tm), lambda b, s: (b, 0, s, 0, 0)),
        ],
        out_shape=[
            jax.ShapeDtypeStruct((B, H, MLA_QK, S), BF16),
            jax.ShapeDtypeStruct((B, H, S, MLA_QK), BF16),
            jax.ShapeDtypeStruct((B, H, S // tm, V_HEAD, tm), BF16),
        ],
        compiler_params=_params(vmem, 2),
        name="mla_prep",
    )(lat, cos2, sin2, cost, sint, q_lat_gain, kv_lat_gain, wqt, wk, wvt, q_norm_col, k_norm)


def _attn_kernel(qt_ref, k_ref, vt_ref, o_ref, sa_ref, sb_ref, mxa_ref, mxb_ref, m_ref, l_ref, acc_ref, *,
                 tq, chain, single):
    c = 0 if single else pl.program_id(2)
    bufs = ((sa_ref, mxa_ref), (sb_ref, mxb_ref))

    def scores(qt, j, buf):
        start = j * tq if isinstance(j, int) else pl.multiple_of(j * tq, tq)
        st = _dot(k_ref[pl.ds(start, tq), :], qt)
        buf[0][...] = st
        buf[1][...] = jnp.max(st, axis=0, keepdims=True)

    def reset(w):
        m_ref[w] = jnp.full(m_ref.shape[1:], -jnp.inf, F32)
        l_ref[w] = jnp.zeros(l_ref.shape[1:], F32)
        acc_ref[w] = jnp.zeros(acc_ref.shape[1:], F32)

    def update(w, j, buf, masked=False):
        st = buf[0][...]
        mx = buf[1][...]
        if masked:
            kpos = lax.broadcasted_iota(jnp.int32, (tq, tq), 0)
            qpos = lax.broadcasted_iota(jnp.int32, (tq, tq), 1)
            st = jnp.where(kpos <= qpos, st, -jnp.inf)
            mx = jnp.max(st, axis=0, keepdims=True)
        m = m_ref[w]
        m_new = jnp.maximum(m, mx)
        alpha = jnp.exp2(m - m_new)
        pr = jnp.exp2(st - m_new)
        m_ref[w] = m_new
        l_ref[w] = alpha * l_ref[w] + jnp.sum(pr, axis=0, keepdims=True)
        acc_ref[w] = alpha * acc_ref[w] + _dot(vt_ref[j], pr.astype(BF16))

    def full_blocks(w, qt, cur, nxt, trips):
        def body(t, _):
            scores(qt, 2 * t + 1, nxt)
            update(w, 2 * t, cur)
            scores(qt, 2 * t + 2, cur)
            update(w, 2 * t + 1, nxt)
            return 0
        if isinstance(trips, int):
            for t in range(trips):
                body(t, 0)
        else:
            lax.fori_loop(0, trips, body, 0)

    def q_block(u):
        return qt_ref[:, u * tq:(u + 1) * tq]

    cur, nxt = bufs
    reset(0)
    scores(q_block(0), 0, cur)
    for u in range(chain):
        w = u % 2
        i = c * chain + u
        full_blocks(w, q_block(u), cur, nxt, c * (chain // 2) + u // 2)
        if u % 2 == 1:
            scores(q_block(u), i, nxt)
            update(w, i - 1, cur)
            cur, nxt = nxt, cur
        if u + 1 < chain:
            reset(1 - w)
            scores(q_block(u + 1), 0, nxt)
        update(w, i, cur, masked=True)
        o_ref[u * tq:(u + 1) * tq, :] = (acc_ref[w] / l_ref[w]).T.astype(BF16)
        cur, nxt = nxt, cur


def _attention(qt, k, vt):
    B, H, S, _ = k.shape
    tq = vt.shape[-1]
    nq = S // tq
    chain = min(nq, ATTN_CHAIN)
    assert chain % 2 == 0 and nq % chain == 0, "q blocks are taken in even-length runs"
    vmem = (8 * tq * tq * 4 + 4 * S * (256 + V_HEAD) * 2 + 4 * chain * tq * (256 + V_HEAD) * 2
            + 16 * tq * V7X_LANES * 4)
    return pl.pallas_call(
        functools.partial(_attn_kernel, tq=tq, chain=chain, single=(nq == chain)),
        grid=(B, H, nq // chain),
        in_specs=[
            pl.BlockSpec((None, None, MLA_QK, chain * tq), lambda b, h, i: (b, h, 0, i)),
            pl.BlockSpec((None, None, S, MLA_QK), lambda b, h, i: (b, h, 0, 0)),
            pl.BlockSpec((None, None, S // tq, V_HEAD, tq), lambda b, h, i: (b, h, 0, 0, 0)),
        ],
        out_specs=pl.BlockSpec((None, chain * tq, V_HEAD), lambda b, h, i: (b, i, h)),
        out_shape=jax.ShapeDtypeStruct((B, S, H * V_HEAD), BF16),
        scratch_shapes=[pltpu.VMEM((tq, tq), F32), pltpu.VMEM((tq, tq), F32),
                        pltpu.VMEM((1, tq), F32), pltpu.VMEM((1, tq), F32),
                        pltpu.VMEM((2, 1, tq), F32), pltpu.VMEM((2, 1, tq), F32),
                        pltpu.VMEM((2, V_HEAD, tq), F32)],
        compiler_params=_params(vmem, 3),
        name="mla_attention",
    )(qt, k, vt)


def _merge_kernel(x_ref, mod_ref, oa_ref, ob_ref, gates_ref, wa_ref, wb_ref, wo_ref, o_ref):
    D = x_ref.shape[-1]
    mod = mod_ref[...]
    gates = _sigmoid(gates_ref[...].astype(F32))
    ya =_dot(oa_ref[...], wa_ref[...])
    yb = _dot(ob_ref[...], wb_ref[...])
    y = (gates[:, :D] * ya + gates[:, D:] * yb).astype(BF16)
    o_ref[...] = x_ref[...] + (1.0 + mod[5:6]) * _dot(y, wo_ref[...])


def _merge(x, mod, oa, ob, gates, wa, wb, wo, l):
    B, S, D = x.shape
    tm = min(S, MERGE_TOKENS)
    na = oa.shape[-1]
    nbv = ob.shape[-1]
    vmem = (na + nbv + D) * D * 2 + 4 * tm * D * 4 + 2 * tm * (na + nbv + 2 * D) * 2 + 4 * tm * D * 4
    return pl.pallas_call(
        _merge_kernel,
        grid=(B, S // tm),
        in_specs=[
            pl.BlockSpec((None, tm, D), lambda b, s: (b, s, 0)),
            pl.BlockSpec((None, None, N_MOD, D), lambda b, s: (l, b, 0, 0)),
            pl.BlockSpec((None, tm, na), lambda b, s: (b, s, 0)),
            pl.BlockSpec((None, tm, nbv), lambda b, s: (b, s, 0)),
            pl.BlockSpec((None, tm, 2 * D), lambda b, s: (b, s, 0)),
            _resident((None, na, D), lambda b, s: (l, 0, 0)),
            _resident((None, nbv, D), lambda b, s: (l, 0, 0)),
            _resident((None, D, D), lambda b, s: (l, 0, 0)),
        ],
        out_specs=pl.BlockSpec((None, tm, D), lambda b, s: (b, s, 0)),
        out_shape=jax.ShapeDtypeStruct((B, S, D), F32),
        compiler_params=_params(vmem, 2),
        name="merge",
    )(x, mod, oa, ob, gates, wa, wb, wo)


def _pack_w_in(w_in):
    conv_c = 2 * GDN_HEADS * GDN_DK + GDN_HEADS * GDN_DV
    gdn_v = GDN_HEADS * GDN_DV
    D = w_in.shape[1]
    sizes = (conv_c, gdn_v, GDN_HEADS, GDN_HEADS, Q_LORA, KV_LORA, QK_ROPE, D, D)
    offs = np.cumsum((0,) + sizes)
    qkv, z, b_raw, a_raw, q_lat, kv_lat, k_pe, gate_a, gate_b = (
        w_in[:, :, int(offs[i]):int(offs[i + 1])] for i in range(len(sizes)))
    n_lat = Q_LORA + KV_LORA + QK_ROPE
    pad = (-n_lat) % V7X_LANES
    lat = jnp.concatenate([q_lat, kv_lat, k_pe, jnp.zeros(w_in.shape[:2] + (pad,), w_in.dtype)], axis=-1)
    ba = jnp.swapaxes(jnp.concatenate([b_raw, a_raw], axis=-1), 1, 2)
    wall = jnp.concatenate([qkv, z, gate_a, gate_b, lat], axis=-1)
    return wall.astype(BF16), ba.astype(BF16)


def _pack_kv_up(w_kv_up):
    L = w_kv_up.shape[0]
    w = w_kv_up.reshape(L, KV_LORA, MLA_HEADS, QK_NOPE + V_HEAD)
    wk = w[..., :QK_NOPE].reshape(L, KV_LORA, MLA_HEADS * QK_NOPE)
    wv = w[..., QK_NOPE:].reshape(L, KV_LORA, MLA_HEADS * V_HEAD)
    return wk.astype(BF16), jnp.swapaxes(wv, 1, 2).astype(BF16)


def kernel(x, c, positions, ada_w, ada_b, norm_ffn1, ffn1_w1, ffn1_w3, ffn1_w2, norm_mix, w_in, gdn_conv, gdn_a_log, gdn_dt_bias, gdn_out_gain, mla_q_lat_gain, mla_kv_lat_gain, mla_w_q_up, mla_w_kv_up, mla_q_norm, mla_k_norm, w_branch_a, w_branch_b, w_out, norm_ffn2, ffn2_w1, ffn2_w3, ffn2_w2):
    L = ada_w.shape[0]
    S = x.shape[1]
    assert S % CHUNK == 0 and S % min(S, GDN_TOKENS) == 0 and S % min(S, FFN_TOKENS) == 0
    assert S % min(S, MERGE_TOKENS) == 0 and S % min(S, ATTN_Q) == 0

    mod = _ada_mod(c, ada_w, ada_b)
    cos2, sin2, cost, sint = _rope_tables(positions)

    row = lambda a: a.reshape(L, 1, a.shape[-1])
    col = lambda a: a.reshape(L, a.shape[-1], 1)
    bf = lambda a: a.astype(BF16)
    wall, wba = _pack_w_in(w_in)
    wqt = bf(jnp.swapaxes(mla_w_q_up, 1, 2))
    wk, wvt = _pack_kv_up(mla_w_kv_up)
    f1 = (bf(ffn1_w1), bf(ffn1_w3), bf(ffn1_w2))
    f2 = (bf(ffn2_w1), bf(ffn2_w3), bf(ffn2_w2))
    wa, wb, wo = bf(w_branch_a), bf(w_branch_b), bf(w_out)
    g_ffn1, g_mix, g_ffn2 = row(norm_ffn1), row(norm_mix), row(norm_ffn2)
    og, qlg, kvlg = row(gdn_out_gain), row(mla_q_lat_gain), row(mla_kv_lat_gain)
    qnc, kn = col(mla_q_norm), row(mla_k_norm)
    a_log, dt_bias = col(gdn_a_log), col(gdn_dt_bias)

    for l in range(L):
        x = _ffn(x, mod, g_ffn1, *f1, l, 0)
        qkv, zs, gates, lat, ba = _mixin(x, mod, g_mix, wall, wba, gdn_conv, l)
        grow, gcol = _gdn_gates(ba, a_log, dt_bias, l)
        o_gdn = _gdn(qkv, zs, grow, gcol, og, l)
        qt, k, vt = _mla_prep(lat, cos2, sin2, cost, sint, qlg, kvlg, wqt, wk, wvt, qnc, kn, l)
        o_mla = _attention(qt, k, vt)
        x = _merge(x, mod, o_gdn, o_mla, gates, wa, wb, wo, l)
        x = _ffn(x, mod, g_ffn2, *f2, l, 6)
    return x
```

```python
import functools
import math

import numpy as np
import jax
import jax.numpy as jnp
from jax import lax
from jax.experimental import pallas as pl
from jax.experimental.pallas import tpu as pltpu

F32 = jnp.float32
BF16 = jnp.bfloat16

EPS = 1e-6
N_MOD = 9
GDN_HEADS = 8
GDN_DK = 128
GDN_DV = 128
CONV_K = 4
CHUNK = 64
LOG2_CHUNK = 6
MLA_HEADS = 8
Q_LORA = 384
KV_LORA = 256
QK_NOPE = 128
QK_ROPE = 64
V_HEAD = 128
ROPE_THETA = 10000.0
MLA_QK = QK_NOPE + QK_ROPE

V7X_VMEM_BYTES = 64 * 1024 * 1024
V7X_LANES = 128
V7X_SUBLANES = 8
VMEM_CAP_BYTES = V7X_VMEM_BYTES - 6 * 1024 * 1024

FFN_TOKENS = 1024
MIX_TOKENS = 256
GATE_TOKENS = 512
GATE_BATCH = 4
GDN_TOKENS = 256
GDN_GROUP = 8
GDN_BATCH = 4
ATTN_Q = 512
ATTN_CHAIN = 8
MERGE_TOKENS = 1024
HALO = V7X_SUBLANES


def _params(vmem_bytes, n_axes):
    limit = int(min(VMEM_CAP_BYTES, max(vmem_bytes, 16 * 1024 * 1024)))
    return pltpu.CompilerParams(dimension_semantics=("arbitrary",) * n_axes, vmem_limit_bytes=limit)


def _resident(block_shape, index_map):
    return pl.BlockSpec(block_shape, index_map, pipeline_mode=pl.Buffered(1))


def _dot(a, b):
    return jnp.dot(a, b, preferred_element_type=F32)


def _dot_nt(a, b):
    return lax.dot_general(a, b, (((1,), (1,)), ((), ())), preferred_element_type=F32)


def _sigmoid(x):
    return 0.5 * jnp.tanh(0.5 * x) + 0.5


def _silu(x):
    hx = 0.5 * x
    return hx * jnp.tanh(hx) + hx


def _modulate(x, gain, shift, scale):
    y = x * lax.rsqrt(jnp.mean(x * x, axis=-1, keepdims=True) + EPS)
    return (y * gain) * (1.0 + scale) + shift


def _split3(x):
    x1 = x.astype(BF16)
    r1 = x - x1.astype(F32)
    x2 = r1.astype(BF16)
    x3 = (r1 - x2.astype(F32)).astype(BF16)
    return x1, x2, x3


def _mod_kernel(c_ref, w_ref, b_ref, o_ref):
    cond = _silu(c_ref[...]).astype(BF16)
    o_ref[...] = _dot(cond, w_ref[...].astype(BF16)) + b_ref[...]


def _ada_mod(c, ada_w, ada_b):
    L, D, _ = ada_w.shape
    B = c.shape[0]
    ada_b3 = ada_b.reshape(L, N_MOD, 1, D)
    out = pl.pallas_call(
        _mod_kernel,
        grid=(L, N_MOD),
        in_specs=[
            pl.BlockSpec((B, D), lambda l, j: (0, 0)),
            pl.BlockSpec((None, D, D), lambda l, j: (l, 0, j)),
            pl.BlockSpec((None, None, 1, D), lambda l, j: (l, j, 0, 0)),
        ],
        out_specs=pl.BlockSpec((None, None, B, D), lambda l, j: (l, j, 0, 0)),
        out_shape=jax.ShapeDtypeStruct((L, N_MOD, B, D), F32),
        compiler_params=_params(4 * D * D * 4, 2),
        name="ada_mod",
    )(c, ada_w, ada_b3)
    return jnp.transpose(out, (0, 2, 1, 3))


def _rope_kernel(pos_ref, post_ref, freq_ref, sign_ref, freqc_ref, cos_ref, sin_ref, cost_ref, sint_ref):
    ang = pos_ref[...].astype(F32) * freq_ref[...]
    cos_ref[...] = jnp.cos(ang)
    sin_ref[...] = jnp.sin(ang) * sign_ref[...]
    angt = freqc_ref[...] * post_ref[...].astype(F32)
    cost_ref[...] = jnp.cos(angt)
    sint_ref[...] = jnp.sin(angt)


def _rope_tables(positions):
    B, S = positions.shape
    half = QK_ROPE // 2
    inv_freq = ROPE_THETA ** (-jnp.arange(half, dtype=F32) / half)
    freq2 = jnp.concatenate([inv_freq, inv_freq]).reshape(1, QK_ROPE)
    sign2 = jnp.concatenate([-jnp.ones((half,), F32), jnp.ones((half,), F32)]).reshape(1, QK_ROPE)
    ts = min(S, 1024)
    return pl.pallas_call(
        _rope_kernel,
        grid=(B, S // ts),
        in_specs=[
            pl.BlockSpec((None, ts, 1), lambda b, s: (b, s, 0)),
            pl.BlockSpec((None, 1, ts), lambda b, s: (b, 0, s)),
            pl.BlockSpec((1, QK_ROPE), lambda b, s: (0, 0)),
            pl.BlockSpec((1, QK_ROPE), lambda b, s: (0, 0)),
            pl.BlockSpec((half, 1), lambda b, s: (0, 0)),
        ],
        out_specs=[pl.BlockSpec((None, ts, QK_ROPE), lambda b, s: (b, s, 0))] * 2
        + [pl.BlockSpec((None, half, ts), lambda b, s: (b, 0, s))] * 2,
        out_shape=[jax.ShapeDtypeStruct((B, S, QK_ROPE), F32)] * 2
        + [jax.ShapeDtypeStruct((B, half, S), F32)] * 2,
        compiler_params=_params(8 * ts * V7X_LANES * 4, 2),
        name="rope_tables",
    )(positions.reshape(B, S, 1), positions.reshape(B, 1, S), freq2, sign2, inv_freq.reshape(half, 1))


def _ffn_kernel(x_ref, mod_ref, gain_ref, w1_ref, w3_ref, w2_ref, o_ref, *, row):
    x = x_ref[...]
    mod = mod_ref[...]
    h = _modulate(x, gain_ref[...], mod[row:row + 1], mod[row + 1:row + 2]).astype(BF16)
    a = _dot(h, w1_ref[...])
    b = _dot(h, w3_ref[...])
    g = (_silu(a) * b).astype(BF16)
    y = _dot(g, w2_ref[...])
    o_ref[...] = x + (0.5 * (1.0 + mod[row + 2:row + 3])) * y


def _ffn(x, mod, gain, w1, w3, w2, l, row):
    B, S, D = x.shape
    F = w1.shape[-1]
    tm = min(S, FFN_TOKENS)
    vmem = 3 * D * F * 2 + 4 * tm * D * 4 + 3 * tm * F * 4 + 2 * tm * D * 4
    return pl.pallas_call(
        functools.partial(_ffn_kernel, row=row),
        grid=(B, S // tm),
        in_specs=[
            pl.BlockSpec((None, tm, D), lambda b, s: (b, s, 0)),
            pl.BlockSpec((None, None, N_MOD, D), lambda b, s: (l, b, 0, 0)),
            pl.BlockSpec((None, 1, D), lambda b, s: (l, 0, 0)),
            _resident((None, D, F), lambda b, s: (l, 0, 0)),
            _resident((None, D, F), lambda b, s: (l, 0, 0)),
            _resident((None, F, D), lambda b, s: (l, 0, 0)),
        ],
        out_specs=pl.BlockSpec((None, tm, D), lambda b, s: (b, s, 0)),
        out_shape=jax.ShapeDtypeStruct((B, S, D), F32),
        compiler_params=_params(vmem, 2),
        name="ffn",
    )(x, mod, gain, w1, w3, w2)


def _mixin_kernel(x_ref, mod_ref, gain_ref, wall_ref, wba_ref, conv_ref,
                  qkv_ref, zs_ref, gates_ref, lat_ref, ba_ref, pbuf_ref, *, tm):
    @pl.when(pl.program_id(1) == 0)
    def _():
        pbuf_ref[0:HALO, :] = jnp.zeros((HALO, pbuf_ref.shape[1]), F32)

    x = x_ref[...]
    mod = mod_ref[...]
    h = _modulate(x, gain_ref[...], mod[3:4], mod[4:5]).astype(BF16)

    C = pbuf_ref.shape[1]
    nz = zs_ref.shape[0] * zs_ref.shape[2]
    ng = gates_ref.shape[1]
    proj = _dot(h, wall_ref[...])
    pbuf_ref[HALO:HALO + tm, :] = proj[:, :C]
    conv = conv_ref[...]
    assert CONV_K == 4
    n_qk = 2 * GDN_HEADS
    for cb in range(pbuf_ref.shape[1] // V7X_LANES):
        c0 = cb * V7X_LANES
        xs = pbuf_ref[:, c0:c0 + V7X_LANES]
        w = [conv[j:j + 1, c0:c0 + V7X_LANES] for j in range(CONV_K)]
        x1 = pltpu.roll(xs, 1, axis=0)
        near = w[3] * xs + w[2] * x1
        far = w[1] * xs + w[0] * x1
        y = _silu(near[HALO:] + pltpu.roll(far, 2, axis=0)[HALO:])
        if cb < n_qk:
            y = y * lax.rsqrt(jnp.sum(y * y, axis=-1, keepdims=True) + EPS)
        qkv_ref[cb] = y.astype(BF16)
    pbuf_ref[0:HALO, :] = pbuf_ref[tm:tm + HALO, :]

    for hh in range(GDN_HEADS):
        zs_ref[hh] = proj[:, C + hh * GDN_DV:C + (hh + 1) * GDN_DV].astype(BF16)
    gates_ref[...] = proj[:, C + nz:C + nz + ng].astype(BF16)
    lat_ref[...] = proj[:, C + nz + ng:]
    ba_ref[...] = _dot_nt(wba_ref[...], h)


def _mixin(x, mod, gain, wall, wba, conv, l):
    B, S, D = x.shape
    C = conv.shape[-1]
    nz = GDN_HEADS * GDN_DV
    ng = 2 * D
    nl = wall.shape[-1] - C - nz - ng
    nb = wba.shape[1]
    tm = min(S, MIX_TOKENS)
    ncb = C // V7X_LANES
    vmem = ((C + nz + ng + nl) * D * 2 + 2 * tm * D * 4 + 2 * tm * (C + nz + ng) * 2 + 2 * tm * nl * 4
            + (tm + HALO) * C * 4 + tm * C * 4 + tm * (nz + ng) * 4 + 4 * 1024 * 1024)
    return pl.pallas_call(
        functools.partial(_mixin_kernel, tm=tm),
        grid=(B, S // tm),
        in_specs=[
            pl.BlockSpec((None, tm, D), lambda b, s: (b, s, 0)),
            pl.BlockSpec((None, None, N_MOD, D), lambda b, s: (l, b, 0, 0)),
            pl.BlockSpec((None, 1, D), lambda b, s: (l, 0, 0)),
            _resident((None, D, C + nz + ng + nl), lambda b, s: (l, 0, 0)),
            _resident((None, nb, D), lambda b, s: (l, 0, 0)),
            pl.BlockSpec((None, CONV_K, C), lambda b, s: (l, 0, 0)),
        ],
        out_specs=[
            pl.BlockSpec((None, ncb, tm, V7X_LANES), lambda b, s: (b, 0, s, 0)),
            pl.BlockSpec((None, GDN_HEADS, tm, GDN_DV), lambda b, s: (b, 0, s, 0)),
            pl.BlockSpec((None, tm, ng), lambda b, s: (b, s, 0)),
            pl.BlockSpec((None, tm, nl), lambda b, s: (b, s, 0)),
            pl.BlockSpec((None, nb, tm), lambda b, s: (b, 0, s)),
        ],
        out_shape=[
            jax.ShapeDtypeStruct((B, ncb, S, V7X_LANES), BF16),
            jax.ShapeDtypeStruct((B, GDN_HEADS, S, GDN_DV), BF16),
            jax.ShapeDtypeStruct((B, S, ng), BF16),
            jax.ShapeDtypeStruct((B, S, nl), F32),
            jax.ShapeDtypeStruct((B, nb, S), F32),
        ],
        scratch_shapes=[pltpu.VMEM((tm + HALO, C), F32)],
        compiler_params=_params(vmem, 2),
        name="mixin",
    )(x, mod, gain, wall, wba, conv)


def _gate_kernel(ba_ref, alog_ref, dtb_ref, grow_ref, gcol_ref, *, tm, NB):
    H = GDN_HEADS
    betas, gs = [], []
    for bb in range(NB):
        ba = ba_ref[bb]
        betas.append(_sigmoid(ba[0:H]))
        xa = ba[H:2 * H] + dtb_ref[...]
        softplus = jnp.maximum(xa, 0.0) + jnp.log(1.0 + jnp.exp(-jnp.abs(xa)))
        gs.append(-jnp.exp(alog_ref[...]) * softplus)
    beta = jnp.concatenate(betas, axis=0)
    g = jnp.concatenate(gs, axis=0)

    ii = lax.broadcasted_iota(jnp.int32, (tm, tm), 0)
    jj = lax.broadcasted_iota(jnp.int32, (tm, tm), 1)
    same = (ii >> LOG2_CHUNK) == (jj >> LOG2_CHUNK)
    upper =jnp.where(same & (ii <= jj), 1.0, 0.0).astype(BF16)
    block = jnp.where(same, 1.0, 0.0).astype(BF16)
    eye = jnp.where(ii == jj, 1.0, 0.0).astype(BF16)
    gam = None
    glast = None
    for part in _split3(g):
        t1 = _dot(part, upper)
        t2 = _dot(part, block)
        gam = t1 if gam is None else gam + t1
        glast = t2 if glast is None else glast + t2
    egam = jnp.exp(gam)
    ekd = jnp.exp(glast - gam)
    egl = jnp.exp(glast)
    zero = jnp.zeros_like(gam[0:1])
    rows = []
    for n in range(NB * H):
        r = jnp.concatenate([beta[n:n + 1], gam[n:n + 1], egam[n:n + 1], ekd[n:n + 1],
                             egl[n:n + 1], zero, zero, zero], axis=0)
        grow_ref[n // H, n % H] = r
        rows.append(r)
    allrows = jnp.concatenate(rows, axis=0)
    cols = None
    for part in _split3(allrows):
        t = _dot_nt(eye, part)
        cols = t if cols is None else cols + t
    for bb in range(NB):
        gcol_ref[bb] = cols[:, 8 * H * bb:8 * H * (bb + 1)]


def _gdn_gates(ba, a_log, dt_bias, l):
    B, nb, S = ba.shape
    H = GDN_HEADS
    NB = math.gcd(B, GATE_BATCH)
    tm = min(S, GATE_TOKENS)
    vmem = 8 * tm * tm * 4 + 6 * NB * H * tm * V7X_LANES * 4
    return pl.pallas_call(
        functools.partial(_gate_kernel, tm=tm, NB=NB),
        grid=(B // NB, S // tm),
        in_specs=[
            pl.BlockSpec((NB, nb, tm), lambda b, s: (b, 0, s)),
            pl.BlockSpec((None, H, 1), lambda b, s: (l, 0, 0)),
            pl.BlockSpec((None, H, 1), lambda b, s: (l, 0, 0)),
        ],
        out_specs=[
            pl.BlockSpec((NB, H, 8, tm), lambda b, s: (b, 0, 0, s)),
            pl.BlockSpec((NB, tm, 8 * H), lambda b, s: (b, s, 0)),
        ],
        out_shape=[
            jax.ShapeDtypeStruct((B, H, 8, S), F32),
            jax.ShapeDtypeStruct((B, S, 8 * H), F32),
        ],
        compiler_params=_params(vmem, 2),
        name="gdn_gates",
    )(ba, a_log, dt_bias)


def _gdn_kernel(q_ref, k_ref, v_ref, zs_ref, grow_ref, gcol_ref, og_ref, o_ref, state_ref, *, tc, G, NB):
    s = pl.program_id(2)

    @pl.when(s == 0)
    def _():
        state_ref[...] = jnp.zeros(state_ref.shape, F32)

    heads = range(NB * G)

    def at(ref, n):
        return ref[n // G, n % G]

    n_chunks = tc // CHUNK
    scale = GDN_DK ** -0.5
    ii = lax.broadcasted_iota(jnp.int32, (tc, tc), 0)
    jj = lax.broadcasted_iota(jnp.int32, (tc, tc), 1)
    same = (ii >> LOG2_CHUNK) == (jj >> LOG2_CHUNK)
    causal = same & (ii >= jj)
    strict = same & (ii > jj)
    lane = lax.broadcasted_iota(jnp.int32, (GDN_DK, tc), 1) >> LOG2_CHUNK

    gc = [gcol_ref[g // G][:, 8 * (g % G):8 * (g % G) + 8] for g in heads]
    gr = [at(grow_ref, g) for g in heads]
    k = [at(k_ref, g) for g in heads]
    kf = [k[g].astype(F32) for g in heads]
    kb = [kf[g] * gc[g][:, 0:1] for g in heads]
    decay = [jnp.where(causal, jnp.exp(jnp.minimum(gc[g][:, 1:2] - gr[g][1:2, :], 0.0)), 0.0) for g in heads]

    akq = [_dot_nt(jnp.concatenate([kb[g].astype(BF16), at(q_ref, g)], axis=0), k[g]) for g in heads]
    m = [jnp.where(strict, akq[g][:tc] * decay[g], 0.0) for g in heads]
    qk = [(akq[g][tc:] * (decay[g] * scale)).astype(BF16) for g in heads]

    rlane = lax.broadcasted_iota(jnp.int32, (CHUNK, tc), 1)
    rrow = lax.broadcasted_iota(jnp.int32, (CHUNK, tc), 0)
    eye_row = jnp.where((rlane & (CHUNK - 1)) == rrow, 1.0, 0.0)
    rblk = rlane >> LOG2_CHUNK

    def blockdiag(x_row):
        return jnp.concatenate([jnp.where(rblk == c, x_row, 0.0) for c in range(n_chunks)], axis=0).astype(BF16)

    def side_by_side(x):
        acc = x[0:CHUNK]
        for c in range(1, n_chunks):
            acc = acc + x[c * CHUNK:(c + 1) * CHUNK]
        return acc

    m_row = [side_by_side(m[g]) for g in heads]
    p = [_dot(m_row[g].astype(BF16), blockdiag(m_row[g])) for g in heads]
    t_row = [eye_row - m_row[g] for g in heads]
    for _ in range(LOG2_CHUNK - 2):
        r = [_dot(jnp.concatenate([p[g], t_row[g]], axis=0).astype(BF16), blockdiag(p[g])) for g in heads]
        p = [r[g][:CHUNK] for g in heads]
        t_row = [t_row[g] + r[g][CHUNK:] for g in heads]
    t_row = [t_row[g] + _dot(t_row[g].astype(BF16), blockdiag(p[g])) for g in heads]

    rhs = [jnp.concatenate([at(v_ref, g).astype(F32) * gc[g][:, 0:1], kb[g] * gc[g][:, 2:3]],
                           axis=1).astype(BF16) for g in heads]
    uw = [_dot(blockdiag(t_row[g]), rhs[g]).astype(BF16) for g in heads]

    kdt = [(kf[g] * gc[g][:, 3:4]).T for g in heads]
    pre_lhs = [jnp.concatenate([jnp.where(lane == c, kdt[g], 0.0).astype(BF16) for c in range(n_chunks)]
                               + [qk[g]], axis=0) for g in heads]
    pre = [_dot(pre_lhs[g], uw[g]) for g in heads]
    qd = [at(q_ref, g).astype(F32) * (gc[g][:, 2:3] * scale) for g in heads]
    base = n_chunks * GDN_DK
    qmod = [(qd[g] - pre[g][base:, GDN_DV:]).astype(BF16) for g in heads]

    states = [at(state_ref, g) for g in heads]
    outs = [[] for _ in heads]
    for c in range(n_chunks):
        r0 = c * CHUNK
        for g in heads:
            kblk = pre[g][c * GDN_DK:(c + 1) * GDN_DK]
            lhs = jnp.concatenate([kblk[:, GDN_DV:].astype(BF16), qmod[g][r0:r0 + CHUNK]], axis=0)
            r = _dot(lhs, states[g].astype(BF16))
            outs[g].append(r[GDN_DK:] + pre[g][base + r0:base + r0 + CHUNK, :GDN_DV])
            states[g] = states[g] * gr[g][4:5, r0:r0 + 1] - r[:GDN_DK] + kblk[:, :GDN_DV]
    for g in heads:
        bb, hh = g // G, g % G
        state_ref[bb, hh] = states[g]
        o = jnp.concatenate(outs[g], axis=0)
        o = o * lax.rsqrt(jnp.mean(o * o, axis=-1, keepdims=True) + EPS) * og_ref[...]
        o_ref[bb, :, hh * GDN_DV:(hh + 1) * GDN_DV] = (o * _silu(zs_ref[bb, hh].astype(F32))).astype(BF16)


def _gdn(qkv, zs, grow, gcol, out_gain, l):
    B, _, S, _ = qkv.shape
    H = GDN_HEADS
    G = GDN_GROUP
    NB = math.gcd(B, GDN_BATCH)
    assert G == H, "the gate columns are read per head from one lane-packed block"
    tc = min(S, GDN_TOKENS)
    vmem = NB * G * (24 * tc * tc * 4 + 24 * tc * V7X_LANES * 4)
    return pl.pallas_call(
        functools.partial(_gdn_kernel, tc=tc, G=G, NB=NB),
        grid=(B // NB, H // G, S // tc),
        in_specs=[
            pl.BlockSpec((NB, G, tc, GDN_DK), lambda b, h, s: (b, h, s, 0)),
            pl.BlockSpec((NB, G, tc, GDN_DK), lambda b, h, s: (b, H // G + h, s, 0)),
            pl.BlockSpec((NB, G, tc, GDN_DV), lambda b, h, s: (b, 2 * H // G + h, s, 0)),
            pl.BlockSpec((NB, G, tc, GDN_DV), lambda b, h, s: (b, h, s, 0)),
            pl.BlockSpec((NB, G, 8, tc), lambda b, h, s: (b, h, 0, s)),
            pl.BlockSpec((NB, tc, 8 * H), lambda b, h, s: (b, s, 0)),
            pl.BlockSpec((None, 1, GDN_DV), lambda b, h, s: (l, 0, 0)),
        ],
        out_specs=pl.BlockSpec((NB, tc, G * GDN_DV), lambda b, h, s: (b, s, h)),
        out_shape=jax.ShapeDtypeStruct((B, S, H * GDN_DV), BF16),
        scratch_shapes=[pltpu.VMEM((NB, G, GDN_DK, GDN_DV), F32)],
        compiler_params=_params(vmem, 3),
        name="gdn",
    )(qkv, qkv, qkv, zs, grow, gcol, out_gain)


def _swap_halves(y):
    half = y.shape[-1] // 2
    return jnp.concatenate([y[:, half:], y[:, :half]], axis=1)


def _mla_prep_kernel(lat_ref, cos_ref, sin_ref, cost_ref, sint_ref, qg_ref, kvg_ref, wqt_ref, wk_ref, wvt_ref,
                     qnc_ref, kn_ref, qt_ref, k_ref, vt_ref):
    H = MLA_HEADS
    half = QK_ROPE // 2
    lat = lat_ref[...]
    ql = lat[:, :Q_LORA]
    kvl = lat[:, Q_LORA:Q_LORA + KV_LORA]
    kpe = lat[:, Q_LORA + KV_LORA:Q_LORA + KV_LORA + QK_ROPE]
    qln = (ql * lax.rsqrt(jnp.mean(ql * ql, axis=-1, keepdims=True) + EPS) * qg_ref[...]).astype(BF16)
    kvn = (kvl * lax.rsqrt(jnp.mean(kvl * kvl, axis=-1, keepdims=True) + EPS) * kvg_ref[...]).astype(BF16)
    qt = _dot_nt(wqt_ref[...], qln)
    knope = _dot(kvn, wk_ref[...])
    vt = _dot_nt(wvt_ref[...], kvn)

    cost = cost_ref[...]
    sint = sint_ref[...]
    qnc = qnc_ref[...]
    scale = MLA_QK ** -0.5 * math.log2(math.e)
    for hh in range(H):
        blk = qt[hh * MLA_QK:(hh + 1) * MLA_QK]
        r = lax.rsqrt(jnp.sum(blk * blk, axis=0, keepdims=True) * (1.0 / MLA_QK) + EPS) * scale
        y = blk * qnc * r
        y1 = y[QK_NOPE:QK_NOPE + half]
        y2 = y[QK_NOPE + half:]
        qt_ref[hh, 0:QK_NOPE, :] = y[:QK_NOPE].astype(BF16)
        qt_ref[hh, QK_NOPE:QK_NOPE + half, :] = (y1 * cost - y2 * sint).astype(BF16)
        qt_ref[hh, QK_NOPE + half:MLA_QK, :] = (y2 * cost + y1 * sint).astype(BF16)
        vt_ref[hh] = vt[hh * V_HEAD:(hh + 1) * V_HEAD].astype(BF16)

    kn = kn_ref[...]
    yk = kpe * kn[:, QK_NOPE:]
    kr = yk * cos_ref[...] + _swap_halves(yk) * sin_ref[...]
    kpe_ss = jnp.sum(kpe * kpe, axis=-1, keepdims=True)
    for hh in range(H):
        k_nope = knope[:, hh * QK_NOPE:(hh + 1) * QK_NOPE]
        ss = jnp.sum(k_nope * k_nope, axis=-1, keepdims=True) + kpe_ss
        r = lax.rsqrt(ss * (1.0 / MLA_QK) + EPS)
        k_ref[hh, :, 0:QK_NOPE] = (k_nope * r * kn[:, :QK_NOPE]).astype(BF16)
        k_ref[hh, :, QK_NOPE:MLA_QK] = (kr * r).astype(BF16)


def _mla_prep(lat, cos2, sin2, cost, sint, q_lat_gain, kv_lat_gain, wqt, wk, wvt, q_norm_col, k_norm, l):
    B, S, nl = lat.shape
    H = MLA_HEADS
    half = QK_ROPE // 2
    tm = min(S, ATTN_Q)
    nq = wqt.shape[1]
    nk = wk.shape[-1]
    nv = wvt.shape[1]
    vmem = ((nq * Q_LORA + (nk + nv) * KV_LORA) * 2 + 2 * tm * nl * 4 + 3 * tm * (nq + nk + nv) * 4
            + 4 * H * tm * (256 + MLA_QK + V_HEAD) * 2)
    return pl.pallas_call(
        _mla_prep_kernel,
        grid=(B, S // tm),
        in_specs=[
            pl.BlockSpec((None, tm, nl), lambda b, s: (b, s, 0)),
            pl.BlockSpec((None, tm, QK_ROPE), lambda b, s: (b, s, 0)),
            pl.BlockSpec((None, tm, QK_ROPE), lambda b, s: (b, s, 0)),
            pl.BlockSpec((None, half, tm), lambda b, s: (b, 0, s)),
            pl.BlockSpec((None, half, tm), lambda b, s: (b, 0, s)),
            pl.BlockSpec((None, 1, Q_LORA), lambda b, s: (l, 0, 0)),
            pl.BlockSpec((None, 1, KV_LORA), lambda b, s: (l, 0, 0)),
            _resident((None, nq, Q_LORA), lambda b, s: (l, 0, 0)),
            _resident((None, KV_LORA, nk), lambda b, s: (l, 0, 0)),
            _resident((None, nv, KV_LORA), lambda b, s: (l, 0, 0)),
            pl.BlockSpec((None, MLA_QK, 1), lambda b, s: (l, 0, 0)),
            pl.BlockSpec((None, 1, MLA_QK), lambda b, s: (l, 0, 0)),
        ],
        out_specs=[
            pl.BlockSpec((None, H, MLA_QK, tm), lambda b, s: (b, 0, 0, s)),
            pl.BlockSpec((None, H, tm, MLA_QK), lambda b, s: (b, 0, s, 0)),
            pl.BlockSpec((None, H, None, V_HEAD, tm), lambda b, s: (b, 0, s, 0, 0)),
        ],
        out_shape=[
            jax.ShapeDtypeStruct((B, H, MLA_QK, S), BF16),
            jax.ShapeDtypeStruct((B, H, S, MLA_QK), BF16),
            jax.ShapeDtypeStruct((B, H, S // tm, V_HEAD, tm), BF16),
        ],
        compiler_params=_params(vmem, 2),
        name="mla_prep",
    )(lat, cos2, sin2, cost, sint, q_lat_gain, kv_lat_gain, wqt, wk, wvt, q_norm_col, k_norm)


def _attn_kernel(qt_ref, k_ref, vt_ref, o_ref, sa_ref, sb_ref, mxa_ref, mxb_ref, m_ref, l_ref, acc_ref, *,
                 tq, chain, single):
    c = 0 if single else pl.program_id(2)
    bufs = ((sa_ref, mxa_ref), (sb_ref, mxb_ref))

    def scores(qt, j, buf):
        start = j * tq if isinstance(j, int) else pl.multiple_of(j * tq, tq)
        st = _dot(k_ref[pl.ds(start, tq), :], qt)
        buf[0][...] = st
        buf[1][...] = jnp.max(st, axis=0, keepdims=True)

    def reset(w):
        m_ref[w] = jnp.full(m_ref.shape[1:], -jnp.inf, F32)
        l_ref[w] = jnp.zeros(l_ref.shape[1:], F32)
        acc_ref[w] = jnp.zeros(acc_ref.shape[1:], F32)

    def update(w, j, buf, masked=False):
        st = buf[0][...]
        mx = buf[1][...]
        if masked:
            kpos = lax.broadcasted_iota(jnp.int32, (tq, tq), 0)
            qpos = lax.broadcasted_iota(jnp.int32, (tq, tq), 1)
            st = jnp.where(kpos <= qpos, st, -jnp.inf)
            mx = jnp.max(st, axis=0, keepdims=True)
        m = m_ref[w]
        m_new = jnp.maximum(m, mx)
        alpha = jnp.exp2(m - m_new)
        pr = jnp.exp2(st - m_new)
        m_ref[w] = m_new
        l_ref[w] = alpha * l_ref[w] + jnp.sum(pr, axis=0, keepdims=True)
        acc_ref[w] = alpha * acc_ref[w] + _dot(vt_ref[j], pr.astype(BF16))

    def full_blocks(w, qt, cur, nxt, trips):
        def body(t, _):
            scores(qt, 2 * t + 1, nxt)
            update(w, 2 * t, cur)
            scores(qt, 2 * t + 2, cur)
            update(w, 2 * t + 1, nxt)
            return 0
        if isinstance(trips, int):
            for t in range(trips):
                body(t, 0)
        else:
            lax.fori_loop(0, trips, body, 0)

    def q_block(u):
        return qt_ref[:, u * tq:(u + 1) * tq]

    cur, nxt = bufs
    reset(0)
    scores(q_block(0), 0, cur)
    for u in range(chain):
        w = u % 2
        i = c * chain + u
        full_blocks(w, q_block(u), cur, nxt, c * (chain // 2) + u // 2)
        if u % 2 == 1:
            scores(q_block(u), i, nxt)
            update(w, i - 1, cur)
            cur, nxt = nxt, cur
        if u + 1 < chain:
            reset(1 - w)
            scores(q_block(u + 1), 0, nxt)
        update(w, i, cur, masked=True)
        o_ref[u * tq:(u + 1) * tq, :] = (acc_ref[w] / l_ref[w]).T.astype(BF16)
        cur, nxt = nxt, cur


def _attention(qt, k, vt):
    B, H, S, _ = k.shape
    tq = vt.shape[-1]
    nq = S // tq
    chain = min(nq, ATTN_CHAIN)
    assert chain % 2 == 0 and nq % chain == 0, "q blocks are taken in even-length runs"
    vmem = (8 * tq * tq * 4 + 4 * S * (256 + V_HEAD) * 2 + 4 * chain * tq * (256 + V_HEAD) * 2
            + 16 * tq * V7X_LANES * 4)
    return pl.pallas_call(
        functools.partial(_attn_kernel, tq=tq, chain=chain, single=(nq == chain)),
        grid=(B, H, nq // chain),
        in_specs=[
            pl.BlockSpec((None, None, MLA_QK, chain * tq), lambda b, h, i: (b, h, 0, i)),
            pl.BlockSpec((None, None, S, MLA_QK), lambda b, h, i: (b, h, 0, 0)),
            pl.BlockSpec((None, None, S // tq, V_HEAD, tq), lambda b, h, i: (b, h, 0, 0, 0)),
        ],
        out_specs=pl.BlockSpec((None, chain * tq, V_HEAD), lambda b, h, i: (b, i, h)),
        out_shape=jax.ShapeDtypeStruct((B, S, H * V_HEAD), BF16),
        scratch_shapes=[pltpu.VMEM((tq, tq), F32), pltpu.VMEM((tq, tq), F32),
                        pltpu.VMEM((1, tq), F32), pltpu.VMEM((1, tq), F32),
                        pltpu.VMEM((2, 1, tq), F32), pltpu.VMEM((2, 1, tq), F32),
                        pltpu.VMEM((2, V_HEAD, tq), F32)],
        compiler_params=_params(vmem, 3),
        name="mla_attention",
    )(qt, k, vt)


def _merge_kernel(x_ref, mod_ref, oa_ref, ob_ref, gates_ref, wa_ref, wb_ref, wo_ref, o_ref):
    D = x_ref.shape[-1]
    mod = mod_ref[...]
    gates = _sigmoid(gates_ref[...].astype(F32))
    ya =_dot(oa_ref[...], wa_ref[...])
    yb = _dot(ob_ref[...], wb_ref[...])
    y = (gates[:, :D] * ya + gates[:, D:] * yb).astype(BF16)
    o_ref[...] = x_ref[...] + (1.0 + mod[5:6]) * _dot(y, wo_ref[...])


def _merge(x, mod, oa, ob, gates, wa, wb, wo, l):
    B, S, D = x.shape
    tm = min(S, MERGE_TOKENS)
    na = oa.shape[-1]
    nbv = ob.shape[-1]
    vmem = (na + nbv + D) * D * 2 + 4 * tm * D * 4 + 2 * tm * (na + nbv + 2 * D) * 2 + 4 * tm * D * 4
    return pl.pallas_call(
        _merge_kernel,
        grid=(B, S // tm),
        in_specs=[
            pl.BlockSpec((None, tm, D), lambda b, s: (b, s, 0)),
            pl.BlockSpec((None, None, N_MOD, D), lambda b, s: (l, b, 0, 0)),
            pl.BlockSpec((None, tm, na), lambda b, s: (b, s, 0)),
            pl.BlockSpec((None, tm, nbv), lambda b, s: (b, s, 0)),
            pl.BlockSpec((None, tm, 2 * D), lambda b, s: (b, s, 0)),
            _resident((None, na, D), lambda b, s: (l, 0, 0)),
            _resident((None, nbv, D), lambda b, s: (l, 0, 0)),
            _resident((None, D, D), lambda b, s: (l, 0, 0)),
        ],
        out_specs=pl.BlockSpec((None, tm, D), lambda b, s: (b, s, 0)),
        out_shape=jax.ShapeDtypeStruct((B, S, D), F32),
        compiler_params=_params(vmem, 2),
        name="merge",
    )(x, mod, oa, ob, gates, wa, wb, wo)


def _pack_w_in(w_in):
    conv_c = 2 * GDN_HEADS * GDN_DK + GDN_HEADS * GDN_DV
    gdn_v = GDN_HEADS * GDN_DV
    D = w_in.shape[1]
    sizes = (conv_c, gdn_v, GDN_HEADS, GDN_HEADS, Q_LORA, KV_LORA, QK_ROPE, D, D)
    offs = np.cumsum((0,) + sizes)
    w_in = w_in.astype(BF16)
    qkv, z, b_raw, a_raw, q_lat, kv_lat, k_pe, gate_a, gate_b = (
        w_in[:, :, int(offs[i]):int(offs[i + 1])] for i in range(len(sizes)))
    n_lat = Q_LORA + KV_LORA + QK_ROPE
    pad = (-n_lat) % V7X_LANES
    lat = jnp.concatenate([q_lat, kv_lat, k_pe, jnp.zeros(w_in.shape[:2] + (pad,), w_in.dtype)], axis=-1)
    ba = jnp.swapaxes(jnp.concatenate([b_raw, a_raw], axis=-1), 1, 2)
    wall = jnp.concatenate([qkv, z, gate_a, gate_b, lat], axis=-1)
    return wall, ba


def _pack_kv_up(w_kv_up):
    L = w_kv_up.shape[0]
    w = w_kv_up.astype(BF16).reshape(L, KV_LORA, MLA_HEADS, QK_NOPE + V_HEAD)
    wk = w[..., :QK_NOPE].reshape(L, KV_LORA, MLA_HEADS * QK_NOPE)
    wv = w[..., QK_NOPE:].reshape(L, KV_LORA, MLA_HEADS * V_HEAD)
    return wk, jnp.swapaxes(wv, 1, 2)


def kernel(x, c, positions, ada_w, ada_b, norm_ffn1, ffn1_w1, ffn1_w3, ffn1_w2, norm_mix, w_in, gdn_conv, gdn_a_log, gdn_dt_bias, gdn_out_gain, mla_q_lat_gain, mla_kv_lat_gain, mla_w_q_up, mla_w_kv_up, mla_q_norm, mla_k_norm, w_branch_a, w_branch_b, w_out, norm_ffn2, ffn2_w1, ffn2_w3, ffn2_w2):
    L = ada_w.shape[0]
    S = x.shape[1]
    assert S % CHUNK == 0 and S % min(S, GDN_TOKENS) == 0 and S % min(S, FFN_TOKENS) == 0
    assert S % min(S, MERGE_TOKENS) == 0 and S % min(S, ATTN_Q) == 0

    mod = _ada_mod(c, ada_w, ada_b)
    cos2, sin2, cost, sint = _rope_tables(positions)

    row = lambda a: a.reshape(L, 1, a.shape[-1])
    col = lambda a: a.reshape(L, a.shape[-1], 1)
    bf = lambda a: a.astype(BF16)
    wall, wba = _pack_w_in(w_in)
    wqt = jnp.swapaxes(bf(mla_w_q_up), 1, 2)
    wk, wvt = _pack_kv_up(mla_w_kv_up)
    f1 = (bf(ffn1_w1), bf(ffn1_w3), bf(ffn1_w2))
    f2 = (bf(ffn2_w1), bf(ffn2_w3), bf(ffn2_w2))
    wa, wb, wo = bf(w_branch_a), bf(w_branch_b), bf(w_out)
    g_ffn1, g_mix, g_ffn2 = row(norm_ffn1), row(norm_mix), row(norm_ffn2)
    og, qlg, kvlg = row(gdn_out_gain), row(mla_q_lat_gain), row(mla_kv_lat_gain)
    qnc, kn = col(mla_q_norm), row(mla_k_norm)
    a_log, dt_bias = col(gdn_a_log), col(gdn_dt_bias)

    for l in range(L):
        x = _ffn(x, mod, g_ffn1, *f1, l, 0)
        qkv, zs, gates, lat, ba = _mixin(x, mod, g_mix, wall, wba, gdn_conv, l)
        grow, gcol = _gdn_gates(ba, a_log, dt_bias, l)
        o_gdn = _gdn(qkv, zs, grow, gcol, og, l)
        qt, k, vt = _mla_prep(lat, cos2, sin2, cost, sint, qlg, kvlg, wqt, wk, wvt, qnc, kn, l)
        o_mla = _attention(qt, k, vt)
        x = _merge(x, mod, o_gdn, o_mla, gates, wa, wb, wo, l)
        x = _ffn(x, mod, g_ffn2, *f2, l, 6)
    return x
```

```python
import functools
import math

import numpy as np
import jax
import jax.numpy as jnp
from jax import lax
from jax.experimental import pallas as pl
from jax.experimental.pallas import tpu as pltpu

F32 = jnp.float32
BF16 = jnp.bfloat16

EPS = 1e-6
N_MOD = 9
GDN_HEADS = 8
GDN_DK = 128
GDN_DV = 128
CONV_K = 4
CHUNK = 64
LOG2_CHUNK = 6
MLA_HEADS = 8
Q_LORA = 384
KV_LORA = 256
QK_NOPE = 128
QK_ROPE = 64
V_HEAD = 128
ROPE_THETA = 10000.0
MLA_QK = QK_NOPE + QK_ROPE

V7X_VMEM_BYTES = 64 * 1024 * 1024
V7X_LANES = 128
V7X_SUBLANES = 8
VMEM_CAP_BYTES = V7X_VMEM_BYTES - 6 * 1024 * 1024

FFN_TOKENS = 1024
MIX_TOKENS = 256
GATE_TOKENS = 512
GATE_BATCH = 4
GDN_TOKENS = 256
GDN_GROUP = 8
GDN_BATCH = 4
ATTN_Q = 512
ATTN_CHAIN = 8
MERGE_TOKENS = 1024
HALO = V7X_SUBLANES


def _params(vmem_bytes, n_axes):
    limit = int(min(VMEM_CAP_BYTES, max(vmem_bytes, 16 * 1024 * 1024)))
    return pltpu.CompilerParams(dimension_semantics=("arbitrary",) * n_axes, vmem_limit_bytes=limit)


def _resident(block_shape, index_map):
    return pl.BlockSpec(block_shape, index_map, pipeline_mode=pl.Buffered(1))


def _dot(a, b):
    return jnp.dot(a, b, preferred_element_type=F32)


def _dot_nt(a, b):
    return lax.dot_general(a, b, (((1,), (1,)), ((), ())), preferred_element_type=F32)


def _sigmoid(x):
    return 0.5 * jnp.tanh(0.5 * x) + 0.5


def _silu(x):
    hx = 0.5 * x
    return hx * jnp.tanh(hx) + hx


def _modulate(x, gain, shift, scale):
    y = x * lax.rsqrt(jnp.mean(x * x, axis=-1, keepdims=True) + EPS)
    return (y * gain) * (1.0 + scale) + shift


def _split3(x):
    x1 = x.astype(BF16)
    r1 = x - x1.astype(F32)
    x2 = r1.astype(BF16)
    x3 = (r1 - x2.astype(F32)).astype(BF16)
    return x1, x2, x3


def _mod_kernel(c_ref, w_ref, b_ref, o_ref):
    cond = _silu(c_ref[...]).astype(BF16)
    o_ref[...] = _dot(cond, w_ref[...].astype(BF16)) + b_ref[...]


def _ada_mod(c, ada_w, ada_b):
    L, D, _ = ada_w.shape
    B = c.shape[0]
    ada_b3 = ada_b.reshape(L, N_MOD, 1, D)
    out = pl.pallas_call(
        _mod_kernel,
        grid=(L, N_MOD),
        in_specs=[
            pl.BlockSpec((B, D), lambda l, j: (0, 0)),
            pl.BlockSpec((None, D, D), lambda l, j: (l, 0, j)),
            pl.BlockSpec((None, None, 1, D), lambda l, j: (l, j, 0, 0)),
        ],
        out_specs=pl.BlockSpec((None, None, B, D), lambda l, j: (l, j, 0, 0)),
        out_shape=jax.ShapeDtypeStruct((L, N_MOD, B, D), F32),
        compiler_params=_params(4 * D * D * 4, 2),
        name="ada_mod",
    )(c, ada_w, ada_b3)
    return jnp.transpose(out, (0, 2, 1, 3))


def _rope_kernel(post_ref, freqc_ref, signc_ref, cos_ref, sin_ref, cost_ref, sint_ref):
    half = QK_ROPE // 2
    angt = freqc_ref[...] * post_ref[...].astype(F32)
    ct = jnp.cos(angt)
    st = jnp.sin(angt)
    cost_ref[...] = ct[0:half]
    sint_ref[...] = st[0:half]
    cos_ref[...] = ct.T[:, 0:QK_ROPE]
    sin_ref[...] = (st * signc_ref[...]).T[:, 0:QK_ROPE]


def _rope_tables(positions):
    B, S = positions.shape
    half = QK_ROPE // 2
    inv_freq = ROPE_THETA ** (-jnp.arange(half, dtype=F32) / half)
    reps = V7X_LANES // half
    freqc = jnp.tile(inv_freq, reps).reshape(V7X_LANES, 1)
    signc = jnp.tile(jnp.concatenate([-jnp.ones((half,), F32), jnp.ones((half,), F32)]),
                     reps // 2).reshape(V7X_LANES, 1)
    ts = min(S, 1024)
    return pl.pallas_call(
        _rope_kernel,
        grid=(B, S // ts),
        in_specs=[
            pl.BlockSpec((None, 1, ts), lambda b, s: (b, 0, s)),
            pl.BlockSpec((V7X_LANES, 1), lambda b, s: (0, 0)),
            pl.BlockSpec((V7X_LANES, 1), lambda b, s: (0, 0)),
        ],
        out_specs=[pl.BlockSpec((None, ts, QK_ROPE), lambda b, s: (b, s, 0))] * 2
        + [pl.BlockSpec((None, half, ts), lambda b, s: (b, 0, s))] * 2,
        out_shape=[jax.ShapeDtypeStruct((B, S, QK_ROPE), F32)] * 2
        + [jax.ShapeDtypeStruct((B, half, S), F32)] * 2,
        compiler_params=_params(16 * ts * V7X_LANES * 4, 2),
        name="rope_tables",
    )(positions.reshape(B, 1, S), freqc, signc)


def _ffn_kernel(x_ref, mod_ref, gain_ref, w1_ref, w3_ref, w2_ref, o_ref, *, row):
    x = x_ref[...]
    mod = mod_ref[...]
    h = _modulate(x, gain_ref[...], mod[row:row + 1], mod[row + 1:row + 2]).astype(BF16)
    a = _dot(h, w1_ref[...])
    b = _dot(h, w3_ref[...])
    g = (_silu(a) * b).astype(BF16)
    y = _dot(g, w2_ref[...])
    o_ref[...] = x + (0.5 * (1.0 + mod[row + 2:row + 3])) * y


def _ffn(x, mod, gain, w1, w3, w2, l, row):
    B, S, D = x.shape
    F = w1.shape[-1]
    tm = min(S, FFN_TOKENS)
    vmem = 3 * D * F * 2 + 4 * tm * D * 4 + 3 * tm * F * 4 + 2 * tm * D * 4
    return pl.pallas_call(
        functools.partial(_ffn_kernel, row=row),
        grid=(B, S // tm),
        in_specs=[
            pl.BlockSpec((None, tm, D), lambda b, s: (b, s, 0)),
            pl.BlockSpec((None, None, N_MOD, D), lambda b, s: (l, b, 0, 0)),
            pl.BlockSpec((None, 1, D), lambda b, s: (l, 0, 0)),
            _resident((None, D, F), lambda b, s: (l, 0, 0)),
            _resident((None, D, F), lambda b, s: (l, 0, 0)),
            _resident((None, F, D), lambda b, s: (l, 0, 0)),
        ],
        out_specs=pl.BlockSpec((None, tm, D), lambda b, s: (b, s, 0)),
        out_shape=jax.ShapeDtypeStruct((B, S, D), F32),
        compiler_params=_params(vmem, 2),
        name="ffn",
    )(x, mod, gain, w1, w3, w2)


def _mixin_kernel(x_ref, mod_ref, gain_ref, wall_ref, wba_ref, conv_ref,
                  qkv_ref, zs_ref, gates_ref, lat_ref, ba_ref, pbuf_ref, *, tm):
    @pl.when(pl.program_id(1) == 0)
    def _():
        pbuf_ref[0:HALO, :] = jnp.zeros((HALO, pbuf_ref.shape[1]), F32)

    x = x_ref[...]
    mod = mod_ref[...]
    h = _modulate(x, gain_ref[...], mod[3:4], mod[4:5]).astype(BF16)

    C = pbuf_ref.shape[1]
    nz = zs_ref.shape[0] * zs_ref.shape[2]
    ng = gates_ref.shape[1]
    proj = _dot(h, wall_ref[...])
    pbuf_ref[HALO:HALO + tm, :] = proj[:, :C]
    conv = conv_ref[...]
    assert CONV_K == 4
    n_qk = 2 * GDN_HEADS
    for cb in range(pbuf_ref.shape[1] // V7X_LANES):
        c0 = cb * V7X_LANES
        xs = pbuf_ref[:, c0:c0 + V7X_LANES]
        w = [conv[j:j + 1, c0:c0 + V7X_LANES] for j in range(CONV_K)]
        x1 = pltpu.roll(xs, 1, axis=0)
        near = w[3] * xs + w[2] * x1
        far = w[1] * xs + w[0] * x1
        y = _silu(near[HALO:] + pltpu.roll(far, 2, axis=0)[HALO:])
        if cb < n_qk:
            y = y * lax.rsqrt(jnp.sum(y * y, axis=-1, keepdims=True) + EPS)
        qkv_ref[cb] = y.astype(BF16)
    pbuf_ref[0:HALO, :] = pbuf_ref[tm:tm + HALO, :]

    for hh in range(GDN_HEADS):
        zs_ref[hh] = proj[:, C + hh * GDN_DV:C + (hh + 1) * GDN_DV].astype(BF16)
    gates_ref[...] = proj[:, C + nz:C + nz + ng].astype(BF16)
    lat_ref[...] = proj[:, C + nz + ng:]
    ba_ref[...] = _dot_nt(wba_ref[...], h)


def _mixin(x, mod, gain, wall, wba, conv, l):
    B, S, D = x.shape
    C = conv.shape[-1]
    nz = GDN_HEADS * GDN_DV
    ng = 2 * D
    nl = wall.shape[-1] - C - nz - ng
    nb = wba.shape[1]
    tm = min(S, MIX_TOKENS)
    ncb = C // V7X_LANES
    vmem = ((C + nz + ng + nl) * D * 2 + 2 * tm * D * 4 + 2 * tm * (C + nz + ng) * 2 + 2 * tm * nl * 4
            + (tm + HALO) * C * 4 + tm * C * 4 + tm * (nz + ng) * 4 + 4 * 1024 * 1024)
    return pl.pallas_call(
        functools.partial(_mixin_kernel, tm=tm),
        grid=(B, S // tm),
        in_specs=[
            pl.BlockSpec((None, tm, D), lambda b, s: (b, s, 0)),
            pl.BlockSpec((None, None, N_MOD, D), lambda b, s: (l, b, 0, 0)),
            pl.BlockSpec((None, 1, D), lambda b, s: (l, 0, 0)),
            _resident((None, D, C + nz + ng + nl), lambda b, s: (l, 0, 0)),
            _resident((None, nb, D), lambda b, s: (l, 0, 0)),
            pl.BlockSpec((None, CONV_K, C), lambda b, s: (l, 0, 0)),
        ],
        out_specs=[
            pl.BlockSpec((None, ncb, tm, V7X_LANES), lambda b, s: (b, 0, s, 0)),
            pl.BlockSpec((None, GDN_HEADS, tm, GDN_DV), lambda b, s: (b, 0, s, 0)),
            pl.BlockSpec((None, tm, ng), lambda b, s: (b, s, 0)),
            pl.BlockSpec((None, tm, nl), lambda b, s: (b, s, 0)),
            pl.BlockSpec((None, nb, tm), lambda b, s: (b, 0, s)),
        ],
        out_shape=[
            jax.ShapeDtypeStruct((B, ncb, S, V7X_LANES), BF16),
            jax.ShapeDtypeStruct((B, GDN_HEADS, S, GDN_DV), BF16),
            jax.ShapeDtypeStruct((B, S, ng), BF16),
            jax.ShapeDtypeStruct((B, S, nl), F32),
            jax.ShapeDtypeStruct((B, nb, S), F32),
        ],
        scratch_shapes=[pltpu.VMEM((tm + HALO, C), F32)],
        compiler_params=_params(vmem, 2),
        name="mixin",
    )(x, mod, gain, wall, wba, conv)


def _gate_kernel(ba_ref, alog_ref, dtb_ref, grow_ref, gcol_ref, *, tm, NB):
    H = GDN_HEADS
    betas, gs = [], []
    for bb in range(NB):
        ba = ba_ref[bb]
        betas.append(_sigmoid(ba[0:H]))
        xa = ba[H:2 * H] + dtb_ref[...]
        softplus = jnp.maximum(xa, 0.0) + jnp.log(1.0 + jnp.exp(-jnp.abs(xa)))
        gs.append(-jnp.exp(alog_ref[...]) * softplus)
    beta = jnp.concatenate(betas, axis=0)
    g = jnp.concatenate(gs, axis=0)

    ii = lax.broadcasted_iota(jnp.int32, (tm, tm), 0)
    jj = lax.broadcasted_iota(jnp.int32, (tm, tm), 1)
    same = (ii >> LOG2_CHUNK) == (jj >> LOG2_CHUNK)
    upper =jnp.where(same & (ii <= jj), 1.0, 0.0).astype(BF16)
    block = jnp.where(same, 1.0, 0.0).astype(BF16)
    eye = jnp.where(ii == jj, 1.0, 0.0).astype(BF16)
    gam = None
    glast = None
    for part in _split3(g):
        t1 = _dot(part, upper)
        t2 = _dot(part, block)
        gam = t1 if gam is None else gam + t1
        glast = t2 if glast is None else glast + t2
    egam = jnp.exp(gam)
    ekd = jnp.exp(glast - gam)
    egl = jnp.exp(glast)
    zero = jnp.zeros_like(gam[0:1])
    rows = []
    for n in range(NB * H):
        r = jnp.concatenate([beta[n:n + 1], gam[n:n + 1], egam[n:n + 1], ekd[n:n + 1],
                             egl[n:n + 1], zero, zero, zero], axis=0)
        grow_ref[n // H, n % H] = r
        rows.append(r)
    allrows = jnp.concatenate(rows, axis=0)
    cols = None
    for part in _split3(allrows):
        t = _dot_nt(eye, part)
        cols = t if cols is None else cols + t
    for bb in range(NB):
        gcol_ref[bb] = cols[:, 8 * H * bb:8 * H * (bb + 1)]


def _gdn_gates(ba, a_log, dt_bias, l):
    B, nb, S = ba.shape
    H = GDN_HEADS
    NB = math.gcd(B, GATE_BATCH)
    tm = min(S, GATE_TOKENS)
    vmem = 8 * tm * tm * 4 + 6 * NB * H * tm * V7X_LANES * 4
    return pl.pallas_call(
        functools.partial(_gate_kernel, tm=tm, NB=NB),
        grid=(B // NB, S // tm),
        in_specs=[
            pl.BlockSpec((NB, nb, tm), lambda b, s: (b, 0, s)),
            pl.BlockSpec((None, H, 1), lambda b, s: (l, 0, 0)),
            pl.BlockSpec((None, H, 1), lambda b, s: (l, 0, 0)),
        ],
        out_specs=[
            pl.BlockSpec((NB, H, 8, tm), lambda b, s: (b, 0, 0, s)),
            pl.BlockSpec((NB, tm, 8 * H), lambda b, s: (b, s, 0)),
        ],
        out_shape=[
            jax.ShapeDtypeStruct((B, H, 8, S), F32),
            jax.ShapeDtypeStruct((B, S, 8 * H), F32),
        ],
        compiler_params=_params(vmem, 2),
        name="gdn_gates",
    )(ba, a_log, dt_bias)


def _gdn_kernel(q_ref, k_ref, v_ref, zs_ref, grow_ref, gcol_ref, og_ref, o_ref, state_ref, *, tc, G, NB):
    s = pl.program_id(2)

    @pl.when(s == 0)
    def _():
        state_ref[...] = jnp.zeros(state_ref.shape, F32)

    heads = range(NB * G)

    def at(ref, n):
        return ref[n // G, n % G]

    n_chunks = tc // CHUNK
    scale = GDN_DK ** -0.5
    ii = lax.broadcasted_iota(jnp.int32, (tc, tc), 0)
    jj = lax.broadcasted_iota(jnp.int32, (tc, tc), 1)
    same = (ii >> LOG2_CHUNK) == (jj >> LOG2_CHUNK)
    causal = same & (ii >= jj)
    strict = same & (ii > jj)
    lane = lax.broadcasted_iota(jnp.int32, (GDN_DK, tc), 1) >> LOG2_CHUNK

    gc = [gcol_ref[g // G][:, 8 * (g % G):8 * (g % G) + 8] for g in heads]
    gr = [at(grow_ref, g) for g in heads]
    k = [at(k_ref, g) for g in heads]
    kf = [k[g].astype(F32) for g in heads]
    kb = [kf[g] * gc[g][:, 0:1] for g in heads]
    decay = [jnp.where(causal, jnp.exp(jnp.minimum(gc[g][:, 1:2] - gr[g][1:2, :], 0.0)), 0.0) for g in heads]

    akq = [_dot_nt(jnp.concatenate([kb[g].astype(BF16), at(q_ref, g)], axis=0), k[g]) for g in heads]
    m = [jnp.where(strict, akq[g][:tc] * decay[g], 0.0) for g in heads]
    qk = [(akq[g][tc:] * (decay[g] * scale)).astype(BF16) for g in heads]

    rlane = lax.broadcasted_iota(jnp.int32, (CHUNK, tc), 1)
    rrow = lax.broadcasted_iota(jnp.int32, (CHUNK, tc), 0)
    eye_row = jnp.where((rlane & (CHUNK - 1)) == rrow, 1.0, 0.0)
    rblk = rlane >> LOG2_CHUNK

    def blockdiag(x_row):
        return jnp.concatenate([jnp.where(rblk == c, x_row, 0.0) for c in range(n_chunks)], axis=0).astype(BF16)

    def side_by_side(x):
        acc = x[0:CHUNK]
        for c in range(1, n_chunks):
            acc = acc + x[c * CHUNK:(c + 1) * CHUNK]
        return acc

    m_row = [side_by_side(m[g]) for g in heads]
    p = [_dot(m_row[g].astype(BF16), blockdiag(m_row[g])) for g in heads]
    t_row = [eye_row - m_row[g] for g in heads]
    for _ in range(LOG2_CHUNK - 2):
        r = [_dot(jnp.concatenate([p[g], t_row[g]], axis=0).astype(BF16), blockdiag(p[g])) for g in heads]
        p = [r[g][:CHUNK] for g in heads]
        t_row = [t_row[g] + r[g][CHUNK:] for g in heads]
    t_row = [t_row[g] + _dot(t_row[g].astype(BF16), blockdiag(p[g])) for g in heads]

    rhs = [jnp.concatenate([at(v_ref, g).astype(F32) * gc[g][:, 0:1], kb[g] * gc[g][:, 2:3]],
                           axis=1).astype(BF16) for g in heads]
    uw = [_dot(blockdiag(t_row[g]), rhs[g]).astype(BF16) for g in heads]

    kdt = [(kf[g] * gc[g][:, 3:4]).T for g in heads]
    pre_lhs = [jnp.concatenate([jnp.where(lane == c, kdt[g], 0.0).astype(BF16) for c in range(n_chunks)]
                               + [qk[g]], axis=0) for g in heads]
    pre = [_dot(pre_lhs[g], uw[g]) for g in heads]
    qd = [at(q_ref, g).astype(F32) * (gc[g][:, 2:3] * scale) for g in heads]
    base = n_chunks * GDN_DK
    qmod = [(qd[g] - pre[g][base:, GDN_DV:]).astype(BF16) for g in heads]

    states = [at(state_ref, g) for g in heads]
    outs = [[] for _ in heads]
    for c in range(n_chunks):
        r0 = c * CHUNK
        for g in heads:
            kblk = pre[g][c * GDN_DK:(c + 1) * GDN_DK]
            lhs = jnp.concatenate([kblk[:, GDN_DV:].astype(BF16), qmod[g][r0:r0 + CHUNK]], axis=0)
            r = _dot(lhs, states[g].astype(BF16))
            outs[g].append(r[GDN_DK:] + pre[g][base + r0:base + r0 + CHUNK, :GDN_DV])
            states[g] = states[g] * gr[g][4:5, r0:r0 + 1] - r[:GDN_DK] + kblk[:, :GDN_DV]
    for g in heads:
        bb, hh = g // G, g % G
        state_ref[bb, hh] = states[g]
        o = jnp.concatenate(outs[g], axis=0)
        o = o * lax.rsqrt(jnp.mean(o * o, axis=-1, keepdims=True) + EPS) * og_ref[...]
        o_ref[bb, :, hh * GDN_DV:(hh + 1) * GDN_DV] = (o * _silu(zs_ref[bb, hh].astype(F32))).astype(BF16)


def _gdn(qkv, zs, grow, gcol, out_gain, l):
    B, _, S, _ = qkv.shape
    H = GDN_HEADS
    G = GDN_GROUP
    NB = math.gcd(B, GDN_BATCH)
    assert G == H, "the gate columns are read per head from one lane-packed block"
    tc = min(S, GDN_TOKENS)
    vmem = NB * G * (24 * tc * tc * 4 + 24 * tc * V7X_LANES * 4)
    return pl.pallas_call(
        functools.partial(_gdn_kernel, tc=tc, G=G, NB=NB),
        grid=(B // NB, H // G, S // tc),
        in_specs=[
            pl.BlockSpec((NB, G, tc, GDN_DK), lambda b, h, s: (b, h, s, 0)),
            pl.BlockSpec((NB, G, tc, GDN_DK), lambda b, h, s: (b, H // G + h, s, 0)),
            pl.BlockSpec((NB, G, tc, GDN_DV), lambda b, h, s: (b, 2 * H // G + h, s, 0)),
            pl.BlockSpec((NB, G, tc, GDN_DV), lambda b, h, s: (b, h, s, 0)),
            pl.BlockSpec((NB, G, 8, tc), lambda b, h, s: (b, h, 0, s)),
            pl.BlockSpec((NB, tc, 8 * H), lambda b, h, s: (b, s, 0)),
            pl.BlockSpec((None, 1, GDN_DV), lambda b, h, s: (l, 0, 0)),
        ],
        out_specs=pl.BlockSpec((NB, tc, G * GDN_DV), lambda b, h, s: (b, s, h)),
        out_shape=jax.ShapeDtypeStruct((B, S, H * GDN_DV), BF16),
        scratch_shapes=[pltpu.VMEM((NB, G, GDN_DK, GDN_DV), F32)],
        compiler_params=_params(vmem, 3),
        name="gdn",
    )(qkv, qkv, qkv, zs, grow, gcol, out_gain)


def _swap_halves(y):
    half = y.shape[-1] // 2
    return jnp.concatenate([y[:, half:], y[:, :half]], axis=1)


def _mla_prep_kernel(lat_ref, cos_ref, sin_ref, cost_ref, sint_ref, qg_ref, kvg_ref, wqt_ref, wk_ref, wvt_ref,
                     qnc_ref, kn_ref, qt_ref, k_ref, vt_ref):
    H = MLA_HEADS
    half = QK_ROPE // 2
    lat = lat_ref[...]
    ql = lat[:, :Q_LORA]
    kvl = lat[:, Q_LORA:Q_LORA + KV_LORA]
    kpe = lat[:, Q_LORA + KV_LORA:Q_LORA + KV_LORA + QK_ROPE]
    qln = (ql * lax.rsqrt(jnp.mean(ql * ql, axis=-1, keepdims=True) + EPS) * qg_ref[...]).astype(BF16)
    kvn = (kvl * lax.rsqrt(jnp.mean(kvl * kvl, axis=-1, keepdims=True) + EPS) * kvg_ref[...]).astype(BF16)
    qt = _dot_nt(wqt_ref[...], qln)
    knope = _dot(kvn, wk_ref[...])
    vt = _dot_nt(wvt_ref[...], kvn)

    cost = cost_ref[...]
    sint = sint_ref[...]
    qnc = qnc_ref[...]
    scale = MLA_QK ** -0.5 * math.log2(math.e)
    for hh in range(H):
        blk = qt[hh * MLA_QK:(hh + 1) * MLA_QK]
        r = lax.rsqrt(jnp.sum(blk * blk, axis=0, keepdims=True) * (1.0 / MLA_QK) + EPS) * scale
        y = blk * qnc * r
        y1 = y[QK_NOPE:QK_NOPE + half]
        y2 = y[QK_NOPE + half:]
        qt_ref[hh, 0:QK_NOPE, :] = y[:QK_NOPE].astype(BF16)
        qt_ref[hh, QK_NOPE:QK_NOPE + half, :] = (y1 * cost - y2 * sint).astype(BF16)
        qt_ref[hh, QK_NOPE + half:MLA_QK, :] = (y2 * cost + y1 * sint).astype(BF16)
        vt_ref[hh] = vt[hh * V_HEAD:(hh + 1) * V_HEAD].astype(BF16)

    kn = kn_ref[...]
    yk = kpe * kn[:, QK_NOPE:]
    kr = yk * cos_ref[...] + _swap_halves(yk) * sin_ref[...]
    kpe_ss = jnp.sum(kpe * kpe, axis=-1, keepdims=True)
    for hh in range(H):
        k_nope = knope[:, hh * QK_NOPE:(hh + 1) * QK_NOPE]
        ss = jnp.sum(k_nope * k_nope, axis=-1, keepdims=True) + kpe_ss
        r = lax.rsqrt(ss * (1.0 / MLA_QK) + EPS)
        k_ref[hh, :, 0:QK_NOPE] = (k_nope * r * kn[:, :QK_NOPE]).astype(BF16)
        k_ref[hh, :, QK_NOPE:MLA_QK] = (kr * r).astype(BF16)


def _mla_prep(lat, cos2, sin2, cost, sint, q_lat_gain, kv_lat_gain, wqt, wk, wvt, q_norm_col, k_norm, l):
    B, S, nl = lat.shape
    H = MLA_HEADS
    half = QK_ROPE // 2
    tm = min(S, ATTN_Q)
    nq = wqt.shape[1]
    nk = wk.shape[-1]
    nv = wvt.shape[1]
    vmem = ((nq * Q_LORA + (nk + nv) * KV_LORA) * 2 + 2 * tm * nl * 4 + 3 * tm * (nq + nk + nv) * 4
            + 4 * H * tm * (256 + MLA_QK + V_HEAD) * 2)
    return pl.pallas_call(
        _mla_prep_kernel,
        grid=(B, S // tm),
        in_specs=[
            pl.BlockSpec((None, tm, nl), lambda b, s: (b, s, 0)),
            pl.BlockSpec((None, tm, QK_ROPE), lambda b, s: (b, s, 0)),
            pl.BlockSpec((None, tm, QK_ROPE), lambda b, s: (b, s, 0)),
            pl.BlockSpec((None, half, tm), lambda b, s: (b, 0, s)),
            pl.BlockSpec((None, half, tm), lambda b, s: (b, 0, s)),
            pl.BlockSpec((None, 1, Q_LORA), lambda b, s: (l, 0, 0)),
            pl.BlockSpec((None, 1, KV_LORA), lambda b, s: (l, 0, 0)),
            _resident((None, nq, Q_LORA), lambda b, s: (l, 0, 0)),
            _resident((None, KV_LORA, nk), lambda b, s: (l, 0, 0)),
            _resident((None, nv, KV_LORA), lambda b, s: (l, 0, 0)),
            pl.BlockSpec((None, MLA_QK, 1), lambda b, s: (l, 0, 0)),
            pl.BlockSpec((None, 1, MLA_QK), lambda b, s: (l, 0, 0)),
        ],
        out_specs=[
            pl.BlockSpec((None, H, MLA_QK, tm), lambda b, s: (b, 0, 0, s)),
            pl.BlockSpec((None, H, tm, MLA_QK), lambda b, s: (b, 0, s, 0)),
            pl.BlockSpec((None, H, None, V_HEAD, tm), lambda b, s: (b, 0, s, 0, 0)),
        ],
        out_shape=[
            jax.ShapeDtypeStruct((B, H, MLA_QK, S), BF16),
            jax.ShapeDtypeStruct((B, H, S, MLA_QK), BF16),
            jax.ShapeDtypeStruct((B, H, S // tm, V_HEAD, tm), BF16),
        ],
        compiler_params=_params(vmem, 2),
        name="mla_prep",
    )(lat, cos2, sin2, cost, sint, q_lat_gain, kv_lat_gain, wqt, wk, wvt, q_norm_col, k_norm)


def _attn_kernel(qt_ref, k_ref, vt_ref, o_ref, sa_ref, sb_ref, mxa_ref, mxb_ref, m_ref, l_ref, acc_ref, *,
                 tq, chain, single):
    c = 0 if single else pl.program_id(2)
    bufs = ((sa_ref, mxa_ref), (sb_ref, mxb_ref))

    def scores(qt, j, buf):
        start = j * tq if isinstance(j, int) else pl.multiple_of(j * tq, tq)
        st = _dot(k_ref[pl.ds(start, tq), :], qt)
        buf[0][...] = st
        buf[1][...] = jnp.max(st, axis=0, keepdims=True)

    def reset(w):
        m_ref[w] = jnp.full(m_ref.shape[1:], -jnp.inf, F32)
        l_ref[w] = jnp.zeros(l_ref.shape[1:], F32)
        acc_ref[w] = jnp.zeros(acc_ref.shape[1:], F32)

    def update(w, j, buf, masked=False):
        st = buf[0][...]
        mx = buf[1][...]
        if masked:
            kpos = lax.broadcasted_iota(jnp.int32, (tq, tq), 0)
            qpos = lax.broadcasted_iota(jnp.int32, (tq, tq), 1)
            st = jnp.where(kpos <= qpos, st, -jnp.inf)
            mx = jnp.max(st, axis=0, keepdims=True)
        m = m_ref[w]
        m_new = jnp.maximum(m, mx)
        alpha = jnp.exp2(m - m_new)
        pr = jnp.exp2(st - m_new)
        m_ref[w] = m_new
        l_ref[w] = alpha * l_ref[w] + jnp.sum(pr, axis=0, keepdims=True)
        acc_ref[w] = alpha * acc_ref[w] + _dot(vt_ref[j], pr.astype(BF16))

    def full_blocks(w, qt, cur, nxt, trips):
        def body(t, _):
            scores(qt, 2 * t + 1, nxt)
            update(w, 2 * t, cur)
            scores(qt, 2 * t + 2, cur)
            update(w, 2 * t + 1, nxt)
            return 0
        if isinstance(trips, int):
            for t in range(trips):
                body(t, 0)
        else:
            lax.fori_loop(0, trips, body, 0)

    def q_block(u):
        return qt_ref[:, u * tq:(u + 1) * tq]

    cur, nxt = bufs
    reset(0)
    scores(q_block(0), 0, cur)
    for u in range(chain):
        w = u % 2
        i = c * chain + u
        full_blocks(w, q_block(u), cur, nxt, c * (chain // 2) + u // 2)
        if u % 2 == 1:
            scores(q_block(u), i, nxt)
            update(w, i - 1, cur)
            cur, nxt = nxt, cur
        if u + 1 < chain:
            reset(1 - w)
            scores(q_block(u + 1), 0, nxt)
        update(w, i, cur, masked=True)
        o_ref[u * tq:(u + 1) * tq, :] = (acc_ref[w] / l_ref[w]).T.astype(BF16)
        cur, nxt = nxt, cur


def _attention(qt, k, vt):
    B, H, S, _ = k.shape
    tq = vt.shape[-1]
    nq = S // tq
    chain = min(nq, ATTN_CHAIN)
    assert chain % 2 == 0 and nq % chain == 0, "q blocks are taken in even-length runs"
    vmem = (8 * tq * tq * 4 + 4 * S * (256 + V_HEAD) * 2 + 4 * chain * tq * (256 + V_HEAD) * 2
            + 16 * tq * V7X_LANES * 4)
    return pl.pallas_call(
        functools.partial(_attn_kernel, tq=tq, chain=chain, single=(nq == chain)),
        grid=(B, H, nq // chain),
        in_specs=[
            pl.BlockSpec((None, None, MLA_QK, chain * tq), lambda b, h, i: (b, h, 0, i)),
            pl.BlockSpec((None, None, S, MLA_QK), lambda b, h, i: (b, h, 0, 0)),
            pl.BlockSpec((None, None, S // tq, V_HEAD, tq), lambda b, h, i: (b, h, 0, 0, 0)),
        ],
        out_specs=pl.BlockSpec((None, chain * tq, V_HEAD), lambda b, h, i: (b, i, h)),
        out_shape=jax.ShapeDtypeStruct((B, S, H * V_HEAD), BF16),
        scratch_shapes=[pltpu.VMEM((tq, tq), F32), pltpu.VMEM((tq, tq), F32),
                        pltpu.VMEM((1, tq), F32), pltpu.VMEM((1, tq), F32),
                        pltpu.VMEM((2, 1, tq), F32), pltpu.VMEM((2, 1, tq), F32),
                        pltpu.VMEM((2, V_HEAD, tq), F32)],
        compiler_params=_params(vmem, 3),
        name="mla_attention",
    )(qt, k, vt)


def _merge_kernel(x_ref, mod_ref, oa_ref, ob_ref, gates_ref, wa_ref, wb_ref, wo_ref, o_ref):
    D = x_ref.shape[-1]
    mod = mod_ref[...]
    gates = _sigmoid(gates_ref[...].astype(F32))
    ya =_dot(oa_ref[...], wa_ref[...])
    yb = _dot(ob_ref[...], wb_ref[...])
    y = (gates[:, :D] * ya + gates[:, D:] * yb).astype(BF16)
    o_ref[...] = x_ref[...] + (1.0 + mod[5:6]) * _dot(y, wo_ref[...])


def _merge(x, mod, oa, ob, gates, wa, wb, wo, l):
    B, S, D = x.shape
    tm = min(S, MERGE_TOKENS)
    na = oa.shape[-1]
    nbv = ob.shape[-1]
    vmem = (na + nbv + D) * D * 2 + 4 * tm * D * 4 + 2 * tm * (na + nbv + 2 * D) * 2 + 4 * tm * D * 4
    return pl.pallas_call(
        _merge_kernel,
        grid=(B, S // tm),
        in_specs=[
            pl.BlockSpec((None, tm, D), lambda b, s: (b, s, 0)),
            pl.BlockSpec((None, None, N_MOD, D), lambda b, s: (l, b, 0, 0)),
            pl.BlockSpec((None, tm, na), lambda b, s: (b, s, 0)),
            pl.BlockSpec((None, tm, nbv), lambda b, s: (b, s, 0)),
            pl.BlockSpec((None, tm, 2 * D), lambda b, s: (b, s, 0)),
            _resident((None, na, D), lambda b, s: (l, 0, 0)),
            _resident((None, nbv, D), lambda b, s: (l, 0, 0)),
            _resident((None, D, D), lambda b, s: (l, 0, 0)),
        ],
        out_specs=pl.BlockSpec((None, tm, D), lambda b, s: (b, s, 0)),
        out_shape=jax.ShapeDtypeStruct((B, S, D), F32),
        compiler_params=_params(vmem, 2),
        name="merge",
    )(x, mod, oa, ob, gates, wa, wb, wo)


def _pack_w_in(w_in):
    conv_c = 2 * GDN_HEADS * GDN_DK + GDN_HEADS * GDN_DV
    gdn_v = GDN_HEADS * GDN_DV
    D = w_in.shape[1]
    sizes = (conv_c, gdn_v, GDN_HEADS, GDN_HEADS, Q_LORA, KV_LORA, QK_ROPE, D, D)
    offs = np.cumsum((0,) + sizes)
    w_in = w_in.astype(BF16)
    qkv, z, b_raw, a_raw, q_lat, kv_lat, k_pe, gate_a, gate_b = (
        w_in[:, :, int(offs[i]):int(offs[i + 1])] for i in range(len(sizes)))
    n_lat = Q_LORA + KV_LORA + QK_ROPE
    pad = (-n_lat) % V7X_LANES
    lat = jnp.concatenate([q_lat, kv_lat, k_pe, jnp.zeros(w_in.shape[:2] + (pad,), w_in.dtype)], axis=-1)
    ba = jnp.swapaxes(jnp.concatenate([b_raw, a_raw], axis=-1), 1, 2)
    wall = jnp.concatenate([qkv, z, gate_a, gate_b, lat], axis=-1)
    return wall, ba


def _pack_kv_up(w_kv_up):
    L = w_kv_up.shape[0]
    w = w_kv_up.astype(BF16).reshape(L, KV_LORA, MLA_HEADS, QK_NOPE + V_HEAD)
    wk = w[..., :QK_NOPE].reshape(L, KV_LORA, MLA_HEADS * QK_NOPE)
    wv = w[..., QK_NOPE:].reshape(L, KV_LORA, MLA_HEADS * V_HEAD)
    return wk, jnp.swapaxes(wv, 1, 2)


def kernel(x, c, positions, ada_w, ada_b, norm_ffn1, ffn1_w1, ffn1_w3, ffn1_w2, norm_mix, w_in, gdn_conv, gdn_a_log, gdn_dt_bias, gdn_out_gain, mla_q_lat_gain, mla_kv_lat_gain, mla_w_q_up, mla_w_kv_up, mla_q_norm, mla_k_norm, w_branch_a, w_branch_b, w_out, norm_ffn2, ffn2_w1, ffn2_w3, ffn2_w2):
    L = ada_w.shape[0]
    S = x.shape[1]
    assert S % CHUNK == 0 and S % min(S, GDN_TOKENS) == 0 and S % min(S, FFN_TOKENS) == 0
    assert S % min(S, MERGE_TOKENS) == 0 and S % min(S, ATTN_Q) == 0

    mod = _ada_mod(c, ada_w, ada_b)
    cos2, sin2, cost, sint = _rope_tables(positions)

    row = lambda a: a.reshape(L, 1, a.shape[-1])
    col = lambda a: a.reshape(L, a.shape[-1], 1)
    bf = lambda a: a.astype(BF16)
    wall, wba = _pack_w_in(w_in)
    wqt = jnp.swapaxes(bf(mla_w_q_up), 1, 2)
    wk, wvt = _pack_kv_up(mla_w_kv_up)
    f1 = (bf(ffn1_w1), bf(ffn1_w3), bf(ffn1_w2))
    f2 = (bf(ffn2_w1), bf(ffn2_w3), bf(ffn2_w2))
    wa, wb, wo = bf(w_branch_a), bf(w_branch_b), bf(w_out)
    g_ffn1, g_mix, g_ffn2 = row(norm_ffn1), row(norm_mix), row(norm_ffn2)
    og, qlg, kvlg = row(gdn_out_gain), row(mla_q_lat_gain), row(mla_kv_lat_gain)
    qnc, kn = col(mla_q_norm), row(mla_k_norm)
    a_log, dt_bias = col(gdn_a_log), col(gdn_dt_bias)

    for l in range(L):
        x = _ffn(x, mod, g_ffn1, *f1, l, 0)
        qkv, zs, gates, lat, ba = _mixin(x, mod, g_mix, wall, wba, gdn_conv, l)
        grow, gcol = _gdn_gates(ba, a_log, dt_bias, l)
        o_gdn = _gdn(qkv, zs, grow, gcol, og, l)
        qt, k, vt = _mla_prep(lat, cos2, sin2, cost, sint, qlg, kvlg, wqt, wk, wvt, qnc, kn, l)
        o_mla = _attention(qt, k, vt)
        x = _merge(x, mod, o_gdn, o_mla, gates, wa, wb, wo, l)
        x = _ffn(x, mod, g_ffn2, *f2, l, 6)
    return x
```

```python
import functools
import math

import numpy as np
import jax
import jax.numpy as jnp
from jax import lax
from jax.experimental import pallas as pl
from jax.experimental.pallas import tpu as pltpu

F32 = jnp.float32
BF16 = jnp.bfloat16

EPS = 1e-6
N_MOD = 9
GDN_HEADS = 8
GDN_DK = 128
GDN_DV = 128
CONV_K = 4
CHUNK = 64
LOG2_CHUNK = 6
MLA_HEADS = 8
Q_LORA = 384
KV_LORA = 256
QK_NOPE = 128
QK_ROPE = 64
V_HEAD = 128
ROPE_THETA = 10000.0
MLA_QK = QK_NOPE + QK_ROPE

V7X_VMEM_BYTES = 64 * 1024 * 1024
V7X_LANES = 128
V7X_SUBLANES = 8
VMEM_CAP_BYTES = V7X_VMEM_BYTES - 6 * 1024 * 1024

FFN_TOKENS = 1024
MIX_TOKENS = 256
GATE_TOKENS = 512
GATE_BATCH = 4
GDN_TOKENS = 256
GDN_GROUP = 8
GDN_BATCH = 4
ATTN_Q = 512
ATTN_CHAIN = 8
MERGE_TOKENS = 1024
HALO = V7X_SUBLANES


def _params(vmem_bytes, n_axes):
    limit = int(min(VMEM_CAP_BYTES, max(vmem_bytes, 16 * 1024 * 1024)))
    return pltpu.CompilerParams(dimension_semantics=("arbitrary",) * n_axes, vmem_limit_bytes=limit)


def _resident(block_shape, index_map):
    return pl.BlockSpec(block_shape, index_map, pipeline_mode=pl.Buffered(1))


def _dot(a, b):
    return jnp.dot(a, b, preferred_element_type=F32)


def _dot_nt(a, b):
    return lax.dot_general(a, b, (((1,), (1,)), ((), ())), preferred_element_type=F32)


def _sigmoid(x):
    return 0.5 * jnp.tanh(0.5 * x) + 0.5


def _silu(x):
    hx = 0.5 * x
    return hx * jnp.tanh(hx) + hx


def _modulate(x, gain, shift, scale):
    y = x * lax.rsqrt(jnp.mean(x * x, axis=-1, keepdims=True) + EPS)
    return (y * gain) * (1.0 + scale) + shift


def _split3(x):
    x1 = x.astype(BF16)
    r1 = x - x1.astype(F32)
    x2 = r1.astype(BF16)
    x3 = (r1 - x2.astype(F32)).astype(BF16)
    return x1, x2, x3


def _mod_kernel(c_ref, w_ref, b_ref, o_ref):
    cond = _silu(c_ref[...]).astype(BF16)
    o_ref[...] = _dot(cond, w_ref[...].astype(BF16)) + b_ref[...]


def _ada_mod(c, ada_w, ada_b):
    L, D, _ = ada_w.shape
    B = c.shape[0]
    ada_b3 = ada_b.reshape(L, N_MOD, 1, D)
    out = pl.pallas_call(
        _mod_kernel,
        grid=(L, N_MOD),
        in_specs=[
            pl.BlockSpec((B, D), lambda l, j: (0, 0)),
            pl.BlockSpec((None, D, D), lambda l, j: (l, 0, j)),
            pl.BlockSpec((None, None, 1, D), lambda l, j: (l, j, 0, 0)),
        ],
        out_specs=pl.BlockSpec((None, None, B, D), lambda l, j: (l, j, 0, 0)),
        out_shape=jax.ShapeDtypeStruct((L, N_MOD, B, D), F32),
        compiler_params=_params(4 * D * D * 4, 2),
        name="ada_mod",
    )(c, ada_w, ada_b3)
    return jnp.transpose(out, (0, 2, 1, 3))


def _rope_kernel(pos_ref, post_ref, freq_ref, sign_ref, freqc_ref, cos_ref, sin_ref, cost_ref, sint_ref):
    ang = pos_ref[...].astype(F32) * freq_ref[...]
    cos_ref[...] = jnp.cos(ang)
    sin_ref[...] = jnp.sin(ang) * sign_ref[...]
    angt = freqc_ref[...] * post_ref[...].astype(F32)
    cost_ref[...] = jnp.cos(angt)
    sint_ref[...] = jnp.sin(angt)


def _rope_tables(positions):
    B, S = positions.shape
    half = QK_ROPE // 2
    inv_freq = ROPE_THETA ** (-jnp.arange(half, dtype=F32) / half)
    freq2 = jnp.concatenate([inv_freq, inv_freq]).reshape(1, QK_ROPE)
    sign2 = jnp.concatenate([-jnp.ones((half,), F32), jnp.ones((half,), F32)]).reshape(1, QK_ROPE)
    ts = min(S, 1024)
    return pl.pallas_call(
        _rope_kernel,
        grid=(B, S // ts),
        in_specs=[
            pl.BlockSpec((None, ts, 1), lambda b, s: (b, s, 0)),
            pl.BlockSpec((None, 1, ts), lambda b, s: (b, 0, s)),
            pl.BlockSpec((1, QK_ROPE), lambda b, s: (0, 0)),
            pl.BlockSpec((1, QK_ROPE), lambda b, s: (0, 0)),
            pl.BlockSpec((half, 1), lambda b, s: (0, 0)),
        ],
        out_specs=[pl.BlockSpec((None, ts, QK_ROPE), lambda b, s: (b, s, 0))] * 2
        + [pl.BlockSpec((None, half, ts), lambda b, s: (b, 0, s))] * 2,
        out_shape=[jax.ShapeDtypeStruct((B, S, QK_ROPE), F32)] * 2
        + [jax.ShapeDtypeStruct((B, half, S), F32)] * 2,
        compiler_params=_params(8 * ts * V7X_LANES * 4, 2),
        name="rope_tables",
    )(positions.reshape(B, S, 1), positions.reshape(B, 1, S), freq2, sign2, inv_freq.reshape(half, 1))


def _ffn_kernel(x_ref, mod_ref, gain_ref, w1_ref, w3_ref, w2_ref, o_ref, *, row):
    x = x_ref[...]
    mod = mod_ref[...]
    h = _modulate(x, gain_ref[...], mod[row:row + 1], mod[row + 1:row + 2]).astype(BF16)
    a = _dot(h, w1_ref[...])
    b = _dot(h, w3_ref[...])
    g = (_silu(a) * b).astype(BF16)
    y = _dot(g, w2_ref[...])
    o_ref[...] = x + (0.5 * (1.0 + mod[row + 2:row + 3])) * y


def _ffn(x, mod, gain, w1, w3, w2, l, row):
    B, S, D = x.shape
    F = w1.shape[-1]
    tm = min(S, FFN_TOKENS)
    vmem = 3 * D * F * 2 + 4 * tm * D * 4 + 3 * tm * F * 4 + 2 * tm * D * 4
    return pl.pallas_call(
        functools.partial(_ffn_kernel, row=row),
        grid=(B, S // tm),
        in_specs=[
            pl.BlockSpec((None, tm, D), lambda b, s: (b, s, 0)),
            pl.BlockSpec((None, None, N_MOD, D), lambda b, s: (l, b, 0, 0)),
            pl.BlockSpec((None, 1, D), lambda b, s: (l, 0, 0)),
            _resident((None, D, F), lambda b, s: (l, 0, 0)),
            _resident((None, D, F), lambda b, s: (l, 0, 0)),
            _resident((None, F, D), lambda b, s: (l, 0, 0)),
        ],
        out_specs=pl.BlockSpec((None, tm, D), lambda b, s: (b, s, 0)),
        out_shape=jax.ShapeDtypeStruct((B, S, D), F32),
        compiler_params=_params(vmem, 2),
        name="ffn",
    )(x, mod, gain, w1, w3, w2)


def _mixin_kernel(x_ref, mod_ref, gain_ref, wall_ref, wba_ref, conv_ref,
                  qkv_ref, zs_ref, gates_ref, lat_ref, ba_ref, pbuf_ref, *, tm):
    @pl.when(pl.program_id(1) == 0)
    def _():
        pbuf_ref[0:HALO, :] = jnp.zeros((HALO, pbuf_ref.shape[1]), F32)

    x = x_ref[...]
    mod = mod_ref[...]
    h = _modulate(x, gain_ref[...], mod[3:4], mod[4:5]).astype(BF16)

    C = pbuf_ref.shape[1]
    nz = zs_ref.shape[0] * zs_ref.shape[2]
    ng = gates_ref.shape[1]
    proj = _dot(h, wall_ref[...])
    pbuf_ref[HALO:HALO + tm, :] = proj[:, :C]
    conv = conv_ref[...]
    assert CONV_K == 4
    n_qk = 2 * GDN_HEADS
    for cb in range(pbuf_ref.shape[1] // V7X_LANES):
        c0 = cb * V7X_LANES
        xs = pbuf_ref[:, c0:c0 + V7X_LANES]
        w = [conv[j:j + 1, c0:c0 + V7X_LANES] for j in range(CONV_K)]
        x1 = pltpu.roll(xs, 1, axis=0)
        near = w[3] * xs + w[2] * x1
        far = w[1] * xs + w[0] * x1
        y = _silu(near[HALO:] + pltpu.roll(far, 2, axis=0)[HALO:])
        if cb < n_qk:
            y = y * lax.rsqrt(jnp.sum(y * y, axis=-1, keepdims=True) + EPS)
        qkv_ref[cb] = y.astype(BF16)
    pbuf_ref[0:HALO, :] = pbuf_ref[tm:tm + HALO, :]

    for hh in range(GDN_HEADS):
        zs_ref[hh] = proj[:, C + hh * GDN_DV:C + (hh + 1) * GDN_DV].astype(BF16)
    gates_ref[...] = proj[:, C + nz:C + nz + ng].astype(BF16)
    lat_ref[...] = proj[:, C + nz + ng:]
    ba_ref[...] = _dot_nt(wba_ref[...], h)


def _mixin(x, mod, gain, wall, wba, conv, l):
    B, S, D = x.shape
    C = conv.shape[-1]
    nz = GDN_HEADS * GDN_DV
    ng = 2 * D
    nl = wall.shape[-1] - C - nz - ng
    nb = wba.shape[1]
    tm = min(S, MIX_TOKENS)
    ncb = C // V7X_LANES
    vmem = ((C + nz + ng + nl) * D * 2 + 2 * tm * D * 4 + 2 * tm * (C + nz + ng) * 2 + 2 * tm * nl * 4
            + (tm + HALO) * C * 4 + tm * C * 4 + tm * (nz + ng) * 4 + 4 * 1024 * 1024)
    return pl.pallas_call(
        functools.partial(_mixin_kernel, tm=tm),
        grid=(B, S // tm),
        in_specs=[
            pl.BlockSpec((None, tm, D), lambda b, s: (b, s, 0)),
            pl.BlockSpec((None, None, N_MOD, D), lambda b, s: (l, b, 0, 0)),
            pl.BlockSpec((None, 1, D), lambda b, s: (l, 0, 0)),
            _resident((None, D, C + nz + ng + nl), lambda b, s: (l, 0, 0)),
            _resident((None, nb, D), lambda b, s: (l, 0, 0)),
            pl.BlockSpec((None, CONV_K, C), lambda b, s: (l, 0, 0)),
        ],
        out_specs=[
            pl.BlockSpec((None, ncb, tm, V7X_LANES), lambda b, s: (b, 0, s, 0)),
            pl.BlockSpec((None, GDN_HEADS, tm, GDN_DV), lambda b, s: (b, 0, s, 0)),
            pl.BlockSpec((None, tm, ng), lambda b, s: (b, s, 0)),
            pl.BlockSpec((None, tm, nl), lambda b, s: (b, s, 0)),
            pl.BlockSpec((None, nb, tm), lambda b, s: (b, 0, s)),
        ],
        out_shape=[
            jax.ShapeDtypeStruct((B, ncb, S, V7X_LANES), BF16),
            jax.ShapeDtypeStruct((B, GDN_HEADS, S, GDN_DV), BF16),
            jax.ShapeDtypeStruct((B, S, ng), BF16),
            jax.ShapeDtypeStruct((B, S, nl), F32),
            jax.ShapeDtypeStruct((B, nb, S), F32),
        ],
        scratch_shapes=[pltpu.VMEM((tm + HALO, C), F32)],
        compiler_params=_params(vmem, 2),
        name="mixin",
    )(x, mod, gain, wall, wba, conv)


def _gate_kernel(ba_ref, alog_ref, dtb_ref, grow_ref, gcol_ref, *, tm, NB):
    H = GDN_HEADS
    betas, gs = [], []
    for bb in range(NB):
        ba = ba_ref[bb]
        betas.append(_sigmoid(ba[0:H]))
        xa = ba[H:2 * H] + dtb_ref[...]
        softplus = jnp.maximum(xa, 0.0) + jnp.log(1.0 + jnp.exp(-jnp.abs(xa)))
        gs.append(-jnp.exp(alog_ref[...]) * softplus)
    beta = jnp.concatenate(betas, axis=0)
    g = jnp.concatenate(gs, axis=0)

    ii = lax.broadcasted_iota(jnp.int32, (tm, tm), 0)
    jj = lax.broadcasted_iota(jnp.int32, (tm, tm), 1)
    same = (ii >> LOG2_CHUNK) == (jj >> LOG2_CHUNK)
    upper =jnp.where(same & (ii <= jj), 1.0, 0.0).astype(BF16)
    block = jnp.where(same, 1.0, 0.0).astype(BF16)
    eye = jnp.where(ii == jj, 1.0, 0.0).astype(BF16)
    gam = None
    glast = None
    for part in _split3(g):
        t1 = _dot(part, upper)
        t2 = _dot(part, block)
        gam = t1 if gam is None else gam + t1
        glast = t2 if glast is None else glast + t2
    egam = jnp.exp(gam)
    ekd = jnp.exp(glast - gam)
    egl = jnp.exp(glast)
    zero = jnp.zeros_like(gam[0:1])
    rows = []
    for n in range(NB * H):
        r = jnp.concatenate([beta[n:n + 1], gam[n:n + 1], egam[n:n + 1], ekd[n:n + 1],
                             egl[n:n + 1], zero, zero, zero], axis=0)
        grow_ref[n // H, n % H] = r
        rows.append(r)
    allrows = jnp.concatenate(rows, axis=0)
    cols = None
    for part in _split3(allrows):
        t = _dot_nt(eye, part)
        cols = t if cols is None else cols + t
    for bb in range(NB):
        gcol_ref[bb] = cols[:, 8 * H * bb:8 * H * (bb + 1)]


def _gdn_gates(ba, a_log, dt_bias, l):
    B, nb, S = ba.shape
    H = GDN_HEADS
    NB = math.gcd(B, GATE_BATCH)
    tm = min(S, GATE_TOKENS)
    vmem = 8 * tm * tm * 4 + 6 * NB * H * tm * V7X_LANES * 4
    return pl.pallas_call(
        functools.partial(_gate_kernel, tm=tm, NB=NB),
        grid=(B // NB, S // tm),
        in_specs=[
            pl.BlockSpec((NB, nb, tm), lambda b, s: (b, 0, s)),
            pl.BlockSpec((None, H, 1), lambda b, s: (l, 0, 0)),
            pl.BlockSpec((None, H, 1), lambda b, s: (l, 0, 0)),
        ],
        out_specs=[
            pl.BlockSpec((NB, H, 8, tm), lambda b, s: (b, 0, 0, s)),
            pl.BlockSpec((NB, tm, 8 * H), lambda b, s: (b, s, 0)),
        ],
        out_shape=[
            jax.ShapeDtypeStruct((B, H, 8, S), F32),
            jax.ShapeDtypeStruct((B, S, 8 * H), F32),
        ],
        compiler_params=_params(vmem, 2),
        name="gdn_gates",
    )(ba, a_log, dt_bias)


def _gdn_kernel(q_ref, k_ref, v_ref, zs_ref, grow_ref, gcol_ref, og_ref, o_ref, state_ref, *, tc, G, NB):
    s = pl.program_id(2)

    @pl.when(s == 0)
    def _():
        state_ref[...] = jnp.zeros(state_ref.shape, F32)

    heads = range(NB * G)

    def at(ref, n):
        return ref[n // G, n % G]

    n_chunks = tc // CHUNK
    scale = GDN_DK ** -0.5
    ii = lax.broadcasted_iota(jnp.int32, (tc, tc), 0)
    jj = lax.broadcasted_iota(jnp.int32, (tc, tc), 1)
    same = (ii >> LOG2_CHUNK) == (jj >> LOG2_CHUNK)
    causal = same & (ii >= jj)
    strict = same & (ii > jj)
    lane = lax.broadcasted_iota(jnp.int32, (GDN_DK, tc), 1) >> LOG2_CHUNK

    gc = [gcol_ref[g // G][:, 8 * (g % G):8 * (g % G) + 8] for g in heads]
    gr = [at(grow_ref, g) for g in heads]
    k = [at(k_ref, g) for g in heads]
    kf = [k[g].astype(F32) for g in heads]
    kb = [kf[g] * gc[g][:, 0:1] for g in heads]
    decay = [jnp.where(causal, jnp.exp(jnp.minimum(gc[g][:, 1:2] - gr[g][1:2, :], 0.0)), 0.0) for g in heads]

    akq = [_dot_nt(jnp.concatenate([kb[g].astype(BF16), at(q_ref, g)], axis=0), k[g]) for g in heads]
    m = [jnp.where(strict, akq[g][:tc] * decay[g], 0.0) for g in heads]
    qk = [(akq[g][tc:] * (decay[g] * scale)).astype(BF16) for g in heads]

    rlane = lax.broadcasted_iota(jnp.int32, (CHUNK, tc), 1)
    rrow = lax.broadcasted_iota(jnp.int32, (CHUNK, tc), 0)
    eye_row = jnp.where((rlane & (CHUNK - 1)) == rrow, 1.0, 0.0)
    rblk = rlane >> LOG2_CHUNK

    def blockdiag(x_row):
        return jnp.concatenate([jnp.where(rblk == c, x_row, 0.0) for c in range(n_chunks)], axis=0).astype(BF16)

    def side_by_side(x):
        acc = x[0:CHUNK]
        for c in range(1, n_chunks):
            acc = acc + x[c * CHUNK:(c + 1) * CHUNK]
        return acc

    m_row = [side_by_side(m[g]) for g in heads]
    p = [_dot(m_row[g].astype(BF16), blockdiag(m_row[g])) for g in heads]
    t_row = [eye_row - m_row[g] for g in heads]
    for _ in range(LOG2_CHUNK - 2):
        r = [_dot(jnp.concatenate([p[g], t_row[g]], axis=0).astype(BF16), blockdiag(p[g])) for g in heads]
        p = [r[g][:CHUNK] for g in heads]
        t_row = [t_row[g] + r[g][CHUNK:] for g in heads]
    t_row = [t_row[g] + _dot(t_row[g].astype(BF16), blockdiag(p[g])) for g in heads]

    rhs = [jnp.concatenate([at(v_ref, g).astype(F32) * gc[g][:, 0:1], kb[g] * gc[g][:, 2:3]],
                           axis=1).astype(BF16) for g in heads]
    uw = [_dot(blockdiag(t_row[g]), rhs[g]).astype(BF16) for g in heads]

    kdt = [(kf[g] * gc[g][:, 3:4]).T for g in heads]
    pre_lhs = [jnp.concatenate([jnp.where(lane == c, kdt[g], 0.0).astype(BF16) for c in range(n_chunks)]
                               + [qk[g]], axis=0) for g in heads]
    pre = [_dot(pre_lhs[g], uw[g]) for g in heads]
    qd = [at(q_ref, g).astype(F32) * (gc[g][:, 2:3] * scale) for g in heads]
    base = n_chunks * GDN_DK
    qmod = [(qd[g] - pre[g][base:, GDN_DV:]).astype(BF16) for g in heads]

    states = [at(state_ref, g) for g in heads]
    outs = [[] for _ in heads]
    for c in range(n_chunks):
        r0 = c * CHUNK
        for g in heads:
            kblk = pre[g][c * GDN_DK:(c + 1) * GDN_DK]
            lhs = jnp.concatenate([kblk[:, GDN_DV:].astype(BF16), qmod[g][r0:r0 + CHUNK]], axis=0)
            r = _dot(lhs, states[g].astype(BF16))
            outs[g].append(r[GDN_DK:] + pre[g][base + r0:base + r0 + CHUNK, :GDN_DV])
            states[g] = states[g] * gr[g][4:5, r0:r0 + 1] - r[:GDN_DK] + kblk[:, :GDN_DV]
    for g in heads:
        bb, hh = g // G, g % G
        state_ref[bb, hh] = states[g]
        o = jnp.concatenate(outs[g], axis=0)
        o = o * lax.rsqrt(jnp.mean(o * o, axis=-1, keepdims=True) + EPS) * og_ref[...]
        o_ref[bb, :, hh * GDN_DV:(hh + 1) * GDN_DV] = (o * _silu(zs_ref[bb, hh].astype(F32))).astype(BF16)


def _gdn(qkv, zs, grow, gcol, out_gain, l):
    B, _, S, _ = qkv.shape
    H = GDN_HEADS
    G = GDN_GROUP
    NB = math.gcd(B, GDN_BATCH)
    assert G == H, "the gate columns are read per head from one lane-packed block"
    tc = min(S, GDN_TOKENS)
    vmem = NB * G * (24 * tc * tc * 4 + 24 * tc * V7X_LANES * 4)
    return pl.pallas_call(
        functools.partial(_gdn_kernel, tc=tc, G=G, NB=NB),
        grid=(B // NB, H // G, S // tc),
        in_specs=[
            pl.BlockSpec((NB, G, tc, GDN_DK), lambda b, h, s: (b, h, s, 0)),
            pl.BlockSpec((NB, G, tc, GDN_DK), lambda b, h, s: (b, H // G + h, s, 0)),
            pl.BlockSpec((NB, G, tc, GDN_DV), lambda b, h, s: (b, 2 * H // G + h, s, 0)),
            pl.BlockSpec((NB, G, tc, GDN_DV), lambda b, h, s: (b, h, s, 0)),
            pl.BlockSpec((NB, G, 8, tc), lambda b, h, s: (b, h, 0, s)),
            pl.BlockSpec((NB, tc, 8 * H), lambda b, h, s: (b, s, 0)),
            pl.BlockSpec((None, 1, GDN_DV), lambda b, h, s: (l, 0, 0)),
        ],
        out_specs=pl.BlockSpec((NB, tc, G * GDN_DV), lambda b, h, s: (b, s, h)),
        out_shape=jax.ShapeDtypeStruct((B, S, H * GDN_DV), BF16),
        scratch_shapes=[pltpu.VMEM((NB, G, GDN_DK, GDN_DV), F32)],
        compiler_params=_params(vmem, 3),
        name="gdn",
    )(qkv, qkv, qkv, zs, grow, gcol, out_gain)


def _swap_halves(y):
    half = y.shape[-1] // 2
    return jnp.concatenate([y[:, half:], y[:, :half]], axis=1)


def _mla_prep_kernel(lat_ref, cos_ref, sin_ref, cost_ref, sint_ref, qg_ref, kvg_ref, wqt_ref, wk_ref, wvt_ref,
                     qnc_ref, kn_ref, qt_ref, k_ref, vt_ref, qs_ref, ks_ref):
    H = MLA_HEADS
    half = QK_ROPE // 2
    lat = lat_ref[...]
    ql = lat[:, :Q_LORA]
    kvl = lat[:, Q_LORA:Q_LORA + KV_LORA]
    kpe = lat[:, Q_LORA + KV_LORA:Q_LORA + KV_LORA + QK_ROPE]
    qln = (ql * lax.rsqrt(jnp.mean(ql * ql, axis=-1, keepdims=True) + EPS) * qg_ref[...]).astype(BF16)
    kvn = (kvl * lax.rsqrt(jnp.mean(kvl * kvl, axis=-1, keepdims=True) + EPS) * kvg_ref[...]).astype(BF16)
    qs_ref[...] = _dot_nt(wqt_ref[...], qln)
    ks_ref[...] = _dot(kvn, wk_ref[...])
    vt = _dot_nt(wvt_ref[...], kvn)

    cost = cost_ref[...]
    sint = sint_ref[...]
    qnc = qnc_ref[...]
    scale = MLA_QK ** -0.5 * math.log2(math.e)
    for hh in range(H):
        blk = qs_ref[hh * MLA_QK:(hh + 1) * MLA_QK, :]
        r = lax.rsqrt(jnp.sum(blk * blk, axis=0, keepdims=True) * (1.0 / MLA_QK) + EPS) * scale
        y = blk * qnc * r
        y1 = y[QK_NOPE:QK_NOPE + half]
        y2 = y[QK_NOPE + half:]
        qt_ref[hh, 0:QK_NOPE, :] = y[:QK_NOPE].astype(BF16)
        qt_ref[hh, QK_NOPE:QK_NOPE + half, :] = (y1 * cost - y2 * sint).astype(BF16)
        qt_ref[hh, QK_NOPE + half:MLA_QK, :] = (y2 * cost + y1 * sint).astype(BF16)
        vt_ref[hh] = vt[hh * V_HEAD:(hh + 1) * V_HEAD].astype(BF16)

    kn = kn_ref[...]
    yk = kpe * kn[:, QK_NOPE:]
    kr = yk * cos_ref[...] + _swap_halves(yk) * sin_ref[...]
    kpe_ss = jnp.sum(kpe * kpe, axis=-1, keepdims=True)
    for hh in range(H):
        k_nope = ks_ref[:, hh * QK_NOPE:(hh + 1) * QK_NOPE]
        ss = jnp.sum(k_nope * k_nope, axis=-1, keepdims=True) + kpe_ss
        r = lax.rsqrt(ss * (1.0 / MLA_QK) + EPS)
        k_ref[hh, :, 0:QK_NOPE] = (k_nope * r * kn[:, :QK_NOPE]).astype(BF16)
        k_ref[hh, :, QK_NOPE:MLA_QK] = (kr * r).astype(BF16)


def _mla_prep(lat, cos2, sin2, cost, sint, q_lat_gain, kv_lat_gain, wqt, wk, wvt, q_norm_col, k_norm, l):
    B, S, nl = lat.shape
    H = MLA_HEADS
    half = QK_ROPE // 2
    tm = min(S, ATTN_Q)
    nq = wqt.shape[1]
    nk = wk.shape[-1]
    nv = wvt.shape[1]
    vmem = ((nq * Q_LORA + (nk + nv) * KV_LORA) * 2 + 2 * tm * nl * 4 + 3 * tm * (nq + nk + nv) * 4
            + 4 * H * tm * (256 + MLA_QK + V_HEAD) * 2)
    return pl.pallas_call(
        _mla_prep_kernel,
        grid=(B, S // tm),
        in_specs=[
            pl.BlockSpec((None, tm, nl), lambda b, s: (b, s, 0)),
            pl.BlockSpec((None, tm, QK_ROPE), lambda b, s: (b, s, 0)),
            pl.BlockSpec((None, tm, QK_ROPE), lambda b, s: (b, s, 0)),
            pl.BlockSpec((None, half, tm), lambda b, s: (b, 0, s)),
            pl.BlockSpec((None, half, tm), lambda b, s: (b, 0, s)),
            pl.BlockSpec((None, 1, Q_LORA), lambda b, s: (l, 0, 0)),
            pl.BlockSpec((None, 1, KV_LORA), lambda b, s: (l, 0, 0)),
            _resident((None, nq, Q_LORA), lambda b, s: (l, 0, 0)),
            _resident((None, KV_LORA, nk), lambda b, s: (l, 0, 0)),
            _resident((None, nv, KV_LORA), lambda b, s: (l, 0, 0)),
            pl.BlockSpec((None, MLA_QK, 1), lambda b, s: (l, 0, 0)),
            pl.BlockSpec((None, 1, MLA_QK), lambda b, s: (l, 0, 0)),
        ],
        out_specs=[
            pl.BlockSpec((None, H, MLA_QK, tm), lambda b, s: (b, 0, 0, s)),
            pl.BlockSpec((None, H, tm, MLA_QK), lambda b, s: (b, 0, s, 0)),
            pl.BlockSpec((None, H, None, V_HEAD, tm), lambda b, s: (b, 0, s, 0, 0)),
        ],
        out_shape=[
            jax.ShapeDtypeStruct((B, H, MLA_QK, S), BF16),
            jax.ShapeDtypeStruct((B, H, S, MLA_QK), BF16),
            jax.ShapeDtypeStruct((B, H, S // tm, V_HEAD, tm), BF16),
        ],
        scratch_shapes=[pltpu.VMEM((nq, tm), F32), pltpu.VMEM((tm, nk), F32)],
        compiler_params=_params(vmem, 2),
        name="mla_prep",
    )(lat, cos2, sin2, cost, sint, q_lat_gain, kv_lat_gain, wqt, wk, wvt, q_norm_col, k_norm)


def _attn_kernel(qt_ref, k_ref, vt_ref, o_ref, sa_ref, sb_ref, mxa_ref, mxb_ref, m_ref, l_ref, acc_ref, *,
                 tq, chain, single):
    c = 0 if single else pl.program_id(2)
    bufs = ((sa_ref, mxa_ref), (sb_ref, mxb_ref))

    def scores(qt, j, buf):
        start = j * tq if isinstance(j, int) else pl.multiple_of(j * tq, tq)
        st = _dot(k_ref[pl.ds(start, tq), :], qt)
        buf[0][...] = st
        buf[1][...] = jnp.max(st, axis=0, keepdims=True)

    def reset(w):
        m_ref[w] = jnp.full(m_ref.shape[1:], -jnp.inf, F32)
        l_ref[w] = jnp.zeros(l_ref.shape[1:], F32)
        acc_ref[w] = jnp.zeros(acc_ref.shape[1:], F32)

    def update(w, j, buf, masked=False):
        st = buf[0][...]
        mx = buf[1][...]
        if masked:
            kpos = lax.broadcasted_iota(jnp.int32, (tq, tq), 0)
            qpos = lax.broadcasted_iota(jnp.int32, (tq, tq), 1)
            st = jnp.where(kpos <= qpos, st, -jnp.inf)
            mx = jnp.max(st, axis=0, keepdims=True)
        m = m_ref[w]
        m_new = jnp.maximum(m, mx)
        alpha = jnp.exp2(m - m_new)
        pr = jnp.exp2(st - m_new)
        m_ref[w] = m_new
        l_ref[w] = alpha * l_ref[w] + jnp.sum(pr, axis=0, keepdims=True)
        acc_ref[w] = alpha * acc_ref[w] + _dot(vt_ref[j], pr.astype(BF16))

    def full_blocks(w, qt, cur, nxt, trips):
        def body(t, _):
            scores(qt, 2 * t + 1, nxt)
            update(w, 2 * t, cur)
            scores(qt, 2 * t + 2, cur)
            update(w, 2 * t + 1, nxt)
            return 0
        if isinstance(trips, int):
            for t in range(trips):
                body(t, 0)
        else:
            lax.fori_loop(0, trips, body, 0)

    def q_block(u):
        return qt_ref[:, u * tq:(u + 1) * tq]

    cur, nxt = bufs
    reset(0)
    scores(q_block(0), 0, cur)
    for u in range(chain):
        w = u % 2
        i = c * chain + u
        full_blocks(w, q_block(u), cur, nxt, c * (chain // 2) + u // 2)
        if u % 2 == 1:
            scores(q_block(u), i, nxt)
            update(w, i - 1, cur)
            cur, nxt = nxt, cur
        if u + 1 < chain:
            reset(1 - w)
            scores(q_block(u + 1), 0, nxt)
        update(w, i, cur, masked=True)
        o_ref[u * tq:(u + 1) * tq, :] = (acc_ref[w] / l_ref[w]).T.astype(BF16)
        cur, nxt = nxt, cur


def _attention(qt, k, vt):
    B, H, S, _ = k.shape
    tq = vt.shape[-1]
    nq = S // tq
    chain = min(nq, ATTN_CHAIN)
    assert chain % 2 == 0 and nq % chain == 0, "q blocks are taken in even-length runs"
    vmem = (8 * tq * tq * 4 + 4 * S * (256 + V_HEAD) * 2 + 4 * chain * tq * (256 + V_HEAD) * 2
            + 16 * tq * V7X_LANES * 4)
    return pl.pallas_call(
        functools.partial(_attn_kernel, tq=tq, chain=chain, single=(nq == chain)),
        grid=(B, H, nq // chain),
        in_specs=[
            pl.BlockSpec((None, None, MLA_QK, chain * tq), lambda b, h, i: (b, h, 0, i)),
            pl.BlockSpec((None, None, S, MLA_QK), lambda b, h, i: (b, h, 0, 0)),
            pl.BlockSpec((None, None, S // tq, V_HEAD, tq), lambda b, h, i: (b, h, 0, 0, 0)),
        ],
        out_specs=pl.BlockSpec((None, chain * tq, V_HEAD), lambda b, h, i: (b, i, h)),
        out_shape=jax.ShapeDtypeStruct((B, S, H * V_HEAD), BF16),
        scratch_shapes=[pltpu.VMEM((tq, tq), F32), pltpu.VMEM((tq, tq), F32),
                        pltpu.VMEM((1, tq), F32), pltpu.VMEM((1, tq), F32),
                        pltpu.VMEM((2, 1, tq), F32), pltpu.VMEM((2, 1, tq), F32),
                        pltpu.VMEM((2, V_HEAD, tq), F32)],
        compiler_params=_params(vmem, 3),
        name="mla_attention",
    )(qt, k, vt)


def _merge_kernel(x_ref, mod_ref, oa_ref, ob_ref, gates_ref, wa_ref, wb_ref, wo_ref, o_ref):
    D = x_ref.shape[-1]
    mod = mod_ref[...]
    gates = _sigmoid(gates_ref[...].astype(F32))
    ya =_dot(oa_ref[...], wa_ref[...])
    yb = _dot(ob_ref[...], wb_ref[...])
    y = (gates[:, :D] * ya + gates[:, D:] * yb).astype(BF16)
    o_ref[...] = x_ref[...] + (1.0 + mod[5:6]) * _dot(y, wo_ref[...])


def _merge(x, mod, oa, ob, gates, wa, wb, wo, l):
    B, S, D = x.shape
    tm = min(S, MERGE_TOKENS)
    na = oa.shape[-1]
    nbv = ob.shape[-1]
    vmem = (na + nbv + D) * D * 2 + 4 * tm * D * 4 + 2 * tm * (na + nbv + 2 * D) * 2 + 4 * tm * D * 4
    return pl.pallas_call(
        _merge_kernel,
        grid=(B, S // tm),
        in_specs=[
            pl.BlockSpec((None, tm, D), lambda b, s: (b, s, 0)),
            pl.BlockSpec((None, None, N_MOD, D), lambda b, s: (l, b, 0, 0)),
            pl.BlockSpec((None, tm, na), lambda b, s: (b, s, 0)),
            pl.BlockSpec((None, tm, nbv), lambda b, s: (b, s, 0)),
            pl.BlockSpec((None, tm, 2 * D), lambda b, s: (b, s, 0)),
            _resident((None, na, D), lambda b, s: (l, 0, 0)),
            _resident((None, nbv, D), lambda b, s: (l, 0, 0)),
            _resident((None, D, D), lambda b, s: (l, 0, 0)),
        ],
        out_specs=pl.BlockSpec((None, tm, D), lambda b, s: (b, s, 0)),
        out_shape=jax.ShapeDtypeStruct((B, S, D), F32),
        compiler_params=_params(vmem, 2),
        name="merge",
    )(x, mod, oa, ob, gates, wa, wb, wo)


def _pack_w_in(w_in):
    conv_c = 2 * GDN_HEADS * GDN_DK + GDN_HEADS * GDN_DV
    gdn_v = GDN_HEADS * GDN_DV
    D = w_in.shape[1]
    sizes = (conv_c, gdn_v, GDN_HEADS, GDN_HEADS, Q_LORA, KV_LORA, QK_ROPE, D, D)
    offs = np.cumsum((0,) + sizes)
    w_in = w_in.astype(BF16)
    qkv, z, b_raw, a_raw, q_lat, kv_lat, k_pe, gate_a, gate_b = (
        w_in[:, :, int(offs[i]):int(offs[i + 1])] for i in range(len(sizes)))
    n_lat = Q_LORA + KV_LORA + QK_ROPE
    pad = (-n_lat) % V7X_LANES
    lat = jnp.concatenate([q_lat, kv_lat, k_pe, jnp.zeros(w_in.shape[:2] + (pad,), w_in.dtype)], axis=-1)
    ba = jnp.swapaxes(jnp.concatenate([b_raw, a_raw], axis=-1), 1, 2)
    wall = jnp.concatenate([qkv, z, gate_a, gate_b, lat], axis=-1)
    return wall, ba


def _pack_kv_up(w_kv_up):
    L = w_kv_up.shape[0]
    w = w_kv_up.astype(BF16).reshape(L, KV_LORA, MLA_HEADS, QK_NOPE + V_HEAD)
    wk = w[..., :QK_NOPE].reshape(L, KV_LORA, MLA_HEADS * QK_NOPE)
    wv = w[..., QK_NOPE:].reshape(L, KV_LORA, MLA_HEADS * V_HEAD)
    return wk, jnp.swapaxes(wv, 1, 2)


def kernel(x, c, positions, ada_w, ada_b, norm_ffn1, ffn1_w1, ffn1_w3, ffn1_w2, norm_mix, w_in, gdn_conv, gdn_a_log, gdn_dt_bias, gdn_out_gain, mla_q_lat_gain, mla_kv_lat_gain, mla_w_q_up, mla_w_kv_up, mla_q_norm, mla_k_norm, w_branch_a, w_branch_b, w_out, norm_ffn2, ffn2_w1, ffn2_w3, ffn2_w2):
    L = ada_w.shape[0]
    S = x.shape[1]
    assert S % CHUNK == 0 and S % min(S, GDN_TOKENS) == 0 and S % min(S, FFN_TOKENS) == 0
    assert S % min(S, MERGE_TOKENS) == 0 and S % min(S, ATTN_Q) == 0

    mod = _ada_mod(c, ada_w, ada_b)
    cos2, sin2, cost, sint = _rope_tables(positions)

    row = lambda a: a.reshape(L, 1, a.shape[-1])
    col = lambda a: a.reshape(L, a.shape[-1], 1)
    bf = lambda a: a.astype(BF16)
    wall, wba = _pack_w_in(w_in)
    wqt = jnp.swapaxes(bf(mla_w_q_up), 1, 2)
    wk, wvt = _pack_kv_up(mla_w_kv_up)
    f1 = (bf(ffn1_w1), bf(ffn1_w3), bf(ffn1_w2))
    f2 = (bf(ffn2_w1), bf(ffn2_w3), bf(ffn2_w2))
    wa, wb, wo = bf(w_branch_a), bf(w_branch_b), bf(w_out)
    g_ffn1, g_mix, g_ffn2 = row(norm_ffn1), row(norm_mix), row(norm_ffn2)
    og, qlg, kvlg = row(gdn_out_gain), row(mla_q_lat_gain), row(mla_kv_lat_gain)
    qnc, kn = col(mla_q_norm), row(mla_k_norm)
    a_log, dt_bias = col(gdn_a_log), col(gdn_dt_bias)

    for l in range(L):
        x = _ffn(x, mod, g_ffn1, *f1, l, 0)
        qkv, zs, gates, lat, ba = _mixin(x, mod, g_mix, wall, wba, gdn_conv, l)
        grow, gcol = _gdn_gates(ba, a_log, dt_bias, l)
        o_gdn = _gdn(qkv, zs, grow, gcol, og, l)
        qt, k, vt = _mla_prep(lat, cos2, sin2, cost, sint, qlg, kvlg, wqt, wk, wvt, qnc, kn, l)
        o_mla = _attention(qt, k, vt)
        x = _merge(x, mod, o_gdn, o_mla, gates, wa, wb, wo, l)
        x = _ffn(x, mod, g_ffn2, *f2, l, 6)
    return x
```
